```python
import jax, jax.numpy as jnp
from jax import lax
import numpy as np

D_MODEL = 1024
BATCH = 8
SEQ = 2048
DEPTH = 1
DEC_BATCH = 32
DEC_SEQ = 1
PAST_LEN = 8192
PAGE_SIZE = 128

H_A = 8
DH_A = 64
Q_BLOCK = 128
FORGET_BIAS = 7.0
H_B = 4
DK_B = 64
DV_B = 128
GLA_RANK = 16
GLA_TAU = 16.0
GLA_CHUNK = 64
N_EXPERTS = 32
TOP_K = 4
D_FF = 1024
SWIGLU_LIMIT = 7.0
SWIGLU_ALPHA = 1.702
MOE_BLOCK = 128
RMS_EPS = 1e-6

W_A = H_A * DH_A
WK_B = H_B * DK_B
WV_B = H_B * DV_B
IN_SIZES = (W_A, W_A, W_A, H_A, WK_B, WK_B, WV_B, WV_B, GLA_RANK, D_MODEL, D_MODEL)
D_IN = sum(IN_SIZES)

kernel_name = "fox_gla_parallel_moe_adaln_step"


def rmsnorm(x, g):
    xf = x.astype(jnp.float32)
    y = xf * lax.rsqrt(jnp.mean(xf * xf, axis=-1, keepdims=True) + RMS_EPS)
    return (y * g.astype(jnp.float32)).astype(x.dtype)


def split_cols(z, sizes):
    idx = np.cumsum(np.array(sizes))[:-1].tolist()
    return jnp.split(z, idx, axis=-1)


def fox_attention(q, k, v, cq, ck, q_pos, k_pos):
    B, T, H, Dh = q.shape
    qbs = Q_BLOCK if T % Q_BLOCK == 0 else T
    nb = T // qbs
    scale = Dh ** -0.5
    q_blocks = q.reshape(B, nb, qbs, H, Dh).transpose(1, 0, 2, 3, 4)
    cq_blocks = cq.astype(jnp.float32).reshape(B, nb, qbs, H).transpose(1, 0, 3, 2)
    pos_blocks = q_pos.reshape(nb, qbs)
    ck_t = jnp.swapaxes(ck.astype(jnp.float32), 1, 2)

    def one_block(args):
        qi, cqi, pi = args
        s = jnp.einsum('bqhd,bshd->bhqs', qi, k).astype(jnp.float32) * scale
        s = s + cqi[..., :, None] - ck_t[..., None, :]
        causal = k_pos[None, :] <= pi[:, None]
        s = jnp.where(causal, s, -jnp.inf)
        p = jax.nn.softmax(s, axis=-1)
        return jnp.einsum('bhqs,bshd->bqhd', p.astype(v.dtype), v)

    o = lax.map(one_block, (q_blocks, cq_blocks, pos_blocks))
    return o.transpose(1, 0, 2, 3, 4).reshape(B, T, H, Dh)


def gla_recurrence(q, k, v, log_a, s0):
    B, T, H, K = q.shape
    V = v.shape[-1]
    C = GLA_CHUNK if T % GLA_CHUNK == 0 else T
    n = T // C

    def chunks(z):
        return z.reshape(B, n, C, H, z.shape[-1]).transpose(1, 0, 3, 2, 4).astype(jnp.float32)

    tril = jnp.tril(jnp.ones((C, C), dtype=bool))

    def step(S, inp):
        qc, kc, vc, ac = inp
        b = jnp.cumsum(ac, axis=2)
        diff = b[:, :, :, None, :] - b[:, :, None, :, :]
        decay = jnp.exp(jnp.where(tril[:, :, None], diff, -jnp.inf))
        A = jnp.einsum('bhtk,bhsk,bhtsk->bhts', qc, kc, decay)
        o = jnp.einsum('bhts,bhsv->bhtv', A, vc) + jnp.einsum('bhtk,bhkv->bhtv', qc * jnp.exp(b), S)
        b_last = b[:, :, -1, :]
        S_new = jnp.exp(b_last)[..., None] * S + jnp.einsum(
            'bhsk,bhsv->bhkv', kc * jnp.exp(b_last[:, :, None, :] - b), vc)
        return S_new, o

    S_fin, o = lax.scan(step, s0.astype(jnp.float32), (chunks(q), chunks(k), chunks(v), chunks(log_a)))
    o = o.transpose(1, 0, 3, 2, 4).reshape(B, T, H, V)
    return o, S_fin


def expert_ffn(xb, w_gu, b_gu, w_d, b_d):
    gu = xb @ w_gu + b_gu
    gate, up = jnp.split(gu, 2, axis=-1)
    gate = jnp.minimum(gate, SWIGLU_LIMIT)
    up = jnp.clip(up, -SWIGLU_LIMIT, SWIGLU_LIMIT)
    glu = gate * jax.nn.sigmoid(SWIGLU_ALPHA * gate)
    return ((up + 1) * glu) @ w_d + b_d


def moe(h, w_router, b_router, w_gu, b_gu, w_d, b_d):
    lead = h.shape[:-1]
    D = h.shape[-1]
    x = h.reshape(-1, D)
    N = x.shape[0]
    NK = N * TOP_K
    logits = (x @ w_router).astype(jnp.float32) + b_router.astype(jnp.float32)
    top_v, top_e = lax.top_k(logits, TOP_K)
    gates = jax.nn.softmax(top_v, axis=-1)
    flat_e = top_e.reshape(-1).astype(jnp.int32)
    order = jnp.argsort(flat_e)
    sorted_e = flat_e[order]
    counts = jnp.bincount(flat_e, length=N_EXPERTS).astype(jnp.int32)
    padded = (counts + MOE_BLOCK - 1) // MOE_BLOCK * MOE_BLOCK
    pad_end = jnp.cumsum(padded).astype(jnp.int32)
    pad_start = pad_end - padded
    grp_start = jnp.cumsum(counts).astype(jnp.int32) - counts
    rank = jnp.arange(NK, dtype=jnp.int32) - grp_start[sorted_e]
    dest_sorted = (pad_start[sorted_e] + rank).astype(jnp.int32)
    dest = jnp.zeros((NK,), jnp.int32).at[order].set(dest_sorted)
    n_blocks = -(-(NK + N_EXPERTS * (MOE_BLOCK - 1)) // MOE_BLOCK)
    R = n_blocks * MOE_BLOCK
    row_tok = jnp.zeros((R,), jnp.int32).at[dest].set(jnp.arange(NK, dtype=jnp.int32) // TOP_K)
    block_start = jnp.arange(n_blocks, dtype=jnp.int32) * MOE_BLOCK
    block_e = jnp.minimum(jnp.searchsorted(pad_end, block_start, side='right'), N_EXPERTS - 1)
    x_rows = x[row_tok].reshape(n_blocks, MOE_BLOCK, D)

    def run_block(args):
        xb, e = args
        return expert_ffn(xb, w_gu[e], b_gu[e], w_d[e], b_d[e])

    y_rows = lax.map(run_block, (x_rows, block_e)).reshape(R, D)
    y_sel = y_rows[dest].reshape(N, TOP_K, D)
    y = jnp.einsum('nk,nkd->nd', gates.astype(y_sel.dtype), y_sel)
    return y.reshape(*lead, D)


def decoder_layer(x, c, past, lw):
    (g_mix, g_ffn, w_ada, b_ada, w_in, b_f, w_alpha, b_alpha, g_gla,
     w_branch, w_o, w_router, b_router, w_gu, b_gu, w_d, b_d) = lw
    B, T, _ = x.shape
    mod = (jax.nn.silu(c) @ w_ada + b_ada)[:, None, :]
    sh1, sc1, gt1, sh2, sc2, gt2 = jnp.split(mod, 6, axis=-1)

    h = rmsnorm(x, g_mix) * (1 + sc1) + sh1
    qa, ka, va, fa, qb, kb, vb, rb, alr, ga, gb = split_cols(h @ w_in, IN_SIZES)
    qa = qa.reshape(B, T, H_A, DH_A)
    ka = ka.reshape(B, T, H_A, DH_A)
    va = va.reshape(B, T, H_A, DH_A)
    logf = jax.nn.log_sigmoid((fa + b_f).astype(jnp.float32))
    cum_new = jnp.cumsum(logf, axis=1)
    if past is None:
        pos = jnp.arange(T, dtype=jnp.int32)
        o_a = fox_attention(qa, ka, va, cum_new, cum_new, pos, pos)
        s0 = jnp.zeros((B, H_B, DK_B, DV_B), jnp.float32)
    else:
        k_past, v_past, logf_past, s0 = past
        P = k_past.shape[1]
        lp = logf_past.astype(jnp.float32)
        suffix = lax.cumsum(lp, axis=1, reverse=True) - lp
        ck = jnp.concatenate([-suffix, cum_new], axis=1)
        k_all = jnp.concatenate([k_past.astype(ka.dtype), ka], axis=1)
        v_all = jnp.concatenate([v_past.astype(va.dtype), va], axis=1)
        q_pos = P + jnp.arange(T, dtype=jnp.int32)
        k_pos = jnp.arange(P + T, dtype=jnp.int32)
        o_a = fox_attention(qa, k_all, v_all, cum_new, ck, q_pos, k_pos)

    log_a = jax.nn.log_sigmoid((alr @ w_alpha + b_alpha).astype(jnp.float32)) / GLA_TAU
    o_b, s_new = gla_recurrence((qb * DK_B ** -0.5).reshape(B, T, H_B, DK_B),
                                kb.reshape(B, T, H_B, DK_B),
                                vb.reshape(B, T, H_B, DV_B),
                                log_a.reshape(B, T, H_B, DK_B), s0)
    o_b = rmsnorm(o_b.astype(x.dtype), g_gla.reshape(H_B, DV_B))
    o_b = o_b.reshape(B, T, WV_B) * jax.nn.silu(rb)

    y_a = o_a.reshape(B, T, W_A) @ w_branch[:W_A]
    y_b = o_b @ w_branch[W_A:]
    mixed = (jax.nn.sigmoid(ga) * y_a + jax.nn.sigmoid(gb) * y_b) @ w_o
    x = x + gt1 * mixed

    h2 = rmsnorm(x, g_ffn) * (1 + sc2) + sh2
    x = x + gt2 * moe(h2, w_router, b_router, w_gu, b_gu, w_d, b_d)
    return x, ka, va, logf, s_new


def setup_inputs(seed: int = 0) -> dict:
    key = jax.random.key(seed)
    ks = jax.random.split(key, 40)
    n_pages = PAST_LEN // PAGE_SIZE
    n_phys = (DEC_BATCH * n_pages * 5) // 4
    f32 = jnp.float32

    def nrm(k, shape, scale):
        return jax.random.normal(k, shape, f32) * scale

    L = DEPTH
    d = D_MODEL
    inputs = {
        "x_prompt": nrm(ks[0], (BATCH, SEQ, d), 1.0),
        "x_sample": nrm(ks[1], (DEC_BATCH, DEC_SEQ, d), 1.0),
        "c_prompt": nrm(ks[2], (BATCH, d), 1.0),
        "c_sample": nrm(ks[3], (DEC_BATCH, d), 1.0),
        "cache_k": nrm(ks[4], (L, n_phys, PAGE_SIZE, H_A, DH_A), 1.0),
        "cache_v": nrm(ks[5], (L, n_phys, PAGE_SIZE, H_A, DH_A), 1.0),
        "cache_logf": jax.nn.log_sigmoid(FORGET_BIAS + nrm(ks[6], (L, n_phys, PAGE_SIZE, H_A), 1.0)),
        "state_gla": nrm(ks[7], (L, DEC_BATCH, H_B, DK_B, DV_B), 0.5),
        "page_table": jax.random.permutation(ks[8], n_phys)[:DEC_BATCH * n_pages]
                          .reshape(DEC_BATCH, n_pages).astype(jnp.int32),
        "g_mix": 1.0 + nrm(ks[9], (L, d), 0.01),
        "g_ffn": 1.0 + nrm(ks[10], (L, d), 0.01),
        "g_final": 1.0 + nrm(ks[11], (d,), 0.01),
        "w_ada": nrm(ks[12], (L, d, 6 * d), 0.5 * d ** -0.5),
        "b_ada": nrm(ks[13], (L, 6 * d), 0.01),
        "w_in": nrm(ks[14], (L, d, D_IN), d ** -0.5),
        "b_f": FORGET_BIAS + nrm(ks[15], (L, H_A), 0.5),
        "w_alpha": nrm(ks[16], (L, GLA_RANK, WK_B), GLA_RANK ** -0.5),
        "b_alpha": nrm(ks[17], (L, WK_B), 0.5),
        "g_gla": 1.0 + nrm(ks[18], (L, WV_B), 0.01),
        "w_branch": nrm(ks[19], (L, W_A + WV_B, d), W_A ** -0.5),
        "w_o": nrm(ks[20], (L, d, d), d ** -0.5),
        "w_router": nrm(ks[21], (L, d, N_EXPERTS), d ** -0.5),
        "b_router": nrm(ks[22], (L, N_EXPERTS), 0.01),
        "w_gu": nrm(ks[23], (L, N_EXPERTS, d, 2 * D_FF), d ** -0.5),
        "b_gu": nrm(ks[24], (L, N_EXPERTS, 2 * D_FF), 0.01),
        "w_d": nrm(ks[25], (L, N_EXPERTS, D_FF, d), D_FF ** -0.5),
        "b_d": nrm(ks[26], (L, N_EXPERTS, d), 0.01),
    }
    return inputs


def reference(x_prompt, x_sample, c_prompt, c_sample, cache_k, cache_v, cache_logf, state_gla,
              page_table, g_mix, g_ffn, g_final, w_ada, b_ada, w_in, b_f, w_alpha, b_alpha, g_gla,
              w_branch, w_o, w_router, b_router, w_gu, b_gu, w_d, b_d):
    n_pages = PAST_LEN // PAGE_SIZE
    db = x_sample.shape[0]
    xp = x_prompt
    xs = x_sample
    kp_l, vp_l, fp_l, sp_l = [], [], [], []
    ks_l, vs_l, fs_l, ss_l = [], [], [], []
    for layer in range(DEPTH):
        lw = (g_mix[layer], g_ffn[layer], w_ada[layer], b_ada[layer], w_in[layer], b_f[layer],
              w_alpha[layer], b_alpha[layer], g_gla[layer], w_branch[layer], w_o[layer],
              w_router[layer], b_router[layer], w_gu[layer], b_gu[layer], w_d[layer], b_d[layer])
        xp, kp, vp, fp, sp = decoder_layer(xp, c_prompt, None, lw)
        k_past = cache_k[layer][page_table].reshape(db, n_pages * PAGE_SIZE, H_A, DH_A)
        v_past = cache_v[layer][page_table].reshape(db, n_pages * PAGE_SIZE, H_A, DH_A)
        f_past = cache_logf[layer][page_table].reshape(db, n_pages * PAGE_SIZE, H_A)
        xs, ksm, vsm, fsm, ssm = decoder_layer(xs, c_sample, (k_past, v_past, f_past, state_gla[layer]), lw)
        kp_l.append(kp); vp_l.append(vp); fp_l.append(fp); sp_l.append(sp)
        ks_l.append(ksm); vs_l.append(vsm); fs_l.append(fsm); ss_l.append(ssm)
    y_prompt = rmsnorm(xp, g_final)
    y_sample = rmsnorm(xs, g_final)
    k_prompt = jnp.stack(kp_l, axis=0)
    v_prompt = jnp.stack(vp_l, axis=0)
    logf_prompt = jnp.stack(fp_l, axis=0)
    gla_prompt = jnp.stack(sp_l, axis=0)
    k_sample = jnp.stack(ks_l, axis=0)
    v_sample = jnp.stack(vs_l, axis=0)
    logf_sample = jnp.stack(fs_l, axis=0)
    gla_sample = jnp.stack(ss_l, axis=0)
    return (y_prompt, y_sample, k_prompt, v_prompt, logf_prompt, gla_prompt,
            k_sample, v_sample, logf_sample, gla_sample)
```

```python
import functools

import jax
import jax.numpy as jnp
from jax import lax
from jax.experimental import pallas as pl
from jax.experimental.pallas import tpu as pltpu

f32 = jnp.float32
bf16 = jnp.bfloat16
i32 = jnp.int32

H_A, DH_A = 8, 64
H_B, DK_B, DV_B = 4, 64, 128
GLA_RANK = 16
GLA_TAU = 16.0
N_EXPERTS = 32
TOP_K = 4
SWIGLU_LIMIT = 7.0
SWIGLU_ALPHA = 1.702
RMS_EPS = 1e-6
W_A = H_A * DH_A
WK_B = H_B * DK_B
WV_B = H_B * DV_B

LANES = 128
SUBLANES = 8
VMEM_LIMIT_MB = 56

TOKEN_TILE = 256
ATTN_TILE = 256
GLA_TILE = 256
EXPERT_TILE = 256
PAGES_PER_STEP = 8
DISPATCH_CHUNK = 256

_NT = (((1,), (1,)), ((), ()))
_TN = (((0,), (0,)), ((), ()))


def _cparams(semantics, vmem_mb=VMEM_LIMIT_MB):
    return pltpu.CompilerParams(dimension_semantics=semantics,
                                vmem_limit_bytes=vmem_mb << 20)


def _sigmoid(x):
    return 1.0 / (1.0 + jnp.exp(-x))


def _log_sigmoid(x):
    return jnp.minimum(x, 0.0) - jnp.log1p(jnp.exp(-jnp.abs(x)))


def _split3(x):
    hi = x.astype(bf16)
    r = x - hi.astype(f32)
    mid = r.astype(bf16)
    lo = (r - mid.astype(f32)).astype(bf16)
    return hi, mid, lo


def _dot(a, b):
    return jnp.dot(a, b, preferred_element_type=f32)


def _dotg(a, b, dims):
    return lax.dot_general(a, b, dims, preferred_element_type=f32)


def _rms(x):
    return x * lax.rsqrt(jnp.mean(x * x, axis=-1, keepdims=True) + RMS_EPS)


def _divisor_tile(n, cap, mult):
    best = None
    t = mult
    while t <= min(n, cap):
        if n % t == 0:
            best = t
        t += mult
    assert best is not None, (n, cap, mult)
    return best


def _mod_kernel(c_ref, w_ref, b_ref, o_ref):
    c = c_ref[...]
    s = (c * _sigmoid(c)).astype(bf16)
    o_ref[...] = _dot(s, w_ref[...].astype(bf16)) + b_ref[...]


def _adaln(c_all, w_ada, b_ada):
    r, d = c_all.shape
    n = w_ada.shape[1]
    tn = _divisor_tile(n, 1536, LANES)
    return pl.pallas_call(
        _mod_kernel,
        grid=(n // tn,),
        in_specs=[pl.BlockSpec((r, d), lambda j: (0, 0)),
                  pl.BlockSpec((d, tn), lambda j: (0, j)),
                  pl.BlockSpec((1, tn), lambda j: (0, j))],
        out_specs=pl.BlockSpec((r, tn), lambda j: (0, j)),
        out_shape=jax.ShapeDtypeStruct((r, n), f32),
        compiler_params=_cparams(("arbitrary",)),
        name="adaln_mod",
    )(c_all, w_ada, b_ada.reshape(1, n))


def _inproj_kernel(x_ref, sc_ref, sh_ref, g_ref, wbig_ref, wsm_ref, wfa_ref,
                   qat_ref, kat_ref, vat_ref, ka_ref, va_ref, qb_ref, kb_ref,
                   vb_ref, rb_ref, ga_ref, gb_ref, sm_ref, fat_ref, *, d):
    x = x_ref[0]
    h = (_rms(x) * g_ref[...]) * sc_ref[0] + sh_ref[0]
    hb = h.astype(bf16)
    off = [0]

    def mm(width):
        r = _dot(hb, wbig_ref[:, off[0]:off[0] + width])
        off[0] += width
        return r

    qa = mm(W_A)
    ka = mm(W_A)
    va = mm(W_A)
    ka_ref[0] = ka
    va_ref[0] = va
    for hh in range(H_A):
        sl = slice(hh * DH_A, (hh + 1) * DH_A)
        qat_ref[0, hh] = qa[:, sl].astype(bf16)
        kat_ref[0, hh] = ka[:, sl].astype(bf16)
        vat_ref[0, hh] = va[:, sl].astype(bf16)
    qb_ref[0] = mm(WK_B)
    kb_ref[0] = mm(WK_B)
    vb_ref[0] = mm(WV_B)
    rb_ref[0] = mm(WV_B)
    ga_ref[0] = mm(d)
    gb_ref[0] = mm(d)
    sm_ref[0] = _dot(hb, wsm_ref[...])
    fat_ref[0] = _dotg(wfa_ref[...].astype(bf16), hb, _NT)


def _inproj_weights(w_in, d):
    sizes = (W_A, W_A, W_A, H_A, WK_B, WK_B, WV_B, WV_B, GLA_RANK, d, d)
    offs = [0]
    for s in sizes:
        offs.append(offs[-1] + s)
    seg = lambda i: w_in[:, offs[i]:offs[i + 1]]
    wbig = jnp.concatenate(
        [seg(0) * (DH_A ** -0.5), seg(1), seg(2), seg(4) * (DK_B ** -0.5), seg(5),
         seg(6), seg(7), seg(9), seg(10)], axis=1).astype(bf16)
    wsm = jnp.concatenate(
        [seg(3), seg(8), jnp.zeros((d, LANES - H_A - GLA_RANK), f32)], axis=1).astype(bf16)
    wfa = seg(3).T
    return wbig, wsm, wfa


def _inproj(x, sc, sh, g_mix, wbig, wsm, wfa):
    b, t, d = x.shape
    tm = min(TOKEN_TILE, t)
    assert t % tm == 0
    rows = sc.shape[1]
    mod_block = (1, 1, d) if rows == 1 else (1, tm, d)
    mod_map = (lambda bi, i: (bi, 0, 0)) if rows == 1 else (lambda bi, i: (bi, i, 0))
    tok3 = lambda w: pl.BlockSpec((1, tm, w), lambda bi, i: (bi, i, 0))
    head4 = pl.BlockSpec((1, H_A, tm, DH_A), lambda bi, i: (bi, 0, i, 0))
    const2 = lambda a: pl.BlockSpec(a.shape, lambda bi, i: (0, 0))
    sds = jax.ShapeDtypeStruct
    out_shape = (
        sds((b, H_A, t, DH_A), bf16), sds((b, H_A, t, DH_A), bf16), sds((b, H_A, t, DH_A), bf16),
        sds((b, t, W_A), f32), sds((b, t, W_A), f32),
        sds((b, t, WK_B), f32), sds((b, t, WK_B), f32),
        sds((b, t, WV_B), f32), sds((b, t, WV_B), f32),
        sds((b, t, d), f32), sds((b, t, d), f32),
        sds((b, t, LANES), f32), sds((b, H_A, t), f32))
    out_specs = (head4, head4, head4, tok3(W_A), tok3(W_A), tok3(WK_B), tok3(WK_B),
                 tok3(WV_B), tok3(WV_B), tok3(d), tok3(d), tok3(LANES),
                 pl.BlockSpec((1, H_A, tm), lambda bi, i: (bi, 0, i)))
    g2 = g_mix.reshape(1, d)
    return pl.pallas_call(
        functools.partial(_inproj_kernel, d=d),
        grid=(b, t // tm),
        in_specs=[tok3(d), pl.BlockSpec(mod_block, mod_map), pl.BlockSpec(mod_block, mod_map),
                  const2(g2), const2(wbig), const2(wsm), const2(wfa)],
        out_specs=out_specs,
        out_shape=out_shape,
        compiler_params=_cparams(("parallel", "arbitrary")),
        name="inproj",
    )(x, sc, sh, g2, wbig, wsm, wfa)


def _logf_kernel(fa_ref, bf_ref, logf_ref, *cum_refs, t, cb):
    lf = _log_sigmoid(fa_ref[...] + bf_ref[...])
    logf_ref[...] = lf
    if not cum_refs:
        return
    cum_ref, = cum_refs
    r = lax.broadcasted_iota(i32, (cb, cb), 0)
    c = lax.broadcasted_iota(i32, (cb, cb), 1)
    triu = jnp.where(r <= c, 1.0, 0.0).astype(bf16)
    carry = jnp.zeros((lf.shape[0], 1), f32)
    for j in range(t // cb):
        hi, mid, lo = _split3(lf[:, j * cb:(j + 1) * cb])
        blk = _dot(hi, triu) + _dot(mid, triu) + _dot(lo, triu) + carry
        cum_ref[:, j * cb:(j + 1) * cb] = blk
        carry = blk[:, cb - 1:cb]


def _logf(fat, b_f, with_cumsum):
    b, _, t = fat.shape
    rows = b * H_A
    fa2 = fat.reshape(rows, t)
    bf2 = jnp.tile(b_f.reshape(H_A, 1), (b, 1))
    cb = min(256, t)
    full = lambda a: pl.BlockSpec(a.shape, lambda: (0,) * a.ndim)
    n_out = 2 if with_cumsum else 1
    outs = pl.pallas_call(
        functools.partial(_logf_kernel, t=t, cb=cb),
        in_specs=[full(fa2), full(bf2)],
        out_specs=tuple(pl.BlockSpec((rows, t), lambda: (0, 0)) for _ in range(n_out)),
        out_shape=tuple(jax.ShapeDtypeStruct((rows, t), f32) for _ in range(n_out)),
        name="log_forget",
    )(fa2, bf2)
    return tuple(o.reshape(b, H_A, t) for o in outs)


def _fox_kernel(q_ref, k_ref, v_ref, cum_ref, cumt_ref, o_ref, *, tq):
    i = pl.program_id(1)
    row = lax.broadcasted_iota(i32, (tq, tq), 0)
    col = lax.broadcasted_iota(i32, (tq, tq), 1)
    cum = cum_ref[0]
    for h in range(H_A):
        q = q_ref[0, h]
        cq = cum[:, h:h + 1]

        def step(j, carry, masked, q=q, cq=cq, h=h):
            m, l, acc = carry
            start = pl.multiple_of(j * tq, tq)
            kk = k_ref[0, h, pl.ds(start, tq), :]
            vv = v_ref[0, h, pl.ds(start, tq), :]
            ck = cumt_ref[0, h:h + 1, pl.ds(start, tq)]
            s = _dotg(q, kk, _NT) + (cq - ck)
            if masked:
                s = jnp.where(col <= row, s, -jnp.inf)
            m_new = jnp.maximum(m, jnp.max(s, axis=-1, keepdims=True))
            alpha = jnp.exp(m - m_new)
            p = jnp.exp(s - m_new)
            l = alpha * l + jnp.sum(p, axis=-1, keepdims=True)
            acc = alpha * acc + _dot(p.astype(bf16), vv)
            return m_new, l, acc

        init = (jnp.full((tq, 1), -jnp.inf, f32), jnp.zeros((tq, 1), f32),
                jnp.zeros((tq, DH_A), f32))
        carry = lax.fori_loop(0, i, functools.partial(step, masked=False), init)
        _, l, acc = step(i, carry, True)
        o_ref[0, :, h * DH_A:(h + 1) * DH_A] = (acc / l).astype(bf16)


def _fox_prompt(qat, kat, vat, cum, cumt):
    b, _, t, _ = qat.shape
    tq = min(ATTN_TILE, t)
    assert t % tq == 0
    seq4 = pl.BlockSpec((1, H_A, t, DH_A), lambda bi, i: (bi, 0, 0, 0))
    return pl.pallas_call(
        functools.partial(_fox_kernel, tq=tq),
        grid=(b, t // tq),
        in_specs=[pl.BlockSpec((1, H_A, tq, DH_A), lambda bi, i: (bi, 0, i, 0)), seq4, seq4,
                  pl.BlockSpec((1, tq, H_A), lambda bi, i: (bi, i, 0)),
                  pl.BlockSpec((1, H_A, t), lambda bi, i: (bi, 0, 0))],
        out_specs=pl.BlockSpec((1, tq, W_A), lambda bi, i: (bi, i, 0)),
        out_shape=jax.ShapeDtypeStruct((b, t, W_A), bf16),
        compiler_params=_cparams(("parallel", "arbitrary")),
        name="fox_prompt",
    )(qat, kat, vat, cum, cumt)


def _fox_decode_kernel(pt_ref, qt_ref, knew_ref, vnew_ref, lfnew_ref, *refs, pg, nstep):
    del pt_ref
    k_refs, v_refs, lf_refs = refs[:pg], refs[pg:2 * pg], refs[2 * pg:3 * pg]
    o_ref = refs[3 * pg]
    p_scr, m_scr, self_scr, acc_scr, carry_scr = refs[3 * pg + 1:]
    j = pl.program_id(1)
    qt = qt_ref[0]
    lane = lax.broadcasted_iota(i32, (H_A, W_A), 1)
    sub = lax.broadcasted_iota(i32, (H_A, W_A), 0)
    own = (lane // DH_A) == sub
    psz = lf_refs[0].shape[1]

    @pl.when(j == 0)
    def _():
        s_self = jnp.sum(qt.astype(f32) * knew_ref[0].astype(bf16).astype(f32), axis=1,
                         keepdims=True)
        self_scr[...] = s_self
        m_scr[...] = s_self
        carry_scr[...] = lfnew_ref[0]

    @pl.when(j < nstep)
    def _():
        r = lax.broadcasted_iota(i32, (psz, 2 * psz), 0)
        c = lax.broadcasted_iota(i32, (psz, 2 * psz), 1)
        later = jnp.where(c < psz, jnp.where(r > c, 1.0, 0.0), 1.0).astype(bf16)
        group = nstep - 1 - j
        carry = carry_scr[...]
        m = m_scr[...]
        for p in reversed(range(pg)):
            hi, mid, lo = _split3(lf_refs[p][0])
            suf = _dotg(hi, later, _TN) + _dotg(mid, later, _TN) + _dotg(lo, later, _TN)
            s = _dotg(qt, k_refs[p][0].astype(bf16), _NT) + (carry + suf[:, :psz])
            carry = carry + suf[:, psz:psz + 1]
            p_scr[group * pg + p] = s
            m = jnp.maximum(m, jnp.max(s, axis=1, keepdims=True))
        carry_scr[...] = carry
        m_scr[...] = m

    @pl.when(j == nstep)
    def _():
        m = m_scr[...]
        e_self = jnp.exp(self_scr[...] - m)

        def expo(i, l):
            e = jnp.exp(p_scr[i] - m)
            p_scr[i] = e
            return l + jnp.sum(e, axis=1, keepdims=True)

        l = lax.fori_loop(0, nstep * pg, expo, e_self)

        def norm(i, carry):
            p_scr[i] = p_scr[i] / l
            return carry

        lax.fori_loop(0, nstep * pg, norm, 0)
        rnd = lambda z: z.astype(bf16).astype(f32)
        acc_scr[...] = jnp.where(own, rnd(e_self / l) * rnd(vnew_ref[0]), 0.0)

    @pl.when(j >= nstep)
    def _():
        group = 2 * nstep - 1 - j
        acc = acc_scr[...]
        for p in range(pg):
            acc = acc + _dot(p_scr[group * pg + p].astype(bf16), v_refs[p][0].astype(bf16))
        acc_scr[...] = acc

    @pl.when(j == 2 * nstep - 1)
    def _():
        o_ref[0] = jnp.sum(jnp.where(own, acc_scr[...], 0.0), axis=0, keepdims=True)


def _fox_decode(qt, knew, vnew, lfnew, cache_k, cache_v, cache_logf, page_table):
    db = qt.shape[0]
    n_phys, psz = cache_k.shape[0], cache_k.shape[1]
    n_pages = page_table.shape[1]
    pg = PAGES_PER_STEP if n_pages % PAGES_PER_STEP == 0 else 1
    nstep = n_pages // pg
    ck = cache_k.reshape(n_phys, psz, W_A)
    cv = cache_v.reshape(n_phys, psz, W_A)
    pt = page_table.reshape(-1).astype(i32)

    def page_spec(p, width, score_phase):
        def imap(bi, j, pt_ref):
            step = jnp.minimum(j, nstep - 1) if score_phase else jnp.maximum(j - nstep, 0)
            return (pt_ref[bi * n_pages + (nstep - 1 - step) * pg + p], 0, 0)
        return pl.BlockSpec((1, psz, width), imap)

    row3 = lambda a: pl.BlockSpec((1,) + a.shape[1:], lambda bi, j, pt_ref: (bi, 0, 0))
    grid_spec = pltpu.PrefetchScalarGridSpec(
        num_scalar_prefetch=1,
        grid=(db, 2 * nstep),
        in_specs=[row3(qt), row3(knew), row3(vnew), row3(lfnew)]
        + [page_spec(p, W_A, True) for p in range(pg)]
        + [page_spec(p, W_A, False) for p in range(pg)]
        + [page_spec(p, H_A, True) for p in range(pg)],
        out_specs=pl.BlockSpec((1, 1, W_A), lambda bi, j, pt_ref: (bi, 0, 0)),
        scratch_shapes=[pltpu.VMEM((n_pages, H_A, psz), f32), pltpu.VMEM((H_A, 1), f32),
                        pltpu.VMEM((H_A, 1), f32), pltpu.VMEM((H_A, W_A), f32),
                        pltpu.VMEM((H_A, 1), f32)])
    return pl.pallas_call(
        functools.partial(_fox_decode_kernel, pg=pg, nstep=nstep),
        grid_spec=grid_spec,
        out_shape=jax.ShapeDtypeStruct((db, 1, W_A), f32),
        compiler_params=_cparams(("parallel", "arbitrary")),
        name="fox_decode",
    )(pt, qt, knew, vnew, lfnew, *([ck] * pg), *([cv] * pg), *([cache_logf] * pg))


def _gla_kernel(qb_ref, kb_ref, vb_ref, rb_ref, sm_ref, wal_ref, bal_ref, gg_ref,
                ob_ref, sout_ref, s_scr, *, tt):
    i = pl.program_id(1)

    @pl.when(i == 0)
    def _():
        s_scr[...] = jnp.zeros_like(s_scr)

    q = qb_ref[0]
    k = kb_ref[0]
    z = _dot(sm_ref[0].astype(bf16), wal_ref[...]) + bal_ref[...]
    a = _log_sigmoid(z) * (1.0 / GLA_TAU)

    row = lax.broadcasted_iota(i32, (tt, tt), 0)
    col = lax.broadcasted_iota(i32, (tt, tt), 1)
    tril = jnp.where(col <= row, 1.0, 0.0).astype(bf16)
    hi, mid, lo = _split3(a)
    b = _dot(tril, hi) + _dot(tril, mid) + _dot(tril, lo)

    rowk = lax.broadcasted_iota(i32, (tt, WK_B), 0)
    hs = [slice(h * DK_B, (h + 1) * DK_B) for h in range(H_B)]
    amat = [jnp.zeros((tt, tt), f32) for _ in range(H_B)]
    edge = b
    for lvl in range(tt.bit_length() - 1):
        half = 1 << lvl
        upper = ((rowk >> lvl) & 1) == 1
        kt = jnp.where(upper, 0.0, k * jnp.exp(edge - b)).astype(bf16)
        edge_q = pltpu.roll(edge, half, 0)
        qt = jnp.where(upper, q * jnp.exp(b - edge_q), 0.0).astype(bf16)
        same = (row >> (lvl + 1)) == (col >> (lvl + 1))
        for h in range(H_B):
            amat[h] = amat[h] + jnp.where(same, _dotg(qt[:, hs[h]], kt[:, hs[h]], _NT), 0.0)
        edge = jnp.where(upper, edge, pltpu.roll(edge, tt - half, 0))
    q16 = q.astype(bf16)
    k16 = k.astype(bf16)
    for h in range(H_B):
        amat[h] = amat[h] + jnp.where(row == col, _dotg(q16[:, hs[h]], k16[:, hs[h]], _NT), 0.0)

    qe = (q * jnp.exp(b)).astype(bf16)
    ke = (k * jnp.exp(edge - b)).astype(bf16)
    e_last = jnp.exp(edge[0:1, :])
    v16 = vb_ref[0].astype(bf16)
    r = rb_ref[0]
    er = lax.broadcasted_iota(i32, (DK_B, DK_B), 0)
    ec = lax.broadcasted_iota(i32, (DK_B, DK_B), 1)
    for h in range(H_B):
        vs = slice(h * DV_B, (h + 1) * DV_B)
        vh = v16[:, vs]
        state = s_scr[h]
        o = _dot(amat[h].astype(bf16), vh) + _dot(qe[:, hs[h]], state.astype(bf16))
        decay_col = jnp.sum(
            jnp.where(er == ec, jnp.broadcast_to(e_last[:, hs[h]], (DK_B, DK_B)), 0.0),
            axis=1, keepdims=True)
        s_scr[h] = decay_col * state + _dotg(ke[:, hs[h]], vh, _TN)
        rh = r[:, vs]
        ob_ref[0, :, vs] = (_rms(o) * gg_ref[:, vs] * (rh * _sigmoid(rh))).astype(bf16)

    @pl.when(i == pl.num_programs(1) - 1)
    def _():
        sout_ref[0] = s_scr[...]


def _gla_prompt(qb, kb, vb, rb, sm, wal, bal, gg):
    b, t, _ = qb.shape
    tt = min(GLA_TILE, t)
    assert t % tt == 0 and tt & (tt - 1) == 0
    tok3 = lambda w: pl.BlockSpec((1, tt, w), lambda bi, i: (bi, i, 0))
    const2 = lambda a: pl.BlockSpec(a.shape, lambda bi, i: (0, 0))
    return pl.pallas_call(
        functools.partial(_gla_kernel, tt=tt),
        grid=(b, t // tt),
        in_specs=[tok3(WK_B), tok3(WK_B), tok3(WV_B), tok3(WV_B), tok3(LANES),
                  const2(wal), const2(bal), const2(gg)],
        out_specs=(tok3(WV_B),
                   pl.BlockSpec((1, H_B, DK_B, DV_B), lambda bi, i: (bi, 0, 0, 0))),
        out_shape=(jax.ShapeDtypeStruct((b, t, WV_B), bf16),
                   jax.ShapeDtypeStruct((b, H_B, DK_B, DV_B), f32)),
        scratch_shapes=[pltpu.VMEM((H_B, DK_B, DV_B), f32)],
        compiler_params=_cparams(("parallel", "arbitrary")),
        name="gla_prompt",
    )(qb, kb, vb, rb, sm, wal, bal, gg)


def _loga_kernel(sm_ref, wal_ref, bal_ref, o_ref):
    z = _dot(sm_ref[...].astype(bf16), wal_ref[...]) + bal_ref[...]
    o_ref[...] = _log_sigmoid(z) * (1.0 / GLA_TAU)


def _gla_decode_kernel(q_ref, k_ref, a_ref, v_ref, r_ref, s_ref, gg_ref, o_ref, so_ref):
    for h in range(H_B):
        q, k, a = q_ref[0, h], k_ref[0, h], a_ref[0, h]
        v = v_ref[0, h]
        state = s_ref[0, h]
        ea = jnp.exp(a)
        rnd = lambda z: z.astype(bf16).astype(f32)
        qk = jnp.sum(q * k, axis=0, keepdims=True)
        o = qk * v + jnp.sum(rnd(q * ea) * rnd(state), axis=0, keepdims=True)
        so_ref[0, h] = ea * state + k * v
        r = r_ref[0, h]
        o_ref[0, h] = _rms(o) * gg_ref[h] * (r * _sigmoid(r))


def _gla_decode(qb, kb, vb, rb, sm, wal, bal, gg, state):
    db = qb.shape[0]
    full = lambda a: pl.BlockSpec(a.shape, lambda: (0,) * a.ndim)
    loga = pl.pallas_call(
        _loga_kernel,
        in_specs=[full(sm), full(wal), full(bal)],
        out_specs=pl.BlockSpec((db, WK_B), lambda: (0, 0)),
        out_shape=jax.ShapeDtypeStruct((db, WK_B), f32),
        name="gla_log_decay",
    )(sm, wal, bal)
    col = lambda a: a.reshape(db, H_B, DK_B, 1)
    rowv = lambda a: a.reshape(db, H_B, 1, DV_B)
    gg4 = gg.reshape(H_B, 1, DV_B)
    b4 = lambda shp: pl.BlockSpec((1,) + shp, lambda bi: (bi, 0, 0, 0))
    o, s_new = pl.pallas_call(
        _gla_decode_kernel,
        grid=(db,),
        in_specs=[b4((H_B, DK_B, 1))] * 3 + [b4((H_B, 1, DV_B))] * 2
        + [b4((H_B, DK_B, DV_B)), pl.BlockSpec(gg4.shape, lambda bi: (0, 0, 0))],
        out_specs=(b4((H_B, 1, DV_B)), b4((H_B, DK_B, DV_B))),
        out_shape=(jax.ShapeDtypeStruct((db, H_B, 1, DV_B), f32),
                   jax.ShapeDtypeStruct((db, H_B, DK_B, DV_B), f32)),
        compiler_params=_cparams(("parallel",)),
        name="gla_decode",
    )(col(qb), col(kb), col(loga), rowv(vb), rowv(rb), state, gg4)
    return o.reshape(db, WV_B), s_new


def _mix_kernel(x_ref, oa_ref, ob_ref, ga_ref, gb_ref, gt1_ref, sc2_ref, sh2_ref, gf_ref,
                wba_ref, wbb_ref, wo_ref, wr_ref, br_ref, x1_ref, h2_ref, lg_ref, *, nsub):
    x = x_ref[0]
    tm = x.shape[0]
    ya = _dot(oa_ref[0].astype(bf16), wba_ref[...])
    yb = _dot(ob_ref[0].astype(bf16), wbb_ref[...])
    m = _sigmoid(ga_ref[0]) * ya + _sigmoid(gb_ref[0]) * yb
    x1 = x + gt1_ref[0] * _dot(m.astype(bf16), wo_ref[...])
    x1_ref[...] = x1
    h2 = (_rms(x1) * gf_ref[...]) * sc2_ref[0] + sh2_ref[0]
    for s in range(nsub):
        h2_ref[pl.ds(s, tm, stride=nsub), :] = h2[:, s * LANES:(s + 1) * LANES]
    lg_ref[...] = _dot(h2.astype(bf16), wr_ref[...]) + br_ref[...]


def _mix(x, oa, ob, ga, gb, gt1, sc2, sh2, g_ffn, wba, wbb, wo, wr, br):
    b, t, d = x.shape
    nt = b * t
    nsub = d // LANES
    tm = min(TOKEN_TILE, t)
    assert t % tm == 0
    rows = gt1.shape[1]
    mod_block = (1, 1, d) if rows == 1 else (1, tm, d)
    mod_map = (lambda bi, i: (bi, 0, 0)) if rows == 1 else (lambda bi, i: (bi, i, 0))
    mod = pl.BlockSpec(mod_block, mod_map)
    tok3 = lambda w: pl.BlockSpec((1, tm, w), lambda bi, i: (bi, i, 0))
    const2 = lambda a: pl.BlockSpec(a.shape, lambda bi, i: (0, 0))
    nti = t // tm
    blk = lambda bi, i: (bi * nti + i, 0)
    g2 = g_ffn.reshape(1, d)
    ins = [x, oa, ob, ga, gb, gt1, sc2, sh2, g2, wba, wbb, wo, wr, br]
    in_specs = [tok3(d), tok3(W_A), tok3(WV_B), tok3(d), tok3(d), mod, mod, mod,
                const2(g2), const2(wba), const2(wbb), const2(wo), const2(wr), const2(br)]
    return pl.pallas_call(
        functools.partial(_mix_kernel, nsub=nsub),
        grid=(b, nti),
        in_specs=in_specs,
        out_specs=(pl.BlockSpec((tm, d), blk), pl.BlockSpec((tm * nsub, LANES), blk),
                   pl.BlockSpec((tm, LANES), blk)),
        out_shape=(jax.ShapeDtypeStruct((nt, d), f32),
                   jax.ShapeDtypeStruct((nt * nsub, LANES), f32),
                   jax.ShapeDtypeStruct((nt, LANES), f32)),
        compiler_params=_cparams(("parallel", "arbitrary")),
        name="branch_mix",
    )(*ins)


def _route_kernel(lg_ref, cin_ref, rec_ref, gate_ref, cnt_ref, carry_scr, *, tr):
    @pl.when(pl.program_id(0) == 0)
    def _():
        carry_scr[...] = cin_ref[...]

    lane = lax.broadcasted_iota(i32, (tr, LANES), 1)
    lanef = lane.astype(f32)
    v = jnp.where(lane < N_EXPERTS, lg_ref[...], -jnp.inf)
    onehot = jnp.zeros((tr, LANES), f32)
    ids, vals = [], []
    for _ in range(TOP_K):
        mx = jnp.max(v, axis=1, keepdims=True)
        idx = jnp.min(jnp.where(v == mx, lanef, float(LANES)), axis=1, keepdims=True)
        sel = lanef == idx
        onehot = jnp.where(sel, 1.0, onehot)
        v = jnp.where(sel, -jnp.inf, v)
        ids.append(idx)
        vals.append(mx)
    es = [jnp.exp(vk - vals[0]) for vk in vals]
    tot = es[0]
    for e in es[1:]:
        tot = tot + e

    r2 = lax.broadcasted_iota(i32, (tr, tr), 0)
    c2 = lax.broadcasted_iota(i32, (tr, tr), 1)
    before = jnp.where(c2 < r2, 1.0, 0.0).astype(bf16)
    pref = _dot(before, onehot.astype(bf16)) + carry_scr[...]
    carry_scr[...] = carry_scr[...] + jnp.sum(onehot, axis=0, keepdims=True)

    rec = jnp.zeros((tr, LANES), f32)
    gate = jnp.zeros((tr, LANES), f32)
    for k in range(TOP_K):
        rank = jnp.sum(jnp.where(lanef == ids[k], pref, 0.0), axis=1, keepdims=True)
        rec = jnp.where(lane == k, ids[k], rec)
        rec = jnp.where(lane == TOP_K + k, rank, rec)
        gate = jnp.where(lane == k, es[k] / tot, gate)
    rec_ref[...] = rec.astype(i32)
    gate_ref[...] = gate
    cnt_ref[...] = carry_scr[...]


def _route(logits, counts_in):
    nt = logits.shape[0]
    tr = _divisor_tile(nt, 512, SUBLANES)
    tile = pl.BlockSpec((tr, LANES), lambda i: (i, 0))
    cnt = pl.BlockSpec((1, LANES), lambda i: (0, 0))
    return pl.pallas_call(
        functools.partial(_route_kernel, tr=tr),
        grid=(nt // tr,),
        in_specs=[tile, cnt],
        out_specs=(tile, tile, cnt),
        out_shape=(jax.ShapeDtypeStruct((nt, LANES), i32),
                   jax.ShapeDtypeStruct((nt, LANES), f32),
                   jax.ShapeDtypeStruct((1, LANES), f32)),
        scratch_shapes=[pltpu.VMEM((1, LANES), f32)],
        compiler_params=_cparams(("arbitrary",)),
        name="route_topk",
    )(logits, counts_in)


def _dispatch_kernel(dest_ref, h_hbm, xin_hbm, xout_hbm, sem, *, nsub, ch):
    del xin_hbm
    base = pl.program_id(0) * ch

    def row_copy(n, k):
        src = pl.multiple_of((base + n) * nsub, nsub)
        dst = pl.multiple_of(dest_ref[(base + n) * TOP_K + k] * nsub, nsub)
        return pltpu.make_async_copy(h_hbm.at[pl.ds(src, nsub)], xout_hbm.at[pl.ds(dst, nsub)],
                                     sem)

    def for_rows(fn):
        def body(n, carry):
            for k in range(TOP_K):
                fn(row_copy(n, k))
            return carry
        lax.fori_loop(0, ch, body, 0)

    for_rows(lambda cp: cp.start())
    for_rows(lambda cp: cp.wait())


def _dispatch(dest_flat, h2rows, xrows, nsub):
    n_tok = dest_flat.shape[0] // TOP_K
    ch = min(DISPATCH_CHUNK, n_tok)
    assert n_tok % ch == 0
    grid_spec = pltpu.PrefetchScalarGridSpec(
        num_scalar_prefetch=1,
        grid=(n_tok // ch,),
        in_specs=[pl.BlockSpec(memory_space=pl.ANY), pl.BlockSpec(memory_space=pl.ANY)],
        out_specs=pl.BlockSpec(memory_space=pl.ANY),
        scratch_shapes=[pltpu.SemaphoreType.DMA(())])
    return pl.pallas_call(
        functools.partial(_dispatch_kernel, nsub=nsub, ch=ch),
        grid_spec=grid_spec,
        out_shape=jax.ShapeDtypeStruct(xrows.shape, xrows.dtype),
        input_output_aliases={2: 0},
        compiler_params=_cparams(("arbitrary",)),
        name="moe_dispatch",
    )(dest_flat, h2rows, xrows)


def _expert_kernel(be_ref, nused_ref, x_ref, wgu_ref, bgu_ref, wd_ref, bd_ref, y_ref,
                   *, tmx, nsub, dff):
    del be_ref
    live = pl.program_id(0) < nused_ref[0]

    @pl.when(jnp.logical_not(live))
    def _():
        y_ref[...] = jnp.zeros_like(y_ref)

    @pl.when(live)
    def _():
        x = jnp.concatenate(
            [x_ref[pl.ds(s, tmx, stride=nsub), :].astype(bf16) for s in range(nsub)], axis=1)
        gu = _dot(x, wgu_ref[0]) + bgu_ref[0]
        gate = jnp.minimum(gu[:, :dff], SWIGLU_LIMIT)
        up = jnp.clip(gu[:, dff:], -SWIGLU_LIMIT, SWIGLU_LIMIT)
        glu = gate * _sigmoid(SWIGLU_ALPHA * gate)
        y = _dot(((up + 1.0) * glu).astype(bf16), wd_ref[0]) + bd_ref[0]
        for s in range(nsub):
            y_ref[pl.ds(s, tmx, stride=nsub), :] = y[:, s * LANES:(s + 1) * LANES]


def _experts(block_e, nused, xrows, wgu, bgu, wd, bd, nsub):
    n_blocks = block_e.shape[0]
    tmx = EXPERT_TILE
    e, d, dff2 = wgu.shape
    dff = dff2 // 2
    live = lambda r, nu: jnp.minimum(r, nu[0] - 1)
    rows = pl.BlockSpec((tmx * nsub, LANES), lambda r, be, nu: (live(r, nu), 0))
    per_e = lambda shp: pl.BlockSpec((1,) + shp, lambda r, be, nu: (be[live(r, nu)], 0, 0))
    grid_spec = pltpu.PrefetchScalarGridSpec(
        num_scalar_prefetch=2,
        grid=(n_blocks,),
        in_specs=[rows, per_e((d, dff2)), per_e((1, dff2)), per_e((dff, d)), per_e((1, d))],
        out_specs=pl.BlockSpec((tmx * nsub, LANES), lambda r, be, nu: (r, 0)))
    return pl.pallas_call(
        functools.partial(_expert_kernel, tmx=tmx, nsub=nsub, dff=dff),
        grid_spec=grid_spec,
        out_shape=jax.ShapeDtypeStruct(xrows.shape, f32),
        compiler_params=_cparams(("arbitrary",)),
        name="moe_experts",
    )(block_e, nused, xrows, wgu, bgu.reshape(e, 1, dff2), wd, bd.reshape(e, 1, d))


def _combine_kernel(dest_ref, y_hbm, gate_ref, x1_ref, gt2_ref, gfin_ref, out_ref, buf, sem,
                    *, tc, nsub):
    base = pl.program_id(0) * tc

    def row_copy(n, k):
        src = pl.multiple_of(dest_ref[(base + n) * TOP_K + k] * nsub, nsub)
        dst = pl.multiple_of((k * tc + n) * nsub, nsub)
        return pltpu.make_async_copy(y_hbm.at[pl.ds(src, nsub)], buf.at[pl.ds(dst, nsub)], sem)

    def for_rows(fn):
        def body(n, carry):
            for k in range(TOP_K):
                fn(row_copy(n, k))
            return carry
        lax.fori_loop(0, tc, body, 0)

    for_rows(lambda cp: cp.start())
    for_rows(lambda cp: cp.wait())

    g = gate_ref[...]
    cols = []
    for s in range(nsub):
        acc = None
        for k in range(TOP_K):
            term = g[:, k:k + 1] * buf[pl.ds(k * tc * nsub + s, tc, stride=nsub), :]
            acc = term if acc is None else acc + term
        cols.append(acc)
    y = jnp.concatenate(cols, axis=1)
    out_ref[...] = _rms(x1_ref[...] + gt2_ref[0] * y) * gfin_ref[...]


def _combine(dest_flat, yrows, gates, x1, gt2, g_final, t_per_mod, nsub):
    n_tok, d = x1.shape
    tc = min(TOKEN_TILE, n_tok)
    assert n_tok % tc == 0
    rows = gt2.shape[1]
    if rows == 1:
        per_mod = t_per_mod // tc
        mod = pl.BlockSpec((1, 1, d), lambda i, dr: (i // per_mod, 0, 0))
    else:
        mod = pl.BlockSpec((1, tc, d), lambda i, dr: (0, i, 0))
    blk = lambda i, dr: (i, 0)
    gfin = g_final.reshape(1, d)
    grid_spec = pltpu.PrefetchScalarGridSpec(
        num_scalar_prefetch=1,
        grid=(n_tok // tc,),
        in_specs=[pl.BlockSpec(memory_space=pl.ANY), pl.BlockSpec((tc, LANES), blk),
                  pl.BlockSpec((tc, d), blk), mod, pl.BlockSpec((1, d), lambda i, dr: (0, 0))],
        out_specs=pl.BlockSpec((tc, d), lambda i, dr: (i, 0)),
        scratch_shapes=[pltpu.VMEM((TOP_K * tc * nsub, LANES), f32), pltpu.SemaphoreType.DMA(())])
    return pl.pallas_call(
        functools.partial(_combine_kernel, tc=tc, nsub=nsub),
        grid_spec=grid_spec,
        out_shape=jax.ShapeDtypeStruct((n_tok, d), f32),
        compiler_params=_cparams(("arbitrary",)),
        name="moe_combine",
    )(dest_flat, yrows, gates, x1, gt2, gfin)


def kernel(x_prompt, x_sample, c_prompt, c_sample, cache_k, cache_v, cache_logf, state_gla,
           page_table, g_mix, g_ffn, g_final, w_ada, b_ada, w_in, b_f, w_alpha, b_alpha, g_gla,
           w_branch, w_o, w_router, b_router, w_gu, b_gu, w_d, b_d):
    depth = g_mix.shape[0]
    assert depth == 1, "one decoder layer"
    bp, t, d = x_prompt.shape
    db = x_sample.shape[0]
    assert x_sample.shape[1] == 1
    nsub = d // LANES
    ntp = bp * t
    nt = ntp + db

    nmod = bp + db
    pad = (-nmod) % SUBLANES
    c_all = jnp.concatenate([c_prompt, c_sample, jnp.zeros((pad, d), f32)], axis=0)
    mod = _adaln(c_all, w_ada[0], b_ada[0])
    sh1, sc1, gt1, sh2, sc2, gt2 = [mod[:, i * d:(i + 1) * d] for i in range(6)]
    grp_p = lambda a: a[:bp].reshape(bp, 1, d)
    grp_s = lambda a: a[bp:nmod].reshape(1, db, d)

    wbig, wsm, wfa = _inproj_weights(w_in[0], d)
    wal = jnp.zeros((LANES, WK_B), f32).at[H_A:H_A + GLA_RANK].set(w_alpha[0]).astype(bf16)
    bal = b_alpha[0].reshape(1, WK_B)
    gg = g_gla[0].reshape(1, WV_B)
    wba = w_branch[0, :W_A].astype(bf16)
    wbb = w_branch[0, W_A:].astype(bf16)
    wo = w_o[0].astype(bf16)
    wr = jnp.zeros((d, LANES), f32).at[:, :N_EXPERTS].set(w_router[0]).astype(bf16)
    br = jnp.zeros((1, LANES), f32).at[0, :N_EXPERTS].set(b_router[0])

    (qat, kat, vat, ka, va, qb, kb, vb, rb, ga, gb, sm, fat) = _inproj(
        x_prompt, grp_p(1.0 + sc1), grp_p(sh1), g_mix[0], wbig, wsm, wfa)
    logft, cumt = _logf(fat, b_f[0], True)
    cum = jnp.transpose(cumt, (0, 2, 1))
    oa = _fox_prompt(qat, kat, vat, cum, cumt)
    ob, gla_p = _gla_prompt(qb, kb, vb, rb, sm, wal, bal, gg)
    x1_p, h2_p, lg_p = _mix(x_prompt, oa, ob, ga, gb, grp_p(gt1), grp_p(1.0 + sc2), grp_p(sh2),
                            g_ffn[0], wba, wbb, wo, wr, br)

    xs = x_sample.reshape(1, db, d)
    (qat_s, _, _, ka_s, va_s, qb_s, kb_s, vb_s, rb_s, ga_s, gb_s, sm_s, fat_s) = _inproj(
        xs, grp_s(1.0 + sc1), grp_s(sh1), g_mix[0], wbig, wsm, wfa)
    logft_s, = _logf(fat_s, b_f[0], False)
    eye = jnp.eye(H_A, dtype=bf16)
    qt = jnp.einsum('hnd,hc->nchd', qat_s[0], eye).reshape(db, H_A, W_A)
    lfnew = jnp.transpose(logft_s[0], (1, 0)).reshape(db, H_A, 1)
    oa_s = _fox_decode(qt, ka_s.reshape(db, 1, W_A), va_s.reshape(db, 1, W_A), lfnew,
                       cache_k[0], cache_v[0], cache_logf[0], page_table)
    ob_s, gla_s = _gla_decode(qb_s[0], kb_s[0], vb_s[0], rb_s[0], sm_s[0], wal, bal, gg,
                              state_gla[0])
    x1_s, h2_s, lg_s = _mix(xs, oa_s.reshape(1, db, W_A), ob_s.reshape(1, db, WV_B), ga_s, gb_s,
                            grp_s(gt1), grp_s(1.0 + sc2), grp_s(sh2), g_ffn[0],
                            wba, wbb, wo, wr, br)

    rec_p, gates_p, cnt_p = _route(lg_p, jnp.zeros((1, LANES), f32))
    rec_s, gates_s, cnt = _route(lg_s, cnt_p)
    counts = cnt[0, :N_EXPERTS].astype(i32)
    padded = (counts + EXPERT_TILE - 1) // EXPERT_TILE * EXPERT_TILE
    pad_end = jnp.cumsum(padded).astype(i32)
    pad_start = pad_end - padded

    def dest_rows(rec):
        sel = rec[:, :TOP_K, None] == jnp.arange(N_EXPERTS, dtype=i32)
        start = jnp.sum(jnp.where(sel, pad_start, 0), axis=-1).astype(i32)
        return (start + rec[:, TOP_K:2 * TOP_K]).reshape(-1)

    dest_p = dest_rows(rec_p)
    dest_s = dest_rows(rec_s)
    n_blocks = -(-(nt * TOP_K + N_EXPERTS * (EXPERT_TILE - 1)) // EXPERT_TILE)
    block_start = jnp.arange(n_blocks, dtype=i32) * EXPERT_TILE
    block_e = jnp.minimum(jnp.searchsorted(pad_end, block_start, side='right'),
                          N_EXPERTS - 1).astype(i32)
    nused = (pad_end[-1:] // EXPERT_TILE).astype(i32)

    xrows = jnp.zeros((n_blocks * EXPERT_TILE * nsub, LANES), f32)
    xrows = _dispatch(dest_p, h2_p, xrows, nsub)
    xrows = _dispatch(dest_s, h2_s, xrows, nsub)
    yrows = _experts(block_e, nused, xrows, w_gu[0].astype(bf16), b_gu[0], w_d[0].astype(bf16),
                     b_d[0], nsub)
    y_p = _combine(dest_p, yrows, gates_p, x1_p, grp_p(gt2), g_final, t, nsub)
    y_s = _combine(dest_s, yrows, gates_s, x1_s, grp_s(gt2), g_final, 1, nsub)

    logf_p = jnp.transpose(logft, (0, 2, 1))
    return (y_p.reshape(bp, t, d), y_s.reshape(db, 1, d),
            ka.reshape(1, bp, t, H_A, DH_A), va.reshape(1, bp, t, H_A, DH_A),
            logf_p.reshape(1, bp, t, H_A), gla_p.reshape(1, bp, H_B, DK_B, DV_B),
            ka_s.reshape(1, db, 1, H_A, DH_A), va_s.reshape(1, db, 1, H_A, DH_A),
            lfnew.reshape(1, db, 1, H_A), gla_s.reshape(1, db, H_B, DK_B, DV_B))
```

```python
import functools

import jax
import jax.numpy as jnp
from jax import lax
from jax.experimental import pallas as pl
from jax.experimental.pallas import tpu as pltpu

f32 = jnp.float32
bf16 = jnp.bfloat16
i32 = jnp.int32

H_A, DH_A = 8, 64
H_B, DK_B, DV_B = 4, 64, 128
GLA_RANK = 16
GLA_TAU = 16.0
N_EXPERTS = 32
TOP_K = 4
SWIGLU_LIMIT = 7.0
SWIGLU_ALPHA = 1.702
RMS_EPS = 1e-6
W_A = H_A * DH_A
WK_B = H_B * DK_B
WV_B = H_B * DV_B

LANES = 128
SUBLANES = 8
VMEM_LIMIT_MB = 56

TOKEN_TILE = 256
ATTN_TILE = 256
GLA_TILE = 256
EXPERT_TILE = 256
PAGES_PER_STEP = 8
DISPATCH_CHUNK = 256

_NT = (((1,), (1,)), ((), ()))
_TN = (((0,), (0,)), ((), ()))


def _cparams(semantics, vmem_mb=VMEM_LIMIT_MB):
    return pltpu.CompilerParams(dimension_semantics=semantics,
                                vmem_limit_bytes=vmem_mb << 20)


def _sigmoid(x):
    return 1.0 / (1.0 + jnp.exp(-x))


def _log_sigmoid(x):
    return jnp.minimum(x, 0.0) - jnp.log1p(jnp.exp(-jnp.abs(x)))


def _split3(x):
    hi = x.astype(bf16)
    r = x - hi.astype(f32)
    mid = r.astype(bf16)
    lo = (r - mid.astype(f32)).astype(bf16)
    return hi, mid, lo


def _dot(a, b):
    return jnp.dot(a, b, preferred_element_type=f32)


def _dotg(a, b, dims):
    return lax.dot_general(a, b, dims, preferred_element_type=f32)


def _rms(x):
    return x * lax.rsqrt(jnp.mean(x * x, axis=-1, keepdims=True) + RMS_EPS)


def _divisor_tile(n, cap, mult):
    best = None
    t = mult
    while t <= min(n, cap):
        if n % t == 0:
            best = t
        t += mult
    assert best is not None, (n, cap, mult)
    return best


def _mod_kernel(c_ref, w_ref, b_ref, o_ref):
    c = c_ref[...]
    s = (c * _sigmoid(c)).astype(bf16)
    o_ref[...] = _dot(s, w_ref[...].astype(bf16)) + b_ref[...]


def _adaln(c_all, w_ada, b_ada):
    r, d = c_all.shape
    n = w_ada.shape[1]
    tn = _divisor_tile(n, 1536, LANES)
    return pl.pallas_call(
        _mod_kernel,
        grid=(n // tn,),
        in_specs=[pl.BlockSpec((r, d), lambda j: (0, 0)),
                  pl.BlockSpec((d, tn), lambda j: (0, j)),
                  pl.BlockSpec((1, tn), lambda j: (0, j))],
        out_specs=pl.BlockSpec((r, tn), lambda j: (0, j)),
        out_shape=jax.ShapeDtypeStruct((r, n), f32),
        compiler_params=_cparams(("arbitrary",)),
        name="adaln_mod",
    )(c_all, w_ada, b_ada.reshape(1, n))


def _inproj_kernel(x_ref, sc_ref, sh_ref, g_ref, wbig_ref, wsm_ref, wfa_ref, wvt_ref,
                   qat_ref, kat_ref, vt_ref, ka_ref, va_ref, qb_ref, kb_ref,
                   vb_ref, rb_ref, ga_ref, gb_ref, sm_ref, fat_ref, *, d):
    x = x_ref[0]
    h = (_rms(x) * g_ref[...]) * sc_ref[0] + sh_ref[0]
    hb = h.astype(bf16)
    off = [0]

    def mm(width):
        r = _dot(hb, wbig_ref[:, off[0]:off[0] + width])
        off[0] += width
        return r

    qa = mm(W_A)
    ka = mm(W_A)
    va = mm(W_A)
    ka_ref[0] = ka
    va_ref[0] = va
    for hh in range(H_A):
        sl = slice(hh * DH_A, (hh + 1) * DH_A)
        qat_ref[0, hh] = qa[:, sl].astype(bf16)
        kat_ref[0, hh] = ka[:, sl].astype(bf16)
    vt = _dotg(wvt_ref[...], hb, _NT)
    vt_ref[0] = vt.reshape(H_A, DH_A, vt.shape[1]).astype(bf16)
    qb_ref[0] = mm(WK_B)
    kb_ref[0] = mm(WK_B)
    vb_ref[0] = mm(WV_B)
    rb_ref[0] = mm(WV_B)
    ga_ref[0] = mm(d)
    gb_ref[0] = mm(d)
    sm_ref[0] = _dot(hb, wsm_ref[...])
    fat_ref[0] = _dotg(wfa_ref[...].astype(bf16), hb, _NT)


def _inproj_weights(w_in, d):
    sizes = (W_A, W_A, W_A, H_A, WK_B, WK_B, WV_B, WV_B, GLA_RANK, d, d)
    offs = [0]
    for s in sizes:
        offs.append(offs[-1] + s)
    seg = lambda i: w_in[:, offs[i]:offs[i + 1]]
    wbig = jnp.concatenate(
        [seg(0) * (DH_A ** -0.5), seg(1), seg(2), seg(4) * (DK_B ** -0.5), seg(5),
         seg(6), seg(7), seg(9), seg(10)], axis=1).astype(bf16)
    wsm = jnp.concatenate(
        [seg(3), seg(8), jnp.zeros((d, LANES - H_A - GLA_RANK), f32)], axis=1).astype(bf16)
    wfa = seg(3).T
    wvt = seg(2).T.astype(bf16)
    return wbig, wsm, wfa, wvt


def _inproj(x, sc, sh, g_mix, wbig, wsm, wfa, wvt):
    b, t, d = x.shape
    tm = min(TOKEN_TILE, t)
    assert t % tm == 0
    rows = sc.shape[1]
    mod_block = (1, 1, d) if rows == 1 else (1, tm, d)
    mod_map = (lambda bi, i: (bi, 0, 0)) if rows == 1 else (lambda bi, i: (bi, i, 0))
    tok3 = lambda w: pl.BlockSpec((1, tm, w), lambda bi, i: (bi, i, 0))
    head4 = pl.BlockSpec((1, H_A, tm, DH_A), lambda bi, i: (bi, 0, i, 0))
    const2 = lambda a: pl.BlockSpec(a.shape, lambda bi, i: (0, 0))
    sds = jax.ShapeDtypeStruct
    out_shape = (
        sds((b, H_A, t, DH_A), bf16), sds((b, H_A, t, DH_A), bf16), sds((b, H_A, DH_A, t), bf16),
        sds((b, t, W_A), f32), sds((b, t, W_A), f32),
        sds((b, t, WK_B), f32), sds((b, t, WK_B), f32),
        sds((b, t, WV_B), f32), sds((b, t, WV_B), f32),
        sds((b, t, d), f32), sds((b, t, d), f32),
        sds((b, t, LANES), f32), sds((b, H_A, t), f32))
    out_specs = (head4, head4, pl.BlockSpec((1, H_A, DH_A, tm), lambda bi, i: (bi, 0, 0, i)),
                 tok3(W_A), tok3(W_A), tok3(WK_B), tok3(WK_B),
                 tok3(WV_B), tok3(WV_B), tok3(d), tok3(d), tok3(LANES),
                 pl.BlockSpec((1, H_A, tm), lambda bi, i: (bi, 0, i)))
    g2 = g_mix.reshape(1, d)
    return pl.pallas_call(
        functools.partial(_inproj_kernel, d=d),
        grid=(b, t // tm),
        in_specs=[tok3(d), pl.BlockSpec(mod_block, mod_map), pl.BlockSpec(mod_block, mod_map),
                  const2(g2), const2(wbig), const2(wsm), const2(wfa), const2(wvt)],
        out_specs=out_specs,
        out_shape=out_shape,
        compiler_params=_cparams(("parallel", "arbitrary")),
        name="inproj",
    )(x, sc, sh, g2, wbig, wsm, wfa, wvt)


def _logf_kernel(fa_ref, bf_ref, logf_ref, *cum_refs, t, cb):
    lf = _log_sigmoid(fa_ref[...] + bf_ref[...])
    logf_ref[...] = lf
    if not cum_refs:
        return
    cum_ref, = cum_refs
    r = lax.broadcasted_iota(i32, (cb, cb), 0)
    c = lax.broadcasted_iota(i32, (cb, cb), 1)
    triu = jnp.where(r <= c, 1.0, 0.0).astype(bf16)
    carry = jnp.zeros((lf.shape[0], 1), f32)
    for j in range(t // cb):
        hi, mid, lo = _split3(lf[:, j * cb:(j + 1) * cb])
        blk = _dot(hi, triu) + _dot(mid, triu) + _dot(lo, triu) + carry
        cum_ref[:, j * cb:(j + 1) * cb] = blk
        carry = blk[:, cb - 1:cb]


def _logf(fat, b_f, with_cumsum):
    b, _, t = fat.shape
    rows = b * H_A
    fa2 = fat.reshape(rows, t)
    bf2 = jnp.tile(b_f.reshape(H_A, 1), (b, 1))
    cb = min(256, t)
    full = lambda a: pl.BlockSpec(a.shape, lambda: (0,) * a.ndim)
    n_out = 2 if with_cumsum else 1
    outs = pl.pallas_call(
        functools.partial(_logf_kernel, t=t, cb=cb),
        in_specs=[full(fa2), full(bf2)],
        out_specs=tuple(pl.BlockSpec((rows, t), lambda: (0, 0)) for _ in range(n_out)),
        out_shape=tuple(jax.ShapeDtypeStruct((rows, t), f32) for _ in range(n_out)),
        name="log_forget",
    )(fa2, bf2)
    return tuple(o.reshape(b, H_A, t) for o in outs)


def _bias_lanes(col, ones_first):
    hi, mid, lo = [p.astype(f32) for p in _split3(col)]
    lane = lax.broadcasted_iota(i32, (col.shape[0], DH_A), 1)
    base = 3 if ones_first else 0
    parts = jnp.where(lane == base, hi, jnp.where(lane == base + 1, mid,
                      jnp.where(lane == base + 2, lo, 0.0)))
    ones = (lane < 3) if ones_first else ((lane >= 3) & (lane < 6))
    return jnp.where(ones, 1.0, parts)


def _fox_kernel(q_ref, k_ref, vt_ref, cum_ref, o_ref, kaug, qaug, m_scr, l_scr, acc_scr, *, tq):
    i = pl.program_id(1)

    @pl.when(i == 0)
    def _():
        cum_all = cum_ref[0]
        for h in range(H_A):
            kaug[h, :, :DH_A] = k_ref[0, h]
            kaug[h, :, DH_A:] = _bias_lanes(-cum_all[:, h:h + 1], False).astype(bf16)

    cum_q = cum_ref[0, pl.ds(pl.multiple_of(i * tq, tq), tq), :]
    for h in range(H_A):
        qaug[h, :, :DH_A] = q_ref[0, h]
        qaug[h, :, DH_A:] = _bias_lanes(cum_q[:, h:h + 1], True).astype(bf16)
    m_scr[...] = jnp.full_like(m_scr, -jnp.inf)
    l_scr[...] = jnp.zeros_like(l_scr)
    acc_scr[...] = jnp.zeros_like(acc_scr)

    key = lax.broadcasted_iota(i32, (tq, tq), 0)
    qry = lax.broadcasted_iota(i32, (tq, tq), 1)

    def step(j, masked):
        start = pl.multiple_of(j * tq, tq)
        for h in range(H_A):
            s = _dotg(kaug[h, pl.ds(start, tq), :], qaug[h], _NT)
            if masked:
                s = jnp.where(key <= qry, s, -jnp.inf)
            m_old = m_scr[h]
            m_new = jnp.maximum(m_old, jnp.max(s, axis=0, keepdims=True))
            alpha = jnp.exp(m_old - m_new)
            p = jnp.exp(s - m_new)
            m_scr[h] = m_new
            l_scr[h] = alpha * l_scr[h] + jnp.sum(p, axis=0, keepdims=True)
            acc_scr[h] = alpha * acc_scr[h] + _dot(vt_ref[0, h, :, pl.ds(start, tq)],
                                                   p.astype(bf16))

    def body(j, carry):
        step(j, False)
        return carry

    lax.fori_loop(0, i, body, 0)
    step(i, True)
    for h in range(H_A):
        o_ref[0, h * DH_A:(h + 1) * DH_A, :] = (acc_scr[h] / l_scr[h]).astype(bf16)


def _fox_prompt(qat, kat, vt, cum):
    b, _, t, _ = qat.shape
    tq = min(ATTN_TILE, t)
    assert t % tq == 0
    return pl.pallas_call(
        functools.partial(_fox_kernel, tq=tq),
        grid=(b, t // tq),
        in_specs=[pl.BlockSpec((1, H_A, tq, DH_A), lambda bi, i: (bi, 0, i, 0)),
                  pl.BlockSpec((1, H_A, t, DH_A), lambda bi, i: (bi, 0, 0, 0)),
                  pl.BlockSpec((1, H_A, DH_A, t), lambda bi, i: (bi, 0, 0, 0)),
                  pl.BlockSpec((1, t, H_A), lambda bi, i: (bi, 0, 0))],
        out_specs=pl.BlockSpec((1, W_A, tq), lambda bi, i: (bi, 0, i)),
        out_shape=jax.ShapeDtypeStruct((b, W_A, t), bf16),
        scratch_shapes=[pltpu.VMEM((H_A, t, 2 * DH_A), bf16), pltpu.VMEM((H_A, tq, 2 * DH_A), bf16),
                        pltpu.VMEM((H_A, 1, tq), f32), pltpu.VMEM((H_A, 1, tq), f32),
                        pltpu.VMEM((H_A, DH_A, tq), f32)],
        compiler_params=_cparams(("parallel", "arbitrary")),
        name="fox_prompt",
    )(qat, kat, vt, cum)


def _fox_decode_kernel(pt_ref, q_ref, knew_ref, vnew_ref, lfnew_ref, *refs, pg, nstep):
    del pt_ref
    k_refs, v_refs, lf_refs = refs[:pg], refs[pg:2 * pg], refs[2 * pg:3 * pg]
    o_ref = refs[3 * pg]
    p_scr, m_scr, self_scr, acc_scr, carry_scr = refs[3 * pg + 1:]
    j = pl.program_id(1)
    q = q_ref[0]
    psz = lf_refs[0].shape[1]
    flat = psz * H_A
    rnd = lambda z: z.astype(bf16).astype(f32)

    @pl.when(j == 0)
    def _():
        s_self = jnp.sum(q.astype(f32) * rnd(knew_ref[0]), axis=1, keepdims=True)
        self_scr[...] = s_self
        m_scr[...] = s_self
        carry_scr[...] = lfnew_ref[0]

    @pl.when(j < nstep)
    def _():
        r = lax.broadcasted_iota(i32, (psz, flat + LANES), 0)
        c = lax.broadcasted_iota(i32, (psz, flat + LANES), 1)
        later = jnp.where(c < flat, jnp.where(r > c // H_A, 1.0, 0.0), 1.0).astype(bf16)
        lane = lax.broadcasted_iota(i32, (H_A, flat), 1)
        sub = lax.broadcasted_iota(i32, (H_A, flat), 0)
        own = (lane % H_A) == sub
        group = nstep - 1 - j
        carry = carry_scr[...]
        m = m_scr[...]
        for p in reversed(range(pg)):
            hi, mid, lo = _split3(lf_refs[p][0])
            suf = _dotg(hi, later, _TN) + _dotg(mid, later, _TN) + _dotg(lo, later, _TN)
            k2 = k_refs[p][0].reshape(flat, DH_A).astype(bf16)
            s = _dotg(q, k2, _NT) + (carry + suf[:, :flat])
            s = jnp.where(own, s, -jnp.inf)
            carry = carry + suf[:, flat:flat + 1]
            p_scr[group * pg + p] = s
            m = jnp.maximum(m, jnp.max(s, axis=1, keepdims=True))
        carry_scr[...] = carry
        m_scr[...] = m

    @pl.when(j == nstep)
    def _():
        m = m_scr[...]
        e_self = jnp.exp(self_scr[...] - m)

        def expo(i, l):
            e = jnp.exp(p_scr[i] - m)
            p_scr[i] = e
            return l + jnp.sum(e, axis=1, keepdims=True)

        l = lax.fori_loop(0, nstep * pg, expo, e_self)

        def norm(i, carry):
            p_scr[i] = p_scr[i] / l
            return carry

        lax.fori_loop(0, nstep * pg, norm, 0)
        acc_scr[...] = rnd(e_self / l) * rnd(vnew_ref[0])

    @pl.when(j >= nstep)
    def _():
        group = 2 * nstep - 1 - j
        acc = acc_scr[...]
        for p in range(pg):
            v2 = v_refs[p][0].reshape(flat, DH_A).astype(bf16)
            acc = acc + _dot(p_scr[group * pg + p].astype(bf16), v2)
        acc_scr[...] = acc

    @pl.when(j == 2 * nstep - 1)
    def _():
        o_ref[0] = acc_scr[...]


def _fox_decode(q, knew, vnew, lfnew, cache_k, cache_v, cache_logf, page_table):
    db = q.shape[0]
    psz = cache_k.shape[1]
    n_pages = page_table.shape[1]
    pg = PAGES_PER_STEP if n_pages % PAGES_PER_STEP == 0 else 1
    nstep = n_pages // pg
    pt = page_table.reshape(-1).astype(i32)

    def page_spec(tail, score_phase):
        def spec(p):
            def imap(bi, j, pt_ref):
                step = jnp.minimum(j, nstep - 1) if score_phase else jnp.maximum(j - nstep, 0)
                return (pt_ref[bi * n_pages + (nstep - 1 - step) * pg + p],) + (0,) * (1 + len(tail))
            return pl.BlockSpec((1, psz) + tail, imap)
        return [spec(p) for p in range(pg)]

    row3 = lambda a: pl.BlockSpec((1,) + a.shape[1:], lambda bi, j, pt_ref: (bi, 0, 0))
    grid_spec = pltpu.PrefetchScalarGridSpec(
        num_scalar_prefetch=1,
        grid=(db, 2 * nstep),
        in_specs=[row3(q), row3(knew), row3(vnew), row3(lfnew)]
        + page_spec((H_A, DH_A), True) + page_spec((H_A, DH_A), False) + page_spec((H_A,), True),
        out_specs=pl.BlockSpec((1, H_A, DH_A), lambda bi, j, pt_ref: (bi, 0, 0)),
        scratch_shapes=[pltpu.VMEM((n_pages, H_A, psz * H_A), f32), pltpu.VMEM((H_A, 1), f32),
                        pltpu.VMEM((H_A, 1), f32), pltpu.VMEM((H_A, DH_A), f32),
                        pltpu.VMEM((H_A, 1), f32)])
    return pl.pallas_call(
        functools.partial(_fox_decode_kernel, pg=pg, nstep=nstep),
        grid_spec=grid_spec,
        out_shape=jax.ShapeDtypeStruct((db, H_A, DH_A), f32),
        compiler_params=_cparams(("parallel", "arbitrary")),
        name="fox_decode",
    )(pt, q, knew, vnew, lfnew, *([cache_k] * pg), *([cache_v] * pg), *([cache_logf] * pg))


def _gla_kernel(qb_ref, kb_ref, vb_ref, rb_ref, sm_ref, wal_ref, bal_ref, gg_ref,
                ob_ref, sout_ref, s_scr, *, tt):
    i = pl.program_id(1)

    @pl.when(i == 0)
    def _():
        s_scr[...] = jnp.zeros_like(s_scr)

    q = qb_ref[0]
    k = kb_ref[0]
    z = _dot(sm_ref[0].astype(bf16), wal_ref[...]) + bal_ref[...]
    a = _log_sigmoid(z) * (1.0 / GLA_TAU)

    row = lax.broadcasted_iota(i32, (tt, tt), 0)
    col = lax.broadcasted_iota(i32, (tt, tt), 1)
    tril = jnp.where(col <= row, 1.0, 0.0).astype(bf16)
    hi, mid, lo = _split3(a)
    b = _dot(tril, hi) + _dot(tril, mid) + _dot(tril, lo)

    rowk = lax.broadcasted_iota(i32, (tt, WK_B), 0)
    hs = [slice(h * DK_B, (h + 1) * DK_B) for h in range(H_B)]
    amat = [jnp.zeros((tt, tt), f32) for _ in range(H_B)]
    edge = b
    for lvl in range(tt.bit_length() - 1):
        half = 1 << lvl
        upper = ((rowk >> lvl) & 1) == 1
        kt = jnp.where(upper, 0.0, k * jnp.exp(edge - b)).astype(bf16)
        edge_q = pltpu.roll(edge, half, 0)
        qt = jnp.where(upper, q * jnp.exp(b - edge_q), 0.0).astype(bf16)
        same = (row >> (lvl + 1)) == (col >> (lvl + 1))
        for h in range(H_B):
            amat[h] = amat[h] + jnp.where(same, _dotg(qt[:, hs[h]], kt[:, hs[h]], _NT), 0.0)
        edge = jnp.where(upper, edge, pltpu.roll(edge, tt - half, 0))
    q16 = q.astype(bf16)
    k16 = k.astype(bf16)
    for h in range(H_B):
        amat[h] = amat[h] + jnp.where(row == col, _dotg(q16[:, hs[h]], k16[:, hs[h]], _NT), 0.0)

    qe = (q * jnp.exp(b)).astype(bf16)
    ke = (k * jnp.exp(edge - b)).astype(bf16)
    e_last = jnp.exp(edge[0:1, :])
    v16 = vb_ref[0].astype(bf16)
    r = rb_ref[0]
    er = lax.broadcasted_iota(i32, (DK_B, DK_B), 0)
    ec = lax.broadcasted_iota(i32, (DK_B, DK_B), 1)
    for h in range(H_B):
        vs = slice(h * DV_B, (h + 1) * DV_B)
        vh = v16[:, vs]
        state = s_scr[h]
        o = _dot(amat[h].astype(bf16), vh) + _dot(qe[:, hs[h]], state.astype(bf16))
        decay_col = jnp.sum(
            jnp.where(er == ec, jnp.broadcast_to(e_last[:, hs[h]], (DK_B, DK_B)), 0.0),
            axis=1, keepdims=True)
        s_scr[h] = decay_col * state + _dotg(ke[:, hs[h]], vh, _TN)
        rh = r[:, vs]
        ob_ref[0, :, vs] = (_rms(o) * gg_ref[:, vs] * (rh * _sigmoid(rh))).astype(bf16)

    @pl.when(i == pl.num_programs(1) - 1)
    def _():
        sout_ref[0] = s_scr[...]


def _gla_prompt(qb, kb, vb, rb, sm, wal, bal, gg):
    b, t, _ = qb.shape
    tt = min(GLA_TILE, t)
    assert t % tt == 0 and tt & (tt - 1) == 0
    tok3 = lambda w: pl.BlockSpec((1, tt, w), lambda bi, i: (bi, i, 0))
    const2 = lambda a: pl.BlockSpec(a.shape, lambda bi, i: (0, 0))
    return pl.pallas_call(
        functools.partial(_gla_kernel, tt=tt),
        grid=(b, t // tt),
        in_specs=[tok3(WK_B), tok3(WK_B), tok3(WV_B), tok3(WV_B), tok3(LANES),
                  const2(wal), const2(bal), const2(gg)],
        out_specs=(tok3(WV_B),
                   pl.BlockSpec((1, H_B, DK_B, DV_B), lambda bi, i: (bi, 0, 0, 0))),
        out_shape=(jax.ShapeDtypeStruct((b, t, WV_B), bf16),
                   jax.ShapeDtypeStruct((b, H_B, DK_B, DV_B), f32)),
        scratch_shapes=[pltpu.VMEM((H_B, DK_B, DV_B), f32)],
        compiler_params=_cparams(("parallel", "arbitrary")),
        name="gla_prompt",
    )(qb, kb, vb, rb, sm, wal, bal, gg)


def _loga_kernel(sm_ref, wal_ref, bal_ref, o_ref):
    z = _dot(sm_ref[...].astype(bf16), wal_ref[...]) + bal_ref[...]
    o_ref[...] = _log_sigmoid(z) * (1.0 / GLA_TAU)


def _gla_decode_kernel(q_ref, k_ref, a_ref, v_ref, r_ref, s_ref, gg_ref, o_ref, so_ref):
    for h in range(H_B):
        q, k, a = q_ref[0, h], k_ref[0, h], a_ref[0, h]
        v = v_ref[0, h]
        state = s_ref[0, h]
        ea = jnp.exp(a)
        rnd = lambda z: z.astype(bf16).astype(f32)
        qk = jnp.sum(q * k, axis=0, keepdims=True)
        o = qk * v + jnp.sum(rnd(q * ea) * rnd(state), axis=0, keepdims=True)
        so_ref[0, h] = ea * state + k * v
        r = r_ref[0, h]
        o_ref[0, h] = _rms(o) * gg_ref[h] * (r * _sigmoid(r))


def _gla_decode(qb, kb, vb, rb, sm, wal, bal, gg, state):
    db = qb.shape[0]
    full = lambda a: pl.BlockSpec(a.shape, lambda: (0,) * a.ndim)
    loga = pl.pallas_call(
        _loga_kernel,
        in_specs=[full(sm), full(wal), full(bal)],
        out_specs=pl.BlockSpec((db, WK_B), lambda: (0, 0)),
        out_shape=jax.ShapeDtypeStruct((db, WK_B), f32),
        name="gla_log_decay",
    )(sm, wal, bal)
    col = lambda a: a.reshape(db, H_B, DK_B, 1)
    rowv = lambda a: a.reshape(db, H_B, 1, DV_B)
    gg4 = gg.reshape(H_B, 1, DV_B)
    b4 = lambda shp: pl.BlockSpec((1,) + shp, lambda bi: (bi, 0, 0, 0))
    o, s_new = pl.pallas_call(
        _gla_decode_kernel,
        grid=(db,),
        in_specs=[b4((H_B, DK_B, 1))] * 3 + [b4((H_B, 1, DV_B))] * 2
        + [b4((H_B, DK_B, DV_B)), pl.BlockSpec(gg4.shape, lambda bi: (0, 0, 0))],
        out_specs=(b4((H_B, 1, DV_B)), b4((H_B, DK_B, DV_B))),
        out_shape=(jax.ShapeDtypeStruct((db, H_B, 1, DV_B), f32),
                   jax.ShapeDtypeStruct((db, H_B, DK_B, DV_B), f32)),
        compiler_params=_cparams(("parallel",)),
        name="gla_decode",
    )(col(qb), col(kb), col(loga), rowv(vb), rowv(rb), state, gg4)
    return o.reshape(db, WV_B), s_new


def _mix_kernel(x_ref, oa_ref, ob_ref, ga_ref, gb_ref, gt1_ref, sc2_ref, sh2_ref, gf_ref,
                wba_ref, wbb_ref, wo_ref, wr_ref, br_ref, x1_ref, h2_ref, lg_ref, *, nsub):
    x = x_ref[0]
    tm = x.shape[0]
    ya = _dotg(oa_ref[0].astype(bf16), wba_ref[...], _TN)
    yb = _dot(ob_ref[0].astype(bf16), wbb_ref[...])
    m = _sigmoid(ga_ref[0]) * ya + _sigmoid(gb_ref[0]) * yb
    x1 = x + gt1_ref[0] * _dot(m.astype(bf16), wo_ref[...])
    x1_ref[...] = x1
    h2 = (_rms(x1) * gf_ref[...]) * sc2_ref[0] + sh2_ref[0]
    for s in range(nsub):
        h2_ref[pl.ds(s, tm, stride=nsub), :] = h2[:, s * LANES:(s + 1) * LANES]
    lg_ref[...] = _dot(h2.astype(bf16), wr_ref[...]) + br_ref[...]


def _mix(x, oa, ob, ga, gb, gt1, sc2, sh2, g_ffn, wba, wbb, wo, wr, br):
    b, t, d = x.shape
    nt = b * t
    nsub = d // LANES
    tm = min(TOKEN_TILE, t)
    assert t % tm == 0
    rows = gt1.shape[1]
    mod_block = (1, 1, d) if rows == 1 else (1, tm, d)
    mod_map = (lambda bi, i: (bi, 0, 0)) if rows == 1 else (lambda bi, i: (bi, i, 0))
    mod = pl.BlockSpec(mod_block, mod_map)
    tok3 = lambda w: pl.BlockSpec((1, tm, w), lambda bi, i: (bi, i, 0))
    const2 = lambda a: pl.BlockSpec(a.shape, lambda bi, i: (0, 0))
    nti = t // tm
    blk = lambda bi, i: (bi * nti + i, 0)
    g2 = g_ffn.reshape(1, d)
    ins = [x, oa, ob, ga, gb, gt1, sc2, sh2, g2, wba, wbb, wo, wr, br]
    in_specs = [tok3(d), pl.BlockSpec((1, W_A, tm), lambda bi, i: (bi, 0, i)), tok3(WV_B),
                tok3(d), tok3(d), mod, mod, mod,
                const2(g2), const2(wba), const2(wbb), const2(wo), const2(wr), const2(br)]
    return pl.pallas_call(
        functools.partial(_mix_kernel, nsub=nsub),
        grid=(b, nti),
        in_specs=in_specs,
        out_specs=(pl.BlockSpec((tm, d), blk), pl.BlockSpec((tm * nsub, LANES), blk),
                   pl.BlockSpec((tm, LANES), blk)),
        out_shape=(jax.ShapeDtypeStruct((nt, d), f32),
                   jax.ShapeDtypeStruct((nt * nsub, LANES), f32),
                   jax.ShapeDtypeStruct((nt, LANES), f32)),
        compiler_params=_cparams(("parallel", "arbitrary")),
        name="branch_mix",
    )(*ins)


def _route_kernel(lg_ref, cin_ref, rec_ref, gate_ref, cnt_ref, carry_scr, *, tr):
    @pl.when(pl.program_id(0) == 0)
    def _():
        carry_scr[...] = cin_ref[...]

    lane = lax.broadcasted_iota(i32, (tr, LANES), 1)
    lanef = lane.astype(f32)
    v = jnp.where(lane < N_EXPERTS, lg_ref[...], -jnp.inf)
    onehot = jnp.zeros((tr, LANES), f32)
    ids, vals = [], []
    for _ in range(TOP_K):
        mx = jnp.max(v, axis=1, keepdims=True)
        idx = jnp.min(jnp.where(v == mx, lanef, float(LANES)), axis=1, keepdims=True)
        sel = lanef == idx
        onehot = jnp.where(sel, 1.0, onehot)
        v = jnp.where(sel, -jnp.inf, v)
        ids.append(idx)
        vals.append(mx)
    es = [jnp.exp(vk - vals[0]) for vk in vals]
    tot = es[0]
    for e in es[1:]:
        tot = tot + e

    r2 = lax.broadcasted_iota(i32, (tr, tr), 0)
    c2 = lax.broadcasted_iota(i32, (tr, tr), 1)
    before = jnp.where(c2 < r2, 1.0, 0.0).astype(bf16)
    pref = _dot(before, onehot.astype(bf16)) + carry_scr[...]
    carry_scr[...] = carry_scr[...] + jnp.sum(onehot, axis=0, keepdims=True)

    rec = jnp.zeros((tr, LANES), f32)
    gate = jnp.zeros((tr, LANES), f32)
    for k in range(TOP_K):
        rank = jnp.sum(jnp.where(lanef == ids[k], pref, 0.0), axis=1, keepdims=True)
        rec = jnp.where(lane == k, rank * float(N_EXPERTS) + ids[k], rec)
        gate = jnp.where(lane == k, es[k] / tot, gate)
    rec_ref[...] = rec.astype(i32)
    gate_ref[...] = gate
    cnt_ref[...] = carry_scr[...]


def _route(logits, counts_in):
    nt = logits.shape[0]
    tr = _divisor_tile(nt, 512, SUBLANES)
    tile = pl.BlockSpec((tr, LANES), lambda i: (i, 0))
    cnt = pl.BlockSpec((1, LANES), lambda i: (0, 0))
    return pl.pallas_call(
        functools.partial(_route_kernel, tr=tr),
        grid=(nt // tr,),
        in_specs=[tile, cnt],
        out_specs=(tile, tile, cnt),
        out_shape=(jax.ShapeDtypeStruct((nt, LANES), i32),
                   jax.ShapeDtypeStruct((nt, LANES), f32),
                   jax.ShapeDtypeStruct((1, LANES), f32)),
        scratch_shapes=[pltpu.VMEM((1, LANES), f32)],
        compiler_params=_cparams(("arbitrary",)),
        name="route_topk",
    )(logits, counts_in)


def _dest_row(rec_ref, start_ref, i):
    packed = rec_ref[i]
    return start_ref[packed % N_EXPERTS] + packed // N_EXPERTS


def _dispatch_kernel(rec_ref, start_ref, h_ref, xin_hbm, xout_hbm, sem, *, nsub, ch):
    del xin_hbm
    base = pl.program_id(0) * ch

    def row_copy(n, k):
        src = pl.multiple_of(n * nsub, nsub)
        dst = pl.multiple_of(_dest_row(rec_ref, start_ref, (base + n) * TOP_K + k) * nsub, nsub)
        return pltpu.make_async_copy(h_ref.at[pl.ds(src, nsub)], xout_hbm.at[pl.ds(dst, nsub)],
                                     sem)

    def for_rows(fn):
        def body(n, carry):
            for k in range(TOP_K):
                fn(row_copy(n, k))
            return carry
        lax.fori_loop(0, ch, body, 0)

    for_rows(lambda cp: cp.start())
    for_rows(lambda cp: cp.wait())


def _dispatch(rec_flat, pad_start, h2rows, xrows, nsub):
    n_tok = rec_flat.shape[0] // TOP_K
    ch = min(DISPATCH_CHUNK, n_tok)
    assert n_tok % ch == 0
    grid_spec = pltpu.PrefetchScalarGridSpec(
        num_scalar_prefetch=2,
        grid=(n_tok // ch,),
        in_specs=[pl.BlockSpec((ch * nsub, LANES), lambda i, rr, sr: (i, 0)),
                  pl.BlockSpec(memory_space=pl.ANY)],
        out_specs=pl.BlockSpec(memory_space=pl.ANY),
        scratch_shapes=[pltpu.SemaphoreType.DMA(())])
    return pl.pallas_call(
        functools.partial(_dispatch_kernel, nsub=nsub, ch=ch),
        grid_spec=grid_spec,
        out_shape=jax.ShapeDtypeStruct(xrows.shape, xrows.dtype),
        input_output_aliases={3: 0},
        compiler_params=_cparams(("arbitrary",)),
        name="moe_dispatch",
    )(rec_flat, pad_start, h2rows, xrows)


def _expert_kernel(be_ref, nused_ref, x_ref, wgu_ref, bgu_ref, wd_ref, bd_ref, y_ref,
                   *, tmx, nsub, dff):
    del be_ref
    live = pl.program_id(0) < nused_ref[0]

    @pl.when(jnp.logical_not(live))
    def _():
        y_ref[...] = jnp.zeros_like(y_ref)

    @pl.when(live)
    def _():
        x = jnp.concatenate(
            [x_ref[pl.ds(s, tmx, stride=nsub), :].astype(bf16) for s in range(nsub)], axis=1)
        gu = _dot(x, wgu_ref[0]) + bgu_ref[0]
        gate = jnp.minimum(gu[:, :dff], SWIGLU_LIMIT)
        up = jnp.clip(gu[:, dff:], -SWIGLU_LIMIT, SWIGLU_LIMIT)
        glu = gate * _sigmoid(SWIGLU_ALPHA * gate)
        y = _dot(((up + 1.0) * glu).astype(bf16), wd_ref[0]) + bd_ref[0]
        for s in range(nsub):
            y_ref[pl.ds(s, tmx, stride=nsub), :] = y[:, s * LANES:(s + 1) * LANES]


def _experts(block_e, nused, xrows, wgu, bgu, wd, bd, nsub):
    n_blocks = block_e.shape[0]
    tmx = EXPERT_TILE
    e, d, dff2 = wgu.shape
    dff = dff2 // 2
    live = lambda r, nu: jnp.minimum(r, nu[0] - 1)
    rows = pl.BlockSpec((tmx * nsub, LANES), lambda r, be, nu: (live(r, nu), 0))
    per_e = lambda shp: pl.BlockSpec((1,) + shp, lambda r, be, nu: (be[live(r, nu)], 0, 0))
    grid_spec = pltpu.PrefetchScalarGridSpec(
        num_scalar_prefetch=2,
        grid=(n_blocks,),
        in_specs=[rows, per_e((d, dff2)), per_e((1, dff2)), per_e((dff, d)), per_e((1, d))],
        out_specs=pl.BlockSpec((tmx * nsub, LANES), lambda r, be, nu: (r, 0)))
    return pl.pallas_call(
        functools.partial(_expert_kernel, tmx=tmx, nsub=nsub, dff=dff),
        grid_spec=grid_spec,
        out_shape=jax.ShapeDtypeStruct(xrows.shape, f32),
        compiler_params=_cparams(("arbitrary",)),
        name="moe_experts",
    )(block_e, nused, xrows, wgu, bgu.reshape(e, 1, dff2), wd, bd.reshape(e, 1, d))


def _combine_kernel(rec_ref, start_ref, y_hbm, gate_ref, x1_ref, gt2_ref, gfin_ref, out_ref,
                    buf, sem, *, tc, nsub):
    base = pl.program_id(0) * tc

    def row_copy(n, k):
        src = pl.multiple_of(_dest_row(rec_ref, start_ref, (base + n) * TOP_K + k) * nsub, nsub)
        dst = pl.multiple_of((k * tc + n) * nsub, nsub)
        return pltpu.make_async_copy(y_hbm.at[pl.ds(src, nsub)], buf.at[pl.ds(dst, nsub)], sem)

    def for_rows(fn):
        def body(n, carry):
            for k in range(TOP_K):
                fn(row_copy(n, k))
            return carry
        lax.fori_loop(0, tc, body, 0)

    for_rows(lambda cp: cp.start())
    for_rows(lambda cp: cp.wait())

    g = gate_ref[...]
    cols = []
    for s in range(nsub):
        acc = None
        for k in range(TOP_K):
            term = g[:, k:k + 1] * buf[pl.ds(k * tc * nsub + s, tc, stride=nsub), :]
            acc = term if acc is None else acc + term
        cols.append(acc)
    y = jnp.concatenate(cols, axis=1)
    out_ref[...] = _rms(x1_ref[...] + gt2_ref[0] * y) * gfin_ref[...]


def _combine(rec_flat, pad_start, yrows, gates, x1, gt2, g_final, t_per_mod, nsub):
    n_tok, d = x1.shape
    tc = min(TOKEN_TILE, n_tok)
    assert n_tok % tc == 0
    rows = gt2.shape[1]
    if rows == 1:
        per_mod = t_per_mod // tc
        mod = pl.BlockSpec((1, 1, d), lambda i, rr, sr: (i // per_mod, 0, 0))
    else:
        mod = pl.BlockSpec((1, tc, d), lambda i, rr, sr: (0, i, 0))
    blk = lambda i, rr, sr: (i, 0)
    gfin = g_final.reshape(1, d)
    grid_spec = pltpu.PrefetchScalarGridSpec(
        num_scalar_prefetch=2,
        grid=(n_tok // tc,),
        in_specs=[pl.BlockSpec(memory_space=pl.ANY), pl.BlockSpec((tc, LANES), blk),
                  pl.BlockSpec((tc, d), blk), mod,
                  pl.BlockSpec((1, d), lambda i, rr, sr: (0, 0))],
        out_specs=pl.BlockSpec((tc, d), blk),
        scratch_shapes=[pltpu.VMEM((TOP_K * tc * nsub, LANES), f32), pltpu.SemaphoreType.DMA(())])
    return pl.pallas_call(
        functools.partial(_combine_kernel, tc=tc, nsub=nsub),
        grid_spec=grid_spec,
        out_shape=jax.ShapeDtypeStruct((n_tok, d), f32),
        compiler_params=_cparams(("arbitrary",)),
        name="moe_combine",
    )(rec_flat, pad_start, yrows, gates, x1, gt2, gfin)


def kernel(x_prompt, x_sample, c_prompt, c_sample, cache_k, cache_v, cache_logf, state_gla,
           page_table, g_mix, g_ffn, g_final, w_ada, b_ada, w_in, b_f, w_alpha, b_alpha, g_gla,
           w_branch, w_o, w_router, b_router, w_gu, b_gu, w_d, b_d):
    depth = g_mix.shape[0]
    assert depth == 1, "one decoder layer"
    bp, t, d = x_prompt.shape
    db = x_sample.shape[0]
    assert x_sample.shape[1] == 1
    nsub = d // LANES
    ntp = bp * t
    nt = ntp + db

    nmod = bp + db
    pad = (-nmod) % SUBLANES
    c_all = jnp.concatenate([c_prompt, c_sample, jnp.zeros((pad, d), f32)], axis=0)
    mod = _adaln(c_all, w_ada[0], b_ada[0])
    sh1, sc1, gt1, sh2, sc2, gt2 = [mod[:, i * d:(i + 1) * d] for i in range(6)]
    grp_p = lambda a: a[:bp].reshape(bp, 1, d)
    grp_s = lambda a: a[bp:nmod].reshape(1, db, d)

    wbig, wsm, wfa, wvt = _inproj_weights(w_in[0], d)
    wal = jnp.zeros((LANES, WK_B), f32).at[H_A:H_A + GLA_RANK].set(w_alpha[0]).astype(bf16)
    bal = b_alpha[0].reshape(1, WK_B)
    gg = g_gla[0].reshape(1, WV_B)
    wba = w_branch[0, :W_A].astype(bf16)
    wbb = w_branch[0, W_A:].astype(bf16)
    wo = w_o[0].astype(bf16)
    wr = jnp.zeros((d, LANES), f32).at[:, :N_EXPERTS].set(w_router[0]).astype(bf16)
    br = jnp.zeros((1, LANES), f32).at[0, :N_EXPERTS].set(b_router[0])

    (qat, kat, vt, ka, va, qb, kb, vb, rb, ga, gb, sm, fat) = _inproj(
        x_prompt, grp_p(1.0 + sc1), grp_p(sh1), g_mix[0], wbig, wsm, wfa, wvt)
    logft, cumt = _logf(fat, b_f[0], True)
    oa = _fox_prompt(qat, kat, vt, jnp.transpose(cumt, (0, 2, 1)))
    ob, gla_p = _gla_prompt(qb, kb, vb, rb, sm, wal, bal, gg)
    x1_p, h2_p, lg_p = _mix(x_prompt, oa, ob, ga, gb, grp_p(gt1), grp_p(1.0 + sc2), grp_p(sh2),
                            g_ffn[0], wba, wbb, wo, wr, br)

    xs = x_sample.reshape(1, db, d)
    (qat_s, _, _, ka_s, va_s, qb_s, kb_s, vb_s, rb_s, ga_s, gb_s, sm_s, fat_s) = _inproj(
        xs, grp_s(1.0 + sc1), grp_s(sh1), g_mix[0], wbig, wsm, wfa, wvt)
    logft_s, = _logf(fat_s, b_f[0], False)
    lfnew = jnp.transpose(logft_s[0], (1, 0)).reshape(db, H_A, 1)
    oa_s = _fox_decode(jnp.transpose(qat_s[0], (1, 0, 2)), ka_s.reshape(db, H_A, DH_A),
                       va_s.reshape(db, H_A, DH_A), lfnew,
                       cache_k[0], cache_v[0], cache_logf[0], page_table)
    ob_s, gla_s = _gla_decode(qb_s[0], kb_s[0], vb_s[0], rb_s[0], sm_s[0], wal, bal, gg,
                              state_gla[0])
    x1_s, h2_s, lg_s = _mix(xs, oa_s.reshape(db, W_A).T[None], ob_s.reshape(1, db, WV_B), ga_s, gb_s,
                            grp_s(gt1), grp_s(1.0 + sc2), grp_s(sh2), g_ffn[0],
                            wba, wbb, wo, wr, br)

    rec_p, gates_p, cnt_p = _route(lg_p, jnp.zeros((1, LANES), f32))
    rec_s, gates_s, cnt = _route(lg_s, cnt_p)
    counts = cnt[0, :N_EXPERTS].astype(i32)
    padded = (counts + EXPERT_TILE - 1) // EXPERT_TILE * EXPERT_TILE
    pad_end = jnp.cumsum(padded).astype(i32)
    pad_start = pad_end - padded

    rflat_p = rec_p[:, :TOP_K].reshape(-1)
    rflat_s = rec_s[:, :TOP_K].reshape(-1)
    n_blocks = -(-(nt * TOP_K + N_EXPERTS * (EXPERT_TILE - 1)) // EXPERT_TILE)
    block_start = jnp.arange(n_blocks, dtype=i32) * EXPERT_TILE
    block_e = jnp.minimum(jnp.sum(pad_end[None, :] <= block_start[:, None], axis=1),
                          N_EXPERTS - 1).astype(i32)
    nused = (pad_end[-1:] // EXPERT_TILE).astype(i32)

    xrows = jnp.zeros((n_blocks * EXPERT_TILE * nsub, LANES), f32)
    xrows = _dispatch(rflat_p, pad_start, h2_p, xrows, nsub)
    xrows = _dispatch(rflat_s, pad_start, h2_s, xrows, nsub)
    yrows = _experts(block_e, nused, xrows, w_gu[0].astype(bf16), b_gu[0], w_d[0].astype(bf16),
                     b_d[0], nsub)
    y_p = _combine(rflat_p, pad_start, yrows, gates_p, x1_p, grp_p(gt2), g_final, t, nsub)
    y_s = _combine(rflat_s, pad_start, yrows, gates_s, x1_s, grp_s(gt2), g_final, 1, nsub)

    logf_p = jnp.transpose(logft, (0, 2, 1))
    return (y_p.reshape(bp, t, d), y_s.reshape(db, 1, d),
            ka.reshape(1, bp, t, H_A, DH_A), va.reshape(1, bp, t, H_A, DH_A),
            logf_p.reshape(1, bp, t, H_A), gla_p.reshape(1, bp, H_B, DK_B, DV_B),
            ka_s.reshape(1, db, 1, H_A, DH_A), va_s.reshape(1, db, 1, H_A, DH_A),
            lfnew.reshape(1, db, 1, H_A), gla_s.reshape(1, db, H_B, DK_B, DV_B))
```

```python
import functools

import jax
import jax.numpy as jnp
from jax import lax
from jax.experimental import pallas as pl
from jax.experimental.pallas import tpu as pltpu

f32 = jnp.float32
bf16 = jnp.bfloat16
i32 = jnp.int32

H_A, DH_A = 8, 64
H_B, DK_B, DV_B = 4, 64, 128
GLA_RANK = 16
GLA_TAU = 16.0
N_EXPERTS = 32
TOP_K = 4
SWIGLU_LIMIT = 7.0
SWIGLU_ALPHA = 1.702
RMS_EPS = 1e-6
W_A = H_A * DH_A
WK_B = H_B * DK_B
WV_B = H_B * DV_B

LANES = 128
SUBLANES = 8
VMEM_LIMIT_MB = 56

TOKEN_TILE = 256
ATTN_TILE = 256
GLA_TILE = 256
EXPERT_TILE = 256
PAGES_PER_STEP = 8
DISPATCH_CHUNK = 256

_NT = (((1,), (1,)), ((), ()))
_TN = (((0,), (0,)), ((), ()))


def _cparams(semantics, vmem_mb=VMEM_LIMIT_MB):
    return pltpu.CompilerParams(dimension_semantics=semantics,
                                vmem_limit_bytes=vmem_mb << 20)


def _sigmoid(x):
    return 1.0 / (1.0 + jnp.exp(-x))


def _log_sigmoid(x):
    return jnp.minimum(x, 0.0) - jnp.log1p(jnp.exp(-jnp.abs(x)))


def _split3(x):
    hi = x.astype(bf16)
    r = x - hi.astype(f32)
    mid = r.astype(bf16)
    lo = (r - mid.astype(f32)).astype(bf16)
    return hi, mid, lo


def _dot(a, b):
    return jnp.dot(a, b, preferred_element_type=f32)


def _dotg(a, b, dims):
    return lax.dot_general(a, b, dims, preferred_element_type=f32)


def _rms(x):
    return x * lax.rsqrt(jnp.mean(x * x, axis=-1, keepdims=True) + RMS_EPS)


def _divisor_tile(n, cap, mult):
    best = None
    t = mult
    while t <= min(n, cap):
        if n % t == 0:
            best = t
        t += mult
    assert best is not None, (n, cap, mult)
    return best


def _mod_kernel(c_ref, w_ref, b_ref, o_ref):
    c = c_ref[...]
    s = (c * _sigmoid(c)).astype(bf16)
    o_ref[...] = _dot(s, w_ref[...].astype(bf16)) + b_ref[...]


def _adaln(c_all, w_ada, b_ada):
    r, d = c_all.shape
    n = w_ada.shape[1]
    tn = _divisor_tile(n, 1536, LANES)
    return pl.pallas_call(
        _mod_kernel,
        grid=(n // tn,),
        in_specs=[pl.BlockSpec((r, d), lambda j: (0, 0)),
                  pl.BlockSpec((d, tn), lambda j: (0, j)),
                  pl.BlockSpec((1, tn), lambda j: (0, j))],
        out_specs=pl.BlockSpec((r, tn), lambda j: (0, j)),
        out_shape=jax.ShapeDtypeStruct((r, n), f32),
        compiler_params=_cparams(("arbitrary",)),
        name="adaln_mod",
    )(c_all, w_ada, b_ada.reshape(1, n))


def _inproj_kernel(x_ref, sc_ref, sh_ref, g_ref, wbig_ref, wsm_ref, wfa_ref, wkt_ref, wvt_ref,
                   *out_refs, d, time_minor):
    if time_minor:
        (qat_ref, kat_ref, vt_ref, ka_ref, va_ref, qb_ref, kb_ref, vb_ref, rb_ref,
         ga_ref, gb_ref, sm_ref, fat_ref) = out_refs
    else:
        (qat_ref, ka_ref, va_ref, qb_ref, kb_ref, vb_ref, rb_ref,
         ga_ref, gb_ref, sm_ref, fat_ref) = out_refs
    x = x_ref[0]
    h = (_rms(x) * g_ref[...]) * sc_ref[0] + sh_ref[0]
    hb = h.astype(bf16)
    off = [0]

    def mm(width):
        r = _dot(hb, wbig_ref[:, off[0]:off[0] + width])
        off[0] += width
        return r

    qa = mm(W_A)
    ka = mm(W_A)
    va = mm(W_A)
    for hh in range(H_A):
        sl = slice(hh * DH_A, (hh + 1) * DH_A)
        qat_ref[0, hh] = qa[:, sl].astype(bf16)
        if time_minor:
            kat_ref[0, hh] = ka[:, sl].astype(bf16)
    if time_minor:
        tm = x.shape[0]
        kt = _dotg(wkt_ref[...], hb, _NT).reshape(H_A, DH_A, tm)
        vt = _dotg(wvt_ref[...], hb, _NT).reshape(H_A, DH_A, tm)
        ka_ref[0] = kt
        va_ref[0] = vt
        vt_ref[0] = vt.astype(bf16)
    else:
        ka_ref[0] = ka
        va_ref[0] = va
    qb_ref[0] = mm(WK_B)
    kb_ref[0] = mm(WK_B)
    vb_ref[0] = mm(WV_B)
    rb_ref[0] = mm(WV_B)
    ga_ref[0] = mm(d)
    gb_ref[0] = mm(d)
    sm_ref[0] = _dot(hb, wsm_ref[...])
    fat_ref[0] = _dotg(wfa_ref[...].astype(bf16), hb, _NT)


def _inproj_weights(w_in, d):
    sizes = (W_A, W_A, W_A, H_A, WK_B, WK_B, WV_B, WV_B, GLA_RANK, d, d)
    offs = [0]
    for s in sizes:
        offs.append(offs[-1] + s)
    seg = lambda i: w_in[:, offs[i]:offs[i + 1]]
    wbig = jnp.concatenate(
        [seg(0) * (DH_A ** -0.5), seg(1), seg(2), seg(4) * (DK_B ** -0.5), seg(5),
         seg(6), seg(7), seg(9), seg(10)], axis=1).astype(bf16)
    wsm = jnp.concatenate(
        [seg(3), seg(8), jnp.zeros((d, LANES - H_A - GLA_RANK), f32)], axis=1).astype(bf16)
    wfa = seg(3).T
    wkt = seg(1).T.astype(bf16)
    wvt = seg(2).T.astype(bf16)
    return wbig, wsm, wfa, wkt, wvt


def _inproj(x, sc, sh, g_mix, weights, time_minor):
    wbig, wsm, wfa, wkt, wvt = weights
    b, t, d = x.shape
    tm = min(TOKEN_TILE, t)
    assert t % tm == 0
    rows = sc.shape[1]
    mod_block = (1, 1, d) if rows == 1 else (1, tm, d)
    mod_map = (lambda bi, i: (bi, 0, 0)) if rows == 1 else (lambda bi, i: (bi, i, 0))
    tok3 = lambda w: pl.BlockSpec((1, tm, w), lambda bi, i: (bi, i, 0))
    head4 = pl.BlockSpec((1, H_A, tm, DH_A), lambda bi, i: (bi, 0, i, 0))
    const2 = lambda a: pl.BlockSpec(a.shape, lambda bi, i: (0, 0))
    sds = jax.ShapeDtypeStruct
    headt = pl.BlockSpec((1, H_A, DH_A, tm), lambda bi, i: (bi, 0, 0, i))
    if time_minor:
        kv_shape = [sds((b, H_A, t, DH_A), bf16), sds((b, H_A, DH_A, t), bf16),
                    sds((b, H_A, DH_A, t), f32), sds((b, H_A, DH_A, t), f32)]
        kv_specs = [head4, headt, headt, headt]
    else:
        kv_shape = [sds((b, t, W_A), f32), sds((b, t, W_A), f32)]
        kv_specs = [tok3(W_A), tok3(W_A)]
    out_shape = tuple(
        [sds((b, H_A, t, DH_A), bf16)] + kv_shape
        + [sds((b, t, WK_B), f32), sds((b, t, WK_B), f32),
           sds((b, t, WV_B), f32), sds((b, t, WV_B), f32),
           sds((b, t, d), f32), sds((b, t, d), f32),
           sds((b, t, LANES), f32), sds((b, H_A, t), f32)])
    out_specs = tuple(
        [head4] + kv_specs
        + [tok3(WK_B), tok3(WK_B), tok3(WV_B), tok3(WV_B), tok3(d), tok3(d), tok3(LANES),
           pl.BlockSpec((1, H_A, tm), lambda bi, i: (bi, 0, i))])
    g2 = g_mix.reshape(1, d)
    return pl.pallas_call(
        functools.partial(_inproj_kernel, d=d, time_minor=time_minor),
        grid=(b, t // tm),
        in_specs=[tok3(d), pl.BlockSpec(mod_block, mod_map), pl.BlockSpec(mod_block, mod_map),
                  const2(g2), const2(wbig), const2(wsm), const2(wfa), const2(wkt), const2(wvt)],
        out_specs=out_specs,
        out_shape=out_shape,
        compiler_params=_cparams(("parallel", "arbitrary")),
        name="inproj",
    )(x, sc, sh, g2, wbig, wsm, wfa, wkt, wvt)


def _logf_kernel(fa_ref, bf_ref, logf_ref, *cum_refs, t, cb):
    lf = _log_sigmoid(fa_ref[...] + bf_ref[...])
    logf_ref[...] = lf
    if not cum_refs:
        return
    cum_ref, = cum_refs
    r = lax.broadcasted_iota(i32, (cb, cb), 0)
    c = lax.broadcasted_iota(i32, (cb, cb), 1)
    triu = jnp.where(r <= c, 1.0, 0.0).astype(bf16)
    carry = jnp.zeros((lf.shape[0], 1), f32)
    for j in range(t // cb):
        hi, mid, lo = _split3(lf[:, j * cb:(j + 1) * cb])
        blk = _dot(hi, triu) + _dot(mid, triu) + _dot(lo, triu) + carry
        cum_ref[:, j * cb:(j + 1) * cb] = blk
        carry = blk[:, cb - 1:cb]


def _logf(fat, b_f, with_cumsum):
    b, _, t = fat.shape
    rows = b * H_A
    fa2 = fat.reshape(rows, t)
    bf2 = jnp.tile(b_f.reshape(H_A, 1), (b, 1))
    cb = min(256, t)
    full = lambda a: pl.BlockSpec(a.shape, lambda: (0,) * a.ndim)
    n_out = 2 if with_cumsum else 1
    outs = pl.pallas_call(
        functools.partial(_logf_kernel, t=t, cb=cb),
        in_specs=[full(fa2), full(bf2)],
        out_specs=tuple(pl.BlockSpec((rows, t), lambda: (0, 0)) for _ in range(n_out)),
        out_shape=tuple(jax.ShapeDtypeStruct((rows, t), f32) for _ in range(n_out)),
        name="log_forget",
    )(fa2, bf2)
    return tuple(o.reshape(b, H_A, t) for o in outs)


def _bias_lanes(col, ones_first):
    hi, mid, lo = [p.astype(f32) for p in _split3(col)]
    lane = lax.broadcasted_iota(i32, (col.shape[0], DH_A), 1)
    base = 3 if ones_first else 0
    parts = jnp.where(lane == base, hi, jnp.where(lane == base + 1, mid,
                      jnp.where(lane == base + 2, lo, 0.0)))
    ones = (lane < 3) if ones_first else ((lane >= 3) & (lane < 6))
    return jnp.where(ones, 1.0, parts)


def _fox_kernel(q_ref, k_ref, vt_ref, cum_ref, o_ref, kaug, qaug, *, tq):
    i = pl.program_id(1)

    @pl.when(i == 0)
    def _():
        cum_all = cum_ref[0]
        for h in range(H_A):
            kaug[h, :, :DH_A] = k_ref[0, h]
            kaug[h, :, DH_A:] = _bias_lanes(-cum_all[:, h:h + 1], False).astype(bf16)

    cum_q = cum_ref[0, pl.ds(pl.multiple_of(i * tq, tq), tq), :]
    for h in range(H_A):
        qaug[h, :, :DH_A] = q_ref[0, h]
        qaug[h, :, DH_A:] = _bias_lanes(cum_q[:, h:h + 1], True).astype(bf16)
    key = lax.broadcasted_iota(i32, (tq, tq), 0)
    qry = lax.broadcasted_iota(i32, (tq, tq), 1)

    def step(j, carry, masked):
        start = pl.multiple_of(j * tq, tq)
        scores = [_dotg(kaug[h, pl.ds(start, tq), :], qaug[h], _NT) for h in range(H_A)]
        stats = []
        for h in range(H_A):
            m_old, l_old, _ = carry[h]
            s = scores[h]
            if masked:
                s = jnp.where(key <= qry, s, -jnp.inf)
            m_new = jnp.maximum(m_old, jnp.max(s, axis=0, keepdims=True))
            alpha = jnp.exp(m_old - m_new)
            p = jnp.exp(s - m_new)
            l_new = alpha * l_old + jnp.sum(p, axis=0, keepdims=True)
            stats.append((m_new, l_new, alpha, p.astype(bf16)))
        out = []
        for h in range(H_A):
            m_new, l_new, alpha, p = stats[h]
            acc_new = alpha * carry[h][2] + _dot(vt_ref[0, h, :, pl.ds(start, tq)], p)
            out.append((m_new, l_new, acc_new))
        return tuple(out)

    init = tuple((jnp.full((1, tq), -jnp.inf, f32), jnp.zeros((1, tq), f32),
                  jnp.zeros((DH_A, tq), f32)) for _ in range(H_A))
    carry = lax.fori_loop(0, i, functools.partial(step, masked=False), init)
    carry = step(i, carry, True)
    for h in range(H_A):
        _, l, acc = carry[h]
        o_ref[0, h * DH_A:(h + 1) * DH_A, :] = (acc / l).astype(bf16)


def _fox_prompt(qat, kat, vt, cum):
    b, _, t, _ = qat.shape
    tq = min(ATTN_TILE, t)
    assert t % tq == 0
    return pl.pallas_call(
        functools.partial(_fox_kernel, tq=tq),
        grid=(b, t // tq),
        in_specs=[pl.BlockSpec((1, H_A, tq, DH_A), lambda bi, i: (bi, 0, i, 0)),
                  pl.BlockSpec((1, H_A, t, DH_A), lambda bi, i: (bi, 0, 0, 0)),
                  pl.BlockSpec((1, H_A, DH_A, t), lambda bi, i: (bi, 0, 0, 0)),
                  pl.BlockSpec((1, t, H_A), lambda bi, i: (bi, 0, 0))],
        out_specs=pl.BlockSpec((1, W_A, tq), lambda bi, i: (bi, 0, i)),
        out_shape=jax.ShapeDtypeStruct((b, W_A, t), bf16),
        scratch_shapes=[pltpu.VMEM((H_A, t, 2 * DH_A), bf16), pltpu.VMEM((H_A, tq, 2 * DH_A), bf16)],
        compiler_params=_cparams(("parallel", "arbitrary")),
        name="fox_prompt",
    )(qat, kat, vt, cum)


def _fox_decode_kernel(pt_ref, q_ref, knew_ref, vnew_ref, lfnew_ref, *refs, pg, nstep):
    del pt_ref
    k_refs, v_refs, lf_refs = refs[:pg], refs[pg:2 * pg], refs[2 * pg:3 * pg]
    o_ref = refs[3 * pg]
    p_scr, m_scr, self_scr, acc_scr, carry_scr = refs[3 * pg + 1:]
    j = pl.program_id(1)
    q = q_ref[0]
    psz = lf_refs[0].shape[2]
    lane = lax.broadcasted_iota(i32, (H_A, W_A), 1)
    sub = lax.broadcasted_iota(i32, (H_A, W_A), 0)
    own = (lane // DH_A) == sub
    rnd = lambda z: z.astype(bf16).astype(f32)

    @pl.when(j == 0)
    def _():
        s_self = jnp.sum(q.astype(f32) * rnd(knew_ref[0]), axis=1, keepdims=True)
        self_scr[...] = s_self
        m_scr[...] = s_self
        carry_scr[...] = lfnew_ref[0]

    @pl.when(j < nstep)
    def _():
        r = lax.broadcasted_iota(i32, (psz, 2 * psz), 0)
        c = lax.broadcasted_iota(i32, (psz, 2 * psz), 1)
        later = jnp.where(c < psz, jnp.where(r > c, 1.0, 0.0), 1.0).astype(bf16)
        group = nstep - 1 - j
        carry = carry_scr[...]
        m = m_scr[...]
        for p in reversed(range(pg)):
            hi, mid, lo = _split3(lf_refs[p][0])
            suf = _dot(hi, later) + _dot(mid, later) + _dot(lo, later)
            kt = k_refs[p][0].reshape(W_A, psz).astype(bf16)
            s = _dot(q, kt) + (carry + suf[:, :psz])
            carry = carry + suf[:, psz:psz + 1]
            p_scr[group * pg + p] = s
            m = jnp.maximum(m, jnp.max(s, axis=1, keepdims=True))
        carry_scr[...] = carry
        m_scr[...] = m

    @pl.when(j == nstep)
    def _():
        m = m_scr[...]
        e_self = jnp.exp(self_scr[...] - m)
        e = jnp.exp(p_scr[...] - m[None])
        l = e_self + jnp.sum(jnp.sum(e, axis=0), axis=1, keepdims=True)
        p_scr[...] = e / l[None]
        acc_scr[...] = jnp.where(own, rnd(e_self / l) * rnd(vnew_ref[0]), 0.0)

    @pl.when(j >= nstep)
    def _():
        group = 2 * nstep - 1 - j
        acc = acc_scr[...]
        for p in range(pg):
            vt = v_refs[p][0].reshape(W_A, psz).astype(bf16)
            acc = acc + _dotg(p_scr[group * pg + p].astype(bf16), vt, _NT)
        acc_scr[...] = acc

    @pl.when(j == 2 * nstep - 1)
    def _():
        o_ref[0] = jnp.sum(jnp.where(own, acc_scr[...], 0.0), axis=0, keepdims=True)


def _fox_decode(qbd, knew, vnew, lfnew, cache_kt, cache_vt, cache_lft, page_table):
    db = qbd.shape[0]
    psz = cache_kt.shape[-1]
    n_pages = page_table.shape[1]
    pg = PAGES_PER_STEP if n_pages % PAGES_PER_STEP == 0 else 1
    nstep = n_pages // pg
    pt = page_table.reshape(-1).astype(i32)

    def page_spec(mid, score_phase):
        def spec(p):
            def imap(bi, j, pt_ref):
                step = jnp.minimum(j, nstep - 1) if score_phase else jnp.maximum(j - nstep, 0)
                return (pt_ref[bi * n_pages + (nstep - 1 - step) * pg + p],) + (0,) * (1 + len(mid))
            return pl.BlockSpec((1,) + mid + (psz,), imap)
        return [spec(p) for p in range(pg)]

    row3 = lambda a: pl.BlockSpec((1,) + a.shape[1:], lambda bi, j, pt_ref: (bi, 0, 0))
    grid_spec = pltpu.PrefetchScalarGridSpec(
        num_scalar_prefetch=1,
        grid=(db, 2 * nstep),
        in_specs=[row3(qbd), row3(knew), row3(vnew), row3(lfnew)]
        + page_spec((H_A, DH_A), True) + page_spec((H_A, DH_A), False) + page_spec((H_A,), True),
        out_specs=pl.BlockSpec((1, 1, W_A), lambda bi, j, pt_ref: (bi, 0, 0)),
        scratch_shapes=[pltpu.VMEM((n_pages, H_A, psz), f32), pltpu.VMEM((H_A, 1), f32),
                        pltpu.VMEM((H_A, 1), f32), pltpu.VMEM((H_A, W_A), f32),
                        pltpu.VMEM((H_A, 1), f32)])
    return pl.pallas_call(
        functools.partial(_fox_decode_kernel, pg=pg, nstep=nstep),
        grid_spec=grid_spec,
        out_shape=jax.ShapeDtypeStruct((db, 1, W_A), f32),
        compiler_params=_cparams(("parallel", "arbitrary")),
        name="fox_decode",
    )(pt, qbd, knew, vnew, lfnew, *([cache_kt] * pg), *([cache_vt] * pg), *([cache_lft] * pg))


def _gla_kernel(qb_ref, kb_ref, vb_ref, rb_ref, sm_ref, wal_ref, bal_ref, gg_ref,
                ob_ref, sout_ref, s_scr, *, tt):
    i = pl.program_id(1)

    @pl.when(i == 0)
    def _():
        s_scr[...] = jnp.zeros_like(s_scr)

    q = qb_ref[0]
    k = kb_ref[0]
    z = _dot(sm_ref[0].astype(bf16), wal_ref[...]) + bal_ref[...]
    a = _log_sigmoid(z) * (1.0 / GLA_TAU)

    row = lax.broadcasted_iota(i32, (tt, tt), 0)
    col = lax.broadcasted_iota(i32, (tt, tt), 1)
    tril = jnp.where(col <= row, 1.0, 0.0).astype(bf16)
    hi, mid, lo = _split3(a)
    b = _dot(tril, hi) + _dot(tril, mid) + _dot(tril, lo)

    rowk = lax.broadcasted_iota(i32, (tt, WK_B), 0)
    hs = [slice(h * DK_B, (h + 1) * DK_B) for h in range(H_B)]
    amat = [jnp.zeros((tt, tt), f32) for _ in range(H_B)]
    edge = b
    for lvl in range(tt.bit_length() - 1):
        half = 1 << lvl
        upper = ((rowk >> lvl) & 1) == 1
        kt = jnp.where(upper, 0.0, k * jnp.exp(edge - b)).astype(bf16)
        edge_q = pltpu.roll(edge, half, 0)
        qt = jnp.where(upper, q * jnp.exp(b - edge_q), 0.0).astype(bf16)
        same = (row >> (lvl + 1)) == (col >> (lvl + 1))
        for h in range(H_B):
            amat[h] = amat[h] + jnp.where(same, _dotg(qt[:, hs[h]], kt[:, hs[h]], _NT), 0.0)
        edge = jnp.where(upper, edge, pltpu.roll(edge, tt - half, 0))
    q16 = q.astype(bf16)
    k16 = k.astype(bf16)
    for h in range(H_B):
        amat[h] = amat[h] + jnp.where(row == col, _dotg(q16[:, hs[h]], k16[:, hs[h]], _NT), 0.0)

    qe = (q * jnp.exp(b)).astype(bf16)
    ke = (k * jnp.exp(edge - b)).astype(bf16)
    e_last = jnp.exp(edge[0:1, :])
    v16 = vb_ref[0].astype(bf16)
    r = rb_ref[0]
    er = lax.broadcasted_iota(i32, (DK_B, DK_B), 0)
    ec = lax.broadcasted_iota(i32, (DK_B, DK_B), 1)
    for h in range(H_B):
        vs = slice(h * DV_B, (h + 1) * DV_B)
        vh = v16[:, vs]
        state = s_scr[h]
        o = _dot(amat[h].astype(bf16), vh) + _dot(qe[:, hs[h]], state.astype(bf16))
        decay_col = jnp.sum(
            jnp.where(er == ec, jnp.broadcast_to(e_last[:, hs[h]], (DK_B, DK_B)), 0.0),
            axis=1, keepdims=True)
        s_scr[h] = decay_col * state + _dotg(ke[:, hs[h]], vh, _TN)
        rh = r[:, vs]
        ob_ref[0, :, vs] = (_rms(o) * gg_ref[:, vs] * (rh * _sigmoid(rh))).astype(bf16)

    @pl.when(i == pl.num_programs(1) - 1)
    def _():
        sout_ref[0] = s_scr[...]


def _gla_prompt(qb, kb, vb, rb, sm, wal, bal, gg):
    b, t, _ = qb.shape
    tt = min(GLA_TILE, t)
    assert t % tt == 0 and tt & (tt - 1) == 0
    tok3 = lambda w: pl.BlockSpec((1, tt, w), lambda bi, i: (bi, i, 0))
    const2 = lambda a: pl.BlockSpec(a.shape, lambda bi, i: (0, 0))
    return pl.pallas_call(
        functools.partial(_gla_kernel, tt=tt),
        grid=(b, t // tt),
        in_specs=[tok3(WK_B), tok3(WK_B), tok3(WV_B), tok3(WV_B), tok3(LANES),
                  const2(wal), const2(bal), const2(gg)],
        out_specs=(tok3(WV_B),
                   pl.BlockSpec((1, H_B, DK_B, DV_B), lambda bi, i: (bi, 0, 0, 0))),
        out_shape=(jax.ShapeDtypeStruct((b, t, WV_B), bf16),
                   jax.ShapeDtypeStruct((b, H_B, DK_B, DV_B), f32)),
        scratch_shapes=[pltpu.VMEM((H_B, DK_B, DV_B), f32)],
        compiler_params=_cparams(("parallel", "arbitrary")),
        name="gla_prompt",
    )(qb, kb, vb, rb, sm, wal, bal, gg)


def _loga_kernel(sm_ref, wal_ref, bal_ref, o_ref):
    z = _dot(sm_ref[...].astype(bf16), wal_ref[...]) + bal_ref[...]
    o_ref[...] = _log_sigmoid(z) * (1.0 / GLA_TAU)


def _gla_decode_kernel(q_ref, k_ref, a_ref, v_ref, r_ref, s_ref, gg_ref, o_ref, so_ref):
    for h in range(H_B):
        q, k, a = q_ref[0, h], k_ref[0, h], a_ref[0, h]
        v = v_ref[0, h]
        state = s_ref[0, h]
        ea = jnp.exp(a)
        rnd = lambda z: z.astype(bf16).astype(f32)
        qk = jnp.sum(q * k, axis=0, keepdims=True)
        o = qk * v + jnp.sum(rnd(q * ea) * rnd(state), axis=0, keepdims=True)
        so_ref[0, h] = ea * state + k * v
        r = r_ref[0, h]
        o_ref[0, h] = _rms(o) * gg_ref[h] * (r * _sigmoid(r))


def _gla_decode(qb, kb, vb, rb, sm, wal, bal, gg, state):
    db = qb.shape[0]
    full = lambda a: pl.BlockSpec(a.shape, lambda: (0,) * a.ndim)
    loga = pl.pallas_call(
        _loga_kernel,
        in_specs=[full(sm), full(wal), full(bal)],
        out_specs=pl.BlockSpec((db, WK_B), lambda: (0, 0)),
        out_shape=jax.ShapeDtypeStruct((db, WK_B), f32),
        name="gla_log_decay",
    )(sm, wal, bal)
    col = lambda a: a.reshape(db, H_B, DK_B, 1)
    rowv = lambda a: a.reshape(db, H_B, 1, DV_B)
    gg4 = gg.reshape(H_B, 1, DV_B)
    b4 = lambda shp: pl.BlockSpec((1,) + shp, lambda bi: (bi, 0, 0, 0))
    o, s_new = pl.pallas_call(
        _gla_decode_kernel,
        grid=(db,),
        in_specs=[b4((H_B, DK_B, 1))] * 3 + [b4((H_B, 1, DV_B))] * 2
        + [b4((H_B, DK_B, DV_B)), pl.BlockSpec(gg4.shape, lambda bi: (0, 0, 0))],
        out_specs=(b4((H_B, 1, DV_B)), b4((H_B, DK_B, DV_B))),
        out_shape=(jax.ShapeDtypeStruct((db, H_B, 1, DV_B), f32),
                   jax.ShapeDtypeStruct((db, H_B, DK_B, DV_B), f32)),
        compiler_params=_cparams(("parallel",)),
        name="gla_decode",
    )(col(qb), col(kb), col(loga), rowv(vb), rowv(rb), state, gg4)
    return o.reshape(db, WV_B), s_new


def _mix_kernel(x_ref, oa_ref, ob_ref, ga_ref, gb_ref, gt1_ref, sc2_ref, sh2_ref, gf_ref,
                wba_ref, wbb_ref, wo_ref, wr_ref, br_ref, x1_ref, h2_ref, lg_ref, *, nsub):
    x = x_ref[0]
    tm = x.shape[0]
    ya = _dotg(oa_ref[0].astype(bf16), wba_ref[...], _TN)
    yb = _dot(ob_ref[0].astype(bf16), wbb_ref[...])
    m = _sigmoid(ga_ref[0]) * ya + _sigmoid(gb_ref[0]) * yb
    x1 = x + gt1_ref[0] * _dot(m.astype(bf16), wo_ref[...])
    x1_ref[...] = x1
    h2 = (_rms(x1) * gf_ref[...]) * sc2_ref[0] + sh2_ref[0]
    for s in range(nsub):
        h2_ref[pl.ds(s, tm, stride=nsub), :] = h2[:, s * LANES:(s + 1) * LANES]
    lg_ref[...] = _dot(h2.astype(bf16), wr_ref[...]) + br_ref[...]


def _mix(x, oa, ob, ga, gb, gt1, sc2, sh2, g_ffn, wba, wbb, wo, wr, br):
    b, t, d = x.shape
    nt = b * t
    nsub = d // LANES
    tm = min(TOKEN_TILE, t)
    assert t % tm == 0
    rows = gt1.shape[1]
    mod_block = (1, 1, d) if rows == 1 else (1, tm, d)
    mod_map = (lambda bi, i: (bi, 0, 0)) if rows == 1 else (lambda bi, i: (bi, i, 0))
    mod = pl.BlockSpec(mod_block, mod_map)
    tok3 = lambda w: pl.BlockSpec((1, tm, w), lambda bi, i: (bi, i, 0))
    const2 = lambda a: pl.BlockSpec(a.shape, lambda bi, i: (0, 0))
    nti = t // tm
    blk = lambda bi, i: (bi * nti + i, 0)
    g2 = g_ffn.reshape(1, d)
    ins = [x, oa, ob, ga, gb, gt1, sc2, sh2, g2, wba, wbb, wo, wr, br]
    in_specs = [tok3(d), pl.BlockSpec((1, W_A, tm), lambda bi, i: (bi, 0, i)), tok3(WV_B),
                tok3(d), tok3(d), mod, mod, mod,
                const2(g2), const2(wba), const2(wbb), const2(wo), const2(wr), const2(br)]
    return pl.pallas_call(
        functools.partial(_mix_kernel, nsub=nsub),
        grid=(b, nti),
        in_specs=in_specs,
        out_specs=(pl.BlockSpec((tm, d), blk), pl.BlockSpec((tm * nsub, LANES), blk),
                   pl.BlockSpec((tm, LANES), blk)),
        out_shape=(jax.ShapeDtypeStruct((nt, d), f32),
                   jax.ShapeDtypeStruct((nt * nsub, LANES), f32),
                   jax.ShapeDtypeStruct((nt, LANES), f32)),
        compiler_params=_cparams(("parallel", "arbitrary")),
        name="branch_mix",
    )(*ins)


def _route_kernel(lg_ref, cin_ref, rec_ref, gate_ref, cnt_ref, carry_scr, *, tr):
    @pl.when(pl.program_id(0) == 0)
    def _():
        carry_scr[...] = cin_ref[...]

    lane = lax.broadcasted_iota(i32, (tr, LANES), 1)
    lanef = lane.astype(f32)
    v = jnp.where(lane < N_EXPERTS, lg_ref[...], -jnp.inf)
    onehot = jnp.zeros((tr, LANES), f32)
    ids, vals = [], []
    for _ in range(TOP_K):
        mx = jnp.max(v, axis=1, keepdims=True)
        idx = jnp.min(jnp.where(v == mx, lanef, float(LANES)), axis=1, keepdims=True)
        sel = lanef == idx
        onehot = jnp.where(sel, 1.0, onehot)
        v = jnp.where(sel, -jnp.inf, v)
        ids.append(idx)
        vals.append(mx)
    es = [jnp.exp(vk - vals[0]) for vk in vals]
    tot = es[0]
    for e in es[1:]:
        tot = tot + e

    r2 = lax.broadcasted_iota(i32, (tr, tr), 0)
    c2 = lax.broadcasted_iota(i32, (tr, tr), 1)
    before = jnp.where(c2 < r2, 1.0, 0.0).astype(bf16)
    pref = _dot(before, onehot.astype(bf16)) + carry_scr[...]
    carry_scr[...] = carry_scr[...] + jnp.sum(onehot, axis=0, keepdims=True)

    rec = jnp.zeros((tr, LANES), f32)
    gate = jnp.zeros((tr, LANES), f32)
    for k in range(TOP_K):
        rank = jnp.sum(jnp.where(lanef == ids[k], pref, 0.0), axis=1, keepdims=True)
        rec = jnp.where(lane == k, rank * float(N_EXPERTS) + ids[k], rec)
        gate = jnp.where(lane == k, es[k] / tot, gate)
    rec_ref[...] = rec.astype(i32)
    gate_ref[...] = gate
    cnt_ref[...] = carry_scr[...]


def _route(logits, counts_in):
    nt = logits.shape[0]
    tr = _divisor_tile(nt, 512, SUBLANES)
    tile = pl.BlockSpec((tr, LANES), lambda i: (i, 0))
    cnt = pl.BlockSpec((1, LANES), lambda i: (0, 0))
    return pl.pallas_call(
        functools.partial(_route_kernel, tr=tr),
        grid=(nt // tr,),
        in_specs=[tile, cnt],
        out_specs=(tile, tile, cnt),
        out_shape=(jax.ShapeDtypeStruct((nt, LANES), i32),
                   jax.ShapeDtypeStruct((nt, LANES), f32),
                   jax.ShapeDtypeStruct((1, LANES), f32)),
        scratch_shapes=[pltpu.VMEM((1, LANES), f32)],
        compiler_params=_cparams(("arbitrary",)),
        name="route_topk",
    )(logits, counts_in)


def _dest_row(rec_ref, start_ref, i):
    packed = rec_ref[i]
    return (start_ref[jnp.bitwise_and(packed, N_EXPERTS - 1)]
            + jnp.right_shift(packed, N_EXPERTS.bit_length() - 1))


def _dispatch_kernel(rec_ref, start_ref, h_ref, xin_hbm, xout_hbm, sem, *, nsub, ch):
    del xin_hbm
    base = pl.program_id(0) * ch

    def row_copy(n, k):
        src = pl.multiple_of(n * nsub, nsub)
        dst = pl.multiple_of(_dest_row(rec_ref, start_ref, (base + n) * TOP_K + k) * nsub, nsub)
        return pltpu.make_async_copy(h_ref.at[pl.ds(src, nsub)], xout_hbm.at[pl.ds(dst, nsub)],
                                     sem)

    def body(n, carry):
        for k in range(TOP_K):
            row_copy(n, k).start()
        return carry

    lax.fori_loop(0, ch, body, 0, unroll=4)
    total = ch * TOP_K * nsub
    pltpu.make_async_copy(xout_hbm.at[pl.ds(0, total)], xout_hbm.at[pl.ds(0, total)], sem).wait()


def _dispatch(rec_flat, pad_start, h2rows, xrows, nsub):
    n_tok = rec_flat.shape[0] // TOP_K
    ch = min(DISPATCH_CHUNK, n_tok)
    assert n_tok % ch == 0
    grid_spec = pltpu.PrefetchScalarGridSpec(
        num_scalar_prefetch=2,
        grid=(n_tok // ch,),
        in_specs=[pl.BlockSpec((ch * nsub, LANES), lambda i, rr, sr: (i, 0)),
                  pl.BlockSpec(memory_space=pl.ANY)],
        out_specs=pl.BlockSpec(memory_space=pl.ANY),
        scratch_shapes=[pltpu.SemaphoreType.DMA(())])
    return pl.pallas_call(
        functools.partial(_dispatch_kernel, nsub=nsub, ch=ch),
        grid_spec=grid_spec,
        out_shape=jax.ShapeDtypeStruct(xrows.shape, xrows.dtype),
        input_output_aliases={3: 0},
        compiler_params=_cparams(("arbitrary",)),
        name="moe_dispatch",
    )(rec_flat, pad_start, h2rows, xrows)


def _expert_kernel(be_ref, nused_ref, x_ref, wgu_ref, bgu_ref, wd_ref, bd_ref, y_ref,
                   wgu_b, wd_b, *, tmx, nsub, dff, chunk):
    r = pl.program_id(0)
    live = r < nused_ref[0]

    @pl.when(jnp.logical_not(live))
    def _():
        y_ref[...] = jnp.zeros_like(y_ref)

    new_expert = jnp.logical_or(r == 0, be_ref[r] != be_ref[jnp.maximum(r - 1, 0)])

    @pl.when(jnp.logical_and(live, new_expert))
    def _():
        for c in range(0, wgu_b.shape[0], chunk):
            wgu_b[c:c + chunk, :] = wgu_ref[0, c:c + chunk, :].astype(bf16)
        for c in range(0, wd_b.shape[0], chunk):
            wd_b[c:c + chunk, :] = wd_ref[0, c:c + chunk, :].astype(bf16)

    @pl.when(live)
    def _():
        x = jnp.concatenate(
            [x_ref[pl.ds(s, tmx, stride=nsub), :].astype(bf16) for s in range(nsub)], axis=1)
        gu = _dot(x, wgu_b[...]) + bgu_ref[0]
        gate = jnp.minimum(gu[:, :dff], SWIGLU_LIMIT)
        up = jnp.clip(gu[:, dff:], -SWIGLU_LIMIT, SWIGLU_LIMIT)
        glu = gate * _sigmoid(SWIGLU_ALPHA * gate)
        y = _dot(((up + 1.0) * glu).astype(bf16), wd_b[...]) + bd_ref[0]
        for s in range(nsub):
            y_ref[pl.ds(s, tmx, stride=nsub), :] = y[:, s * LANES:(s + 1) * LANES]


def _experts(block_e, nused, xrows, wgu, bgu, wd, bd, nsub):
    n_blocks = block_e.shape[0]
    tmx = EXPERT_TILE
    e, d, dff2 = wgu.shape
    dff = dff2 // 2
    live = lambda r, nu: jnp.minimum(r, nu[0] - 1)
    rows = pl.BlockSpec((tmx * nsub, LANES), lambda r, be, nu: (live(r, nu), 0))
    per_e = lambda shp: pl.BlockSpec((1,) + shp, lambda r, be, nu: (be[live(r, nu)], 0, 0))
    grid_spec = pltpu.PrefetchScalarGridSpec(
        num_scalar_prefetch=2,
        grid=(n_blocks,),
        in_specs=[rows, per_e((d, dff2)), per_e((1, dff2)), per_e((dff, d)), per_e((1, d))],
        out_specs=pl.BlockSpec((tmx * nsub, LANES), lambda r, be, nu: (r, 0)),
        scratch_shapes=[pltpu.VMEM((d, dff2), bf16), pltpu.VMEM((dff, d), bf16)])
    return pl.pallas_call(
        functools.partial(_expert_kernel, tmx=tmx, nsub=nsub, dff=dff, chunk=min(256, d, dff)),
        grid_spec=grid_spec,
        out_shape=jax.ShapeDtypeStruct(xrows.shape, f32),
        compiler_params=_cparams(("arbitrary",)),
        name="moe_experts",
    )(block_e, nused, xrows, wgu, bgu.reshape(e, 1, dff2), wd, bd.reshape(e, 1, d))


def _combine_kernel(rec_ref, start_ref, y_hbm, gate_ref, x1_ref, gt2_ref, gfin_ref, out_ref,
                    buf, sem, *, tc, nsub):
    base = pl.program_id(0) * tc

    def row_copy(n, k):
        src = pl.multiple_of(_dest_row(rec_ref, start_ref, (base + n) * TOP_K + k) * nsub, nsub)
        dst = pl.multiple_of((k * tc + n) * nsub, nsub)
        return pltpu.make_async_copy(y_hbm.at[pl.ds(src, nsub)], buf.at[pl.ds(dst, nsub)], sem)

    def body(n, carry):
        for k in range(TOP_K):
            row_copy(n, k).start()
        return carry

    lax.fori_loop(0, tc, body, 0, unroll=4)
    pltpu.make_async_copy(y_hbm.at[pl.ds(0, buf.shape[0])], buf, sem).wait()

    g = gate_ref[...]
    cols = []
    for s in range(nsub):
        acc = None
        for k in range(TOP_K):
            term = g[:, k:k + 1] * buf[pl.ds(k * tc * nsub + s, tc, stride=nsub), :]
            acc = term if acc is None else acc + term
        cols.append(acc)
    y = jnp.concatenate(cols, axis=1)
    out_ref[...] = _rms(x1_ref[...] + gt2_ref[0] * y) * gfin_ref[...]


def _combine(rec_flat, pad_start, yrows, gates, x1, gt2, g_final, t_per_mod, nsub):
    n_tok, d = x1.shape
    tc = min(TOKEN_TILE, n_tok)
    assert n_tok % tc == 0
    rows = gt2.shape[1]
    if rows == 1:
        per_mod = t_per_mod // tc
        mod = pl.BlockSpec((1, 1, d), lambda i, rr, sr: (i // per_mod, 0, 0))
    else:
        mod = pl.BlockSpec((1, tc, d), lambda i, rr, sr: (0, i, 0))
    blk = lambda i, rr, sr: (i, 0)
    gfin = g_final.reshape(1, d)
    grid_spec = pltpu.PrefetchScalarGridSpec(
        num_scalar_prefetch=2,
        grid=(n_tok // tc,),
        in_specs=[pl.BlockSpec(memory_space=pl.ANY), pl.BlockSpec((tc, LANES), blk),
                  pl.BlockSpec((tc, d), blk), mod,
                  pl.BlockSpec((1, d), lambda i, rr, sr: (0, 0))],
        out_specs=pl.BlockSpec((tc, d), blk),
        scratch_shapes=[pltpu.VMEM((TOP_K * tc * nsub, LANES), f32), pltpu.SemaphoreType.DMA(())])
    return pl.pallas_call(
        functools.partial(_combine_kernel, tc=tc, nsub=nsub),
        grid_spec=grid_spec,
        out_shape=jax.ShapeDtypeStruct((n_tok, d), f32),
        compiler_params=_cparams(("arbitrary",)),
        name="moe_combine",
    )(rec_flat, pad_start, yrows, gates, x1, gt2, gfin)


def kernel(x_prompt, x_sample, c_prompt, c_sample, cache_k, cache_v, cache_logf, state_gla,
           page_table, g_mix, g_ffn, g_final, w_ada, b_ada, w_in, b_f, w_alpha, b_alpha, g_gla,
           w_branch, w_o, w_router, b_router, w_gu, b_gu, w_d, b_d):
    depth = g_mix.shape[0]
    assert depth == 1, "one decoder layer"
    bp, t, d = x_prompt.shape
    db = x_sample.shape[0]
    assert x_sample.shape[1] == 1
    nsub = d // LANES
    ntp = bp * t
    nt = ntp + db

    nmod = bp + db
    pad = (-nmod) % SUBLANES
    c_all = jnp.concatenate([c_prompt, c_sample, jnp.zeros((pad, d), f32)], axis=0)
    mod = _adaln(c_all, w_ada[0], b_ada[0])
    sh1, sc1, gt1, sh2, sc2, gt2 = [mod[:, i * d:(i + 1) * d] for i in range(6)]
    grp_p = lambda a: a[:bp].reshape(bp, 1, d)
    grp_s = lambda a: a[bp:nmod].reshape(1, db, d)

    w_proj = _inproj_weights(w_in[0], d)
    wal = jnp.zeros((LANES, WK_B), f32).at[H_A:H_A + GLA_RANK].set(w_alpha[0]).astype(bf16)
    bal = b_alpha[0].reshape(1, WK_B)
    gg = g_gla[0].reshape(1, WV_B)
    wba = w_branch[0, :W_A].astype(bf16)
    wbb = w_branch[0, W_A:].astype(bf16)
    wo = w_o[0].astype(bf16)
    wr = jnp.zeros((d, LANES), f32).at[:, :N_EXPERTS].set(w_router[0]).astype(bf16)
    br = jnp.zeros((1, LANES), f32).at[0, :N_EXPERTS].set(b_router[0])

    (qat, kat, vt, ka, va, qb, kb, vb, rb, ga, gb, sm, fat) = _inproj(
        x_prompt, grp_p(1.0 + sc1), grp_p(sh1), g_mix[0], w_proj, True)
    logft, cumt = _logf(fat, b_f[0], True)
    oa = _fox_prompt(qat, kat, vt, jnp.transpose(cumt, (0, 2, 1)))
    ob, gla_p = _gla_prompt(qb, kb, vb, rb, sm, wal, bal, gg)
    x1_p, h2_p, lg_p = _mix(x_prompt, oa, ob, ga, gb, grp_p(gt1), grp_p(1.0 + sc2), grp_p(sh2),
                            g_ffn[0], wba, wbb, wo, wr, br)

    xs = x_sample.reshape(1, db, d)
    (qat_s, ka_s, va_s, qb_s, kb_s, vb_s, rb_s, ga_s, gb_s, sm_s, fat_s) = _inproj(
        xs, grp_s(1.0 + sc1), grp_s(sh1), g_mix[0], w_proj, False)
    logft_s, = _logf(fat_s, b_f[0], False)
    lfnew = jnp.transpose(logft_s[0], (1, 0)).reshape(db, H_A, 1)
    qbd = jnp.einsum('hnd,hc->nchd', qat_s[0], jnp.eye(H_A, dtype=bf16)).reshape(db, H_A, W_A)
    oa_s = _fox_decode(qbd, ka_s.reshape(db, 1, W_A), va_s.reshape(db, 1, W_A), lfnew,
                       jnp.transpose(cache_k[0], (0, 2, 3, 1)),
                       jnp.transpose(cache_v[0], (0, 2, 3, 1)),
                       jnp.transpose(cache_logf[0], (0, 2, 1)), page_table)
    ob_s, gla_s = _gla_decode(qb_s[0], kb_s[0], vb_s[0], rb_s[0], sm_s[0], wal, bal, gg,
                              state_gla[0])
    x1_s, h2_s, lg_s = _mix(xs, oa_s.reshape(db, W_A).T[None], ob_s.reshape(1, db, WV_B), ga_s, gb_s,
                            grp_s(gt1), grp_s(1.0 + sc2), grp_s(sh2), g_ffn[0],
                            wba, wbb, wo, wr, br)

    rec_p, gates_p, cnt_p = _route(lg_p, jnp.zeros((1, LANES), f32))
    rec_s, gates_s, cnt = _route(lg_s, cnt_p)
    counts = cnt[0, :N_EXPERTS].astype(i32)
    padded = (counts + EXPERT_TILE - 1) // EXPERT_TILE * EXPERT_TILE
    pad_end = jnp.cumsum(padded).astype(i32)
    pad_start = pad_end - padded

    rflat_p = rec_p[:, :TOP_K].reshape(-1)
    rflat_s = rec_s[:, :TOP_K].reshape(-1)
    n_blocks = -(-(nt * TOP_K + N_EXPERTS * (EXPERT_TILE - 1)) // EXPERT_TILE)
    block_start = jnp.arange(n_blocks, dtype=i32) * EXPERT_TILE
    block_e = jnp.minimum(jnp.sum(pad_end[None, :] <= block_start[:, None], axis=1),
                          N_EXPERTS - 1).astype(i32)
    nused = (pad_end[-1:] // EXPERT_TILE).astype(i32)

    xrows = jnp.zeros((n_blocks * EXPERT_TILE * nsub, LANES), f32)
    xrows = _dispatch(rflat_p, pad_start, h2_p, xrows, nsub)
    xrows = _dispatch(rflat_s, pad_start, h2_s, xrows, nsub)
    yrows = _experts(block_e, nused, xrows, w_gu[0], b_gu[0], w_d[0], b_d[0], nsub)
    y_p = _combine(rflat_p, pad_start, yrows, gates_p, x1_p, grp_p(gt2), g_final, t, nsub)
    y_s = _combine(rflat_s, pad_start, yrows, gates_s, x1_s, grp_s(gt2), g_final, 1, nsub)

    logf_p = jnp.transpose(logft, (0, 2, 1))
    return (y_p.reshape(bp, t, d), y_s.reshape(db, 1, d),
            jnp.transpose(ka, (0, 3, 1, 2))[None], jnp.transpose(va, (0, 3, 1, 2))[None],
            logf_p.reshape(1, bp, t, H_A), gla_p.reshape(1, bp, H_B, DK_B, DV_B),
            ka_s.reshape(1, db, 1, H_A, DH_A), va_s.reshape(1, db, 1, H_A, DH_A),
            lfnew.reshape(1, db, 1, H_A), gla_s.reshape(1, db, H_B, DK_B, DV_B))
```

```python
import functools

import jax
import jax.numpy as jnp
from jax import lax
from jax.experimental import pallas as pl
from jax.experimental.pallas import tpu as pltpu

f32 = jnp.float32
bf16 = jnp.bfloat16
i32 = jnp.int32

H_A, DH_A = 8, 64
H_B, DK_B, DV_B = 4, 64, 128
GLA_RANK = 16
GLA_TAU = 16.0
N_EXPERTS = 32
TOP_K = 4
SWIGLU_LIMIT = 7.0
SWIGLU_ALPHA = 1.702
RMS_EPS = 1e-6
W_A = H_A * DH_A
WK_B = H_B * DK_B
WV_B = H_B * DV_B

LANES = 128
SUBLANES = 8
VMEM_LIMIT_MB = 56

TOKEN_TILE = 256
ATTN_TILE = 256
GLA_TILE = 256
EXPERT_TILE = 256
PAGES_PER_STEP = 16
DISPATCH_CHUNK = 256

_NT = (((1,), (1,)), ((), ()))
_TN = (((0,), (0,)), ((), ()))


def _cparams(semantics, vmem_mb=VMEM_LIMIT_MB):
    return pltpu.CompilerParams(dimension_semantics=semantics,
                                vmem_limit_bytes=vmem_mb << 20)


def _sigmoid(x):
    return 1.0 / (1.0 + jnp.exp(-x))


def _log_sigmoid(x):
    return jnp.minimum(x, 0.0) - jnp.log1p(jnp.exp(-jnp.abs(x)))


def _split3(x):
    hi = x.astype(bf16)
    r = x - hi.astype(f32)
    mid = r.astype(bf16)
    lo = (r - mid.astype(f32)).astype(bf16)
    return hi, mid, lo


def _dot(a, b):
    return jnp.dot(a, b, preferred_element_type=f32)


def _dotg(a, b, dims):
    return lax.dot_general(a, b, dims, preferred_element_type=f32)


def _rms(x):
    return x * lax.rsqrt(jnp.mean(x * x, axis=-1, keepdims=True) + RMS_EPS)


def _divisor_tile(n, cap, mult):
    best = None
    t = mult
    while t <= min(n, cap):
        if n % t == 0:
            best = t
        t += mult
    assert best is not None, (n, cap, mult)
    return best


def _mod_kernel(c_ref, w_ref, b_ref, o_ref):
    c = c_ref[...]
    s = (c * _sigmoid(c)).astype(bf16)
    o_ref[...] = _dot(s, w_ref[...].astype(bf16)) + b_ref[...]


def _adaln(c_all, w_ada, b_ada):
    r, d = c_all.shape
    n = w_ada.shape[1]
    tn = _divisor_tile(n, 1536, LANES)
    return pl.pallas_call(
        _mod_kernel,
        grid=(n // tn,),
        in_specs=[pl.BlockSpec((r, d), lambda j: (0, 0)),
                  pl.BlockSpec((d, tn), lambda j: (0, j)),
                  pl.BlockSpec((1, tn), lambda j: (0, j))],
        out_specs=pl.BlockSpec((r, tn), lambda j: (0, j)),
        out_shape=jax.ShapeDtypeStruct((r, n), f32),
        compiler_params=_cparams(("arbitrary",)),
        name="adaln_mod",
    )(c_all, w_ada, b_ada.reshape(1, n))


def _inproj_kernel(x_ref, sc_ref, sh_ref, g_ref, wbig_ref, wsm_ref, wfa_ref, wkt_ref, wvt_ref,
                   *out_refs, d, time_minor):
    if time_minor:
        (qat_ref, kat_ref, vt_ref, ka_ref, va_ref, qb_ref, kb_ref, vb_ref, rb_ref,
         ga_ref, gb_ref, sm_ref, fat_ref) = out_refs
    else:
        (qat_ref, ka_ref, va_ref, qb_ref, kb_ref, vb_ref, rb_ref,
         ga_ref, gb_ref, sm_ref, fat_ref) = out_refs
    x = x_ref[0]
    h = (_rms(x) * g_ref[...]) * sc_ref[0] + sh_ref[0]
    hb = h.astype(bf16)
    off = [0]

    def mm(width):
        r = _dot(hb, wbig_ref[:, off[0]:off[0] + width])
        off[0] += width
        return r

    qa = mm(W_A)
    ka = mm(W_A)
    va = mm(W_A)
    for hh in range(H_A):
        sl = slice(hh * DH_A, (hh + 1) * DH_A)
        qat_ref[0, hh] = qa[:, sl].astype(bf16)
        if time_minor:
            kat_ref[0, hh] = ka[:, sl].astype(bf16)
    if time_minor:
        tm = x.shape[0]
        kt = _dotg(wkt_ref[...], hb, _NT).reshape(H_A, DH_A, tm)
        vt = _dotg(wvt_ref[...], hb, _NT).reshape(H_A, DH_A, tm)
        ka_ref[0] = kt
        va_ref[0] = vt
        vt_ref[0] = vt.astype(bf16)
    else:
        ka_ref[0] = ka
        va_ref[0] = va
    qb_ref[0] = mm(WK_B)
    kb_ref[0] = mm(WK_B)
    vb_ref[0] = mm(WV_B)
    rb_ref[0] = mm(WV_B)
    ga_ref[0] = mm(d)
    gb_ref[0] = mm(d)
    sm_ref[0] = _dot(hb, wsm_ref[...])
    fat_ref[0] = _dotg(wfa_ref[...].astype(bf16), hb, _NT)


def _inproj_weights(w_in, d):
    sizes = (W_A, W_A, W_A, H_A, WK_B, WK_B, WV_B, WV_B, GLA_RANK, d, d)
    offs = [0]
    for s in sizes:
        offs.append(offs[-1] + s)
    seg = lambda i: w_in[:, offs[i]:offs[i + 1]]
    wbig = jnp.concatenate(
        [seg(0) * (DH_A ** -0.5), seg(1), seg(2), seg(4) * (DK_B ** -0.5), seg(5),
         seg(6), seg(7), seg(9), seg(10)], axis=1).astype(bf16)
    wsm = jnp.concatenate(
        [seg(3), seg(8), jnp.zeros((d, LANES - H_A - GLA_RANK), f32)], axis=1).astype(bf16)
    wfa = seg(3).T
    wkt = seg(1).T.astype(bf16)
    wvt = seg(2).T.astype(bf16)
    return wbig, wsm, wfa, wkt, wvt


def _inproj(x, sc, sh, g_mix, weights, time_minor):
    wbig, wsm, wfa, wkt, wvt = weights
    b, t, d = x.shape
    tm = min(TOKEN_TILE, t)
    assert t % tm == 0
    rows = sc.shape[1]
    mod_block = (1, 1, d) if rows == 1 else (1, tm, d)
    mod_map = (lambda bi, i: (bi, 0, 0)) if rows == 1 else (lambda bi, i: (bi, i, 0))
    tok3 = lambda w: pl.BlockSpec((1, tm, w), lambda bi, i: (bi, i, 0))
    head4 = pl.BlockSpec((1, H_A, tm, DH_A), lambda bi, i: (bi, 0, i, 0))
    const2 = lambda a: pl.BlockSpec(a.shape, lambda bi, i: (0, 0))
    sds = jax.ShapeDtypeStruct
    headt = pl.BlockSpec((1, H_A, DH_A, tm), lambda bi, i: (bi, 0, 0, i))
    if time_minor:
        kv_shape = [sds((b, H_A, t, DH_A), bf16), sds((b, H_A, DH_A, t), bf16),
                    sds((b, H_A, DH_A, t), f32), sds((b, H_A, DH_A, t), f32)]
        kv_specs = [head4, headt, headt, headt]
    else:
        kv_shape = [sds((b, t, W_A), f32), sds((b, t, W_A), f32)]
        kv_specs = [tok3(W_A), tok3(W_A)]
    out_shape = tuple(
        [sds((b, H_A, t, DH_A), bf16)] + kv_shape
        + [sds((b, t, WK_B), f32), sds((b, t, WK_B), f32),
           sds((b, t, WV_B), f32), sds((b, t, WV_B), f32),
           sds((b, t, d), f32), sds((b, t, d), f32),
           sds((b, t, LANES), f32), sds((b, H_A, t), f32)])
    out_specs = tuple(
        [head4] + kv_specs
        + [tok3(WK_B), tok3(WK_B), tok3(WV_B), tok3(WV_B), tok3(d), tok3(d), tok3(LANES),
           pl.BlockSpec((1, H_A, tm), lambda bi, i: (bi, 0, i))])
    g2 = g_mix.reshape(1, d)
    return pl.pallas_call(
        functools.partial(_inproj_kernel, d=d, time_minor=time_minor),
        grid=(b, t // tm),
        in_specs=[tok3(d), pl.BlockSpec(mod_block, mod_map), pl.BlockSpec(mod_block, mod_map),
                  const2(g2), const2(wbig), const2(wsm), const2(wfa), const2(wkt), const2(wvt)],
        out_specs=out_specs,
        out_shape=out_shape,
        compiler_params=_cparams(("parallel", "arbitrary")),
        name="inproj",
    )(x, sc, sh, g2, wbig, wsm, wfa, wkt, wvt)


def _logf_kernel(fa_ref, bf_ref, logf_ref, *cum_refs, t, cb):
    lf = _log_sigmoid(fa_ref[...] + bf_ref[...])
    logf_ref[...] = lf
    if not cum_refs:
        return
    cum_ref, = cum_refs
    r = lax.broadcasted_iota(i32, (cb, cb), 0)
    c = lax.broadcasted_iota(i32, (cb, cb), 1)
    triu = jnp.where(r <= c, 1.0, 0.0).astype(bf16)
    carry = jnp.zeros((lf.shape[0], 1), f32)
    for j in range(t // cb):
        hi, mid, lo = _split3(lf[:, j * cb:(j + 1) * cb])
        blk = _dot(hi, triu) + _dot(mid, triu) + _dot(lo, triu) + carry
        cum_ref[:, j * cb:(j + 1) * cb] = blk
        carry = blk[:, cb - 1:cb]


def _logf(fat, b_f, with_cumsum):
    b, _, t = fat.shape
    rows = b * H_A
    fa2 = fat.reshape(rows, t)
    bf2 = jnp.tile(b_f.reshape(H_A, 1), (b, 1))
    cb = min(256, t)
    full = lambda a: pl.BlockSpec(a.shape, lambda: (0,) * a.ndim)
    n_out = 2 if with_cumsum else 1
    outs = pl.pallas_call(
        functools.partial(_logf_kernel, t=t, cb=cb),
        in_specs=[full(fa2), full(bf2)],
        out_specs=tuple(pl.BlockSpec((rows, t), lambda: (0, 0)) for _ in range(n_out)),
        out_shape=tuple(jax.ShapeDtypeStruct((rows, t), f32) for _ in range(n_out)),
        name="log_forget",
    )(fa2, bf2)
    return tuple(o.reshape(b, H_A, t) for o in outs)


def _bias_lanes(col, ones_first):
    hi, mid, lo = [p.astype(f32) for p in _split3(col)]
    lane = lax.broadcasted_iota(i32, (col.shape[0], DH_A), 1)
    base = 3 if ones_first else 0
    parts = jnp.where(lane == base, hi, jnp.where(lane == base + 1, mid,
                      jnp.where(lane == base + 2, lo, 0.0)))
    ones = (lane < 3) if ones_first else ((lane >= 3) & (lane < 6))
    return jnp.where(ones, 1.0, parts)


def _fox_kernel(q_ref, k_ref, vt_ref, cum_ref, o_ref, kaug, qaug, *, tq):
    i = pl.program_id(1)

    @pl.when(i == 0)
    def _():
        cum_all = cum_ref[0]
        for h in range(H_A):
            kaug[h, :, :DH_A] = k_ref[0, h]
            kaug[h, :, DH_A:] = _bias_lanes(-cum_all[:, h:h + 1], False).astype(bf16)

    cum_q = cum_ref[0, pl.ds(pl.multiple_of(i * tq, tq), tq), :]
    for h in range(H_A):
        qaug[h, :, :DH_A] = q_ref[0, h]
        qaug[h, :, DH_A:] = _bias_lanes(cum_q[:, h:h + 1], True).astype(bf16)
    key = lax.broadcasted_iota(i32, (tq, tq), 0)
    qry = lax.broadcasted_iota(i32, (tq, tq), 1)

    def step(j, carry, masked):
        start = pl.multiple_of(j * tq, tq)
        scores = [_dotg(kaug[h, pl.ds(start, tq), :], qaug[h], _NT) for h in range(H_A)]
        stats = []
        for h in range(H_A):
            m_old, l_old, _ = carry[h]
            s = scores[h]
            if masked:
                s = jnp.where(key <= qry, s, -jnp.inf)
            m_new = jnp.maximum(m_old, jnp.max(s, axis=0, keepdims=True))
            alpha = jnp.exp(m_old - m_new)
            p = jnp.exp(s - m_new)
            l_new = alpha * l_old + jnp.sum(p, axis=0, keepdims=True)
            stats.append((m_new, l_new, alpha, p.astype(bf16)))
        out = []
        for h in range(H_A):
            m_new, l_new, alpha, p = stats[h]
            acc_new = alpha * carry[h][2] + _dot(vt_ref[0, h, :, pl.ds(start, tq)], p)
            out.append((m_new, l_new, acc_new))
        return tuple(out)

    init = tuple((jnp.full((1, tq), -jnp.inf, f32), jnp.zeros((1, tq), f32),
                  jnp.zeros((DH_A, tq), f32)) for _ in range(H_A))
    carry = lax.fori_loop(0, i, functools.partial(step, masked=False), init)
    carry = step(i, carry, True)
    for h in range(H_A):
        _, l, acc = carry[h]
        o_ref[0, h * DH_A:(h + 1) * DH_A, :] = (acc / l).astype(bf16)


def _fox_prompt(qat, kat, vt, cum):
    b, _, t, _ = qat.shape
    tq = min(ATTN_TILE, t)
    assert t % tq == 0
    return pl.pallas_call(
        functools.partial(_fox_kernel, tq=tq),
        grid=(b, t // tq),
        in_specs=[pl.BlockSpec((1, H_A, tq, DH_A), lambda bi, i: (bi, 0, i, 0)),
                  pl.BlockSpec((1, H_A, t, DH_A), lambda bi, i: (bi, 0, 0, 0)),
                  pl.BlockSpec((1, H_A, DH_A, t), lambda bi, i: (bi, 0, 0, 0)),
                  pl.BlockSpec((1, t, H_A), lambda bi, i: (bi, 0, 0))],
        out_specs=pl.BlockSpec((1, W_A, tq), lambda bi, i: (bi, 0, i)),
        out_shape=jax.ShapeDtypeStruct((b, W_A, t), bf16),
        scratch_shapes=[pltpu.VMEM((H_A, t, 2 * DH_A), bf16), pltpu.VMEM((H_A, tq, 2 * DH_A), bf16)],
        compiler_params=_cparams(("parallel", "arbitrary")),
        name="fox_prompt",
    )(qat, kat, vt, cum)


def _fox_decode_kernel(pt_ref, q_ref, knew_ref, vnew_ref, lfnew_ref, *refs, pg, nstep):
    del pt_ref
    k_refs, v_refs, lf_refs = refs[:pg], refs[pg:2 * pg], refs[2 * pg:3 * pg]
    o_ref = refs[3 * pg]
    p_scr, m_scr, self_scr, acc_scr, carry_scr = refs[3 * pg + 1:]
    j = pl.program_id(1)
    q = q_ref[0]
    psz = lf_refs[0].shape[2]
    lane = lax.broadcasted_iota(i32, (H_A, W_A), 1)
    sub = lax.broadcasted_iota(i32, (H_A, W_A), 0)
    own = (lane // DH_A) == sub
    rnd = lambda z: z.astype(bf16).astype(f32)

    @pl.when(j == 0)
    def _():
        s_self = jnp.sum(q.astype(f32) * rnd(knew_ref[0]), axis=1, keepdims=True)
        self_scr[...] = s_self
        m_scr[...] = s_self
        carry_scr[...] = lfnew_ref[0]

    @pl.when(j < nstep)
    def _():
        r = lax.broadcasted_iota(i32, (psz, 2 * psz), 0)
        c = lax.broadcasted_iota(i32, (psz, 2 * psz), 1)
        later = jnp.where(c < psz, jnp.where(r > c, 1.0, 0.0), 1.0).astype(bf16)
        group = nstep - 1 - j
        carry = carry_scr[...]
        m = m_scr[...]
        for p in reversed(range(pg)):
            hi, mid, lo = _split3(lf_refs[p][0])
            suf = _dot(hi, later) + _dot(mid, later) + _dot(lo, later)
            kt = k_refs[p][0].reshape(W_A, psz).astype(bf16)
            s = _dot(q, kt) + (carry + suf[:, :psz])
            carry = carry + suf[:, psz:psz + 1]
            p_scr[group * pg + p] = s
            m = jnp.maximum(m, jnp.max(s, axis=1, keepdims=True))
        carry_scr[...] = carry
        m_scr[...] = m

    @pl.when(j == nstep)
    def _():
        m = m_scr[...]
        e_self = jnp.exp(self_scr[...] - m)
        e = jnp.exp(p_scr[...] - m[None])
        l = e_self + jnp.sum(jnp.sum(e, axis=0), axis=1, keepdims=True)
        p_scr[...] = e / l[None]
        acc_scr[...] = jnp.where(own, rnd(e_self / l) * rnd(vnew_ref[0]), 0.0)

    @pl.when(j >= nstep)
    def _():
        group = 2 * nstep - 1 - j
        acc = acc_scr[...]
        for p in range(pg):
            vt = v_refs[p][0].reshape(W_A, psz).astype(bf16)
            acc = acc + _dotg(p_scr[group * pg + p].astype(bf16), vt, _NT)
        acc_scr[...] = acc

    @pl.when(j == 2 * nstep - 1)
    def _():
        o_ref[0] = jnp.sum(jnp.where(own, acc_scr[...], 0.0), axis=0, keepdims=True)


def _fox_decode(qbd, knew, vnew, lfnew, cache_kt, cache_vt, cache_lft, page_table):
    db = qbd.shape[0]
    psz = cache_kt.shape[-1]
    n_pages = page_table.shape[1]
    pg = PAGES_PER_STEP if n_pages % PAGES_PER_STEP == 0 else 1
    nstep = n_pages // pg
    pt = page_table.reshape(-1).astype(i32)

    def page_spec(mid, score_phase):
        def spec(p):
            def imap(bi, j, pt_ref):
                step = jnp.minimum(j, nstep - 1) if score_phase else jnp.maximum(j - nstep, 0)
                return (pt_ref[bi * n_pages + (nstep - 1 - step) * pg + p],) + (0,) * (1 + len(mid))
            return pl.BlockSpec((1,) + mid + (psz,), imap)
        return [spec(p) for p in range(pg)]

    row3 = lambda a: pl.BlockSpec((1,) + a.shape[1:], lambda bi, j, pt_ref: (bi, 0, 0))
    grid_spec = pltpu.PrefetchScalarGridSpec(
        num_scalar_prefetch=1,
        grid=(db, 2 * nstep),
        in_specs=[row3(qbd), row3(knew), row3(vnew), row3(lfnew)]
        + page_spec((H_A, DH_A), True) + page_spec((H_A, DH_A), False) + page_spec((H_A,), True),
        out_specs=pl.BlockSpec((1, 1, W_A), lambda bi, j, pt_ref: (bi, 0, 0)),
        scratch_shapes=[pltpu.VMEM((n_pages, H_A, psz), f32), pltpu.VMEM((H_A, 1), f32),
                        pltpu.VMEM((H_A, 1), f32), pltpu.VMEM((H_A, W_A), f32),
                        pltpu.VMEM((H_A, 1), f32)])
    return pl.pallas_call(
        functools.partial(_fox_decode_kernel, pg=pg, nstep=nstep),
        grid_spec=grid_spec,
        out_shape=jax.ShapeDtypeStruct((db, 1, W_A), f32),
        compiler_params=_cparams(("parallel", "arbitrary")),
        name="fox_decode",
    )(pt, qbd, knew, vnew, lfnew, *([cache_kt] * pg), *([cache_vt] * pg), *([cache_lft] * pg))


def _gla_kernel(qb_ref, kb_ref, vb_ref, rb_ref, sm_ref, wal_ref, bal_ref, gg_ref,
                ob_ref, sout_ref, s_scr, *, tt):
    i = pl.program_id(1)

    @pl.when(i == 0)
    def _():
        s_scr[...] = jnp.zeros_like(s_scr)

    q = qb_ref[0]
    k = kb_ref[0]
    z = _dot(sm_ref[0].astype(bf16), wal_ref[...]) + bal_ref[...]
    a = _log_sigmoid(z) * (1.0 / GLA_TAU)

    row = lax.broadcasted_iota(i32, (tt, tt), 0)
    col = lax.broadcasted_iota(i32, (tt, tt), 1)
    tril = jnp.where(col <= row, 1.0, 0.0).astype(bf16)
    hi, mid, lo = _split3(a)
    b = _dot(tril, hi) + _dot(tril, mid) + _dot(tril, lo)

    rowk = lax.broadcasted_iota(i32, (tt, WK_B), 0)
    hs = [slice(h * DK_B, (h + 1) * DK_B) for h in range(H_B)]
    amat = [jnp.zeros((tt, tt), f32) for _ in range(H_B)]
    edge = b
    for lvl in range(tt.bit_length() - 1):
        half = 1 << lvl
        upper = ((rowk >> lvl) & 1) == 1
        kt = jnp.where(upper, 0.0, k * jnp.exp(edge - b)).astype(bf16)
        edge_q = pltpu.roll(edge, half, 0)
        qt = jnp.where(upper, q * jnp.exp(b - edge_q), 0.0).astype(bf16)
        same = (row >> (lvl + 1)) == (col >> (lvl + 1))
        for h in range(H_B):
            amat[h] = amat[h] + jnp.where(same, _dotg(qt[:, hs[h]], kt[:, hs[h]], _NT), 0.0)
        edge = jnp.where(upper, edge, pltpu.roll(edge, tt - half, 0))
    q16 = q.astype(bf16)
    k16 = k.astype(bf16)
    for h in range(H_B):
        amat[h] = amat[h] + jnp.where(row == col, _dotg(q16[:, hs[h]], k16[:, hs[h]], _NT), 0.0)

    qe = (q * jnp.exp(b)).astype(bf16)
    ke = (k * jnp.exp(edge - b)).astype(bf16)
    e_last = jnp.exp(edge[0:1, :])
    v16 = vb_ref[0].astype(bf16)
    r = rb_ref[0]
    er = lax.broadcasted_iota(i32, (DK_B, DK_B), 0)
    ec = lax.broadcasted_iota(i32, (DK_B, DK_B), 1)
    for h in range(H_B):
        vs = slice(h * DV_B, (h + 1) * DV_B)
        vh = v16[:, vs]
        state = s_scr[h]
        o = _dot(amat[h].astype(bf16), vh) + _dot(qe[:, hs[h]], state.astype(bf16))
        decay_col = jnp.sum(
            jnp.where(er == ec, jnp.broadcast_to(e_last[:, hs[h]], (DK_B, DK_B)), 0.0),
            axis=1, keepdims=True)
        s_scr[h] = decay_col * state + _dotg(ke[:, hs[h]], vh, _TN)
        rh = r[:, vs]
        ob_ref[0, :, vs] = (_rms(o) * gg_ref[:, vs] * (rh * _sigmoid(rh))).astype(bf16)

    @pl.when(i == pl.num_programs(1) - 1)
    def _():
        sout_ref[0] = s_scr[...]


def _gla_prompt(qb, kb, vb, rb, sm, wal, bal, gg):
    b, t, _ = qb.shape
    tt = min(GLA_TILE, t)
    assert t % tt == 0 and tt & (tt - 1) == 0
    tok3 = lambda w: pl.BlockSpec((1, tt, w), lambda bi, i: (bi, i, 0))
    const2 = lambda a: pl.BlockSpec(a.shape, lambda bi, i: (0, 0))
    return pl.pallas_call(
        functools.partial(_gla_kernel, tt=tt),
        grid=(b, t // tt),
        in_specs=[tok3(WK_B), tok3(WK_B), tok3(WV_B), tok3(WV_B), tok3(LANES),
                  const2(wal), const2(bal), const2(gg)],
        out_specs=(tok3(WV_B),
                   pl.BlockSpec((1, H_B, DK_B, DV_B), lambda bi, i: (bi, 0, 0, 0))),
        out_shape=(jax.ShapeDtypeStruct((b, t, WV_B), bf16),
                   jax.ShapeDtypeStruct((b, H_B, DK_B, DV_B), f32)),
        scratch_shapes=[pltpu.VMEM((H_B, DK_B, DV_B), f32)],
        compiler_params=_cparams(("parallel", "arbitrary")),
        name="gla_prompt",
    )(qb, kb, vb, rb, sm, wal, bal, gg)


def _loga_kernel(sm_ref, wal_ref, bal_ref, o_ref):
    z = _dot(sm_ref[...].astype(bf16), wal_ref[...]) + bal_ref[...]
    o_ref[...] = _log_sigmoid(z) * (1.0 / GLA_TAU)


def _gla_decode_kernel(q_ref, k_ref, a_ref, v_ref, r_ref, s_ref, gg_ref, o_ref, so_ref):
    for h in range(H_B):
        q, k, a = q_ref[0, h], k_ref[0, h], a_ref[0, h]
        v = v_ref[0, h]
        state = s_ref[0, h]
        ea = jnp.exp(a)
        rnd = lambda z: z.astype(bf16).astype(f32)
        qk = jnp.sum(q * k, axis=0, keepdims=True)
        o = qk * v + jnp.sum(rnd(q * ea) * rnd(state), axis=0, keepdims=True)
        so_ref[0, h] = ea * state + k * v
        r = r_ref[0, h]
        o_ref[0, h] = _rms(o) * gg_ref[h] * (r * _sigmoid(r))


def _gla_decode(qb, kb, vb, rb, sm, wal, bal, gg, state):
    db = qb.shape[0]
    full = lambda a: pl.BlockSpec(a.shape, lambda: (0,) * a.ndim)
    loga = pl.pallas_call(
        _loga_kernel,
        in_specs=[full(sm), full(wal), full(bal)],
        out_specs=pl.BlockSpec((db, WK_B), lambda: (0, 0)),
        out_shape=jax.ShapeDtypeStruct((db, WK_B), f32),
        name="gla_log_decay",
    )(sm, wal, bal)
    col = lambda a: a.reshape(db, H_B, DK_B, 1)
    rowv = lambda a: a.reshape(db, H_B, 1, DV_B)
    gg4 = gg.reshape(H_B, 1, DV_B)
    b4 = lambda shp: pl.BlockSpec((1,) + shp, lambda bi: (bi, 0, 0, 0))
    o, s_new = pl.pallas_call(
        _gla_decode_kernel,
        grid=(db,),
        in_specs=[b4((H_B, DK_B, 1))] * 3 + [b4((H_B, 1, DV_B))] * 2
        + [b4((H_B, DK_B, DV_B)), pl.BlockSpec(gg4.shape, lambda bi: (0, 0, 0))],
        out_specs=(b4((H_B, 1, DV_B)), b4((H_B, DK_B, DV_B))),
        out_shape=(jax.ShapeDtypeStruct((db, H_B, 1, DV_B), f32),
                   jax.ShapeDtypeStruct((db, H_B, DK_B, DV_B), f32)),
        compiler_params=_cparams(("parallel",)),
        name="gla_decode",
    )(col(qb), col(kb), col(loga), rowv(vb), rowv(rb), state, gg4)
    return o.reshape(db, WV_B), s_new


def _mix_kernel(x_ref, oa_ref, ob_ref, ga_ref, gb_ref, gt1_ref, sc2_ref, sh2_ref, gf_ref,
                wba_ref, wbb_ref, wo_ref, wr_ref, br_ref, x1_ref, h2_ref, lg_ref, *, nsub):
    x = x_ref[0]
    tm = x.shape[0]
    ya = _dotg(oa_ref[0].astype(bf16), wba_ref[...], _TN)
    yb = _dot(ob_ref[0].astype(bf16), wbb_ref[...])
    m = _sigmoid(ga_ref[0]) * ya + _sigmoid(gb_ref[0]) * yb
    x1 = x + gt1_ref[0] * _dot(m.astype(bf16), wo_ref[...])
    x1_ref[...] = x1
    h2 = (_rms(x1) * gf_ref[...]) * sc2_ref[0] + sh2_ref[0]
    for s in range(nsub):
        h2_ref[pl.ds(s, tm, stride=nsub), :] = h2[:, s * LANES:(s + 1) * LANES]
    lg_ref[...] = _dot(h2.astype(bf16), wr_ref[...]) + br_ref[...]


def _mix(x, oa, ob, ga, gb, gt1, sc2, sh2, g_ffn, wba, wbb, wo, wr, br):
    b, t, d = x.shape
    nt = b * t
    nsub = d // LANES
    tm = min(TOKEN_TILE, t)
    assert t % tm == 0
    rows = gt1.shape[1]
    mod_block = (1, 1, d) if rows == 1 else (1, tm, d)
    mod_map = (lambda bi, i: (bi, 0, 0)) if rows == 1 else (lambda bi, i: (bi, i, 0))
    mod = pl.BlockSpec(mod_block, mod_map)
    tok3 = lambda w: pl.BlockSpec((1, tm, w), lambda bi, i: (bi, i, 0))
    const2 = lambda a: pl.BlockSpec(a.shape, lambda bi, i: (0, 0))
    nti = t // tm
    blk = lambda bi, i: (bi * nti + i, 0)
    g2 = g_ffn.reshape(1, d)
    ins = [x, oa, ob, ga, gb, gt1, sc2, sh2, g2, wba, wbb, wo, wr, br]
    in_specs = [tok3(d), pl.BlockSpec((1, W_A, tm), lambda bi, i: (bi, 0, i)), tok3(WV_B),
                tok3(d), tok3(d), mod, mod, mod,
                const2(g2), const2(wba), const2(wbb), const2(wo), const2(wr), const2(br)]
    return pl.pallas_call(
        functools.partial(_mix_kernel, nsub=nsub),
        grid=(b, nti),
        in_specs=in_specs,
        out_specs=(pl.BlockSpec((tm, d), blk), pl.BlockSpec((tm * nsub, LANES), blk),
                   pl.BlockSpec((tm, LANES), blk)),
        out_shape=(jax.ShapeDtypeStruct((nt, d), f32),
                   jax.ShapeDtypeStruct((nt * nsub, LANES), f32),
                   jax.ShapeDtypeStruct((nt, LANES), f32)),
        compiler_params=_cparams(("parallel", "arbitrary")),
        name="branch_mix",
    )(*ins)


def _route_kernel(lg_ref, cin_ref, rec_ref, gate_ref, cnt_ref, carry_scr, *, tr):
    @pl.when(pl.program_id(0) == 0)
    def _():
        carry_scr[...] = cin_ref[...]

    lane = lax.broadcasted_iota(i32, (tr, LANES), 1)
    lanef = lane.astype(f32)
    v = jnp.where(lane < N_EXPERTS, lg_ref[...], -jnp.inf)
    onehot = jnp.zeros((tr, LANES), f32)
    ids, vals = [], []
    for _ in range(TOP_K):
        mx = jnp.max(v, axis=1, keepdims=True)
        idx = jnp.min(jnp.where(v == mx, lanef, float(LANES)), axis=1, keepdims=True)
        sel = lanef == idx
        onehot = jnp.where(sel, 1.0, onehot)
        v = jnp.where(sel, -jnp.inf, v)
        ids.append(idx)
        vals.append(mx)
    es = [jnp.exp(vk - vals[0]) for vk in vals]
    tot = es[0]
    for e in es[1:]:
        tot = tot + e

    r2 = lax.broadcasted_iota(i32, (tr, tr), 0)
    c2 = lax.broadcasted_iota(i32, (tr, tr), 1)
    before = jnp.where(c2 < r2, 1.0, 0.0).astype(bf16)
    pref = _dot(before, onehot.astype(bf16)) + carry_scr[...]
    carry_scr[...] = carry_scr[...] + jnp.sum(onehot, axis=0, keepdims=True)

    rec = jnp.zeros((tr, LANES), f32)
    gate = jnp.zeros((tr, LANES), f32)
    for k in range(TOP_K):
        rank = jnp.sum(jnp.where(lanef == ids[k], pref, 0.0), axis=1, keepdims=True)
        rec = jnp.where(lane == k, rank * float(N_EXPERTS) + ids[k], rec)
        gate = jnp.where(lane == k, es[k] / tot, gate)
    rec_ref[...] = rec.astype(i32)
    gate_ref[...] = gate
    cnt_ref[...] = carry_scr[...]


def _route(logits, counts_in):
    nt = logits.shape[0]
    tr = _divisor_tile(nt, 512, SUBLANES)
    tile = pl.BlockSpec((tr, LANES), lambda i: (i, 0))
    cnt = pl.BlockSpec((1, LANES), lambda i: (0, 0))
    return pl.pallas_call(
        functools.partial(_route_kernel, tr=tr),
        grid=(nt // tr,),
        in_specs=[tile, cnt],
        out_specs=(tile, tile, cnt),
        out_shape=(jax.ShapeDtypeStruct((nt, LANES), i32),
                   jax.ShapeDtypeStruct((nt, LANES), f32),
                   jax.ShapeDtypeStruct((1, LANES), f32)),
        scratch_shapes=[pltpu.VMEM((1, LANES), f32)],
        compiler_params=_cparams(("arbitrary",)),
        name="route_topk",
    )(logits, counts_in)


def _dest_kernel(rec_ref, start_ref, out_ref, *, nsub):
    rec = rec_ref[...]
    lane = lax.broadcasted_iota(i32, rec.shape, 1)
    eid = jnp.bitwise_and(rec, N_EXPERTS - 1).astype(f32)
    rank = jnp.right_shift(rec, N_EXPERTS.bit_length() - 1)
    lanef = lane.astype(f32)
    start = jnp.zeros(rec.shape, f32)
    for k in range(TOP_K):
        mine = lanef == eid[:, k:k + 1]
        st = jnp.sum(jnp.where(mine, start_ref[...], 0.0), axis=1, keepdims=True)
        start = jnp.where(lane == k, st, start)
    out_ref[...] = (start.astype(i32) + rank) * nsub


def _dest_rows(rec, pad_start, nsub):
    nt = rec.shape[0]
    tr = _divisor_tile(nt, 512, SUBLANES)
    tile = pl.BlockSpec((tr, LANES), lambda i: (i, 0))
    start = jnp.zeros((1, LANES), f32).at[0, :N_EXPERTS].set(pad_start.astype(f32))
    out = pl.pallas_call(
        functools.partial(_dest_kernel, nsub=nsub),
        grid=(nt // tr,),
        in_specs=[tile, pl.BlockSpec((1, LANES), lambda i: (0, 0))],
        out_specs=tile,
        out_shape=jax.ShapeDtypeStruct((nt, LANES), i32),
        compiler_params=_cparams(("parallel",)),
        name="moe_dest_rows",
    )(rec, start)
    return out[:, :TOP_K].reshape(-1)


def _dispatch_kernel(dest_ref, h_ref, xin_hbm, xout_hbm, sem, *, nsub, ch):
    del xin_hbm
    base = pl.program_id(0) * ch

    def row_copy(n, k):
        src = pl.multiple_of(n * nsub, nsub)
        dst = pl.multiple_of(dest_ref[(base + n) * TOP_K + k], nsub)
        return pltpu.make_async_copy(h_ref.at[pl.ds(src, nsub)], xout_hbm.at[pl.ds(dst, nsub)],
                                     sem)

    def body(n, carry):
        for k in range(TOP_K):
            row_copy(n, k).start()
        return carry

    lax.fori_loop(0, ch, body, 0, unroll=4)
    total = ch * TOP_K * nsub
    pltpu.make_async_copy(xout_hbm.at[pl.ds(0, total)], xout_hbm.at[pl.ds(0, total)], sem).wait()


def _dispatch(dest_flat, h2rows, xrows, nsub):
    n_tok = dest_flat.shape[0] // TOP_K
    ch = min(DISPATCH_CHUNK, n_tok)
    assert n_tok % ch == 0
    grid_spec = pltpu.PrefetchScalarGridSpec(
        num_scalar_prefetch=1,
        grid=(n_tok // ch,),
        in_specs=[pl.BlockSpec((ch * nsub, LANES), lambda i, dr: (i, 0)),
                  pl.BlockSpec(memory_space=pl.ANY)],
        out_specs=pl.BlockSpec(memory_space=pl.ANY),
        scratch_shapes=[pltpu.SemaphoreType.DMA(())])
    return pl.pallas_call(
        functools.partial(_dispatch_kernel, nsub=nsub, ch=ch),
        grid_spec=grid_spec,
        out_shape=jax.ShapeDtypeStruct(xrows.shape, xrows.dtype),
        input_output_aliases={2: 0},
        compiler_params=_cparams(("arbitrary",)),
        name="moe_dispatch",
    )(dest_flat, h2rows, xrows)


def _expert_kernel(be_ref, nused_ref, x_ref, wgu_ref, bgu_ref, wd_ref, bd_ref, y_ref,
                   wgu_b, wd_b, *, tmx, nsub, dff, chunk):
    r = pl.program_id(0)
    live = r < nused_ref[0]

    @pl.when(jnp.logical_not(live))
    def _():
        y_ref[...] = jnp.zeros_like(y_ref)

    new_expert = jnp.logical_or(r == 0, be_ref[r] != be_ref[jnp.maximum(r - 1, 0)])

    @pl.when(jnp.logical_and(live, new_expert))
    def _():
        for c in range(0, wgu_b.shape[0], chunk):
            wgu_b[c:c + chunk, :] = wgu_ref[0, c:c + chunk, :].astype(bf16)
        for c in range(0, wd_b.shape[0], chunk):
            wd_b[c:c + chunk, :] = wd_ref[0, c:c + chunk, :].astype(bf16)

    @pl.when(live)
    def _():
        x = jnp.concatenate(
            [x_ref[pl.ds(s, tmx, stride=nsub), :].astype(bf16) for s in range(nsub)], axis=1)
        gu = _dot(x, wgu_b[...]) + bgu_ref[0]
        gate = jnp.minimum(gu[:, :dff], SWIGLU_LIMIT)
        up = jnp.clip(gu[:, dff:], -SWIGLU_LIMIT, SWIGLU_LIMIT)
        glu = gate * _sigmoid(SWIGLU_ALPHA * gate)
        y = _dot(((up + 1.0) * glu).astype(bf16), wd_b[...]) + bd_ref[0]
        for s in range(nsub):
            y_ref[pl.ds(s, tmx, stride=nsub), :] = y[:, s * LANES:(s + 1) * LANES]


def _experts(block_e, nused, xrows, wgu, bgu, wd, bd, nsub):
    n_blocks = block_e.shape[0]
    tmx = EXPERT_TILE
    e, d, dff2 = wgu.shape
    dff = dff2 // 2
    live = lambda r, nu: jnp.minimum(r, nu[0] - 1)
    rows = pl.BlockSpec((tmx * nsub, LANES), lambda r, be, nu: (live(r, nu), 0))
    per_e = lambda shp: pl.BlockSpec((1,) + shp, lambda r, be, nu: (be[live(r, nu)], 0, 0))
    grid_spec = pltpu.PrefetchScalarGridSpec(
        num_scalar_prefetch=2,
        grid=(n_blocks,),
        in_specs=[rows, per_e((d, dff2)), per_e((1, dff2)), per_e((dff, d)), per_e((1, d))],
        out_specs=pl.BlockSpec((tmx * nsub, LANES), lambda r, be, nu: (r, 0)),
        scratch_shapes=[pltpu.VMEM((d, dff2), bf16), pltpu.VMEM((dff, d), bf16)])
    return pl.pallas_call(
        functools.partial(_expert_kernel, tmx=tmx, nsub=nsub, dff=dff, chunk=min(256, d, dff)),
        grid_spec=grid_spec,
        out_shape=jax.ShapeDtypeStruct(xrows.shape, f32),
        compiler_params=_cparams(("arbitrary",)),
        name="moe_experts",
    )(block_e, nused, xrows, wgu, bgu.reshape(e, 1, dff2), wd, bd.reshape(e, 1, d))


def _combine_kernel(dest_ref, y_hbm, gate_ref, x1_ref, gt2_ref, gfin_ref, out_ref,
                    buf, sem, *, tc, nsub):
    base = pl.program_id(0) * tc

    def row_copy(n, k):
        src = pl.multiple_of(dest_ref[(base + n) * TOP_K + k], nsub)
        dst = pl.multiple_of((k * tc + n) * nsub, nsub)
        return pltpu.make_async_copy(y_hbm.at[pl.ds(src, nsub)], buf.at[pl.ds(dst, nsub)], sem)

    def body(n, carry):
        for k in range(TOP_K):
            row_copy(n, k).start()
        return carry

    lax.fori_loop(0, tc, body, 0, unroll=4)
    pltpu.make_async_copy(y_hbm.at[pl.ds(0, buf.shape[0])], buf, sem).wait()

    g = gate_ref[...]
    cols = []
    for s in range(nsub):
        acc = None
        for k in range(TOP_K):
            term = g[:, k:k + 1] * buf[pl.ds(k * tc * nsub + s, tc, stride=nsub), :]
            acc = term if acc is None else acc + term
        cols.append(acc)
    y = jnp.concatenate(cols, axis=1)
    out_ref[...] = _rms(x1_ref[...] + gt2_ref[0] * y) * gfin_ref[...]


def _combine(dest_flat, yrows, gates, x1, gt2, g_final, t_per_mod, nsub):
    n_tok, d = x1.shape
    tc = min(TOKEN_TILE, n_tok)
    assert n_tok % tc == 0
    rows = gt2.shape[1]
    if rows == 1:
        per_mod = t_per_mod // tc
        mod = pl.BlockSpec((1, 1, d), lambda i, dr: (i // per_mod, 0, 0))
    else:
        mod = pl.BlockSpec((1, tc, d), lambda i, dr: (0, i, 0))
    blk = lambda i, dr: (i, 0)
    gfin = g_final.reshape(1, d)
    grid_spec = pltpu.PrefetchScalarGridSpec(
        num_scalar_prefetch=1,
        grid=(n_tok // tc,),
        in_specs=[pl.BlockSpec(memory_space=pl.ANY), pl.BlockSpec((tc, LANES), blk),
                  pl.BlockSpec((tc, d), blk), mod,
                  pl.BlockSpec((1, d), lambda i, dr: (0, 0))],
        out_specs=pl.BlockSpec((tc, d), blk),
        scratch_shapes=[pltpu.VMEM((TOP_K * tc * nsub, LANES), f32), pltpu.SemaphoreType.DMA(())])
    return pl.pallas_call(
        functools.partial(_combine_kernel, tc=tc, nsub=nsub),
        grid_spec=grid_spec,
        out_shape=jax.ShapeDtypeStruct((n_tok, d), f32),
        compiler_params=_cparams(("arbitrary",)),
        name="moe_combine",
    )(dest_flat, yrows, gates, x1, gt2, gfin)


def kernel(x_prompt, x_sample, c_prompt, c_sample, cache_k, cache_v, cache_logf, state_gla,
           page_table, g_mix, g_ffn, g_final, w_ada, b_ada, w_in, b_f, w_alpha, b_alpha, g_gla,
           w_branch, w_o, w_router, b_router, w_gu, b_gu, w_d, b_d):
    depth = g_mix.shape[0]
    assert depth == 1, "one decoder layer"
    bp, t, d = x_prompt.shape
    db = x_sample.shape[0]
    assert x_sample.shape[1] == 1
    nsub = d // LANES
    ntp = bp * t
    nt = ntp + db

    nmod = bp + db
    pad = (-nmod) % SUBLANES
    c_all = jnp.concatenate([c_prompt, c_sample, jnp.zeros((pad, d), f32)], axis=0)
    mod = _adaln(c_all, w_ada[0], b_ada[0])
    sh1, sc1, gt1, sh2, sc2, gt2 = [mod[:, i * d:(i + 1) * d] for i in range(6)]
    grp_p = lambda a: a[:bp].reshape(bp, 1, d)
    grp_s = lambda a: a[bp:nmod].reshape(1, db, d)

    w_proj = _inproj_weights(w_in[0], d)
    wal = jnp.zeros((LANES, WK_B), f32).at[H_A:H_A + GLA_RANK].set(w_alpha[0]).astype(bf16)
    bal = b_alpha[0].reshape(1, WK_B)
    gg = g_gla[0].reshape(1, WV_B)
    wba = w_branch[0, :W_A].astype(bf16)
    wbb = w_branch[0, W_A:].astype(bf16)
    wo = w_o[0].astype(bf16)
    wr = jnp.zeros((d, LANES), f32).at[:, :N_EXPERTS].set(w_router[0]).astype(bf16)
    br = jnp.zeros((1, LANES), f32).at[0, :N_EXPERTS].set(b_router[0])

    (qat, kat, vt, ka, va, qb, kb, vb, rb, ga, gb, sm, fat) = _inproj(
        x_prompt, grp_p(1.0 + sc1), grp_p(sh1), g_mix[0], w_proj, True)
    logft, cumt = _logf(fat, b_f[0], True)
    oa = _fox_prompt(qat, kat, vt, jnp.transpose(cumt, (0, 2, 1)))
    ob, gla_p = _gla_prompt(qb, kb, vb, rb, sm, wal, bal, gg)
    x1_p, h2_p, lg_p = _mix(x_prompt, oa, ob, ga, gb, grp_p(gt1), grp_p(1.0 + sc2), grp_p(sh2),
                            g_ffn[0], wba, wbb, wo, wr, br)

    xs = x_sample.reshape(1, db, d)
    (qat_s, ka_s, va_s, qb_s, kb_s, vb_s, rb_s, ga_s, gb_s, sm_s, fat_s) = _inproj(
        xs, grp_s(1.0 + sc1), grp_s(sh1), g_mix[0], w_proj, False)
    logft_s, = _logf(fat_s, b_f[0], False)
    lfnew = jnp.transpose(logft_s[0], (1, 0)).reshape(db, H_A, 1)
    qbd = jnp.einsum('hnd,hc->nchd', qat_s[0], jnp.eye(H_A, dtype=bf16)).reshape(db, H_A, W_A)
    oa_s = _fox_decode(qbd, ka_s.reshape(db, 1, W_A), va_s.reshape(db, 1, W_A), lfnew,
                       jnp.transpose(cache_k[0], (0, 2, 3, 1)),
                       jnp.transpose(cache_v[0], (0, 2, 3, 1)),
                       jnp.transpose(cache_logf[0], (0, 2, 1)), page_table)
    ob_s, gla_s = _gla_decode(qb_s[0], kb_s[0], vb_s[0], rb_s[0], sm_s[0], wal, bal, gg,
                              state_gla[0])
    x1_s, h2_s, lg_s = _mix(xs, oa_s.reshape(db, W_A).T[None], ob_s.reshape(1, db, WV_B), ga_s, gb_s,
                            grp_s(gt1), grp_s(1.0 + sc2), grp_s(sh2), g_ffn[0],
                            wba, wbb, wo, wr, br)

    rec_p, gates_p, cnt_p = _route(lg_p, jnp.zeros((1, LANES), f32))
    rec_s, gates_s, cnt = _route(lg_s, cnt_p)
    counts = cnt[0, :N_EXPERTS].astype(i32)
    padded = (counts + EXPERT_TILE - 1) // EXPERT_TILE * EXPERT_TILE
    pad_end = jnp.cumsum(padded).astype(i32)
    pad_start = pad_end - padded

    dest_p = _dest_rows(rec_p, pad_start, nsub)
    dest_s = _dest_rows(rec_s, pad_start, nsub)
    n_blocks = -(-(nt * TOP_K + N_EXPERTS * (EXPERT_TILE - 1)) // EXPERT_TILE)
    block_start = jnp.arange(n_blocks, dtype=i32) * EXPERT_TILE
    block_e = jnp.minimum(jnp.sum(pad_end[None, :] <= block_start[:, None], axis=1),
                          N_EXPERTS - 1).astype(i32)
    nused = (pad_end[-1:] // EXPERT_TILE).astype(i32)

    xrows = jnp.zeros((n_blocks * EXPERT_TILE * nsub, LANES), f32)
    xrows = _dispatch(dest_p, h2_p, xrows, nsub)
    xrows = _dispatch(dest_s, h2_s, xrows, nsub)
    yrows = _experts(block_e, nused, xrows, w_gu[0], b_gu[0], w_d[0], b_d[0], nsub)
    y_p = _combine(dest_p, yrows, gates_p, x1_p, grp_p(gt2), g_final, t, nsub)
    y_s = _combine(dest_s, yrows, gates_s, x1_s, grp_s(gt2), g_final, 1, nsub)

    logf_p = jnp.transpose(logft, (0, 2, 1))
    return (y_p.reshape(bp, t, d), y_s.reshape(db, 1, d),
            jnp.transpose(ka, (0, 3, 1, 2))[None], jnp.transpose(va, (0, 3, 1, 2))[None],
            logf_p.reshape(1, bp, t, H_A), gla_p.reshape(1, bp, H_B, DK_B, DV_B),
            ka_s.reshape(1, db, 1, H_A, DH_A), va_s.reshape(1, db, 1, H_A, DH_A),
            lfnew.reshape(1, db, 1, H_A), gla_s.reshape(1, db, H_B, DK_B, DV_B))
```

```python
import functools

import jax
import jax.numpy as jnp
from jax import lax
from jax.experimental import pallas as pl
from jax.experimental.pallas import tpu as pltpu

f32 = jnp.float32
bf16 = jnp.bfloat16
i32 = jnp.int32

H_A, DH_A = 8, 64
H_B, DK_B, DV_B = 4, 64, 128
GLA_RANK = 16
GLA_TAU = 16.0
N_EXPERTS = 32
TOP_K = 4
SWIGLU_LIMIT = 7.0
SWIGLU_ALPHA = 1.702
RMS_EPS = 1e-6
W_A = H_A * DH_A
WK_B = H_B * DK_B
WV_B = H_B * DV_B

LANES = 128
SUBLANES = 8
VMEM_LIMIT_MB = 56

TOKEN_TILE = 256
ATTN_TILE = 256
GLA_TILE = 256
EXPERT_TILE = 512
PAGES_PER_STEP = 16
DISPATCH_CHUNK = 256

_NT = (((1,), (1,)), ((), ()))
_TN = (((0,), (0,)), ((), ()))


def _cparams(semantics, vmem_mb=VMEM_LIMIT_MB):
    return pltpu.CompilerParams(dimension_semantics=semantics,
                                vmem_limit_bytes=vmem_mb << 20)


def _sigmoid(x):
    return 1.0 / (1.0 + jnp.exp(-x))


def _log_sigmoid(x):
    return jnp.minimum(x, 0.0) - jnp.log1p(jnp.exp(-jnp.abs(x)))


def _split3(x):
    hi = x.astype(bf16)
    r = x - hi.astype(f32)
    mid = r.astype(bf16)
    lo = (r - mid.astype(f32)).astype(bf16)
    return hi, mid, lo


def _dot(a, b):
    return jnp.dot(a, b, preferred_element_type=f32)


def _dotg(a, b, dims):
    return lax.dot_general(a, b, dims, preferred_element_type=f32)


def _rms(x):
    return x * lax.rsqrt(jnp.mean(x * x, axis=-1, keepdims=True) + RMS_EPS)


def _divisor_tile(n, cap, mult):
    best = None
    t = mult
    while t <= min(n, cap):
        if n % t == 0:
            best = t
        t += mult
    assert best is not None, (n, cap, mult)
    return best


def _mod_kernel(c_ref, w_ref, b_ref, o_ref):
    c = c_ref[...]
    s = (c * _sigmoid(c)).astype(bf16)
    o_ref[...] = _dot(s, w_ref[...].astype(bf16)) + b_ref[...]


def _adaln(c_all, w_ada, b_ada):
    r, d = c_all.shape
    n = w_ada.shape[1]
    tn = _divisor_tile(n, 1536, LANES)
    return pl.pallas_call(
        _mod_kernel,
        grid=(n // tn,),
        in_specs=[pl.BlockSpec((r, d), lambda j: (0, 0)),
                  pl.BlockSpec((d, tn), lambda j: (0, j)),
                  pl.BlockSpec((1, tn), lambda j: (0, j))],
        out_specs=pl.BlockSpec((r, tn), lambda j: (0, j)),
        out_shape=jax.ShapeDtypeStruct((r, n), f32),
        compiler_params=_cparams(("arbitrary",)),
        name="adaln_mod",
    )(c_all, w_ada, b_ada.reshape(1, n))


def _inproj_kernel(x_ref, sc_ref, sh_ref, g_ref, wbig_ref, wsm_ref, wfa_ref, wkt_ref, wvt_ref,
                   *out_refs, d, time_minor):
    if time_minor:
        (qat_ref, kat_ref, vt_ref, ka_ref, va_ref, qb_ref, kb_ref, vb_ref, rb_ref,
         ga_ref, gb_ref, sm_ref, fat_ref) = out_refs
    else:
        (qat_ref, ka_ref, va_ref, qb_ref, kb_ref, vb_ref, rb_ref,
         ga_ref, gb_ref, sm_ref, fat_ref) = out_refs
    x = x_ref[0]
    h = (_rms(x) * g_ref[...]) * sc_ref[0] + sh_ref[0]
    hb = h.astype(bf16)
    off = [0]

    def mm(width):
        r = _dot(hb, wbig_ref[:, off[0]:off[0] + width])
        off[0] += width
        return r

    qa = mm(W_A)
    ka = mm(W_A)
    va = mm(W_A)
    for hh in range(H_A):
        sl = slice(hh * DH_A, (hh + 1) * DH_A)
        qat_ref[0, hh] = qa[:, sl].astype(bf16)
        if time_minor:
            kat_ref[0, hh] = ka[:, sl].astype(bf16)
    if time_minor:
        tm = x.shape[0]
        kt = _dotg(wkt_ref[...], hb, _NT).reshape(H_A, DH_A, tm)
        vt = _dotg(wvt_ref[...], hb, _NT).reshape(H_A, DH_A, tm)
        ka_ref[0] = kt
        va_ref[0] = vt
        vt_ref[0] = vt.astype(bf16)
    else:
        ka_ref[0] = ka
        va_ref[0] = va
    qb_ref[0] = mm(WK_B)
    kb_ref[0] = mm(WK_B)
    vb_ref[0] = mm(WV_B)
    rb_ref[0] = mm(WV_B)
    ga_ref[0] = mm(d)
    gb_ref[0] = mm(d)
    sm_ref[0] = _dot(hb, wsm_ref[...])
    fat_ref[0] = _dotg(wfa_ref[...].astype(bf16), hb, _NT)


def _inproj_weights(w_in, d):
    sizes = (W_A, W_A, W_A, H_A, WK_B, WK_B, WV_B, WV_B, GLA_RANK, d, d)
    offs = [0]
    for s in sizes:
        offs.append(offs[-1] + s)
    seg = lambda i: w_in[:, offs[i]:offs[i + 1]]
    wbig = jnp.concatenate(
        [seg(0) * (DH_A ** -0.5), seg(1), seg(2), seg(4) * (DK_B ** -0.5), seg(5),
         seg(6), seg(7), seg(9), seg(10)], axis=1).astype(bf16)
    wsm = jnp.concatenate(
        [seg(3), seg(8), jnp.zeros((d, LANES - H_A - GLA_RANK), f32)], axis=1).astype(bf16)
    wfa = seg(3).T
    wkt = seg(1).T.astype(bf16)
    wvt = seg(2).T.astype(bf16)
    return wbig, wsm, wfa, wkt, wvt


def _inproj(x, sc, sh, g_mix, weights, time_minor):
    wbig, wsm, wfa, wkt, wvt = weights
    b, t, d = x.shape
    tm = min(TOKEN_TILE, t)
    assert t % tm == 0
    rows = sc.shape[1]
    mod_block = (1, 1, d) if rows == 1 else (1, tm, d)
    mod_map = (lambda bi, i: (bi, 0, 0)) if rows == 1 else (lambda bi, i: (bi, i, 0))
    tok3 = lambda w: pl.BlockSpec((1, tm, w), lambda bi, i: (bi, i, 0))
    head4 = pl.BlockSpec((1, H_A, tm, DH_A), lambda bi, i: (bi, 0, i, 0))
    const2 = lambda a: pl.BlockSpec(a.shape, lambda bi, i: (0, 0))
    sds = jax.ShapeDtypeStruct
    headt = pl.BlockSpec((1, H_A, DH_A, tm), lambda bi, i: (bi, 0, 0, i))
    if time_minor:
        kv_shape = [sds((b, H_A, t, DH_A), bf16), sds((b, H_A, DH_A, t), bf16),
                    sds((b, H_A, DH_A, t), f32), sds((b, H_A, DH_A, t), f32)]
        kv_specs = [head4, headt, headt, headt]
    else:
        kv_shape = [sds((b, t, W_A), f32), sds((b, t, W_A), f32)]
        kv_specs = [tok3(W_A), tok3(W_A)]
    out_shape = tuple(
        [sds((b, H_A, t, DH_A), bf16)] + kv_shape
        + [sds((b, t, WK_B), f32), sds((b, t, WK_B), f32),
           sds((b, t, WV_B), f32), sds((b, t, WV_B), f32),
           sds((b, t, d), f32), sds((b, t, d), f32),
           sds((b, t, LANES), f32), sds((b, H_A, t), f32)])
    out_specs = tuple(
        [head4] + kv_specs
        + [tok3(WK_B), tok3(WK_B), tok3(WV_B), tok3(WV_B), tok3(d), tok3(d), tok3(LANES),
           pl.BlockSpec((1, H_A, tm), lambda bi, i: (bi, 0, i))])
    g2 = g_mix.reshape(1, d)
    return pl.pallas_call(
        functools.partial(_inproj_kernel, d=d, time_minor=time_minor),
        grid=(b, t // tm),
        in_specs=[tok3(d), pl.BlockSpec(mod_block, mod_map), pl.BlockSpec(mod_block, mod_map),
                  const2(g2), const2(wbig), const2(wsm), const2(wfa), const2(wkt), const2(wvt)],
        out_specs=out_specs,
        out_shape=out_shape,
        compiler_params=_cparams(("parallel", "arbitrary")),
        name="inproj",
    )(x, sc, sh, g2, wbig, wsm, wfa, wkt, wvt)


def _logf_kernel(fa_ref, bf_ref, logf_ref, *cum_refs, t, cb):
    lf = _log_sigmoid(fa_ref[...] + bf_ref[...])
    logf_ref[...] = lf
    if not cum_refs:
        return
    cum_ref, = cum_refs
    r = lax.broadcasted_iota(i32, (cb, cb), 0)
    c = lax.broadcasted_iota(i32, (cb, cb), 1)
    triu = jnp.where(r <= c, 1.0, 0.0).astype(bf16)
    carry = jnp.zeros((lf.shape[0], 1), f32)
    for j in range(t // cb):
        hi, mid, lo = _split3(lf[:, j * cb:(j + 1) * cb])
        blk = _dot(hi, triu) + _dot(mid, triu) + _dot(lo, triu) + carry
        cum_ref[:, j * cb:(j + 1) * cb] = blk
        carry = blk[:, cb - 1:cb]


def _logf(fat, b_f, with_cumsum):
    b, _, t = fat.shape
    rows = b * H_A
    fa2 = fat.reshape(rows, t)
    bf2 = jnp.tile(b_f.reshape(H_A, 1), (b, 1))
    cb = min(256, t)
    full = lambda a: pl.BlockSpec(a.shape, lambda: (0,) * a.ndim)
    n_out = 2 if with_cumsum else 1
    outs = pl.pallas_call(
        functools.partial(_logf_kernel, t=t, cb=cb),
        in_specs=[full(fa2), full(bf2)],
        out_specs=tuple(pl.BlockSpec((rows, t), lambda: (0, 0)) for _ in range(n_out)),
        out_shape=tuple(jax.ShapeDtypeStruct((rows, t), f32) for _ in range(n_out)),
        name="log_forget",
    )(fa2, bf2)
    return tuple(o.reshape(b, H_A, t) for o in outs)


def _bias_lanes(col, ones_first):
    hi, mid, lo = [p.astype(f32) for p in _split3(col)]
    lane = lax.broadcasted_iota(i32, (col.shape[0], DH_A), 1)
    base = 3 if ones_first else 0
    parts = jnp.where(lane == base, hi, jnp.where(lane == base + 1, mid,
                      jnp.where(lane == base + 2, lo, 0.0)))
    ones = (lane < 3) if ones_first else ((lane >= 3) & (lane < 6))
    return jnp.where(ones, 1.0, parts)


def _fox_kernel(q_ref, k_ref, vt_ref, cum_ref, o_ref, kaug, qaug, *, tq):
    i = pl.program_id(1)

    @pl.when(i == 0)
    def _():
        cum_all = cum_ref[0]
        for h in range(H_A):
            kaug[h, :, :DH_A] = k_ref[0, h]
            kaug[h, :, DH_A:] = _bias_lanes(-cum_all[:, h:h + 1], False).astype(bf16)

    cum_q = cum_ref[0, pl.ds(pl.multiple_of(i * tq, tq), tq), :]
    for h in range(H_A):
        qaug[h, :, :DH_A] = q_ref[0, h]
        qaug[h, :, DH_A:] = _bias_lanes(cum_q[:, h:h + 1], True).astype(bf16)
    key = lax.broadcasted_iota(i32, (tq, tq), 0)
    qry = lax.broadcasted_iota(i32, (tq, tq), 1)

    def step(j, carry, masked):
        start = pl.multiple_of(j * tq, tq)
        scores = [_dotg(kaug[h, pl.ds(start, tq), :], qaug[h], _NT) for h in range(H_A)]
        stats = []
        for h in range(H_A):
            m_old, l_old, _ = carry[h]
            s = scores[h]
            if masked:
                s = jnp.where(key <= qry, s, -jnp.inf)
            m_new = jnp.maximum(m_old, jnp.max(s, axis=0, keepdims=True))
            alpha = jnp.exp(m_old - m_new)
            p = jnp.exp(s - m_new)
            l_new = alpha * l_old + jnp.sum(p, axis=0, keepdims=True)
            stats.append((m_new, l_new, alpha, p.astype(bf16)))
        out = []
        for h in range(H_A):
            m_new, l_new, alpha, p = stats[h]
            acc_new = alpha * carry[h][2] + _dot(vt_ref[0, h, :, pl.ds(start, tq)], p)
            out.append((m_new, l_new, acc_new))
        return tuple(out)

    init = tuple((jnp.full((1, tq), -jnp.inf, f32), jnp.zeros((1, tq), f32),
                  jnp.zeros((DH_A, tq), f32)) for _ in range(H_A))
    carry = lax.fori_loop(0, i, functools.partial(step, masked=False), init)
    carry = step(i, carry, True)
    for h in range(H_A):
        _, l, acc = carry[h]
        o_ref[0, h * DH_A:(h + 1) * DH_A, :] = (acc / l).astype(bf16)


def _fox_prompt(qat, kat, vt, cum):
    b, _, t, _ = qat.shape
    tq = min(ATTN_TILE, t)
    assert t % tq == 0
    return pl.pallas_call(
        functools.partial(_fox_kernel, tq=tq),
        grid=(b, t // tq),
        in_specs=[pl.BlockSpec((1, H_A, tq, DH_A), lambda bi, i: (bi, 0, i, 0)),
                  pl.BlockSpec((1, H_A, t, DH_A), lambda bi, i: (bi, 0, 0, 0)),
                  pl.BlockSpec((1, H_A, DH_A, t), lambda bi, i: (bi, 0, 0, 0)),
                  pl.BlockSpec((1, t, H_A), lambda bi, i: (bi, 0, 0))],
        out_specs=pl.BlockSpec((1, W_A, tq), lambda bi, i: (bi, 0, i)),
        out_shape=jax.ShapeDtypeStruct((b, W_A, t), bf16),
        scratch_shapes=[pltpu.VMEM((H_A, t, 2 * DH_A), bf16), pltpu.VMEM((H_A, tq, 2 * DH_A), bf16)],
        compiler_params=_cparams(("parallel", "arbitrary")),
        name="fox_prompt",
    )(qat, kat, vt, cum)


def _fox_decode_kernel(pt_ref, q_ref, knew_ref, vnew_ref, lfnew_ref, *refs, pg, nstep):
    del pt_ref
    k_refs, v_refs, lf_refs = refs[:pg], refs[pg:2 * pg], refs[2 * pg:3 * pg]
    o_ref = refs[3 * pg]
    p_scr, m_scr, self_scr, acc_scr, carry_scr = refs[3 * pg + 1:]
    j = pl.program_id(1)
    q = q_ref[0]
    psz = lf_refs[0].shape[2]
    lane = lax.broadcasted_iota(i32, (H_A, W_A), 1)
    sub = lax.broadcasted_iota(i32, (H_A, W_A), 0)
    own = (lane // DH_A) == sub
    rnd = lambda z: z.astype(bf16).astype(f32)

    @pl.when(j == 0)
    def _():
        s_self = jnp.sum(q.astype(f32) * rnd(knew_ref[0]), axis=1, keepdims=True)
        self_scr[...] = s_self
        m_scr[...] = s_self
        carry_scr[...] = lfnew_ref[0]

    @pl.when(j < nstep)
    def _():
        r = lax.broadcasted_iota(i32, (psz, 2 * psz), 0)
        c = lax.broadcasted_iota(i32, (psz, 2 * psz), 1)
        later = jnp.where(c < psz, jnp.where(r > c, 1.0, 0.0), 1.0).astype(bf16)
        group = nstep - 1 - j
        carry = carry_scr[...]
        m = m_scr[...]
        hi, mid, lo = _split3(jnp.concatenate([lf_refs[p][0] for p in range(pg)], axis=0))
        suf_all = _dot(hi, later) + _dot(mid, later) + _dot(lo, later)
        for p in reversed(range(pg)):
            suf = suf_all[p * H_A:(p + 1) * H_A]
            kt = k_refs[p][0].reshape(W_A, psz).astype(bf16)
            s = _dot(q, kt) + (carry + suf[:, :psz])
            carry = carry + suf[:, psz:psz + 1]
            p_scr[group * pg + p] = s
            m = jnp.maximum(m, jnp.max(s, axis=1, keepdims=True))
        carry_scr[...] = carry
        m_scr[...] = m

    @pl.when(j == nstep)
    def _():
        m = m_scr[...]
        e_self = jnp.exp(self_scr[...] - m)
        e = jnp.exp(p_scr[...] - m[None])
        l = e_self + jnp.sum(jnp.sum(e, axis=0), axis=1, keepdims=True)
        p_scr[...] = e / l[None]
        acc_scr[...] = jnp.where(own, rnd(e_self / l) * rnd(vnew_ref[0]), 0.0)

    @pl.when(j >= nstep)
    def _():
        group = 2 * nstep - 1 - j
        acc = acc_scr[...]
        for p in range(pg):
            vt = v_refs[p][0].reshape(W_A, psz).astype(bf16)
            acc = acc + _dotg(p_scr[group * pg + p].astype(bf16), vt, _NT)
        acc_scr[...] = acc

    @pl.when(j == 2 * nstep - 1)
    def _():
        o_ref[0] = jnp.sum(jnp.where(own, acc_scr[...], 0.0), axis=0, keepdims=True)


def _fox_decode(qbd, knew, vnew, lfnew, cache_kt, cache_vt, cache_lft, page_table):
    db = qbd.shape[0]
    psz = cache_kt.shape[-1]
    n_pages = page_table.shape[1]
    pg = PAGES_PER_STEP if n_pages % PAGES_PER_STEP == 0 else 1
    nstep = n_pages // pg
    pt = page_table.reshape(-1).astype(i32)

    def page_spec(mid, score_phase):
        def spec(p):
            def imap(bi, j, pt_ref):
                step = jnp.minimum(j, nstep - 1) if score_phase else jnp.maximum(j - nstep, 0)
                return (pt_ref[bi * n_pages + (nstep - 1 - step) * pg + p],) + (0,) * (1 + len(mid))
            return pl.BlockSpec((1,) + mid + (psz,), imap)
        return [spec(p) for p in range(pg)]

    row3 = lambda a: pl.BlockSpec((1,) + a.shape[1:], lambda bi, j, pt_ref: (bi, 0, 0))
    grid_spec = pltpu.PrefetchScalarGridSpec(
        num_scalar_prefetch=1,
        grid=(db, 2 * nstep),
        in_specs=[row3(qbd), row3(knew), row3(vnew), row3(lfnew)]
        + page_spec((H_A, DH_A), True) + page_spec((H_A, DH_A), False) + page_spec((H_A,), True),
        out_specs=pl.BlockSpec((1, 1, W_A), lambda bi, j, pt_ref: (bi, 0, 0)),
        scratch_shapes=[pltpu.VMEM((n_pages, H_A, psz), f32), pltpu.VMEM((H_A, 1), f32),
                        pltpu.VMEM((H_A, 1), f32), pltpu.VMEM((H_A, W_A), f32),
                        pltpu.VMEM((H_A, 1), f32)])
    return pl.pallas_call(
        functools.partial(_fox_decode_kernel, pg=pg, nstep=nstep),
        grid_spec=grid_spec,
        out_shape=jax.ShapeDtypeStruct((db, 1, W_A), f32),
        compiler_params=_cparams(("parallel", "arbitrary")),
        name="fox_decode",
    )(pt, qbd, knew, vnew, lfnew, *([cache_kt] * pg), *([cache_vt] * pg), *([cache_lft] * pg))


def _gla_kernel(qb_ref, kb_ref, vb_ref, rb_ref, sm_ref, wal_ref, bal_ref, gg_ref,
                ob_ref, sout_ref, s_scr, *, tt):
    i = pl.program_id(1)

    @pl.when(i == 0)
    def _():
        s_scr[...] = jnp.zeros_like(s_scr)

    q = qb_ref[0]
    k = kb_ref[0]
    z = _dot(sm_ref[0].astype(bf16), wal_ref[...]) + bal_ref[...]
    a = _log_sigmoid(z) * (1.0 / GLA_TAU)

    row = lax.broadcasted_iota(i32, (tt, tt), 0)
    col = lax.broadcasted_iota(i32, (tt, tt), 1)
    tril = jnp.where(col <= row, 1.0, 0.0).astype(bf16)
    hi, mid, lo = _split3(a)
    b = _dot(tril, hi) + _dot(tril, mid) + _dot(tril, lo)

    rowk = lax.broadcasted_iota(i32, (tt, WK_B), 0)
    hs = [slice(h * DK_B, (h + 1) * DK_B) for h in range(H_B)]
    amat = [jnp.zeros((tt, tt), f32) for _ in range(H_B)]
    edge = b
    for lvl in range(tt.bit_length() - 1):
        half = 1 << lvl
        upper = ((rowk >> lvl) & 1) == 1
        kt = jnp.where(upper, 0.0, k * jnp.exp(edge - b)).astype(bf16)
        edge_q = pltpu.roll(edge, half, 0)
        qt = jnp.where(upper, q * jnp.exp(b - edge_q), 0.0).astype(bf16)
        same = (row >> (lvl + 1)) == (col >> (lvl + 1))
        for h in range(H_B):
            amat[h] = amat[h] + jnp.where(same, _dotg(qt[:, hs[h]], kt[:, hs[h]], _NT), 0.0)
        edge = jnp.where(upper, edge, pltpu.roll(edge, tt - half, 0))
    q16 = q.astype(bf16)
    k16 = k.astype(bf16)
    for h in range(H_B):
        amat[h] = amat[h] + jnp.where(row == col, _dotg(q16[:, hs[h]], k16[:, hs[h]], _NT), 0.0)

    qe = (q * jnp.exp(b)).astype(bf16)
    ke = (k * jnp.exp(edge - b)).astype(bf16)
    e_last = jnp.exp(edge[0:1, :])
    v16 = vb_ref[0].astype(bf16)
    r = rb_ref[0]
    er = lax.broadcasted_iota(i32, (DK_B, DK_B), 0)
    ec = lax.broadcasted_iota(i32, (DK_B, DK_B), 1)
    for h in range(H_B):
        vs = slice(h * DV_B, (h + 1) * DV_B)
        vh = v16[:, vs]
        state = s_scr[h]
        o = _dot(amat[h].astype(bf16), vh) + _dot(qe[:, hs[h]], state.astype(bf16))
        decay_col = jnp.sum(
            jnp.where(er == ec, jnp.broadcast_to(e_last[:, hs[h]], (DK_B, DK_B)), 0.0),
            axis=1, keepdims=True)
        s_scr[h] = decay_col * state + _dotg(ke[:, hs[h]], vh, _TN)
        rh = r[:, vs]
        ob_ref[0, :, vs] = (_rms(o) * gg_ref[:, vs] * (rh * _sigmoid(rh))).astype(bf16)

    @pl.when(i == pl.num_programs(1) - 1)
    def _():
        sout_ref[0] = s_scr[...]


def _gla_prompt(qb, kb, vb, rb, sm, wal, bal, gg):
    b, t, _ = qb.shape
    tt = min(GLA_TILE, t)
    assert t % tt == 0 and tt & (tt - 1) == 0
    tok3 = lambda w: pl.BlockSpec((1, tt, w), lambda bi, i: (bi, i, 0))
    const2 = lambda a: pl.BlockSpec(a.shape, lambda bi, i: (0, 0))
    return pl.pallas_call(
        functools.partial(_gla_kernel, tt=tt),
        grid=(b, t // tt),
        in_specs=[tok3(WK_B), tok3(WK_B), tok3(WV_B), tok3(WV_B), tok3(LANES),
                  const2(wal), const2(bal), const2(gg)],
        out_specs=(tok3(WV_B),
                   pl.BlockSpec((1, H_B, DK_B, DV_B), lambda bi, i: (bi, 0, 0, 0))),
        out_shape=(jax.ShapeDtypeStruct((b, t, WV_B), bf16),
                   jax.ShapeDtypeStruct((b, H_B, DK_B, DV_B), f32)),
        scratch_shapes=[pltpu.VMEM((H_B, DK_B, DV_B), f32)],
        compiler_params=_cparams(("parallel", "arbitrary")),
        name="gla_prompt",
    )(qb, kb, vb, rb, sm, wal, bal, gg)


def _loga_kernel(sm_ref, wal_ref, bal_ref, o_ref):
    z = _dot(sm_ref[...].astype(bf16), wal_ref[...]) + bal_ref[...]
    o_ref[...] = _log_sigmoid(z) * (1.0 / GLA_TAU)


def _gla_decode_kernel(q_ref, k_ref, a_ref, v_ref, r_ref, s_ref, gg_ref, o_ref, so_ref):
    for h in range(H_B):
        q, k, a = q_ref[0, h], k_ref[0, h], a_ref[0, h]
        v = v_ref[0, h]
        state = s_ref[0, h]
        ea = jnp.exp(a)
        rnd = lambda z: z.astype(bf16).astype(f32)
        qk = jnp.sum(q * k, axis=0, keepdims=True)
        o = qk * v + jnp.sum(rnd(q * ea) * rnd(state), axis=0, keepdims=True)
        so_ref[0, h] = ea * state + k * v
        r = r_ref[0, h]
        o_ref[0, h] = _rms(o) * gg_ref[h] * (r * _sigmoid(r))


def _gla_decode(qb, kb, vb, rb, sm, wal, bal, gg, state):
    db = qb.shape[0]
    full = lambda a: pl.BlockSpec(a.shape, lambda: (0,) * a.ndim)
    loga = pl.pallas_call(
        _loga_kernel,
        in_specs=[full(sm), full(wal), full(bal)],
        out_specs=pl.BlockSpec((db, WK_B), lambda: (0, 0)),
        out_shape=jax.ShapeDtypeStruct((db, WK_B), f32),
        name="gla_log_decay",
    )(sm, wal, bal)
    col = lambda a: a.reshape(db, H_B, DK_B, 1)
    rowv = lambda a: a.reshape(db, H_B, 1, DV_B)
    gg4 = gg.reshape(H_B, 1, DV_B)
    b4 = lambda shp: pl.BlockSpec((1,) + shp, lambda bi: (bi, 0, 0, 0))
    o, s_new = pl.pallas_call(
        _gla_decode_kernel,
        grid=(db,),
        in_specs=[b4((H_B, DK_B, 1))] * 3 + [b4((H_B, 1, DV_B))] * 2
        + [b4((H_B, DK_B, DV_B)), pl.BlockSpec(gg4.shape, lambda bi: (0, 0, 0))],
        out_specs=(b4((H_B, 1, DV_B)), b4((H_B, DK_B, DV_B))),
        out_shape=(jax.ShapeDtypeStruct((db, H_B, 1, DV_B), f32),
                   jax.ShapeDtypeStruct((db, H_B, DK_B, DV_B), f32)),
        compiler_params=_cparams(("parallel",)),
        name="gla_decode",
    )(col(qb), col(kb), col(loga), rowv(vb), rowv(rb), state, gg4)
    return o.reshape(db, WV_B), s_new


def _mix_kernel(x_ref, oa_ref, ob_ref, ga_ref, gb_ref, gt1_ref, sc2_ref, sh2_ref, gf_ref,
                wba_ref, wbb_ref, wo_ref, wr_ref, br_ref, x1_ref, h2_ref, lg_ref, *, nsub):
    x = x_ref[0]
    tm = x.shape[0]
    ya = _dotg(oa_ref[0].astype(bf16), wba_ref[...], _TN)
    yb = _dot(ob_ref[0].astype(bf16), wbb_ref[...])
    m = _sigmoid(ga_ref[0]) * ya + _sigmoid(gb_ref[0]) * yb
    x1 = x + gt1_ref[0] * _dot(m.astype(bf16), wo_ref[...])
    x1_ref[...] = x1
    h2 = (_rms(x1) * gf_ref[...]) * sc2_ref[0] + sh2_ref[0]
    for s in range(nsub):
        h2_ref[pl.ds(s, tm, stride=nsub), :] = h2[:, s * LANES:(s + 1) * LANES]
    lg_ref[...] = _dot(h2.astype(bf16), wr_ref[...]) + br_ref[...]


def _mix(x, oa, ob, ga, gb, gt1, sc2, sh2, g_ffn, wba, wbb, wo, wr, br):
    b, t, d = x.shape
    nt = b * t
    nsub = d // LANES
    tm = min(TOKEN_TILE, t)
    assert t % tm == 0
    rows = gt1.shape[1]
    mod_block = (1, 1, d) if rows == 1 else (1, tm, d)
    mod_map = (lambda bi, i: (bi, 0, 0)) if rows == 1 else (lambda bi, i: (bi, i, 0))
    mod = pl.BlockSpec(mod_block, mod_map)
    tok3 = lambda w: pl.BlockSpec((1, tm, w), lambda bi, i: (bi, i, 0))
    const2 = lambda a: pl.BlockSpec(a.shape, lambda bi, i: (0, 0))
    nti = t // tm
    blk = lambda bi, i: (bi * nti + i, 0)
    g2 = g_ffn.reshape(1, d)
    ins = [x, oa, ob, ga, gb, gt1, sc2, sh2, g2, wba, wbb, wo, wr, br]
    in_specs = [tok3(d), pl.BlockSpec((1, W_A, tm), lambda bi, i: (bi, 0, i)), tok3(WV_B),
                tok3(d), tok3(d), mod, mod, mod,
                const2(g2), const2(wba), const2(wbb), const2(wo), const2(wr), const2(br)]
    return pl.pallas_call(
        functools.partial(_mix_kernel, nsub=nsub),
        grid=(b, nti),
        in_specs=in_specs,
        out_specs=(pl.BlockSpec((tm, d), blk), pl.BlockSpec((tm * nsub, LANES), blk),
                   pl.BlockSpec((tm, LANES), blk)),
        out_shape=(jax.ShapeDtypeStruct((nt, d), f32),
                   jax.ShapeDtypeStruct((nt * nsub, LANES), f32),
                   jax.ShapeDtypeStruct((nt, LANES), f32)),
        compiler_params=_cparams(("parallel", "arbitrary")),
        name="branch_mix",
    )(*ins)


def _route_kernel(lg_ref, cin_ref, rec_ref, gate_ref, cnt_ref, carry_scr, *, tr):
    @pl.when(pl.program_id(0) == 0)
    def _():
        carry_scr[...] = cin_ref[...]

    lane = lax.broadcasted_iota(i32, (tr, LANES), 1)
    lanef = lane.astype(f32)
    v = jnp.where(lane < N_EXPERTS, lg_ref[...], -jnp.inf)
    onehot = jnp.zeros((tr, LANES), f32)
    ids, vals = [], []
    for _ in range(TOP_K):
        mx = jnp.max(v, axis=1, keepdims=True)
        idx = jnp.min(jnp.where(v == mx, lanef, float(LANES)), axis=1, keepdims=True)
        sel = lanef == idx
        onehot = jnp.where(sel, 1.0, onehot)
        v = jnp.where(sel, -jnp.inf, v)
        ids.append(idx)
        vals.append(mx)
    es = [jnp.exp(vk - vals[0]) for vk in vals]
    tot = es[0]
    for e in es[1:]:
        tot = tot + e

    r2 = lax.broadcasted_iota(i32, (tr, tr), 0)
    c2 = lax.broadcasted_iota(i32, (tr, tr), 1)
    before = jnp.where(c2 < r2, 1.0, 0.0).astype(bf16)
    pref = _dot(before, onehot.astype(bf16)) + carry_scr[...]
    carry_scr[...] = carry_scr[...] + jnp.sum(onehot, axis=0, keepdims=True)

    rec = jnp.zeros((tr, LANES), f32)
    gate = jnp.zeros((tr, LANES), f32)
    for k in range(TOP_K):
        rank = jnp.sum(jnp.where(lanef == ids[k], pref, 0.0), axis=1, keepdims=True)
        rec = jnp.where(lane == k, rank * float(N_EXPERTS) + ids[k], rec)
        gate = jnp.where(lane == k, es[k] / tot, gate)
    rec_ref[...] = rec.astype(i32)
    gate_ref[...] = gate
    cnt_ref[...] = carry_scr[...]


def _route(logits, counts_in):
    nt = logits.shape[0]
    tr = _divisor_tile(nt, 512, SUBLANES)
    tile = pl.BlockSpec((tr, LANES), lambda i: (i, 0))
    cnt = pl.BlockSpec((1, LANES), lambda i: (0, 0))
    return pl.pallas_call(
        functools.partial(_route_kernel, tr=tr),
        grid=(nt // tr,),
        in_specs=[tile, cnt],
        out_specs=(tile, tile, cnt),
        out_shape=(jax.ShapeDtypeStruct((nt, LANES), i32),
                   jax.ShapeDtypeStruct((nt, LANES), f32),
                   jax.ShapeDtypeStruct((1, LANES), f32)),
        scratch_shapes=[pltpu.VMEM((1, LANES), f32)],
        compiler_params=_cparams(("arbitrary",)),
        name="route_topk",
    )(logits, counts_in)


def _dest_kernel(rec_ref, start_ref, out_ref, *, nsub):
    rec = rec_ref[...]
    lane = lax.broadcasted_iota(i32, rec.shape, 1)
    eid = jnp.bitwise_and(rec, N_EXPERTS - 1).astype(f32)
    rank = jnp.right_shift(rec, N_EXPERTS.bit_length() - 1)
    lanef = lane.astype(f32)
    start = jnp.zeros(rec.shape, f32)
    for k in range(TOP_K):
        mine = lanef == eid[:, k:k + 1]
        st = jnp.sum(jnp.where(mine, start_ref[...], 0.0), axis=1, keepdims=True)
        start = jnp.where(lane == k, st, start)
    out_ref[...] = (start.astype(i32) + rank) * nsub


def _dest_rows(rec, pad_start, nsub):
    nt = rec.shape[0]
    tr = _divisor_tile(nt, 512, SUBLANES)
    tile = pl.BlockSpec((tr, LANES), lambda i: (i, 0))
    start = jnp.zeros((1, LANES), f32).at[0, :N_EXPERTS].set(pad_start.astype(f32))
    out = pl.pallas_call(
        functools.partial(_dest_kernel, nsub=nsub),
        grid=(nt // tr,),
        in_specs=[tile, pl.BlockSpec((1, LANES), lambda i: (0, 0))],
        out_specs=tile,
        out_shape=jax.ShapeDtypeStruct((nt, LANES), i32),
        compiler_params=_cparams(("parallel",)),
        name="moe_dest_rows",
    )(rec, start)
    return out[:, :TOP_K].reshape(-1)


def _dispatch_kernel(dest_ref, h_ref, xin_hbm, xout_hbm, sem, *, nsub, ch):
    del xin_hbm
    base = pl.program_id(0) * ch

    def row_copy(n, k):
        src = pl.multiple_of(n * nsub, nsub)
        dst = pl.multiple_of(dest_ref[(base + n) * TOP_K + k], nsub)
        return pltpu.make_async_copy(h_ref.at[pl.ds(src, nsub)], xout_hbm.at[pl.ds(dst, nsub)],
                                     sem)

    def body(n, carry):
        for k in range(TOP_K):
            row_copy(n, k).start()
        return carry

    lax.fori_loop(0, ch, body, 0, unroll=4)
    total = ch * TOP_K * nsub
    pltpu.make_async_copy(xout_hbm.at[pl.ds(0, total)], xout_hbm.at[pl.ds(0, total)], sem).wait()


def _dispatch(dest_flat, h2rows, xrows, nsub):
    n_tok = dest_flat.shape[0] // TOP_K
    ch = min(DISPATCH_CHUNK, n_tok)
    assert n_tok % ch == 0
    grid_spec = pltpu.PrefetchScalarGridSpec(
        num_scalar_prefetch=1,
        grid=(n_tok // ch,),
        in_specs=[pl.BlockSpec((ch * nsub, LANES), lambda i, dr: (i, 0)),
                  pl.BlockSpec(memory_space=pl.ANY)],
        out_specs=pl.BlockSpec(memory_space=pl.ANY),
        scratch_shapes=[pltpu.SemaphoreType.DMA(())])
    return pl.pallas_call(
        functools.partial(_dispatch_kernel, nsub=nsub, ch=ch),
        grid_spec=grid_spec,
        out_shape=jax.ShapeDtypeStruct(xrows.shape, xrows.dtype),
        input_output_aliases={2: 0},
        compiler_params=_cparams(("arbitrary",)),
        name="moe_dispatch",
    )(dest_flat, h2rows, xrows)


def _expert_kernel(be_ref, nused_ref, x_ref, wgu_ref, bgu_ref, wd_ref, bd_ref, y_ref,
                   wgu_b, wd_b, *, tmx, nsub, dff, chunk):
    r = pl.program_id(0)
    live = r < nused_ref[0]

    @pl.when(jnp.logical_not(live))
    def _():
        y_ref[...] = jnp.zeros_like(y_ref)

    new_expert = jnp.logical_or(r == 0, be_ref[r] != be_ref[jnp.maximum(r - 1, 0)])

    @pl.when(jnp.logical_and(live, new_expert))
    def _():
        for c in range(0, wgu_b.shape[0], chunk):
            wgu_b[c:c + chunk, :] = wgu_ref[0, c:c + chunk, :].astype(bf16)
        for c in range(0, wd_b.shape[0], chunk):
            wd_b[c:c + chunk, :] = wd_ref[0, c:c + chunk, :].astype(bf16)

    @pl.when(live)
    def _():
        x = jnp.concatenate(
            [x_ref[pl.ds(s, tmx, stride=nsub), :].astype(bf16) for s in range(nsub)], axis=1)
        gu = _dot(x, wgu_b[...]) + bgu_ref[0]
        gate = jnp.minimum(gu[:, :dff], SWIGLU_LIMIT)
        up = jnp.clip(gu[:, dff:], -SWIGLU_LIMIT, SWIGLU_LIMIT)
        glu = gate * _sigmoid(SWIGLU_ALPHA * gate)
        y = _dot(((up + 1.0) * glu).astype(bf16), wd_b[...]) + bd_ref[0]
        for s in range(nsub):
            y_ref[pl.ds(s, tmx, stride=nsub), :] = y[:, s * LANES:(s + 1) * LANES]


def _experts(block_e, nused, xrows, wgu, bgu, wd, bd, nsub):
    n_blocks = block_e.shape[0]
    tmx = EXPERT_TILE
    e, d, dff2 = wgu.shape
    dff = dff2 // 2
    live = lambda r, nu: jnp.minimum(r, nu[0] - 1)
    rows = pl.BlockSpec((tmx * nsub, LANES), lambda r, be, nu: (live(r, nu), 0))
    per_e = lambda shp: pl.BlockSpec((1,) + shp, lambda r, be, nu: (be[live(r, nu)], 0, 0))
    grid_spec = pltpu.PrefetchScalarGridSpec(
        num_scalar_prefetch=2,
        grid=(n_blocks,),
        in_specs=[rows, per_e((d, dff2)), per_e((1, dff2)), per_e((dff, d)), per_e((1, d))],
        out_specs=pl.BlockSpec((tmx * nsub, LANES), lambda r, be, nu: (r, 0)),
        scratch_shapes=[pltpu.VMEM((d, dff2), bf16), pltpu.VMEM((dff, d), bf16)])
    return pl.pallas_call(
        functools.partial(_expert_kernel, tmx=tmx, nsub=nsub, dff=dff, chunk=min(256, d, dff)),
        grid_spec=grid_spec,
        out_shape=jax.ShapeDtypeStruct(xrows.shape, f32),
        compiler_params=_cparams(("arbitrary",)),
        name="moe_experts",
    )(block_e, nused, xrows, wgu, bgu.reshape(e, 1, dff2), wd, bd.reshape(e, 1, d))


def _combine_kernel(dest_ref, y_hbm, gate_ref, x1_ref, gt2_ref, gfin_ref, out_ref,
                    buf, sem, *, tc, nsub):
    base = pl.program_id(0) * tc

    def row_copy(n, k):
        src = pl.multiple_of(dest_ref[(base + n) * TOP_K + k], nsub)
        dst = pl.multiple_of((k * tc + n) * nsub, nsub)
        return pltpu.make_async_copy(y_hbm.at[pl.ds(src, nsub)], buf.at[pl.ds(dst, nsub)], sem)

    def body(n, carry):
        for k in range(TOP_K):
            row_copy(n, k).start()
        return carry

    lax.fori_loop(0, tc, body, 0, unroll=4)
    pltpu.make_async_copy(y_hbm.at[pl.ds(0, buf.shape[0])], buf, sem).wait()

    g = gate_ref[...]
    cols = []
    for s in range(nsub):
        acc = None
        for k in range(TOP_K):
            term = g[:, k:k + 1] * buf[pl.ds(k * tc * nsub + s, tc, stride=nsub), :]
            acc = term if acc is None else acc + term
        cols.append(acc)
    y = jnp.concatenate(cols, axis=1)
    out_ref[...] = _rms(x1_ref[...] + gt2_ref[0] * y) * gfin_ref[...]


def _combine(dest_flat, yrows, gates, x1, gt2, g_final, t_per_mod, nsub):
    n_tok, d = x1.shape
    tc = min(TOKEN_TILE, n_tok)
    assert n_tok % tc == 0
    rows = gt2.shape[1]
    if rows == 1:
        per_mod = t_per_mod // tc
        mod = pl.BlockSpec((1, 1, d), lambda i, dr: (i // per_mod, 0, 0))
    else:
        mod = pl.BlockSpec((1, tc, d), lambda i, dr: (0, i, 0))
    blk = lambda i, dr: (i, 0)
    gfin = g_final.reshape(1, d)
    grid_spec = pltpu.PrefetchScalarGridSpec(
        num_scalar_prefetch=1,
        grid=(n_tok // tc,),
        in_specs=[pl.BlockSpec(memory_space=pl.ANY), pl.BlockSpec((tc, LANES), blk),
                  pl.BlockSpec((tc, d), blk), mod,
                  pl.BlockSpec((1, d), lambda i, dr: (0, 0))],
        out_specs=pl.BlockSpec((tc, d), blk),
        scratch_shapes=[pltpu.VMEM((TOP_K * tc * nsub, LANES), f32), pltpu.SemaphoreType.DMA(())])
    return pl.pallas_call(
        functools.partial(_combine_kernel, tc=tc, nsub=nsub),
        grid_spec=grid_spec,
        out_shape=jax.ShapeDtypeStruct((n_tok, d), f32),
        compiler_params=_cparams(("arbitrary",)),
        name="moe_combine",
    )(dest_flat, yrows, gates, x1, gt2, gfin)


def kernel(x_prompt, x_sample, c_prompt, c_sample, cache_k, cache_v, cache_logf, state_gla,
           page_table, g_mix, g_ffn, g_final, w_ada, b_ada, w_in, b_f, w_alpha, b_alpha, g_gla,
           w_branch, w_o, w_router, b_router, w_gu, b_gu, w_d, b_d):
    depth = g_mix.shape[0]
    assert depth == 1, "one decoder layer"
    bp, t, d = x_prompt.shape
    db = x_sample.shape[0]
    assert x_sample.shape[1] == 1
    nsub = d // LANES
    ntp = bp * t
    nt = ntp + db

    nmod = bp + db
    pad = (-nmod) % SUBLANES
    c_all = jnp.concatenate([c_prompt, c_sample, jnp.zeros((pad, d), f32)], axis=0)
    mod = _adaln(c_all, w_ada[0], b_ada[0])
    sh1, sc1, gt1, sh2, sc2, gt2 = [mod[:, i * d:(i + 1) * d] for i in range(6)]
    grp_p = lambda a: a[:bp].reshape(bp, 1, d)
    grp_s = lambda a: a[bp:nmod].reshape(1, db, d)

    w_proj = _inproj_weights(w_in[0], d)
    wal = jnp.zeros((LANES, WK_B), f32).at[H_A:H_A + GLA_RANK].set(w_alpha[0]).astype(bf16)
    bal = b_alpha[0].reshape(1, WK_B)
    gg = g_gla[0].reshape(1, WV_B)
    wba = w_branch[0, :W_A].astype(bf16)
    wbb = w_branch[0, W_A:].astype(bf16)
    wo = w_o[0].astype(bf16)
    wr = jnp.zeros((d, LANES), f32).at[:, :N_EXPERTS].set(w_router[0]).astype(bf16)
    br = jnp.zeros((1, LANES), f32).at[0, :N_EXPERTS].set(b_router[0])

    (qat, kat, vt, ka, va, qb, kb, vb, rb, ga, gb, sm, fat) = _inproj(
        x_prompt, grp_p(1.0 + sc1), grp_p(sh1), g_mix[0], w_proj, True)
    logft, cumt = _logf(fat, b_f[0], True)
    oa = _fox_prompt(qat, kat, vt, jnp.transpose(cumt, (0, 2, 1)))
    ob, gla_p = _gla_prompt(qb, kb, vb, rb, sm, wal, bal, gg)
    x1_p, h2_p, lg_p = _mix(x_prompt, oa, ob, ga, gb, grp_p(gt1), grp_p(1.0 + sc2), grp_p(sh2),
                            g_ffn[0], wba, wbb, wo, wr, br)

    xs = x_sample.reshape(1, db, d)
    (qat_s, ka_s, va_s, qb_s, kb_s, vb_s, rb_s, ga_s, gb_s, sm_s, fat_s) = _inproj(
        xs, grp_s(1.0 + sc1), grp_s(sh1), g_mix[0], w_proj, False)
    logft_s, = _logf(fat_s, b_f[0], False)
    lfnew = jnp.transpose(logft_s[0], (1, 0)).reshape(db, H_A, 1)
    qbd = jnp.einsum('hnd,hc->nchd', qat_s[0], jnp.eye(H_A, dtype=bf16)).reshape(db, H_A, W_A)
    oa_s = _fox_decode(qbd, ka_s.reshape(db, 1, W_A), va_s.reshape(db, 1, W_A), lfnew,
                       jnp.transpose(cache_k[0], (0, 2, 3, 1)),
                       jnp.transpose(cache_v[0], (0, 2, 3, 1)),
                       jnp.transpose(cache_logf[0], (0, 2, 1)), page_table)
    ob_s, gla_s = _gla_decode(qb_s[0], kb_s[0], vb_s[0], rb_s[0], sm_s[0], wal, bal, gg,
                              state_gla[0])
    x1_s, h2_s, lg_s = _mix(xs, oa_s.reshape(db, W_A).T[None], ob_s.reshape(1, db, WV_B), ga_s, gb_s,
                            grp_s(gt1), grp_s(1.0 + sc2), grp_s(sh2), g_ffn[0],
                            wba, wbb, wo, wr, br)

    rec_p, gates_p, cnt_p = _route(lg_p, jnp.zeros((1, LANES), f32))
    rec_s, gates_s, cnt = _route(lg_s, cnt_p)
    counts = cnt[0, :N_EXPERTS].astype(i32)
    padded = (counts + EXPERT_TILE - 1) // EXPERT_TILE * EXPERT_TILE
    pad_end = jnp.cumsum(padded).astype(i32)
    pad_start = pad_end - padded

    dest_p = _dest_rows(rec_p, pad_start, nsub)
    dest_s = _dest_rows(rec_s, pad_start, nsub)
    n_blocks = -(-(nt * TOP_K + N_EXPERTS * (EXPERT_TILE - 1)) // EXPERT_TILE)
    block_start = jnp.arange(n_blocks, dtype=i32) * EXPERT_TILE
    block_e = jnp.minimum(jnp.sum(pad_end[None, :] <= block_start[:, None], axis=1),
                          N_EXPERTS - 1).astype(i32)
    nused = (pad_end[-1:] // EXPERT_TILE).astype(i32)

    xrows = jnp.zeros((n_blocks * EXPERT_TILE * nsub, LANES), f32)
    xrows = _dispatch(dest_p, h2_p, xrows, nsub)
    xrows = _dispatch(dest_s, h2_s, xrows, nsub)
    yrows = _experts(block_e, nused, xrows, w_gu[0], b_gu[0], w_d[0], b_d[0], nsub)
    y_p = _combine(dest_p, yrows, gates_p, x1_p, grp_p(gt2), g_final, t, nsub)
    y_s = _combine(dest_s, yrows, gates_s, x1_s, grp_s(gt2), g_final, 1, nsub)

    logf_p = jnp.transpose(logft, (0, 2, 1))
    return (y_p.reshape(bp, t, d), y_s.reshape(db, 1, d),
            jnp.transpose(ka, (0, 3, 1, 2))[None], jnp.transpose(va, (0, 3, 1, 2))[None],
            logf_p.reshape(1, bp, t, H_A), gla_p.reshape(1, bp, H_B, DK_B, DV_B),
            ka_s.reshape(1, db, 1, H_A, DH_A), va_s.reshape(1, db, 1, H_A, DH_A),
            lfnew.reshape(1, db, 1, H_A), gla_s.reshape(1, db, H_B, DK_B, DV_B))
```

```python
import functools

import jax
import jax.numpy as jnp
from jax import lax
from jax.experimental import pallas as pl
from jax.experimental.pallas import tpu as pltpu

f32 = jnp.float32
bf16 = jnp.bfloat16
i32 = jnp.int32

H_A, DH_A = 8, 64
H_B, DK_B, DV_B = 4, 64, 128
GLA_RANK = 16
GLA_TAU = 16.0
N_EXPERTS = 32
TOP_K = 4
SWIGLU_LIMIT = 7.0
SWIGLU_ALPHA = 1.702
RMS_EPS = 1e-6
W_A = H_A * DH_A
WK_B = H_B * DK_B
WV_B = H_B * DV_B

LANES = 128
SUBLANES = 8
VMEM_LIMIT_MB = 56

TOKEN_TILE = 256
ATTN_TILE = 256
GLA_TILE = 256
EXPERT_TILE = 512
PAGES_PER_STEP = 16
DISPATCH_CHUNK = 256

_NT = (((1,), (1,)), ((), ()))
_TN = (((0,), (0,)), ((), ()))


def _cparams(semantics, vmem_mb=VMEM_LIMIT_MB):
    return pltpu.CompilerParams(dimension_semantics=semantics,
                                vmem_limit_bytes=vmem_mb << 20)


def _sigmoid(x):
    return 1.0 / (1.0 + jnp.exp(-x))


def _log_sigmoid(x):
    return jnp.minimum(x, 0.0) - jnp.log1p(jnp.exp(-jnp.abs(x)))


def _split3(x):
    hi = x.astype(bf16)
    r = x - hi.astype(f32)
    mid = r.astype(bf16)
    lo = (r - mid.astype(f32)).astype(bf16)
    return hi, mid, lo


def _dot(a, b):
    return jnp.dot(a, b, preferred_element_type=f32)


def _dotg(a, b, dims):
    return lax.dot_general(a, b, dims, preferred_element_type=f32)


def _rms(x):
    return x * lax.rsqrt(jnp.mean(x * x, axis=-1, keepdims=True) + RMS_EPS)


def _divisor_tile(n, cap, mult):
    best = None
    t = mult
    while t <= min(n, cap):
        if n % t == 0:
            best = t
        t += mult
    assert best is not None, (n, cap, mult)
    return best


def _mod_kernel(c_ref, w_ref, b_ref, o_ref):
    c = c_ref[...]
    s = (c * _sigmoid(c)).astype(bf16)
    o_ref[...] = _dot(s, w_ref[...].astype(bf16)) + b_ref[...]


def _adaln(c_all, w_ada, b_ada):
    r, d = c_all.shape
    n = w_ada.shape[1]
    tn = _divisor_tile(n, 1536, LANES)
    return pl.pallas_call(
        _mod_kernel,
        grid=(n // tn,),
        in_specs=[pl.BlockSpec((r, d), lambda j: (0, 0)),
                  pl.BlockSpec((d, tn), lambda j: (0, j)),
                  pl.BlockSpec((1, tn), lambda j: (0, j))],
        out_specs=pl.BlockSpec((r, tn), lambda j: (0, j)),
        out_shape=jax.ShapeDtypeStruct((r, n), f32),
        compiler_params=_cparams(("arbitrary",)),
        name="adaln_mod",
    )(c_all, w_ada, b_ada.reshape(1, n))


def _inproj_kernel(x_ref, sc_ref, sh_ref, g_ref, wbig_ref, wsm_ref, wfa_ref, wkt_ref, wvt_ref,
                   *out_refs, d, time_minor):
    if time_minor:
        (qat_ref, kat_ref, vt_ref, ka_ref, va_ref, qb_ref, kb_ref, vb_ref, rb_ref,
         ga_ref, gb_ref, sm_ref, fat_ref) = out_refs
    else:
        (qat_ref, ka_ref, va_ref, qb_ref, kb_ref, vb_ref, rb_ref,
         ga_ref, gb_ref, sm_ref, fat_ref) = out_refs
    x = x_ref[0]
    h = (_rms(x) * g_ref[...]) * sc_ref[0] + sh_ref[0]
    hb = h.astype(bf16)
    off = [0]

    def mm(width):
        r = _dot(hb, wbig_ref[:, off[0]:off[0] + width])
        off[0] += width
        return r

    qa = mm(W_A)
    ka = mm(W_A)
    va = mm(W_A)
    for hh in range(H_A):
        sl = slice(hh * DH_A, (hh + 1) * DH_A)
        qat_ref[0, hh] = qa[:, sl].astype(bf16)
        if time_minor:
            kat_ref[0, hh] = ka[:, sl].astype(bf16)
    if time_minor:
        tm = x.shape[0]
        kt = _dotg(wkt_ref[...], hb, _NT).reshape(H_A, DH_A, tm)
        vt = _dotg(wvt_ref[...], hb, _NT).reshape(H_A, DH_A, tm)
        ka_ref[0] = kt
        va_ref[0] = vt
        vt_ref[0] = vt.astype(bf16)
    else:
        ka_ref[0] = ka
        va_ref[0] = va
    qb_ref[0] = mm(WK_B)
    kb_ref[0] = mm(WK_B)
    vb_ref[0] = mm(WV_B)
    rb_ref[0] = mm(WV_B)
    ga_ref[0] = mm(d)
    gb_ref[0] = mm(d)
    sm_ref[0] = _dot(hb, wsm_ref[...])
    fat_ref[0] = _dotg(wfa_ref[...].astype(bf16), hb, _NT)


def _inproj_weights(w_in, d):
    sizes = (W_A, W_A, W_A, H_A, WK_B, WK_B, WV_B, WV_B, GLA_RANK, d, d)
    offs = [0]
    for s in sizes:
        offs.append(offs[-1] + s)
    seg = lambda i: w_in[:, offs[i]:offs[i + 1]]
    wbig = jnp.concatenate(
        [seg(0) * (DH_A ** -0.5), seg(1), seg(2), seg(4) * (DK_B ** -0.5), seg(5),
         seg(6), seg(7), seg(9), seg(10)], axis=1).astype(bf16)
    wsm = jnp.concatenate(
        [seg(3), seg(8), jnp.zeros((d, LANES - H_A - GLA_RANK), f32)], axis=1).astype(bf16)
    wfa = seg(3).T
    wkt = seg(1).T.astype(bf16)
    wvt = seg(2).T.astype(bf16)
    return wbig, wsm, wfa, wkt, wvt


def _inproj(x, sc, sh, g_mix, weights, time_minor):
    wbig, wsm, wfa, wkt, wvt = weights
    b, t, d = x.shape
    tm = min(TOKEN_TILE, t)
    assert t % tm == 0
    rows = sc.shape[1]
    mod_block = (1, 1, d) if rows == 1 else (1, tm, d)
    mod_map = (lambda bi, i: (bi, 0, 0)) if rows == 1 else (lambda bi, i: (bi, i, 0))
    tok3 = lambda w: pl.BlockSpec((1, tm, w), lambda bi, i: (bi, i, 0))
    head4 = pl.BlockSpec((1, H_A, tm, DH_A), lambda bi, i: (bi, 0, i, 0))
    const2 = lambda a: pl.BlockSpec(a.shape, lambda bi, i: (0, 0))
    sds = jax.ShapeDtypeStruct
    headt = pl.BlockSpec((1, H_A, DH_A, tm), lambda bi, i: (bi, 0, 0, i))
    if time_minor:
        kv_shape = [sds((b, H_A, t, DH_A), bf16), sds((b, H_A, DH_A, t), bf16),
                    sds((b, H_A, DH_A, t), f32), sds((b, H_A, DH_A, t), f32)]
        kv_specs = [head4, headt, headt, headt]
    else:
        kv_shape = [sds((b, t, W_A), f32), sds((b, t, W_A), f32)]
        kv_specs = [tok3(W_A), tok3(W_A)]
    out_shape = tuple(
        [sds((b, H_A, t, DH_A), bf16)] + kv_shape
        + [sds((b, t, WK_B), f32), sds((b, t, WK_B), f32),
           sds((b, t, WV_B), f32), sds((b, t, WV_B), f32),
           sds((b, t, d), f32), sds((b, t, d), f32),
           sds((b, t, LANES), f32), sds((b, H_A, t), f32)])
    out_specs = tuple(
        [head4] + kv_specs
        + [tok3(WK_B), tok3(WK_B), tok3(WV_B), tok3(WV_B), tok3(d), tok3(d), tok3(LANES),
           pl.BlockSpec((1, H_A, tm), lambda bi, i: (bi, 0, i))])
    g2 = g_mix.reshape(1, d)
    return pl.pallas_call(
        functools.partial(_inproj_kernel, d=d, time_minor=time_minor),
        grid=(b, t // tm),
        in_specs=[tok3(d), pl.BlockSpec(mod_block, mod_map), pl.BlockSpec(mod_block, mod_map),
                  const2(g2), const2(wbig), const2(wsm), const2(wfa), const2(wkt), const2(wvt)],
        out_specs=out_specs,
        out_shape=out_shape,
        compiler_params=_cparams(("parallel", "arbitrary")),
        name="inproj",
    )(x, sc, sh, g2, wbig, wsm, wfa, wkt, wvt)


def _logf_kernel(fa_ref, bf_ref, logf_ref, *cum_refs, t, cb):
    lf = _log_sigmoid(fa_ref[...] + bf_ref[...])
    logf_ref[...] = lf
    if not cum_refs:
        return
    cum_ref, = cum_refs
    r = lax.broadcasted_iota(i32, (cb, cb), 0)
    c = lax.broadcasted_iota(i32, (cb, cb), 1)
    triu = jnp.where(r <= c, 1.0, 0.0).astype(bf16)
    carry = jnp.zeros((lf.shape[0], 1), f32)
    for j in range(t // cb):
        hi, mid, lo = _split3(lf[:, j * cb:(j + 1) * cb])
        blk = _dot(hi, triu) + _dot(mid, triu) + _dot(lo, triu) + carry
        cum_ref[:, j * cb:(j + 1) * cb] = blk
        carry = blk[:, cb - 1:cb]


def _logf(fat, b_f, with_cumsum):
    b, _, t = fat.shape
    rows = b * H_A
    fa2 = fat.reshape(rows, t)
    bf2 = jnp.tile(b_f.reshape(H_A, 1), (b, 1))
    cb = min(256, t)
    full = lambda a: pl.BlockSpec(a.shape, lambda: (0,) * a.ndim)
    n_out = 2 if with_cumsum else 1
    outs = pl.pallas_call(
        functools.partial(_logf_kernel, t=t, cb=cb),
        in_specs=[full(fa2), full(bf2)],
        out_specs=tuple(pl.BlockSpec((rows, t), lambda: (0, 0)) for _ in range(n_out)),
        out_shape=tuple(jax.ShapeDtypeStruct((rows, t), f32) for _ in range(n_out)),
        name="log_forget",
    )(fa2, bf2)
    return tuple(o.reshape(b, H_A, t) for o in outs)


def _bias_lanes(col, ones_first):
    hi, mid, lo = [p.astype(f32) for p in _split3(col)]
    lane = lax.broadcasted_iota(i32, (col.shape[0], DH_A), 1)
    base = 3 if ones_first else 0
    parts = jnp.where(lane == base, hi, jnp.where(lane == base + 1, mid,
                      jnp.where(lane == base + 2, lo, 0.0)))
    ones = (lane < 3) if ones_first else ((lane >= 3) & (lane < 6))
    return jnp.where(ones, 1.0, parts)


def _fox_kernel(q_ref, k_ref, vt_ref, cum_ref, o_ref, kaug, qaug, *, tq):
    i = pl.program_id(1)

    @pl.when(i == 0)
    def _():
        cum_all = cum_ref[0]
        for h in range(H_A):
            kaug[h, :, :DH_A] = k_ref[0, h]
            kaug[h, :, DH_A:] = _bias_lanes(-cum_all[:, h:h + 1], False).astype(bf16)

    cum_q = cum_ref[0, pl.ds(pl.multiple_of(i * tq, tq), tq), :]
    for h in range(H_A):
        qaug[h, :, :DH_A] = q_ref[0, h]
        qaug[h, :, DH_A:] = _bias_lanes(cum_q[:, h:h + 1], True).astype(bf16)
    key = lax.broadcasted_iota(i32, (tq, tq), 0)
    qry = lax.broadcasted_iota(i32, (tq, tq), 1)

    def step(j, carry, masked):
        start = pl.multiple_of(j * tq, tq)
        scores = [_dotg(kaug[h, pl.ds(start, tq), :], qaug[h], _NT) for h in range(H_A)]
        stats = []
        for h in range(H_A):
            m_old, l_old, _ = carry[h]
            s = scores[h]
            if masked:
                s = jnp.where(key <= qry, s, -jnp.inf)
            m_new = jnp.maximum(m_old, jnp.max(s, axis=0, keepdims=True))
            alpha = jnp.exp(m_old - m_new)
            p = jnp.exp(s - m_new)
            l_new = alpha * l_old + jnp.sum(p, axis=0, keepdims=True)
            stats.append((m_new, l_new, alpha, p.astype(bf16)))
        out = []
        for h in range(H_A):
            m_new, l_new, alpha, p = stats[h]
            acc_new = alpha * carry[h][2] + _dot(vt_ref[0, h, :, pl.ds(start, tq)], p)
            out.append((m_new, l_new, acc_new))
        return tuple(out)

    init = tuple((jnp.full((1, tq), -jnp.inf, f32), jnp.zeros((1, tq), f32),
                  jnp.zeros((DH_A, tq), f32)) for _ in range(H_A))
    carry = lax.fori_loop(0, i, functools.partial(step, masked=False), init)
    carry = step(i, carry, True)
    for h in range(H_A):
        _, l, acc = carry[h]
        o_ref[0, h * DH_A:(h + 1) * DH_A, :] = (acc / l).astype(bf16)


def _fox_prompt(qat, kat, vt, cum):
    b, _, t, _ = qat.shape
    tq = min(ATTN_TILE, t)
    assert t % tq == 0
    return pl.pallas_call(
        functools.partial(_fox_kernel, tq=tq),
        grid=(b, t // tq),
        in_specs=[pl.BlockSpec((1, H_A, tq, DH_A), lambda bi, i: (bi, 0, i, 0)),
                  pl.BlockSpec((1, H_A, t, DH_A), lambda bi, i: (bi, 0, 0, 0)),
                  pl.BlockSpec((1, H_A, DH_A, t), lambda bi, i: (bi, 0, 0, 0)),
                  pl.BlockSpec((1, t, H_A), lambda bi, i: (bi, 0, 0))],
        out_specs=pl.BlockSpec((1, W_A, tq), lambda bi, i: (bi, 0, i)),
        out_shape=jax.ShapeDtypeStruct((b, W_A, t), bf16),
        scratch_shapes=[pltpu.VMEM((H_A, t, 2 * DH_A), bf16), pltpu.VMEM((H_A, tq, 2 * DH_A), bf16)],
        compiler_params=_cparams(("parallel", "arbitrary")),
        name="fox_prompt",
    )(qat, kat, vt, cum)


def _fox_decode_kernel(pt_ref, q_ref, knew_ref, vnew_ref, lfnew_ref, *refs, pg, nstep):
    del pt_ref
    k_refs, v_refs, lf_refs = refs[:pg], refs[pg:2 * pg], refs[2 * pg:3 * pg]
    o_ref = refs[3 * pg]
    p_scr, m_scr, self_scr, acc_scr, carry_scr = refs[3 * pg + 1:]
    j = pl.program_id(1)
    q = q_ref[0]
    psz = lf_refs[0].shape[2]
    lane = lax.broadcasted_iota(i32, (H_A, W_A), 1)
    sub = lax.broadcasted_iota(i32, (H_A, W_A), 0)
    own = (lane // DH_A) == sub
    rnd = lambda z: z.astype(bf16).astype(f32)

    @pl.when(j == 0)
    def _():
        s_self = jnp.sum(q.astype(f32) * rnd(knew_ref[0]), axis=1, keepdims=True)
        self_scr[...] = s_self
        m_scr[...] = s_self
        carry_scr[...] = lfnew_ref[0]

    @pl.when(j < nstep)
    def _():
        r = lax.broadcasted_iota(i32, (psz, 2 * psz), 0)
        c = lax.broadcasted_iota(i32, (psz, 2 * psz), 1)
        later = jnp.where(c < psz, jnp.where(r > c, 1.0, 0.0), 1.0).astype(bf16)
        group = nstep - 1 - j
        carry = carry_scr[...]
        m = m_scr[...]
        hi, mid, lo = _split3(jnp.concatenate([lf_refs[p][0] for p in range(pg)], axis=0))
        suf_all = _dot(hi, later) + _dot(mid, later) + _dot(lo, later)
        for p in reversed(range(pg)):
            suf = suf_all[p * H_A:(p + 1) * H_A]
            kt = k_refs[p][0].reshape(W_A, psz).astype(bf16)
            s = _dot(q, kt) + (carry + suf[:, :psz])
            carry = carry + suf[:, psz:psz + 1]
            p_scr[group * pg + p] = s
            m = jnp.maximum(m, jnp.max(s, axis=1, keepdims=True))
        carry_scr[...] = carry
        m_scr[...] = m

    @pl.when(j == nstep)
    def _():
        m = m_scr[...]
        e_self = jnp.exp(self_scr[...] - m)
        e = jnp.exp(p_scr[...] - m[None])
        l = e_self + jnp.sum(jnp.sum(e, axis=0), axis=1, keepdims=True)
        p_scr[...] = e / l[None]
        acc_scr[...] = jnp.where(own, rnd(e_self / l) * rnd(vnew_ref[0]), 0.0)

    @pl.when(j >= nstep)
    def _():
        group = 2 * nstep - 1 - j
        acc = acc_scr[...]
        for p in range(pg):
            vt = v_refs[p][0].reshape(W_A, psz).astype(bf16)
            acc = acc + _dotg(p_scr[group * pg + p].astype(bf16), vt, _NT)
        acc_scr[...] = acc

    @pl.when(j == 2 * nstep - 1)
    def _():
        o_ref[0] = jnp.sum(jnp.where(own, acc_scr[...], 0.0), axis=0, keepdims=True)


def _fox_decode(qbd, knew, vnew, lfnew, cache_kt, cache_vt, cache_lft, page_table):
    db = qbd.shape[0]
    psz = cache_kt.shape[-1]
    n_pages = page_table.shape[1]
    pg = PAGES_PER_STEP if n_pages % PAGES_PER_STEP == 0 else 1
    nstep = n_pages // pg
    pt = page_table.reshape(-1).astype(i32)

    def page_spec(mid, score_phase):
        def spec(p):
            def imap(bi, j, pt_ref):
                step = jnp.minimum(j, nstep - 1) if score_phase else jnp.maximum(j - nstep, 0)
                return (pt_ref[bi * n_pages + (nstep - 1 - step) * pg + p],) + (0,) * (1 + len(mid))
            return pl.BlockSpec((1,) + mid + (psz,), imap)
        return [spec(p) for p in range(pg)]

    row3 = lambda a: pl.BlockSpec((1,) + a.shape[1:], lambda bi, j, pt_ref: (bi, 0, 0))
    grid_spec = pltpu.PrefetchScalarGridSpec(
        num_scalar_prefetch=1,
        grid=(db, 2 * nstep),
        in_specs=[row3(qbd), row3(knew), row3(vnew), row3(lfnew)]
        + page_spec((H_A, DH_A), True) + page_spec((H_A, DH_A), False) + page_spec((H_A,), True),
        out_specs=pl.BlockSpec((1, 1, W_A), lambda bi, j, pt_ref: (bi, 0, 0)),
        scratch_shapes=[pltpu.VMEM((n_pages, H_A, psz), f32), pltpu.VMEM((H_A, 1), f32),
                        pltpu.VMEM((H_A, 1), f32), pltpu.VMEM((H_A, W_A), f32),
                        pltpu.VMEM((H_A, 1), f32)])
    return pl.pallas_call(
        functools.partial(_fox_decode_kernel, pg=pg, nstep=nstep),
        grid_spec=grid_spec,
        out_shape=jax.ShapeDtypeStruct((db, 1, W_A), f32),
        compiler_params=_cparams(("parallel", "arbitrary")),
        name="fox_decode",
    )(pt, qbd, knew, vnew, lfnew, *([cache_kt] * pg), *([cache_vt] * pg), *([cache_lft] * pg))


def _gla_kernel(qb_ref, kb_ref, vb_ref, rb_ref, sm_ref, wal_ref, bal_ref, gg_ref,
                ob_ref, sout_ref, s_scr, *, tt):
    i = pl.program_id(1)

    @pl.when(i == 0)
    def _():
        s_scr[...] = jnp.zeros_like(s_scr)

    q = qb_ref[0]
    k = kb_ref[0]
    z = _dot(sm_ref[0].astype(bf16), wal_ref[...]) + bal_ref[...]
    a = _log_sigmoid(z) * (1.0 / GLA_TAU)

    row = lax.broadcasted_iota(i32, (tt, tt), 0)
    col = lax.broadcasted_iota(i32, (tt, tt), 1)
    tril = jnp.where(col <= row, 1.0, 0.0).astype(bf16)
    hi, mid, lo = _split3(a)
    b = _dot(tril, hi) + _dot(tril, mid) + _dot(tril, lo)

    rowk = lax.broadcasted_iota(i32, (tt, WK_B), 0)
    hs = [slice(h * DK_B, (h + 1) * DK_B) for h in range(H_B)]
    amat = [jnp.zeros((tt, tt), f32) for _ in range(H_B)]
    edge = b
    for lvl in range(tt.bit_length() - 1):
        half = 1 << lvl
        upper = ((rowk >> lvl) & 1) == 1
        kt = jnp.where(upper, 0.0, k * jnp.exp(edge - b)).astype(bf16)
        edge_q = pltpu.roll(edge, half, 0)
        qt = jnp.where(upper, q * jnp.exp(b - edge_q), 0.0).astype(bf16)
        same = (row >> (lvl + 1)) == (col >> (lvl + 1))
        for h in range(H_B):
            amat[h] = amat[h] + jnp.where(same, _dotg(qt[:, hs[h]], kt[:, hs[h]], _NT), 0.0)
        edge = jnp.where(upper, edge, pltpu.roll(edge, tt - half, 0))
    q16 = q.astype(bf16)
    k16 = k.astype(bf16)
    for h in range(H_B):
        amat[h] = amat[h] + jnp.where(row == col, _dotg(q16[:, hs[h]], k16[:, hs[h]], _NT), 0.0)

    qe = (q * jnp.exp(b)).astype(bf16)
    ke = (k * jnp.exp(edge - b)).astype(bf16)
    e_last = jnp.exp(edge[0:1, :])
    v16 = vb_ref[0].astype(bf16)
    r = rb_ref[0]
    er = lax.broadcasted_iota(i32, (DK_B, DK_B), 0)
    ec = lax.broadcasted_iota(i32, (DK_B, DK_B), 1)
    for h in range(H_B):
        vs = slice(h * DV_B, (h + 1) * DV_B)
        vh = v16[:, vs]
        state = s_scr[h]
        o = _dot(amat[h].astype(bf16), vh) + _dot(qe[:, hs[h]], state.astype(bf16))
        decay_col = jnp.sum(
            jnp.where(er == ec, jnp.broadcast_to(e_last[:, hs[h]], (DK_B, DK_B)), 0.0),
            axis=1, keepdims=True)
        s_scr[h] = decay_col * state + _dotg(ke[:, hs[h]], vh, _TN)
        rh = r[:, vs]
        ob_ref[0, :, vs] = (_rms(o) * gg_ref[:, vs] * (rh * _sigmoid(rh))).astype(bf16)

    @pl.when(i == pl.num_programs(1) - 1)
    def _():
        sout_ref[0] = s_scr[...]


def _gla_prompt(qb, kb, vb, rb, sm, wal, bal, gg):
    b, t, _ = qb.shape
    tt = min(GLA_TILE, t)
    assert t % tt == 0 and tt & (tt - 1) == 0
    tok3 = lambda w: pl.BlockSpec((1, tt, w), lambda bi, i: (bi, i, 0))
    const2 = lambda a: pl.BlockSpec(a.shape, lambda bi, i: (0, 0))
    return pl.pallas_call(
        functools.partial(_gla_kernel, tt=tt),
        grid=(b, t // tt),
        in_specs=[tok3(WK_B), tok3(WK_B), tok3(WV_B), tok3(WV_B), tok3(LANES),
                  const2(wal), const2(bal), const2(gg)],
        out_specs=(tok3(WV_B),
                   pl.BlockSpec((1, H_B, DK_B, DV_B), lambda bi, i: (bi, 0, 0, 0))),
        out_shape=(jax.ShapeDtypeStruct((b, t, WV_B), bf16),
                   jax.ShapeDtypeStruct((b, H_B, DK_B, DV_B), f32)),
        scratch_shapes=[pltpu.VMEM((H_B, DK_B, DV_B), f32)],
        compiler_params=_cparams(("parallel", "arbitrary")),
        name="gla_prompt",
    )(qb, kb, vb, rb, sm, wal, bal, gg)


def _loga_kernel(sm_ref, wal_ref, bal_ref, o_ref):
    z = _dot(sm_ref[...].astype(bf16), wal_ref[...]) + bal_ref[...]
    o_ref[...] = _log_sigmoid(z) * (1.0 / GLA_TAU)


def _gla_decode_kernel(q_ref, k_ref, a_ref, v_ref, r_ref, s_ref, gg_ref, o_ref, so_ref):
    for h in range(H_B):
        q, k, a = q_ref[0, h], k_ref[0, h], a_ref[0, h]
        v = v_ref[0, h]
        state = s_ref[0, h]
        ea = jnp.exp(a)
        rnd = lambda z: z.astype(bf16).astype(f32)
        qk = jnp.sum(q * k, axis=0, keepdims=True)
        o = qk * v + jnp.sum(rnd(q * ea) * rnd(state), axis=0, keepdims=True)
        so_ref[0, h] = ea * state + k * v
        r = r_ref[0, h]
        o_ref[0, h] = _rms(o) * gg_ref[h] * (r * _sigmoid(r))


def _gla_decode(qb, kb, vb, rb, sm, wal, bal, gg, state):
    db = qb.shape[0]
    full = lambda a: pl.BlockSpec(a.shape, lambda: (0,) * a.ndim)
    loga = pl.pallas_call(
        _loga_kernel,
        in_specs=[full(sm), full(wal), full(bal)],
        out_specs=pl.BlockSpec((db, WK_B), lambda: (0, 0)),
        out_shape=jax.ShapeDtypeStruct((db, WK_B), f32),
        name="gla_log_decay",
    )(sm, wal, bal)
    col = lambda a: a.reshape(db, H_B, DK_B, 1)
    rowv = lambda a: a.reshape(db, H_B, 1, DV_B)
    gg4 = gg.reshape(H_B, 1, DV_B)
    b4 = lambda shp: pl.BlockSpec((1,) + shp, lambda bi: (bi, 0, 0, 0))
    o, s_new = pl.pallas_call(
        _gla_decode_kernel,
        grid=(db,),
        in_specs=[b4((H_B, DK_B, 1))] * 3 + [b4((H_B, 1, DV_B))] * 2
        + [b4((H_B, DK_B, DV_B)), pl.BlockSpec(gg4.shape, lambda bi: (0, 0, 0))],
        out_specs=(b4((H_B, 1, DV_B)), b4((H_B, DK_B, DV_B))),
        out_shape=(jax.ShapeDtypeStruct((db, H_B, 1, DV_B), f32),
                   jax.ShapeDtypeStruct((db, H_B, DK_B, DV_B), f32)),
        compiler_params=_cparams(("parallel",)),
        name="gla_decode",
    )(col(qb), col(kb), col(loga), rowv(vb), rowv(rb), state, gg4)
    return o.reshape(db, WV_B), s_new


def _mix_kernel(x_ref, oa_ref, ob_ref, ga_ref, gb_ref, gt1_ref, sc2_ref, sh2_ref, gf_ref,
                wba_ref, wbb_ref, wo_ref, wr_ref, br_ref, x1_ref, h2_ref, lg_ref, *, nsub):
    x = x_ref[0]
    tm = x.shape[0]
    ya = _dotg(oa_ref[0].astype(bf16), wba_ref[...], _TN)
    yb = _dot(ob_ref[0].astype(bf16), wbb_ref[...])
    m = _sigmoid(ga_ref[0]) * ya + _sigmoid(gb_ref[0]) * yb
    x1 = x + gt1_ref[0] * _dot(m.astype(bf16), wo_ref[...])
    x1_ref[...] = x1
    h2 = (_rms(x1) * gf_ref[...]) * sc2_ref[0] + sh2_ref[0]
    hb = h2.astype(bf16)
    lg_ref[...] = _dot(hb, wr_ref[...]) + br_ref[...]
    half = nsub * LANES
    hi = lax.bitcast_convert_type(hb[:, :half].astype(f32), jnp.uint32)
    lo = lax.bitcast_convert_type(hb[:, half:].astype(f32), jnp.uint32)
    packed = jnp.bitwise_or(hi, jnp.right_shift(lo, jnp.uint32(16)))
    for s in range(nsub):
        h2_ref[pl.ds(s, tm, stride=nsub), :] = packed[:, s * LANES:(s + 1) * LANES]


def _mix(x, oa, ob, ga, gb, gt1, sc2, sh2, g_ffn, wba, wbb, wo, wr, br):
    b, t, d = x.shape
    nt = b * t
    nsub = d // (2 * LANES)
    tm = min(TOKEN_TILE, t)
    assert t % tm == 0 and d % (2 * LANES) == 0
    rows = gt1.shape[1]
    mod_block = (1, 1, d) if rows == 1 else (1, tm, d)
    mod_map = (lambda bi, i: (bi, 0, 0)) if rows == 1 else (lambda bi, i: (bi, i, 0))
    mod = pl.BlockSpec(mod_block, mod_map)
    tok3 = lambda w: pl.BlockSpec((1, tm, w), lambda bi, i: (bi, i, 0))
    const2 = lambda a: pl.BlockSpec(a.shape, lambda bi, i: (0, 0))
    nti = t // tm
    blk = lambda bi, i: (bi * nti + i, 0)
    g2 = g_ffn.reshape(1, d)
    ins = [x, oa, ob, ga, gb, gt1, sc2, sh2, g2, wba, wbb, wo, wr, br]
    in_specs = [tok3(d), pl.BlockSpec((1, W_A, tm), lambda bi, i: (bi, 0, i)), tok3(WV_B),
                tok3(d), tok3(d), mod, mod, mod,
                const2(g2), const2(wba), const2(wbb), const2(wo), const2(wr), const2(br)]
    return pl.pallas_call(
        functools.partial(_mix_kernel, nsub=nsub),
        grid=(b, nti),
        in_specs=in_specs,
        out_specs=(pl.BlockSpec((tm, d), blk), pl.BlockSpec((tm * nsub, LANES), blk),
                   pl.BlockSpec((tm, LANES), blk)),
        out_shape=(jax.ShapeDtypeStruct((nt, d), f32),
                   jax.ShapeDtypeStruct((nt * nsub, LANES), jnp.uint32),
                   jax.ShapeDtypeStruct((nt, LANES), f32)),
        compiler_params=_cparams(("parallel", "arbitrary")),
        name="branch_mix",
    )(*ins)


def _route_kernel(lg_ref, cin_ref, rec_ref, gate_ref, cnt_ref, carry_scr, *, tr):
    @pl.when(pl.program_id(0) == 0)
    def _():
        carry_scr[...] = cin_ref[...]

    lane = lax.broadcasted_iota(i32, (tr, LANES), 1)
    lanef = lane.astype(f32)
    v = jnp.where(lane < N_EXPERTS, lg_ref[...], -jnp.inf)
    onehot = jnp.zeros((tr, LANES), f32)
    ids, vals = [], []
    for _ in range(TOP_K):
        mx = jnp.max(v, axis=1, keepdims=True)
        idx = jnp.min(jnp.where(v == mx, lanef, float(LANES)), axis=1, keepdims=True)
        sel = lanef == idx
        onehot = jnp.where(sel, 1.0, onehot)
        v = jnp.where(sel, -jnp.inf, v)
        ids.append(idx)
        vals.append(mx)
    es = [jnp.exp(vk - vals[0]) for vk in vals]
    tot = es[0]
    for e in es[1:]:
        tot = tot + e

    r2 = lax.broadcasted_iota(i32, (tr, tr), 0)
    c2 = lax.broadcasted_iota(i32, (tr, tr), 1)
    before = jnp.where(c2 < r2, 1.0, 0.0).astype(bf16)
    pref = _dot(before, onehot.astype(bf16)) + carry_scr[...]
    carry_scr[...] = carry_scr[...] + jnp.sum(onehot, axis=0, keepdims=True)

    rec = jnp.zeros((tr, LANES), f32)
    gate = jnp.zeros((tr, LANES), f32)
    for k in range(TOP_K):
        rank = jnp.sum(jnp.where(lanef == ids[k], pref, 0.0), axis=1, keepdims=True)
        rec = jnp.where(lane == k, rank * float(N_EXPERTS) + ids[k], rec)
        gate = jnp.where(lane == k, es[k] / tot, gate)
    rec_ref[...] = rec.astype(i32)
    gate_ref[...] = gate
    cnt_ref[...] = carry_scr[...]


def _route(logits, counts_in):
    nt = logits.shape[0]
    tr = _divisor_tile(nt, 512, SUBLANES)
    tile = pl.BlockSpec((tr, LANES), lambda i: (i, 0))
    cnt = pl.BlockSpec((1, LANES), lambda i: (0, 0))
    return pl.pallas_call(
        functools.partial(_route_kernel, tr=tr),
        grid=(nt // tr,),
        in_specs=[tile, cnt],
        out_specs=(tile, tile, cnt),
        out_shape=(jax.ShapeDtypeStruct((nt, LANES), i32),
                   jax.ShapeDtypeStruct((nt, LANES), f32),
                   jax.ShapeDtypeStruct((1, LANES), f32)),
        scratch_shapes=[pltpu.VMEM((1, LANES), f32)],
        compiler_params=_cparams(("arbitrary",)),
        name="route_topk",
    )(logits, counts_in)


def _dest_kernel(rec_ref, start_ref, out_ref):
    rec = rec_ref[...]
    lane = lax.broadcasted_iota(i32, rec.shape, 1)
    eid = jnp.bitwise_and(rec, N_EXPERTS - 1).astype(f32)
    rank = jnp.right_shift(rec, N_EXPERTS.bit_length() - 1)
    lanef = lane.astype(f32)
    start = jnp.zeros(rec.shape, f32)
    for k in range(TOP_K):
        mine = lanef == eid[:, k:k + 1]
        st = jnp.sum(jnp.where(mine, start_ref[...], 0.0), axis=1, keepdims=True)
        start = jnp.where(lane == k, st, start)
    out_ref[...] = start.astype(i32) + rank


def _dest_rows(rec, pad_start):
    nt = rec.shape[0]
    tr = _divisor_tile(nt, 512, SUBLANES)
    tile = pl.BlockSpec((tr, LANES), lambda i: (i, 0))
    start = jnp.zeros((1, LANES), f32).at[0, :N_EXPERTS].set(pad_start.astype(f32))
    out = pl.pallas_call(
        _dest_kernel,
        grid=(nt // tr,),
        in_specs=[tile, pl.BlockSpec((1, LANES), lambda i: (0, 0))],
        out_specs=tile,
        out_shape=jax.ShapeDtypeStruct((nt, LANES), i32),
        compiler_params=_cparams(("parallel",)),
        name="moe_dest_rows",
    )(rec, start)
    return out[:, :TOP_K].reshape(-1)


def _dispatch_kernel(dest_ref, h_ref, xin_hbm, xout_hbm, sem, *, nsub, ch):
    del xin_hbm
    base = pl.program_id(0) * ch

    def row_copy(n, k):
        src = pl.multiple_of(n * nsub, nsub)
        dst = pl.multiple_of(dest_ref[(base + n) * TOP_K + k] * nsub, nsub)
        return pltpu.make_async_copy(h_ref.at[pl.ds(src, nsub)], xout_hbm.at[pl.ds(dst, nsub)],
                                     sem)

    def body(n, carry):
        for k in range(TOP_K):
            row_copy(n, k).start()
        return carry

    lax.fori_loop(0, ch, body, 0, unroll=4)
    total = ch * TOP_K * nsub
    pltpu.make_async_copy(xout_hbm.at[pl.ds(0, total)], xout_hbm.at[pl.ds(0, total)], sem).wait()


def _dispatch(dest_flat, h2rows, xrows, nsub):
    n_tok = dest_flat.shape[0] // TOP_K
    ch = min(DISPATCH_CHUNK, n_tok)
    assert n_tok % ch == 0
    grid_spec = pltpu.PrefetchScalarGridSpec(
        num_scalar_prefetch=1,
        grid=(n_tok // ch,),
        in_specs=[pl.BlockSpec((ch * nsub, LANES), lambda i, dr: (i, 0)),
                  pl.BlockSpec(memory_space=pl.ANY)],
        out_specs=pl.BlockSpec(memory_space=pl.ANY),
        scratch_shapes=[pltpu.SemaphoreType.DMA(())])
    return pl.pallas_call(
        functools.partial(_dispatch_kernel, nsub=nsub, ch=ch),
        grid_spec=grid_spec,
        out_shape=jax.ShapeDtypeStruct(xrows.shape, xrows.dtype),
        input_output_aliases={2: 0},
        compiler_params=_cparams(("arbitrary",)),
        name="moe_dispatch",
    )(dest_flat, h2rows, xrows)


def _expert_kernel(be_ref, nused_ref, x_ref, wgu_ref, bgu_ref, wd_ref, bd_ref, y_ref,
                   wgu_b, wd_b, *, tmx, nsub_x, nsub, dff, chunk):
    r = pl.program_id(0)
    live = r < nused_ref[0]

    @pl.when(jnp.logical_not(live))
    def _():
        y_ref[...] = jnp.zeros_like(y_ref)

    new_expert = jnp.logical_or(r == 0, be_ref[r] != be_ref[jnp.maximum(r - 1, 0)])

    @pl.when(jnp.logical_and(live, new_expert))
    def _():
        for c in range(0, wgu_b.shape[0], chunk):
            wgu_b[c:c + chunk, :] = wgu_ref[0, c:c + chunk, :].astype(bf16)
        for c in range(0, wd_b.shape[0], chunk):
            wd_b[c:c + chunk, :] = wd_ref[0, c:c + chunk, :].astype(bf16)

    @pl.when(live)
    def _():
        words = [x_ref[pl.ds(s, tmx, stride=nsub_x), :] for s in range(nsub_x)]
        as_bf16 = lambda w: lax.bitcast_convert_type(w, f32).astype(bf16)
        x = jnp.concatenate(
            [as_bf16(jnp.bitwise_and(w, jnp.uint32(0xFFFF0000))) for w in words]
            + [as_bf16(jnp.left_shift(w, jnp.uint32(16))) for w in words], axis=1)
        gu = _dot(x, wgu_b[...]) + bgu_ref[0]
        gate = jnp.minimum(gu[:, :dff], SWIGLU_LIMIT)
        up = jnp.clip(gu[:, dff:], -SWIGLU_LIMIT, SWIGLU_LIMIT)
        glu = gate * _sigmoid(SWIGLU_ALPHA * gate)
        y = _dot(((up + 1.0) * glu).astype(bf16), wd_b[...]) + bd_ref[0]
        for s in range(nsub):
            y_ref[pl.ds(s, tmx, stride=nsub), :] = y[:, s * LANES:(s + 1) * LANES]


def _experts(block_e, nused, xrows, wgu, bgu, wd, bd):
    n_blocks = block_e.shape[0]
    tmx = EXPERT_TILE
    e, d, dff2 = wgu.shape
    dff = dff2 // 2
    nsub = d // LANES
    nsub_x = d // (2 * LANES)
    live = lambda r, nu: jnp.minimum(r, nu[0] - 1)
    rows = pl.BlockSpec((tmx * nsub_x, LANES), lambda r, be, nu: (live(r, nu), 0))
    per_e = lambda shp: pl.BlockSpec((1,) + shp, lambda r, be, nu: (be[live(r, nu)], 0, 0))
    grid_spec = pltpu.PrefetchScalarGridSpec(
        num_scalar_prefetch=2,
        grid=(n_blocks,),
        in_specs=[rows, per_e((d, dff2)), per_e((1, dff2)), per_e((dff, d)), per_e((1, d))],
        out_specs=pl.BlockSpec((tmx * nsub, LANES), lambda r, be, nu: (r, 0)),
        scratch_shapes=[pltpu.VMEM((d, dff2), bf16), pltpu.VMEM((dff, d), bf16)])
    return pl.pallas_call(
        functools.partial(_expert_kernel, tmx=tmx, nsub_x=nsub_x, nsub=nsub, dff=dff,
                          chunk=min(256, d, dff)),
        grid_spec=grid_spec,
        out_shape=jax.ShapeDtypeStruct((n_blocks * tmx * nsub, LANES), f32),
        compiler_params=_cparams(("arbitrary",)),
        name="moe_experts",
    )(block_e, nused, xrows, wgu, bgu.reshape(e, 1, dff2), wd, bd.reshape(e, 1, d))


def _combine_kernel(dest_ref, y_hbm, gate_ref, x1_ref, gt2_ref, gfin_ref, out_ref,
                    buf, sem, *, tc, nsub):
    base = pl.program_id(0) * tc

    def row_copy(n, k):
        src = pl.multiple_of(dest_ref[(base + n) * TOP_K + k] * nsub, nsub)
        dst = pl.multiple_of((k * tc + n) * nsub, nsub)
        return pltpu.make_async_copy(y_hbm.at[pl.ds(src, nsub)], buf.at[pl.ds(dst, nsub)], sem)

    def body(n, carry):
        for k in range(TOP_K):
            row_copy(n, k).start()
        return carry

    lax.fori_loop(0, tc, body, 0, unroll=4)
    pltpu.make_async_copy(y_hbm.at[pl.ds(0, buf.shape[0])], buf, sem).wait()

    g = gate_ref[...]
    cols = []
    for s in range(nsub):
        acc = None
        for k in range(TOP_K):
            term = g[:, k:k + 1] * buf[pl.ds(k * tc * nsub + s, tc, stride=nsub), :]
            acc = term if acc is None else acc + term
        cols.append(acc)
    y = jnp.concatenate(cols, axis=1)
    out_ref[...] = _rms(x1_ref[...] + gt2_ref[0] * y) * gfin_ref[...]


def _combine(dest_flat, yrows, gates, x1, gt2, g_final, t_per_mod, nsub):
    n_tok, d = x1.shape
    tc = min(TOKEN_TILE, n_tok)
    assert n_tok % tc == 0
    rows = gt2.shape[1]
    if rows == 1:
        per_mod = t_per_mod // tc
        mod = pl.BlockSpec((1, 1, d), lambda i, dr: (i // per_mod, 0, 0))
    else:
        mod = pl.BlockSpec((1, tc, d), lambda i, dr: (0, i, 0))
    blk = lambda i, dr: (i, 0)
    gfin = g_final.reshape(1, d)
    grid_spec = pltpu.PrefetchScalarGridSpec(
        num_scalar_prefetch=1,
        grid=(n_tok // tc,),
        in_specs=[pl.BlockSpec(memory_space=pl.ANY), pl.BlockSpec((tc, LANES), blk),
                  pl.BlockSpec((tc, d), blk), mod,
                  pl.BlockSpec((1, d), lambda i, dr: (0, 0))],
        out_specs=pl.BlockSpec((tc, d), blk),
        scratch_shapes=[pltpu.VMEM((TOP_K * tc * nsub, LANES), f32), pltpu.SemaphoreType.DMA(())])
    return pl.pallas_call(
        functools.partial(_combine_kernel, tc=tc, nsub=nsub),
        grid_spec=grid_spec,
        out_shape=jax.ShapeDtypeStruct((n_tok, d), f32),
        compiler_params=_cparams(("arbitrary",)),
        name="moe_combine",
    )(dest_flat, yrows, gates, x1, gt2, gfin)


def kernel(x_prompt, x_sample, c_prompt, c_sample, cache_k, cache_v, cache_logf, state_gla,
           page_table, g_mix, g_ffn, g_final, w_ada, b_ada, w_in, b_f, w_alpha, b_alpha, g_gla,
           w_branch, w_o, w_router, b_router, w_gu, b_gu, w_d, b_d):
    depth = g_mix.shape[0]
    assert depth == 1, "one decoder layer"
    bp, t, d = x_prompt.shape
    db = x_sample.shape[0]
    assert x_sample.shape[1] == 1
    nsub = d // LANES
    ntp = bp * t
    nt = ntp + db

    nmod = bp + db
    pad = (-nmod) % SUBLANES
    c_all = jnp.concatenate([c_prompt, c_sample, jnp.zeros((pad, d), f32)], axis=0)
    mod = _adaln(c_all, w_ada[0], b_ada[0])
    sh1, sc1, gt1, sh2, sc2, gt2 = [mod[:, i * d:(i + 1) * d] for i in range(6)]
    grp_p = lambda a: a[:bp].reshape(bp, 1, d)
    grp_s = lambda a: a[bp:nmod].reshape(1, db, d)

    w_proj = _inproj_weights(w_in[0], d)
    wal = jnp.zeros((LANES, WK_B), f32).at[H_A:H_A + GLA_RANK].set(w_alpha[0]).astype(bf16)
    bal = b_alpha[0].reshape(1, WK_B)
    gg = g_gla[0].reshape(1, WV_B)
    wba = w_branch[0, :W_A].astype(bf16)
    wbb = w_branch[0, W_A:].astype(bf16)
    wo = w_o[0].astype(bf16)
    wr = jnp.zeros((d, LANES), f32).at[:, :N_EXPERTS].set(w_router[0]).astype(bf16)
    br = jnp.zeros((1, LANES), f32).at[0, :N_EXPERTS].set(b_router[0])

    (qat, kat, vt, ka, va, qb, kb, vb, rb, ga, gb, sm, fat) = _inproj(
        x_prompt, grp_p(1.0 + sc1), grp_p(sh1), g_mix[0], w_proj, True)
    logft, cumt = _logf(fat, b_f[0], True)
    oa = _fox_prompt(qat, kat, vt, jnp.transpose(cumt, (0, 2, 1)))
    ob, gla_p = _gla_prompt(qb, kb, vb, rb, sm, wal, bal, gg)
    x1_p, h2_p, lg_p = _mix(x_prompt, oa, ob, ga, gb, grp_p(gt1), grp_p(1.0 + sc2), grp_p(sh2),
                            g_ffn[0], wba, wbb, wo, wr, br)

    xs = x_sample.reshape(1, db, d)
    (qat_s, ka_s, va_s, qb_s, kb_s, vb_s, rb_s, ga_s, gb_s, sm_s, fat_s) = _inproj(
        xs, grp_s(1.0 + sc1), grp_s(sh1), g_mix[0], w_proj, False)
    logft_s, = _logf(fat_s, b_f[0], False)
    lfnew = jnp.transpose(logft_s[0], (1, 0)).reshape(db, H_A, 1)
    qbd = jnp.einsum('hnd,hc->nchd', qat_s[0], jnp.eye(H_A, dtype=bf16)).reshape(db, H_A, W_A)
    oa_s = _fox_decode(qbd, ka_s.reshape(db, 1, W_A), va_s.reshape(db, 1, W_A), lfnew,
                       jnp.transpose(cache_k[0], (0, 2, 3, 1)),
                       jnp.transpose(cache_v[0], (0, 2, 3, 1)),
                       jnp.transpose(cache_logf[0], (0, 2, 1)), page_table)
    ob_s, gla_s = _gla_decode(qb_s[0], kb_s[0], vb_s[0], rb_s[0], sm_s[0], wal, bal, gg,
                              state_gla[0])
    x1_s, h2_s, lg_s = _mix(xs, oa_s.reshape(db, W_A).T[None], ob_s.reshape(1, db, WV_B), ga_s, gb_s,
                            grp_s(gt1), grp_s(1.0 + sc2), grp_s(sh2), g_ffn[0],
                            wba, wbb, wo, wr, br)

    rec_p, gates_p, cnt_p = _route(lg_p, jnp.zeros((1, LANES), f32))
    rec_s, gates_s, cnt = _route(lg_s, cnt_p)
    counts = cnt[0, :N_EXPERTS].astype(i32)
    padded = (counts + EXPERT_TILE - 1) // EXPERT_TILE * EXPERT_TILE
    pad_end = jnp.cumsum(padded).astype(i32)
    pad_start = pad_end - padded

    dest_p = _dest_rows(rec_p, pad_start)
    dest_s = _dest_rows(rec_s, pad_start)
    n_blocks = -(-(nt * TOP_K + N_EXPERTS * (EXPERT_TILE - 1)) // EXPERT_TILE)
    block_start = jnp.arange(n_blocks, dtype=i32) * EXPERT_TILE
    block_e = jnp.minimum(jnp.sum(pad_end[None, :] <= block_start[:, None], axis=1),
                          N_EXPERTS - 1).astype(i32)
    nused = (pad_end[-1:] // EXPERT_TILE).astype(i32)

    nsub_x = d // (2 * LANES)
    xrows = jnp.zeros((n_blocks * EXPERT_TILE * nsub_x, LANES), jnp.uint32)
    xrows = _dispatch(dest_p, h2_p, xrows, nsub_x)
    xrows = _dispatch(dest_s, h2_s, xrows, nsub_x)
    yrows = _experts(block_e, nused, xrows, w_gu[0], b_gu[0], w_d[0], b_d[0])
    y_p = _combine(dest_p, yrows, gates_p, x1_p, grp_p(gt2), g_final, t, nsub)
    y_s = _combine(dest_s, yrows, gates_s, x1_s, grp_s(gt2), g_final, 1, nsub)

    logf_p = jnp.transpose(logft, (0, 2, 1))
    return (y_p.reshape(bp, t, d), y_s.reshape(db, 1, d),
            jnp.transpose(ka, (0, 3, 1, 2))[None], jnp.transpose(va, (0, 3, 1, 2))[None],
            logf_p.reshape(1, bp, t, H_A), gla_p.reshape(1, bp, H_B, DK_B, DV_B),
            ka_s.reshape(1, db, 1, H_A, DH_A), va_s.reshape(1, db, 1, H_A, DH_A),
            lfnew.reshape(1, db, 1, H_A), gla_s.reshape(1, db, H_B, DK_B, DV_B))
```

```python
import functools

import jax
import jax.numpy as jnp
from jax import lax
from jax.experimental import pallas as pl
from jax.experimental.pallas import tpu as pltpu

f32 = jnp.float32
bf16 = jnp.bfloat16
i32 = jnp.int32

H_A, DH_A = 8, 64
H_B, DK_B, DV_B = 4, 64, 128
GLA_RANK = 16
GLA_TAU = 16.0
N_EXPERTS = 32
TOP_K = 4
SWIGLU_LIMIT = 7.0
SWIGLU_ALPHA = 1.702
RMS_EPS = 1e-6
W_A = H_A * DH_A
WK_B = H_B * DK_B
WV_B = H_B * DV_B

LANES = 128
SUBLANES = 8
VMEM_LIMIT_MB = 56

TOKEN_TILE = 256
ATTN_Q_TILE = 512
ATTN_K_TILE = 256
GLA_TILE = 256
EXPERT_TILE = 512
PAGES_PER_STEP = 16
DISPATCH_CHUNK = 256

_NT = (((1,), (1,)), ((), ()))
_TN = (((0,), (0,)), ((), ()))


def _cparams(semantics, vmem_mb=VMEM_LIMIT_MB):
    return pltpu.CompilerParams(dimension_semantics=semantics,
                                vmem_limit_bytes=vmem_mb << 20)


def _sigmoid(x):
    return 1.0 / (1.0 + jnp.exp(-x))


def _log_sigmoid(x):
    return jnp.minimum(x, 0.0) - jnp.log1p(jnp.exp(-jnp.abs(x)))


def _split3(x):
    hi = x.astype(bf16)
    r = x - hi.astype(f32)
    mid = r.astype(bf16)
    lo = (r - mid.astype(f32)).astype(bf16)
    return hi, mid, lo


def _dot(a, b):
    return jnp.dot(a, b, preferred_element_type=f32)


def _dotg(a, b, dims):
    return lax.dot_general(a, b, dims, preferred_element_type=f32)


def _rms(x):
    return x * lax.rsqrt(jnp.mean(x * x, axis=-1, keepdims=True) + RMS_EPS)


def _divisor_tile(n, cap, mult):
    best = None
    t = mult
    while t <= min(n, cap):
        if n % t == 0:
            best = t
        t += mult
    assert best is not None, (n, cap, mult)
    return best


def _mod_kernel(c_ref, w_ref, b_ref, o_ref):
    c = c_ref[...]
    s = (c * _sigmoid(c)).astype(bf16)
    o_ref[...] = _dot(s, w_ref[...].astype(bf16)) + b_ref[...]


def _adaln(c_all, w_ada, b_ada):
    r, d = c_all.shape
    n = w_ada.shape[1]
    tn = _divisor_tile(n, 1536, LANES)
    return pl.pallas_call(
        _mod_kernel,
        grid=(n // tn,),
        in_specs=[pl.BlockSpec((r, d), lambda j: (0, 0)),
                  pl.BlockSpec((d, tn), lambda j: (0, j)),
                  pl.BlockSpec((1, tn), lambda j: (0, j))],
        out_specs=pl.BlockSpec((r, tn), lambda j: (0, j)),
        out_shape=jax.ShapeDtypeStruct((r, n), f32),
        compiler_params=_cparams(("arbitrary",)),
        name="adaln_mod",
    )(c_all, w_ada, b_ada.reshape(1, n))


def _inproj_kernel(x_ref, sc_ref, sh_ref, g_ref, wbig_ref, wsm_ref, wfa_ref, wkt_ref, wvt_ref,
                   *out_refs, d, time_minor):
    if time_minor:
        (qat_ref, kat_ref, vt_ref, ka_ref, va_ref, qb_ref, kb_ref, vb_ref, rb_ref,
         ga_ref, gb_ref, sm_ref, fat_ref) = out_refs
    else:
        (qat_ref, ka_ref, va_ref, qb_ref, kb_ref, vb_ref, rb_ref,
         ga_ref, gb_ref, sm_ref, fat_ref) = out_refs
    x = x_ref[0]
    h = (_rms(x) * g_ref[...]) * sc_ref[0] + sh_ref[0]
    hb = h.astype(bf16)
    off = [0]

    def mm(width):
        r = _dot(hb, wbig_ref[:, off[0]:off[0] + width])
        off[0] += width
        return r

    qa = mm(W_A)
    ka = mm(W_A)
    va = mm(W_A)
    for hh in range(H_A):
        sl = slice(hh * DH_A, (hh + 1) * DH_A)
        qat_ref[0, hh] = qa[:, sl].astype(bf16)
        if time_minor:
            kat_ref[0, hh] = ka[:, sl].astype(bf16)
    if time_minor:
        tm = x.shape[0]
        kt = _dotg(wkt_ref[...], hb, _NT).reshape(H_A, DH_A, tm)
        vt = _dotg(wvt_ref[...], hb, _NT).reshape(H_A, DH_A, tm)
        ka_ref[0] = kt
        va_ref[0] = vt
        vt_ref[0] = vt.astype(bf16)
    else:
        ka_ref[0] = ka
        va_ref[0] = va
    qb_ref[0] = mm(WK_B)
    kb_ref[0] = mm(WK_B)
    vb_ref[0] = mm(WV_B)
    rb_ref[0] = mm(WV_B)
    ga_ref[0] = mm(d)
    gb_ref[0] = mm(d)
    sm_ref[0] = _dot(hb, wsm_ref[...])
    fat_ref[0] = _dotg(wfa_ref[...].astype(bf16), hb, _NT)


def _inproj_weights(w_in, d):
    sizes = (W_A, W_A, W_A, H_A, WK_B, WK_B, WV_B, WV_B, GLA_RANK, d, d)
    offs = [0]
    for s in sizes:
        offs.append(offs[-1] + s)
    seg = lambda i: w_in[:, offs[i]:offs[i + 1]]
    wbig = jnp.concatenate(
        [seg(0) * (DH_A ** -0.5), seg(1), seg(2), seg(4) * (DK_B ** -0.5), seg(5),
         seg(6), seg(7), seg(9), seg(10)], axis=1).astype(bf16)
    wsm = jnp.concatenate(
        [seg(3), seg(8), jnp.zeros((d, LANES - H_A - GLA_RANK), f32)], axis=1).astype(bf16)
    wfa = seg(3).T
    wkt = seg(1).T.astype(bf16)
    wvt = seg(2).T.astype(bf16)
    return wbig, wsm, wfa, wkt, wvt


def _inproj(x, sc, sh, g_mix, weights, time_minor):
    wbig, wsm, wfa, wkt, wvt = weights
    b, t, d = x.shape
    tm = min(TOKEN_TILE, t)
    assert t % tm == 0
    rows = sc.shape[1]
    mod_block = (1, 1, d) if rows == 1 else (1, tm, d)
    mod_map = (lambda bi, i: (bi, 0, 0)) if rows == 1 else (lambda bi, i: (bi, i, 0))
    tok3 = lambda w: pl.BlockSpec((1, tm, w), lambda bi, i: (bi, i, 0))
    head4 = pl.BlockSpec((1, H_A, tm, DH_A), lambda bi, i: (bi, 0, i, 0))
    const2 = lambda a: pl.BlockSpec(a.shape, lambda bi, i: (0, 0))
    sds = jax.ShapeDtypeStruct
    headt = pl.BlockSpec((1, H_A, DH_A, tm), lambda bi, i: (bi, 0, 0, i))
    if time_minor:
        kv_shape = [sds((b, H_A, t, DH_A), bf16), sds((b, H_A, DH_A, t), bf16),
                    sds((b, H_A, DH_A, t), f32), sds((b, H_A, DH_A, t), f32)]
        kv_specs = [head4, headt, headt, headt]
    else:
        kv_shape = [sds((b, t, W_A), f32), sds((b, t, W_A), f32)]
        kv_specs = [tok3(W_A), tok3(W_A)]
    out_shape = tuple(
        [sds((b, H_A, t, DH_A), bf16)] + kv_shape
        + [sds((b, t, WK_B), f32), sds((b, t, WK_B), f32),
           sds((b, t, WV_B), f32), sds((b, t, WV_B), f32),
           sds((b, t, d), f32), sds((b, t, d), f32),
           sds((b, t, LANES), f32), sds((b, H_A, t), f32)])
    out_specs = tuple(
        [head4] + kv_specs
        + [tok3(WK_B), tok3(WK_B), tok3(WV_B), tok3(WV_B), tok3(d), tok3(d), tok3(LANES),
           pl.BlockSpec((1, H_A, tm), lambda bi, i: (bi, 0, i))])
    g2 = g_mix.reshape(1, d)
    return pl.pallas_call(
        functools.partial(_inproj_kernel, d=d, time_minor=time_minor),
        grid=(b, t // tm),
        in_specs=[tok3(d), pl.BlockSpec(mod_block, mod_map), pl.BlockSpec(mod_block, mod_map),
                  const2(g2), const2(wbig), const2(wsm), const2(wfa), const2(wkt), const2(wvt)],
        out_specs=out_specs,
        out_shape=out_shape,
        compiler_params=_cparams(("parallel", "arbitrary")),
        name="inproj",
    )(x, sc, sh, g2, wbig, wsm, wfa, wkt, wvt)


def _logf_kernel(fa_ref, bf_ref, logf_ref, *cum_refs, t, cb):
    lf = _log_sigmoid(fa_ref[...] + bf_ref[...])
    logf_ref[...] = lf
    if not cum_refs:
        return
    cum_ref, = cum_refs
    r = lax.broadcasted_iota(i32, (cb, cb), 0)
    c = lax.broadcasted_iota(i32, (cb, cb), 1)
    triu = jnp.where(r <= c, 1.0, 0.0).astype(bf16)
    carry = jnp.zeros((lf.shape[0], 1), f32)
    for j in range(t // cb):
        hi, mid, lo = _split3(lf[:, j * cb:(j + 1) * cb])
        blk = _dot(hi, triu) + _dot(mid, triu) + _dot(lo, triu) + carry
        cum_ref[:, j * cb:(j + 1) * cb] = blk
        carry = blk[:, cb - 1:cb]


def _logf(fat, b_f, with_cumsum):
    b, _, t = fat.shape
    rows = b * H_A
    fa2 = fat.reshape(rows, t)
    bf2 = jnp.tile(b_f.reshape(H_A, 1), (b, 1))
    cb = min(256, t)
    full = lambda a: pl.BlockSpec(a.shape, lambda: (0,) * a.ndim)
    n_out = 2 if with_cumsum else 1
    outs = pl.pallas_call(
        functools.partial(_logf_kernel, t=t, cb=cb),
        in_specs=[full(fa2), full(bf2)],
        out_specs=tuple(pl.BlockSpec((rows, t), lambda: (0, 0)) for _ in range(n_out)),
        out_shape=tuple(jax.ShapeDtypeStruct((rows, t), f32) for _ in range(n_out)),
        name="log_forget",
    )(fa2, bf2)
    return tuple(o.reshape(b, H_A, t) for o in outs)


def _bias_lanes(col, ones_first):
    hi, mid, lo = [p.astype(f32) for p in _split3(col)]
    lane = lax.broadcasted_iota(i32, (col.shape[0], DH_A), 1)
    base = 3 if ones_first else 0
    parts = jnp.where(lane == base, hi, jnp.where(lane == base + 1, mid,
                      jnp.where(lane == base + 2, lo, 0.0)))
    ones = (lane < 3) if ones_first else ((lane >= 3) & (lane < 6))
    return jnp.where(ones, 1.0, parts)


def _fox_kernel(q_ref, k_ref, vt_ref, cum_ref, o_ref, kaug, qaug, *, tq, tk):
    i = pl.program_id(1)
    per_q = tq // tk

    @pl.when(i == 0)
    def _():
        cum_all = cum_ref[0]
        for h in range(H_A):
            kaug[h, :, :DH_A] = k_ref[0, h]
            kaug[h, :, DH_A:] = _bias_lanes(-cum_all[:, h:h + 1], False).astype(bf16)

    cum_q = cum_ref[0, pl.ds(pl.multiple_of(i * tq, tq), tq), :]
    for h in range(H_A):
        qaug[h, :, :DH_A] = q_ref[0, h]
        qaug[h, :, DH_A:] = _bias_lanes(cum_q[:, h:h + 1], True).astype(bf16)
    key = lax.broadcasted_iota(i32, (tk, tq), 0)
    qry = lax.broadcasted_iota(i32, (tk, tq), 1)

    def step(j, carry, key_off=None):
        start = pl.multiple_of(j * tk, tk)
        scores = [_dotg(kaug[h, pl.ds(start, tk), :], qaug[h], _NT) for h in range(H_A)]
        stats = []
        for h in range(H_A):
            m_old, l_old, _ = carry[h]
            s = scores[h]
            if key_off is not None:
                s = jnp.where(key + key_off <= qry, s, -jnp.inf)
            m_new = jnp.maximum(m_old, jnp.max(s, axis=0, keepdims=True))
            alpha = jnp.exp(m_old - m_new)
            p = jnp.exp(s - m_new)
            l_new = alpha * l_old + jnp.sum(p, axis=0, keepdims=True)
            stats.append((m_new, l_new, alpha, p.astype(bf16)))
        out = []
        for h in range(H_A):
            m_new, l_new, alpha, p = stats[h]
            acc_new = alpha * carry[h][2] + _dot(vt_ref[0, h, :, pl.ds(start, tk)], p)
            out.append((m_new, l_new, acc_new))
        return tuple(out)

    init = tuple((jnp.full((1, tq), -jnp.inf, f32), jnp.zeros((1, tq), f32),
                  jnp.zeros((DH_A, tq), f32)) for _ in range(H_A))
    carry = lax.fori_loop(0, i * per_q, step, init)
    for dblk in range(per_q):
        carry = step(i * per_q + dblk, carry, key_off=dblk * tk)
    for h in range(H_A):
        _, l, acc = carry[h]
        o_ref[0, h * DH_A:(h + 1) * DH_A, :] = (acc / l).astype(bf16)


def _fox_prompt(qat, kat, vt, cum):
    b, _, t, _ = qat.shape
    tq = min(ATTN_Q_TILE, t)
    tk = min(ATTN_K_TILE, tq)
    assert t % tq == 0 and tq % tk == 0
    return pl.pallas_call(
        functools.partial(_fox_kernel, tq=tq, tk=tk),
        grid=(b, t // tq),
        in_specs=[pl.BlockSpec((1, H_A, tq, DH_A), lambda bi, i: (bi, 0, i, 0)),
                  pl.BlockSpec((1, H_A, t, DH_A), lambda bi, i: (bi, 0, 0, 0)),
                  pl.BlockSpec((1, H_A, DH_A, t), lambda bi, i: (bi, 0, 0, 0)),
                  pl.BlockSpec((1, t, H_A), lambda bi, i: (bi, 0, 0))],
        out_specs=pl.BlockSpec((1, W_A, tq), lambda bi, i: (bi, 0, i)),
        out_shape=jax.ShapeDtypeStruct((b, W_A, t), bf16),
        scratch_shapes=[pltpu.VMEM((H_A, t, 2 * DH_A), bf16), pltpu.VMEM((H_A, tq, 2 * DH_A), bf16)],
        compiler_params=_cparams(("parallel", "arbitrary")),
        name="fox_prompt",
    )(qat, kat, vt, cum)


def _fox_decode_kernel(pt_ref, q_ref, knew_ref, vnew_ref, lfnew_ref, kt_hbm, vt_hbm, lft_hbm,
                       o_ref, pages, lfpages, p_scr, m_scr, self_scr, acc_scr, carry_scr,
                       sem_kv, sem_lf, *, pg, nstep, n_pages, db):
    b = pl.program_id(0)
    j = pl.program_id(1)
    q = q_ref[0]
    psz = lfpages.shape[-1]
    lane = lax.broadcasted_iota(i32, (H_A, W_A), 1)
    sub = lax.broadcasted_iota(i32, (H_A, W_A), 0)
    own = (lane // DH_A) == sub
    rnd = lambda z: z.astype(bf16).astype(f32)

    def start_load(seq, t, slot):
        def issue(src_hbm, dst, sem, group):
            for p in range(pg):
                page = pt_ref[seq * n_pages + group * pg + p]
                pltpu.make_async_copy(src_hbm.at[page], dst.at[slot, p], sem.at[slot]).start()

        @pl.when(t < nstep)
        def _():
            issue(kt_hbm, pages, sem_kv, nstep - 1 - t)
            issue(lft_hbm, lfpages, sem_lf, nstep - 1 - t)

        @pl.when(t >= nstep)
        def _():
            issue(vt_hbm, pages, sem_kv, 2 * nstep - 1 - t)

    step = b * (2 * nstep) + j
    slot = step % 2

    @pl.when(step == 0)
    def _():
        start_load(0, 0, 0)

    last = j == 2 * nstep - 1
    nxt_seq = jnp.where(last, b + 1, b)
    nxt_t = jnp.where(last, 0, j + 1)

    @pl.when(nxt_seq < db)
    def _():
        start_load(nxt_seq, nxt_t, 1 - slot)

    pltpu.make_async_copy(kt_hbm.at[pl.ds(0, pg)], pages.at[slot], sem_kv.at[slot]).wait()

    @pl.when(j < nstep)
    def _():
        pltpu.make_async_copy(lft_hbm.at[pl.ds(0, pg)], lfpages.at[slot], sem_lf.at[slot]).wait()

    @pl.when(j == 0)
    def _():
        s_self = jnp.sum(q.astype(f32) * rnd(knew_ref[0]), axis=1, keepdims=True)
        self_scr[...] = s_self
        m_scr[...] = s_self
        carry_scr[...] = lfnew_ref[0]

    @pl.when(j < nstep)
    def _():
        r = lax.broadcasted_iota(i32, (psz, 2 * psz), 0)
        c = lax.broadcasted_iota(i32, (psz, 2 * psz), 1)
        later = jnp.where(c < psz, jnp.where(r > c, 1.0, 0.0), 1.0).astype(bf16)
        group = nstep - 1 - j
        carry = carry_scr[...]
        m = m_scr[...]
        hi, mid, lo = _split3(lfpages[slot].reshape(pg * H_A, psz))
        suf_all = _dot(hi, later) + _dot(mid, later) + _dot(lo, later)
        for p in reversed(range(pg)):
            suf = suf_all[p * H_A:(p + 1) * H_A]
            kt = pages[slot, p].reshape(W_A, psz).astype(bf16)
            s = _dot(q, kt) + (carry + suf[:, :psz])
            carry = carry + suf[:, psz:psz + 1]
            p_scr[group * pg + p] = s
            m = jnp.maximum(m, jnp.max(s, axis=1, keepdims=True))
        carry_scr[...] = carry
        m_scr[...] = m

    @pl.when(j == nstep)
    def _():
        m = m_scr[...]
        e_self = jnp.exp(self_scr[...] - m)
        e = jnp.exp(p_scr[...] - m[None])
        l = e_self + jnp.sum(jnp.sum(e, axis=0), axis=1, keepdims=True)
        p_scr[...] = e / l[None]
        acc_scr[...] = jnp.where(own, rnd(e_self / l) * rnd(vnew_ref[0]), 0.0)

    @pl.when(j >= nstep)
    def _():
        group = 2 * nstep - 1 - j
        acc = acc_scr[...]
        for p in range(pg):
            vt = pages[slot, p].reshape(W_A, psz).astype(bf16)
            acc = acc + _dotg(p_scr[group * pg + p].astype(bf16), vt, _NT)
        acc_scr[...] = acc

    @pl.when(j == 2 * nstep - 1)
    def _():
        o_ref[0] = jnp.sum(jnp.where(own, acc_scr[...], 0.0), axis=0, keepdims=True)


def _fox_decode(qbd, knew, vnew, lfnew, cache_kt, cache_vt, cache_lft, page_table):
    db = qbd.shape[0]
    psz = cache_kt.shape[-1]
    n_pages = page_table.shape[1]
    pg = PAGES_PER_STEP if n_pages % PAGES_PER_STEP == 0 else 1
    nstep = n_pages // pg
    pt = page_table.reshape(-1).astype(i32)
    row3 = lambda a: pl.BlockSpec((1,) + a.shape[1:], lambda bi, j, pt_ref: (bi, 0, 0))
    anyspace = pl.BlockSpec(memory_space=pl.ANY)
    grid_spec = pltpu.PrefetchScalarGridSpec(
        num_scalar_prefetch=1,
        grid=(db, 2 * nstep),
        in_specs=[row3(qbd), row3(knew), row3(vnew), row3(lfnew), anyspace, anyspace, anyspace],
        out_specs=pl.BlockSpec((1, 1, W_A), lambda bi, j, pt_ref: (bi, 0, 0)),
        scratch_shapes=[pltpu.VMEM((2, pg, H_A, DH_A, psz), f32), pltpu.VMEM((2, pg, H_A, psz), f32),
                        pltpu.VMEM((n_pages, H_A, psz), f32), pltpu.VMEM((H_A, 1), f32),
                        pltpu.VMEM((H_A, 1), f32), pltpu.VMEM((H_A, W_A), f32),
                        pltpu.VMEM((H_A, 1), f32),
                        pltpu.SemaphoreType.DMA((2,)), pltpu.SemaphoreType.DMA((2,))])
    return pl.pallas_call(
        functools.partial(_fox_decode_kernel, pg=pg, nstep=nstep, n_pages=n_pages, db=db),
        grid_spec=grid_spec,
        out_shape=jax.ShapeDtypeStruct((db, 1, W_A), f32),
        compiler_params=_cparams(("arbitrary", "arbitrary")),
        name="fox_decode",
    )(pt, qbd, knew, vnew, lfnew, cache_kt, cache_vt, cache_lft)


def _gla_kernel(qb_ref, kb_ref, vb_ref, rb_ref, sm_ref, wal_ref, bal_ref, gg_ref,
                ob_ref, sout_ref, s_scr, *, tt):
    i = pl.program_id(1)

    @pl.when(i == 0)
    def _():
        s_scr[...] = jnp.zeros_like(s_scr)

    q = qb_ref[0]
    k = kb_ref[0]
    z = _dot(sm_ref[0].astype(bf16), wal_ref[...]) + bal_ref[...]
    a = _log_sigmoid(z) * (1.0 / GLA_TAU)

    row = lax.broadcasted_iota(i32, (tt, tt), 0)
    col = lax.broadcasted_iota(i32, (tt, tt), 1)
    tril = jnp.where(col <= row, 1.0, 0.0).astype(bf16)
    hi, mid, lo = _split3(a)
    b = _dot(tril, hi) + _dot(tril, mid) + _dot(tril, lo)

    rowk = lax.broadcasted_iota(i32, (tt, WK_B), 0)
    hs = [slice(h * DK_B, (h + 1) * DK_B) for h in range(H_B)]
    amat = [jnp.zeros((tt, tt), f32) for _ in range(H_B)]
    edge = b
    for lvl in range(tt.bit_length() - 1):
        half = 1 << lvl
        upper = ((rowk >> lvl) & 1) == 1
        kt = jnp.where(upper, 0.0, k * jnp.exp(edge - b)).astype(bf16)
        edge_q = pltpu.roll(edge, half, 0)
        qt = jnp.where(upper, q * jnp.exp(b - edge_q), 0.0).astype(bf16)
        same = (row >> (lvl + 1)) == (col >> (lvl + 1))
        for h in range(H_B):
            amat[h] = amat[h] + jnp.where(same, _dotg(qt[:, hs[h]], kt[:, hs[h]], _NT), 0.0)
        edge = jnp.where(upper, edge, pltpu.roll(edge, tt - half, 0))
    q16 = q.astype(bf16)
    k16 = k.astype(bf16)
    for h in range(H_B):
        amat[h] = amat[h] + jnp.where(row == col, _dotg(q16[:, hs[h]], k16[:, hs[h]], _NT), 0.0)

    qe = (q * jnp.exp(b)).astype(bf16)
    ke = (k * jnp.exp(edge - b)).astype(bf16)
    e_last = jnp.exp(edge[0:1, :])
    v16 = vb_ref[0].astype(bf16)
    r = rb_ref[0]
    er = lax.broadcasted_iota(i32, (DK_B, DK_B), 0)
    ec = lax.broadcasted_iota(i32, (DK_B, DK_B), 1)
    for h in range(H_B):
        vs = slice(h * DV_B, (h + 1) * DV_B)
        vh = v16[:, vs]
        state = s_scr[h]
        o = _dot(amat[h].astype(bf16), vh) + _dot(qe[:, hs[h]], state.astype(bf16))
        decay_col = jnp.sum(
            jnp.where(er == ec, jnp.broadcast_to(e_last[:, hs[h]], (DK_B, DK_B)), 0.0),
            axis=1, keepdims=True)
        s_scr[h] = decay_col * state + _dotg(ke[:, hs[h]], vh, _TN)
        rh = r[:, vs]
        ob_ref[0, :, vs] = (_rms(o) * gg_ref[:, vs] * (rh * _sigmoid(rh))).astype(bf16)

    @pl.when(i == pl.num_programs(1) - 1)
    def _():
        sout_ref[0] = s_scr[...]


def _gla_prompt(qb, kb, vb, rb, sm, wal, bal, gg):
    b, t, _ = qb.shape
    tt = min(GLA_TILE, t)
    assert t % tt == 0 and tt & (tt - 1) == 0
    tok3 = lambda w: pl.BlockSpec((1, tt, w), lambda bi, i: (bi, i, 0))
    const2 = lambda a: pl.BlockSpec(a.shape, lambda bi, i: (0, 0))
    return pl.pallas_call(
        functools.partial(_gla_kernel, tt=tt),
        grid=(b, t // tt),
        in_specs=[tok3(WK_B), tok3(WK_B), tok3(WV_B), tok3(WV_B), tok3(LANES),
                  const2(wal), const2(bal), const2(gg)],
        out_specs=(tok3(WV_B),
                   pl.BlockSpec((1, H_B, DK_B, DV_B), lambda bi, i: (bi, 0, 0, 0))),
        out_shape=(jax.ShapeDtypeStruct((b, t, WV_B), bf16),
                   jax.ShapeDtypeStruct((b, H_B, DK_B, DV_B), f32)),
        scratch_shapes=[pltpu.VMEM((H_B, DK_B, DV_B), f32)],
        compiler_params=_cparams(("parallel", "arbitrary")),
        name="gla_prompt",
    )(qb, kb, vb, rb, sm, wal, bal, gg)


def _loga_kernel(sm_ref, wal_ref, bal_ref, o_ref):
    z = _dot(sm_ref[...].astype(bf16), wal_ref[...]) + bal_ref[...]
    o_ref[...] = _log_sigmoid(z) * (1.0 / GLA_TAU)


def _gla_decode_kernel(q_ref, k_ref, a_ref, v_ref, r_ref, s_ref, gg_ref, o_ref, so_ref):
    for h in range(H_B):
        q, k, a = q_ref[0, h], k_ref[0, h], a_ref[0, h]
        v = v_ref[0, h]
        state = s_ref[0, h]
        ea = jnp.exp(a)
        rnd = lambda z: z.astype(bf16).astype(f32)
        qk = jnp.sum(q * k, axis=0, keepdims=True)
        o = qk * v + jnp.sum(rnd(q * ea) * rnd(state), axis=0, keepdims=True)
        so_ref[0, h] = ea * state + k * v
        r = r_ref[0, h]
        o_ref[0, h] = _rms(o) * gg_ref[h] * (r * _sigmoid(r))


def _gla_decode(qb, kb, vb, rb, sm, wal, bal, gg, state):
    db = qb.shape[0]
    full = lambda a: pl.BlockSpec(a.shape, lambda: (0,) * a.ndim)
    loga = pl.pallas_call(
        _loga_kernel,
        in_specs=[full(sm), full(wal), full(bal)],
        out_specs=pl.BlockSpec((db, WK_B), lambda: (0, 0)),
        out_shape=jax.ShapeDtypeStruct((db, WK_B), f32),
        name="gla_log_decay",
    )(sm, wal, bal)
    col = lambda a: a.reshape(db, H_B, DK_B, 1)
    rowv = lambda a: a.reshape(db, H_B, 1, DV_B)
    gg4 = gg.reshape(H_B, 1, DV_B)
    b4 = lambda shp: pl.BlockSpec((1,) + shp, lambda bi: (bi, 0, 0, 0))
    o, s_new = pl.pallas_call(
        _gla_decode_kernel,
        grid=(db,),
        in_specs=[b4((H_B, DK_B, 1))] * 3 + [b4((H_B, 1, DV_B))] * 2
        + [b4((H_B, DK_B, DV_B)), pl.BlockSpec(gg4.shape, lambda bi: (0, 0, 0))],
        out_specs=(b4((H_B, 1, DV_B)), b4((H_B, DK_B, DV_B))),
        out_shape=(jax.ShapeDtypeStruct((db, H_B, 1, DV_B), f32),
                   jax.ShapeDtypeStruct((db, H_B, DK_B, DV_B), f32)),
        compiler_params=_cparams(("parallel",)),
        name="gla_decode",
    )(col(qb), col(kb), col(loga), rowv(vb), rowv(rb), state, gg4)
    return o.reshape(db, WV_B), s_new


def _mix_kernel(x_ref, oa_ref, ob_ref, ga_ref, gb_ref, gt1_ref, sc2_ref, sh2_ref, gf_ref,
                wba_ref, wbb_ref, wo_ref, wr_ref, br_ref, x1_ref, h2_ref, lg_ref, *, nsub):
    x = x_ref[0]
    tm = x.shape[0]
    ya = _dotg(oa_ref[0].astype(bf16), wba_ref[...], _TN)
    yb = _dot(ob_ref[0].astype(bf16), wbb_ref[...])
    m = _sigmoid(ga_ref[0]) * ya + _sigmoid(gb_ref[0]) * yb
    x1 = x + gt1_ref[0] * _dot(m.astype(bf16), wo_ref[...])
    x1_ref[...] = x1
    h2 = (_rms(x1) * gf_ref[...]) * sc2_ref[0] + sh2_ref[0]
    hb = h2.astype(bf16)
    lg_ref[...] = _dot(hb, wr_ref[...]) + br_ref[...]
    half = nsub * LANES
    hi = lax.bitcast_convert_type(hb[:, :half].astype(f32), jnp.uint32)
    lo = lax.bitcast_convert_type(hb[:, half:].astype(f32), jnp.uint32)
    packed = jnp.bitwise_or(hi, jnp.right_shift(lo, jnp.uint32(16)))
    for s in range(nsub):
        h2_ref[pl.ds(s, tm, stride=nsub), :] = packed[:, s * LANES:(s + 1) * LANES]


def _mix(x, oa, ob, ga, gb, gt1, sc2, sh2, g_ffn, wba, wbb, wo, wr, br):
    b, t, d = x.shape
    nt = b * t
    nsub = d // (2 * LANES)
    tm = min(TOKEN_TILE, t)
    assert t % tm == 0 and d % (2 * LANES) == 0
    rows = gt1.shape[1]
    mod_block = (1, 1, d) if rows == 1 else (1, tm, d)
    mod_map = (lambda bi, i: (bi, 0, 0)) if rows == 1 else (lambda bi, i: (bi, i, 0))
    mod = pl.BlockSpec(mod_block, mod_map)
    tok3 = lambda w: pl.BlockSpec((1, tm, w), lambda bi, i: (bi, i, 0))
    const2 = lambda a: pl.BlockSpec(a.shape, lambda bi, i: (0, 0))
    nti = t // tm
    blk = lambda bi, i: (bi * nti + i, 0)
    g2 = g_ffn.reshape(1, d)
    ins = [x, oa, ob, ga, gb, gt1, sc2, sh2, g2, wba, wbb, wo, wr, br]
    in_specs = [tok3(d), pl.BlockSpec((1, W_A, tm), lambda bi, i: (bi, 0, i)), tok3(WV_B),
                tok3(d), tok3(d), mod, mod, mod,
                const2(g2), const2(wba), const2(wbb), const2(wo), const2(wr), const2(br)]
    return pl.pallas_call(
        functools.partial(_mix_kernel, nsub=nsub),
        grid=(b, nti),
        in_specs=in_specs,
        out_specs=(pl.BlockSpec((tm, d), blk), pl.BlockSpec((tm * nsub, LANES), blk),
                   pl.BlockSpec((tm, LANES), blk)),
        out_shape=(jax.ShapeDtypeStruct((nt, d), f32),
                   jax.ShapeDtypeStruct((nt * nsub, LANES), jnp.uint32),
                   jax.ShapeDtypeStruct((nt, LANES), f32)),
        compiler_params=_cparams(("parallel", "arbitrary")),
        name="branch_mix",
    )(*ins)


def _route_kernel(lg_ref, cin_ref, rec_ref, gate_ref, cnt_ref, carry_scr, *, tr):
    @pl.when(pl.program_id(0) == 0)
    def _():
        carry_scr[...] = cin_ref[...]

    lane = lax.broadcasted_iota(i32, (tr, LANES), 1)
    lanef = lane.astype(f32)
    v = jnp.where(lane < N_EXPERTS, lg_ref[...], -jnp.inf)
    onehot = jnp.zeros((tr, LANES), f32)
    ids, vals = [], []
    for _ in range(TOP_K):
        mx = jnp.max(v, axis=1, keepdims=True)
        idx = jnp.min(jnp.where(v == mx, lanef, float(LANES)), axis=1, keepdims=True)
        sel = lanef == idx
        onehot = jnp.where(sel, 1.0, onehot)
        v = jnp.where(sel, -jnp.inf, v)
        ids.append(idx)
        vals.append(mx)
    es = [jnp.exp(vk - vals[0]) for vk in vals]
    tot = es[0]
    for e in es[1:]:
        tot = tot + e

    r2 = lax.broadcasted_iota(i32, (tr, tr), 0)
    c2 = lax.broadcasted_iota(i32, (tr, tr), 1)
    before = jnp.where(c2 < r2, 1.0, 0.0).astype(bf16)
    pref = _dot(before, onehot.astype(bf16)) + carry_scr[...]
    carry_scr[...] = carry_scr[...] + jnp.sum(onehot, axis=0, keepdims=True)

    rec = jnp.zeros((tr, LANES), f32)
    gate = jnp.zeros((tr, LANES), f32)
    for k in range(TOP_K):
        rank = jnp.sum(jnp.where(lanef == ids[k], pref, 0.0), axis=1, keepdims=True)
        rec = jnp.where(lane == k, rank * float(N_EXPERTS) + ids[k], rec)
        gate = jnp.where(lane == k, es[k] / tot, gate)
    rec_ref[...] = rec.astype(i32)
    gate_ref[...] = gate
    cnt_ref[...] = carry_scr[...]


def _route(logits, counts_in):
    nt = logits.shape[0]
    tr = _divisor_tile(nt, 512, SUBLANES)
    tile = pl.BlockSpec((tr, LANES), lambda i: (i, 0))
    cnt = pl.BlockSpec((1, LANES), lambda i: (0, 0))
    return pl.pallas_call(
        functools.partial(_route_kernel, tr=tr),
        grid=(nt // tr,),
        in_specs=[tile, cnt],
        out_specs=(tile, tile, cnt),
        out_shape=(jax.ShapeDtypeStruct((nt, LANES), i32),
                   jax.ShapeDtypeStruct((nt, LANES), f32),
                   jax.ShapeDtypeStruct((1, LANES), f32)),
        scratch_shapes=[pltpu.VMEM((1, LANES), f32)],
        compiler_params=_cparams(("arbitrary",)),
        name="route_topk",
    )(logits, counts_in)


def _dest_kernel(rec_ref, start_ref, out_ref):
    rec = rec_ref[...]
    lane = lax.broadcasted_iota(i32, rec.shape, 1)
    eid = jnp.bitwise_and(rec, N_EXPERTS - 1).astype(f32)
    rank = jnp.right_shift(rec, N_EXPERTS.bit_length() - 1)
    lanef = lane.astype(f32)
    start = jnp.zeros(rec.shape, f32)
    for k in range(TOP_K):
        mine = lanef == eid[:, k:k + 1]
        st = jnp.sum(jnp.where(mine, start_ref[...], 0.0), axis=1, keepdims=True)
        start = jnp.where(lane == k, st, start)
    out_ref[...] = start.astype(i32) + rank


def _dest_rows(rec, pad_start):
    nt = rec.shape[0]
    tr = _divisor_tile(nt, 512, SUBLANES)
    tile = pl.BlockSpec((tr, LANES), lambda i: (i, 0))
    start = jnp.zeros((1, LANES), f32).at[0, :N_EXPERTS].set(pad_start.astype(f32))
    out = pl.pallas_call(
        _dest_kernel,
        grid=(nt // tr,),
        in_specs=[tile, pl.BlockSpec((1, LANES), lambda i: (0, 0))],
        out_specs=tile,
        out_shape=jax.ShapeDtypeStruct((nt, LANES), i32),
        compiler_params=_cparams(("parallel",)),
        name="moe_dest_rows",
    )(rec, start)
    return out[:, :TOP_K].reshape(-1)


def _dispatch_kernel(dest_ref, h_ref, xin_hbm, xout_hbm, sem, *, nsub, ch):
    del xin_hbm
    base = pl.program_id(0) * ch

    def row_copy(n, k):
        src = pl.multiple_of(n * nsub, nsub)
        dst = pl.multiple_of(dest_ref[(base + n) * TOP_K + k] * nsub, nsub)
        return pltpu.make_async_copy(h_ref.at[pl.ds(src, nsub)], xout_hbm.at[pl.ds(dst, nsub)],
                                     sem)

    def body(n, carry):
        for k in range(TOP_K):
            row_copy(n, k).start()
        return carry

    lax.fori_loop(0, ch, body, 0, unroll=4)
    total = ch * TOP_K * nsub
    pltpu.make_async_copy(xout_hbm.at[pl.ds(0, total)], xout_hbm.at[pl.ds(0, total)], sem).wait()


def _dispatch(dest_flat, h2rows, xrows, nsub):
    n_tok = dest_flat.shape[0] // TOP_K
    ch = min(DISPATCH_CHUNK, n_tok)
    assert n_tok % ch == 0
    grid_spec = pltpu.PrefetchScalarGridSpec(
        num_scalar_prefetch=1,
        grid=(n_tok // ch,),
        in_specs=[pl.BlockSpec((ch * nsub, LANES), lambda i, dr: (i, 0)),
                  pl.BlockSpec(memory_space=pl.ANY)],
        out_specs=pl.BlockSpec(memory_space=pl.ANY),
        scratch_shapes=[pltpu.SemaphoreType.DMA(())])
    return pl.pallas_call(
        functools.partial(_dispatch_kernel, nsub=nsub, ch=ch),
        grid_spec=grid_spec,
        out_shape=jax.ShapeDtypeStruct(xrows.shape, xrows.dtype),
        input_output_aliases={2: 0},
        compiler_params=_cparams(("arbitrary",)),
        name="moe_dispatch",
    )(dest_flat, h2rows, xrows)


def _expert_kernel(be_ref, nused_ref, x_ref, wgu_ref, bgu_ref, wd_ref, bd_ref, y_ref,
                   wgu_b, wd_b, *, tmx, nsub_x, nsub, dff, chunk):
    r = pl.program_id(0)
    live = r < nused_ref[0]

    @pl.when(jnp.logical_not(live))
    def _():
        y_ref[...] = jnp.zeros_like(y_ref)

    new_expert = jnp.logical_or(r == 0, be_ref[r] != be_ref[jnp.maximum(r - 1, 0)])

    @pl.when(jnp.logical_and(live, new_expert))
    def _():
        for c in range(0, wgu_b.shape[0], chunk):
            wgu_b[c:c + chunk, :] = wgu_ref[0, c:c + chunk, :].astype(bf16)
        for c in range(0, wd_b.shape[0], chunk):
            wd_b[c:c + chunk, :] = wd_ref[0, c:c + chunk, :].astype(bf16)

    @pl.when(live)
    def _():
        words = [x_ref[pl.ds(s, tmx, stride=nsub_x), :] for s in range(nsub_x)]
        as_bf16 = lambda w: lax.bitcast_convert_type(w, f32).astype(bf16)
        x = jnp.concatenate(
            [as_bf16(jnp.bitwise_and(w, jnp.uint32(0xFFFF0000))) for w in words]
            + [as_bf16(jnp.left_shift(w, jnp.uint32(16))) for w in words], axis=1)
        gu = _dot(x, wgu_b[...]) + bgu_ref[0]
        gate = jnp.minimum(gu[:, :dff], SWIGLU_LIMIT)
        up = jnp.clip(gu[:, dff:], -SWIGLU_LIMIT, SWIGLU_LIMIT)
        glu = gate * _sigmoid(SWIGLU_ALPHA * gate)
        y = _dot(((up + 1.0) * glu).astype(bf16), wd_b[...]) + bd_ref[0]
        for s in range(nsub):
            y_ref[pl.ds(s, tmx, stride=nsub), :] = y[:, s * LANES:(s + 1) * LANES]


def _experts(block_e, nused, xrows, wgu, bgu, wd, bd):
    n_blocks = block_e.shape[0]
    tmx = EXPERT_TILE
    e, d, dff2 = wgu.shape
    dff = dff2 // 2
    nsub = d // LANES
    nsub_x = d // (2 * LANES)
    live = lambda r, nu: jnp.minimum(r, nu[0] - 1)
    rows = pl.BlockSpec((tmx * nsub_x, LANES), lambda r, be, nu: (live(r, nu), 0))
    per_e = lambda shp: pl.BlockSpec((1,) + shp, lambda r, be, nu: (be[live(r, nu)], 0, 0))
    grid_spec = pltpu.PrefetchScalarGridSpec(
        num_scalar_prefetch=2,
        grid=(n_blocks,),
        in_specs=[rows, per_e((d, dff2)), per_e((1, dff2)), per_e((dff, d)), per_e((1, d))],
        out_specs=pl.BlockSpec((tmx * nsub, LANES), lambda r, be, nu: (r, 0)),
        scratch_shapes=[pltpu.VMEM((d, dff2), bf16), pltpu.VMEM((dff, d), bf16)])
    return pl.pallas_call(
        functools.partial(_expert_kernel, tmx=tmx, nsub_x=nsub_x, nsub=nsub, dff=dff,
                          chunk=min(256, d, dff)),
        grid_spec=grid_spec,
        out_shape=jax.ShapeDtypeStruct((n_blocks * tmx * nsub, LANES), f32),
        compiler_params=_cparams(("arbitrary",)),
        name="moe_experts",
    )(block_e, nused, xrows, wgu, bgu.reshape(e, 1, dff2), wd, bd.reshape(e, 1, d))


def _combine_kernel(dest_ref, y_hbm, gate_ref, x1_ref, gt2_ref, gfin_ref, out_ref,
                    buf, sem, *, tc, nsub):
    base = pl.program_id(0) * tc

    def row_copy(n, k):
        src = pl.multiple_of(dest_ref[(base + n) * TOP_K + k] * nsub, nsub)
        dst = pl.multiple_of((k * tc + n) * nsub, nsub)
        return pltpu.make_async_copy(y_hbm.at[pl.ds(src, nsub)], buf.at[pl.ds(dst, nsub)], sem)

    def body(n, carry):
        for k in range(TOP_K):
            row_copy(n, k).start()
        return carry

    lax.fori_loop(0, tc, body, 0, unroll=4)
    pltpu.make_async_copy(y_hbm.at[pl.ds(0, buf.shape[0])], buf, sem).wait()

    g = gate_ref[...]
    cols = []
    for s in range(nsub):
        acc = None
        for k in range(TOP_K):
            term = g[:, k:k + 1] * buf[pl.ds(k * tc * nsub + s, tc, stride=nsub), :]
            acc = term if acc is None else acc + term
        cols.append(acc)
    y = jnp.concatenate(cols, axis=1)
    out_ref[...] = _rms(x1_ref[...] + gt2_ref[0] * y) * gfin_ref[...]


def _combine(dest_flat, yrows, gates, x1, gt2, g_final, t_per_mod, nsub):
    n_tok, d = x1.shape
    tc = min(TOKEN_TILE, n_tok)
    assert n_tok % tc == 0
    rows = gt2.shape[1]
    if rows == 1:
        per_mod = t_per_mod // tc
        mod = pl.BlockSpec((1, 1, d), lambda i, dr: (i // per_mod, 0, 0))
    else:
        mod = pl.BlockSpec((1, tc, d), lambda i, dr: (0, i, 0))
    blk = lambda i, dr: (i, 0)
    gfin = g_final.reshape(1, d)
    grid_spec = pltpu.PrefetchScalarGridSpec(
        num_scalar_prefetch=1,
        grid=(n_tok // tc,),
        in_specs=[pl.BlockSpec(memory_space=pl.ANY), pl.BlockSpec((tc, LANES), blk),
                  pl.BlockSpec((tc, d), blk), mod,
                  pl.BlockSpec((1, d), lambda i, dr: (0, 0))],
        out_specs=pl.BlockSpec((tc, d), blk),
        scratch_shapes=[pltpu.VMEM((TOP_K * tc * nsub, LANES), f32), pltpu.SemaphoreType.DMA(())])
    return pl.pallas_call(
        functools.partial(_combine_kernel, tc=tc, nsub=nsub),
        grid_spec=grid_spec,
        out_shape=jax.ShapeDtypeStruct((n_tok, d), f32),
        compiler_params=_cparams(("arbitrary",)),
        name="moe_combine",
    )(dest_flat, yrows, gates, x1, gt2, gfin)


def kernel(x_prompt, x_sample, c_prompt, c_sample, cache_k, cache_v, cache_logf, state_gla,
           page_table, g_mix, g_ffn, g_final, w_ada, b_ada, w_in, b_f, w_alpha, b_alpha, g_gla,
           w_branch, w_o, w_router, b_router, w_gu, b_gu, w_d, b_d):
    depth = g_mix.shape[0]
    assert depth == 1, "one decoder layer"
    bp, t, d = x_prompt.shape
    db = x_sample.shape[0]
    assert x_sample.shape[1] == 1
    nsub = d // LANES
    ntp = bp * t
    nt = ntp + db

    nmod = bp + db
    pad = (-nmod) % SUBLANES
    c_all = jnp.concatenate([c_prompt, c_sample, jnp.zeros((pad, d), f32)], axis=0)
    mod = _adaln(c_all, w_ada[0], b_ada[0])
    sh1, sc1, gt1, sh2, sc2, gt2 = [mod[:, i * d:(i + 1) * d] for i in range(6)]
    grp_p = lambda a: a[:bp].reshape(bp, 1, d)
    grp_s = lambda a: a[bp:nmod].reshape(1, db, d)

    w_proj = _inproj_weights(w_in[0], d)
    wal = jnp.zeros((LANES, WK_B), f32).at[H_A:H_A + GLA_RANK].set(w_alpha[0]).astype(bf16)
    bal = b_alpha[0].reshape(1, WK_B)
    gg = g_gla[0].reshape(1, WV_B)
    wba = w_branch[0, :W_A].astype(bf16)
    wbb = w_branch[0, W_A:].astype(bf16)
    wo = w_o[0].astype(bf16)
    wr = jnp.zeros((d, LANES), f32).at[:, :N_EXPERTS].set(w_router[0]).astype(bf16)
    br = jnp.zeros((1, LANES), f32).at[0, :N_EXPERTS].set(b_router[0])

    (qat, kat, vt, ka, va, qb, kb, vb, rb, ga, gb, sm, fat) = _inproj(
        x_prompt, grp_p(1.0 + sc1), grp_p(sh1), g_mix[0], w_proj, True)
    logft, cumt = _logf(fat, b_f[0], True)
    oa = _fox_prompt(qat, kat, vt, jnp.transpose(cumt, (0, 2, 1)))
    ob, gla_p = _gla_prompt(qb, kb, vb, rb, sm, wal, bal, gg)
    x1_p, h2_p, lg_p = _mix(x_prompt, oa, ob, ga, gb, grp_p(gt1), grp_p(1.0 + sc2), grp_p(sh2),
                            g_ffn[0], wba, wbb, wo, wr, br)

    xs = x_sample.reshape(1, db, d)
    (qat_s, ka_s, va_s, qb_s, kb_s, vb_s, rb_s, ga_s, gb_s, sm_s, fat_s) = _inproj(
        xs, grp_s(1.0 + sc1), grp_s(sh1), g_mix[0], w_proj, False)
    logft_s, = _logf(fat_s, b_f[0], False)
    lfnew = jnp.transpose(logft_s[0], (1, 0)).reshape(db, H_A, 1)
    qbd = jnp.einsum('hnd,hc->nchd', qat_s[0], jnp.eye(H_A, dtype=bf16)).reshape(db, H_A, W_A)
    oa_s = _fox_decode(qbd, ka_s.reshape(db, 1, W_A), va_s.reshape(db, 1, W_A), lfnew,
                       jnp.transpose(cache_k[0], (0, 2, 3, 1)),
                       jnp.transpose(cache_v[0], (0, 2, 3, 1)),
                       jnp.transpose(cache_logf[0], (0, 2, 1)), page_table)
    ob_s, gla_s = _gla_decode(qb_s[0], kb_s[0], vb_s[0], rb_s[0], sm_s[0], wal, bal, gg,
                              state_gla[0])
    x1_s, h2_s, lg_s = _mix(xs, oa_s.reshape(db, W_A).T[None], ob_s.reshape(1, db, WV_B), ga_s, gb_s,
                            grp_s(gt1), grp_s(1.0 + sc2), grp_s(sh2), g_ffn[0],
                            wba, wbb, wo, wr, br)

    rec_p, gates_p, cnt_p = _route(lg_p, jnp.zeros((1, LANES), f32))
    rec_s, gates_s, cnt = _route(lg_s, cnt_p)
    counts = cnt[0, :N_EXPERTS].astype(i32)
    padded = (counts + EXPERT_TILE - 1) // EXPERT_TILE * EXPERT_TILE
    pad_end = jnp.cumsum(padded).astype(i32)
    pad_start = pad_end - padded

    dest_p = _dest_rows(rec_p, pad_start)
    dest_s = _dest_rows(rec_s, pad_start)
    n_blocks = -(-(nt * TOP_K + N_EXPERTS * (EXPERT_TILE - 1)) // EXPERT_TILE)
    block_start = jnp.arange(n_blocks, dtype=i32) * EXPERT_TILE
    block_e = jnp.minimum(jnp.sum(pad_end[None, :] <= block_start[:, None], axis=1),
                          N_EXPERTS - 1).astype(i32)
    nused = (pad_end[-1:] // EXPERT_TILE).astype(i32)

    nsub_x = d // (2 * LANES)
    xrows = jnp.zeros((n_blocks * EXPERT_TILE * nsub_x, LANES), jnp.uint32)
    xrows = _dispatch(dest_p, h2_p, xrows, nsub_x)
    xrows = _dispatch(dest_s, h2_s, xrows, nsub_x)
    yrows = _experts(block_e, nused, xrows, w_gu[0], b_gu[0], w_d[0], b_d[0])
    y_p = _combine(dest_p, yrows, gates_p, x1_p, grp_p(gt2), g_final, t, nsub)
    y_s = _combine(dest_s, yrows, gates_s, x1_s, grp_s(gt2), g_final, 1, nsub)

    logf_p = jnp.transpose(logft, (0, 2, 1))
    return (y_p.reshape(bp, t, d), y_s.reshape(db, 1, d),
            jnp.transpose(ka, (0, 3, 1, 2))[None], jnp.transpose(va, (0, 3, 1, 2))[None],
            logf_p.reshape(1, bp, t, H_A), gla_p.reshape(1, bp, H_B, DK_B, DV_B),
            ka_s.reshape(1, db, 1, H_A, DH_A), va_s.reshape(1, db, 1, H_A, DH_A),
            lfnew.reshape(1, db, 1, H_A), gla_s.reshape(1, db, H_B, DK_B, DV_B))
```

```python
import functools

import jax
import jax.numpy as jnp
from jax import lax
from jax.experimental import pallas as pl
from jax.experimental.pallas import tpu as pltpu

f32 = jnp.float32
bf16 = jnp.bfloat16
i32 = jnp.int32

H_A, DH_A = 8, 64
H_B, DK_B, DV_B = 4, 64, 128
GLA_RANK = 16
GLA_TAU = 16.0
N_EXPERTS = 32
TOP_K = 4
SWIGLU_LIMIT = 7.0
SWIGLU_ALPHA = 1.702
RMS_EPS = 1e-6
W_A = H_A * DH_A
WK_B = H_B * DK_B
WV_B = H_B * DV_B

LANES = 128
SUBLANES = 8
VMEM_LIMIT_MB = 56

TOKEN_TILE = 256
ATTN_Q_TILE = 512
ATTN_K_TILE = 256
GLA_TILE = 256
EXPERT_TILE = 512
PAGES_PER_STEP = 32
DISPATCH_CHUNK = 256

_NT = (((1,), (1,)), ((), ()))
_TN = (((0,), (0,)), ((), ()))


def _cparams(semantics, vmem_mb=VMEM_LIMIT_MB):
    return pltpu.CompilerParams(dimension_semantics=semantics,
                                vmem_limit_bytes=vmem_mb << 20)


def _sigmoid(x):
    return 1.0 / (1.0 + jnp.exp(-x))


def _log_sigmoid(x):
    return jnp.minimum(x, 0.0) - jnp.log1p(jnp.exp(-jnp.abs(x)))


def _split3(x):
    hi = x.astype(bf16)
    r = x - hi.astype(f32)
    mid = r.astype(bf16)
    lo = (r - mid.astype(f32)).astype(bf16)
    return hi, mid, lo


def _dot(a, b):
    return jnp.dot(a, b, preferred_element_type=f32)


def _dotg(a, b, dims):
    return lax.dot_general(a, b, dims, preferred_element_type=f32)


def _rms(x):
    return x * lax.rsqrt(jnp.mean(x * x, axis=-1, keepdims=True) + RMS_EPS)


def _divisor_tile(n, cap, mult):
    best = None
    t = mult
    while t <= min(n, cap):
        if n % t == 0:
            best = t
        t += mult
    assert best is not None, (n, cap, mult)
    return best


def _mod_kernel(c_ref, w_ref, b_ref, o_ref):
    c = c_ref[...]
    s = (c * _sigmoid(c)).astype(bf16)
    o_ref[...] = _dot(s, w_ref[...].astype(bf16)) + b_ref[...]


def _adaln(c_all, w_ada, b_ada):
    r, d = c_all.shape
    n = w_ada.shape[1]
    tn = _divisor_tile(n, 1536, LANES)
    return pl.pallas_call(
        _mod_kernel,
        grid=(n // tn,),
        in_specs=[pl.BlockSpec((r, d), lambda j: (0, 0)),
                  pl.BlockSpec((d, tn), lambda j: (0, j)),
                  pl.BlockSpec((1, tn), lambda j: (0, j))],
        out_specs=pl.BlockSpec((r, tn), lambda j: (0, j)),
        out_shape=jax.ShapeDtypeStruct((r, n), f32),
        compiler_params=_cparams(("arbitrary",)),
        name="adaln_mod",
    )(c_all, w_ada, b_ada.reshape(1, n))


def _inproj_kernel(x_ref, sc_ref, sh_ref, g_ref, wbig_ref, wsm_ref, wfa_ref, wkt_ref, wvt_ref,
                   *out_refs, d, time_minor):
    if time_minor:
        (qat_ref, kat_ref, vt_ref, ka_ref, va_ref, qb_ref, kb_ref, vb_ref, rb_ref,
         ga_ref, gb_ref, sm_ref, fat_ref) = out_refs
    else:
        (qat_ref, ka_ref, va_ref, qb_ref, kb_ref, vb_ref, rb_ref,
         ga_ref, gb_ref, sm_ref, fat_ref) = out_refs
    x = x_ref[0]
    h = (_rms(x) * g_ref[...]) * sc_ref[0] + sh_ref[0]
    hb = h.astype(bf16)
    off = [0]

    def mm(width):
        r = _dot(hb, wbig_ref[:, off[0]:off[0] + width])
        off[0] += width
        return r

    qa = mm(W_A)
    ka = mm(W_A)
    va = mm(W_A)
    for hh in range(H_A):
        sl = slice(hh * DH_A, (hh + 1) * DH_A)
        qat_ref[0, hh] = qa[:, sl].astype(bf16)
        if time_minor:
            kat_ref[0, hh] = ka[:, sl].astype(bf16)
    if time_minor:
        tm = x.shape[0]
        kt = _dotg(wkt_ref[...], hb, _NT).reshape(H_A, DH_A, tm)
        vt = _dotg(wvt_ref[...], hb, _NT).reshape(H_A, DH_A, tm)
        ka_ref[0] = kt
        va_ref[0] = vt
        vt_ref[0] = vt.astype(bf16)
    else:
        ka_ref[0] = ka
        va_ref[0] = va
    qb_ref[0] = mm(WK_B)
    kb_ref[0] = mm(WK_B)
    vb_ref[0] = mm(WV_B)
    rb_ref[0] = mm(WV_B)
    ga_ref[0] = mm(d)
    gb_ref[0] = mm(d)
    sm_ref[0] = _dot(hb, wsm_ref[...])
    fat_ref[0] = _dotg(wfa_ref[...].astype(bf16), hb, _NT)


def _inproj_weights(w_in, d):
    sizes = (W_A, W_A, W_A, H_A, WK_B, WK_B, WV_B, WV_B, GLA_RANK, d, d)
    offs = [0]
    for s in sizes:
        offs.append(offs[-1] + s)
    seg = lambda i: w_in[:, offs[i]:offs[i + 1]]
    wbig = jnp.concatenate(
        [seg(0) * (DH_A ** -0.5), seg(1), seg(2), seg(4) * (DK_B ** -0.5), seg(5),
         seg(6), seg(7), seg(9), seg(10)], axis=1).astype(bf16)
    wsm = jnp.concatenate(
        [seg(3), seg(8), jnp.zeros((d, LANES - H_A - GLA_RANK), f32)], axis=1).astype(bf16)
    wfa = seg(3).T
    wkt = seg(1).T.astype(bf16)
    wvt = seg(2).T.astype(bf16)
    return wbig, wsm, wfa, wkt, wvt


def _inproj(x, sc, sh, g_mix, weights, time_minor):
    wbig, wsm, wfa, wkt, wvt = weights
    b, t, d = x.shape
    tm = min(TOKEN_TILE, t)
    assert t % tm == 0
    rows = sc.shape[1]
    mod_block = (1, 1, d) if rows == 1 else (1, tm, d)
    mod_map = (lambda bi, i: (bi, 0, 0)) if rows == 1 else (lambda bi, i: (bi, i, 0))
    tok3 = lambda w: pl.BlockSpec((1, tm, w), lambda bi, i: (bi, i, 0))
    head4 = pl.BlockSpec((1, H_A, tm, DH_A), lambda bi, i: (bi, 0, i, 0))
    const2 = lambda a: pl.BlockSpec(a.shape, lambda bi, i: (0, 0))
    sds = jax.ShapeDtypeStruct
    headt = pl.BlockSpec((1, H_A, DH_A, tm), lambda bi, i: (bi, 0, 0, i))
    if time_minor:
        kv_shape = [sds((b, H_A, t, DH_A), bf16), sds((b, H_A, DH_A, t), bf16),
                    sds((b, H_A, DH_A, t), f32), sds((b, H_A, DH_A, t), f32)]
        kv_specs = [head4, headt, headt, headt]
    else:
        kv_shape = [sds((b, t, W_A), f32), sds((b, t, W_A), f32)]
        kv_specs = [tok3(W_A), tok3(W_A)]
    out_shape = tuple(
        [sds((b, H_A, t, DH_A), bf16)] + kv_shape
        + [sds((b, t, WK_B), f32), sds((b, t, WK_B), f32),
           sds((b, t, WV_B), f32), sds((b, t, WV_B), f32),
           sds((b, t, d), f32), sds((b, t, d), f32),
           sds((b, t, LANES), f32), sds((b, H_A, t), f32)])
    out_specs = tuple(
        [head4] + kv_specs
        + [tok3(WK_B), tok3(WK_B), tok3(WV_B), tok3(WV_B), tok3(d), tok3(d), tok3(LANES),
           pl.BlockSpec((1, H_A, tm), lambda bi, i: (bi, 0, i))])
    g2 = g_mix.reshape(1, d)
    return pl.pallas_call(
        functools.partial(_inproj_kernel, d=d, time_minor=time_minor),
        grid=(b, t // tm),
        in_specs=[tok3(d), pl.BlockSpec(mod_block, mod_map), pl.BlockSpec(mod_block, mod_map),
                  const2(g2), const2(wbig), const2(wsm), const2(wfa), const2(wkt), const2(wvt)],
        out_specs=out_specs,
        out_shape=out_shape,
        compiler_params=_cparams(("parallel", "arbitrary")),
        name="inproj",
    )(x, sc, sh, g2, wbig, wsm, wfa, wkt, wvt)


def _logf_kernel(fa_ref, bf_ref, logf_ref, *cum_refs, t, cb):
    lf = _log_sigmoid(fa_ref[...] + bf_ref[...])
    logf_ref[...] = lf
    if not cum_refs:
        return
    cum_ref, = cum_refs
    r = lax.broadcasted_iota(i32, (cb, cb), 0)
    c = lax.broadcasted_iota(i32, (cb, cb), 1)
    triu = jnp.where(r <= c, 1.0, 0.0).astype(bf16)
    carry = jnp.zeros((lf.shape[0], 1), f32)
    for j in range(t // cb):
        hi, mid, lo = _split3(lf[:, j * cb:(j + 1) * cb])
        blk = _dot(hi, triu) + _dot(mid, triu) + _dot(lo, triu) + carry
        cum_ref[:, j * cb:(j + 1) * cb] = blk
        carry = blk[:, cb - 1:cb]


def _logf(fat, b_f, with_cumsum):
    b, _, t = fat.shape
    rows = b * H_A
    fa2 = fat.reshape(rows, t)
    bf2 = jnp.tile(b_f.reshape(H_A, 1), (b, 1))
    cb = min(256, t)
    full = lambda a: pl.BlockSpec(a.shape, lambda: (0,) * a.ndim)
    n_out = 2 if with_cumsum else 1
    outs = pl.pallas_call(
        functools.partial(_logf_kernel, t=t, cb=cb),
        in_specs=[full(fa2), full(bf2)],
        out_specs=tuple(pl.BlockSpec((rows, t), lambda: (0, 0)) for _ in range(n_out)),
        out_shape=tuple(jax.ShapeDtypeStruct((rows, t), f32) for _ in range(n_out)),
        name="log_forget",
    )(fa2, bf2)
    return tuple(o.reshape(b, H_A, t) for o in outs)


def _bias_lanes(col, ones_first):
    hi, mid, lo = [p.astype(f32) for p in _split3(col)]
    lane = lax.broadcasted_iota(i32, (col.shape[0], DH_A), 1)
    base = 3 if ones_first else 0
    parts = jnp.where(lane == base, hi, jnp.where(lane == base + 1, mid,
                      jnp.where(lane == base + 2, lo, 0.0)))
    ones = (lane < 3) if ones_first else ((lane >= 3) & (lane < 6))
    return jnp.where(ones, 1.0, parts)


def _fox_kernel(q_ref, k_ref, vt_ref, cum_ref, o_ref, kaug, qaug, *, tq, tk):
    i = pl.program_id(1)
    per_q = tq // tk

    @pl.when(i == 0)
    def _():
        cum_all = cum_ref[0]
        for h in range(H_A):
            kaug[h, :, :DH_A] = k_ref[0, h]
            kaug[h, :, DH_A:] = _bias_lanes(-cum_all[:, h:h + 1], False).astype(bf16)

    cum_q = cum_ref[0, pl.ds(pl.multiple_of(i * tq, tq), tq), :]
    for h in range(H_A):
        qaug[h, :, :DH_A] = q_ref[0, h]
        qaug[h, :, DH_A:] = _bias_lanes(cum_q[:, h:h + 1], True).astype(bf16)
    key = lax.broadcasted_iota(i32, (tk, tq), 0)
    qry = lax.broadcasted_iota(i32, (tk, tq), 1)

    def step(j, carry, key_off=None):
        start = pl.multiple_of(j * tk, tk)
        scores = [_dotg(kaug[h, pl.ds(start, tk), :], qaug[h], _NT) for h in range(H_A)]
        stats = []
        for h in range(H_A):
            m_old, l_old, _ = carry[h]
            s = scores[h]
            if key_off is not None:
                s = jnp.where(key + key_off <= qry, s, -jnp.inf)
            m_new = jnp.maximum(m_old, jnp.max(s, axis=0, keepdims=True))
            alpha = jnp.exp(m_old - m_new)
            p = jnp.exp(s - m_new)
            l_new = alpha * l_old + jnp.sum(p, axis=0, keepdims=True)
            stats.append((m_new, l_new, alpha, p.astype(bf16)))
        out = []
        for h in range(H_A):
            m_new, l_new, alpha, p = stats[h]
            acc_new = alpha * carry[h][2] + _dot(vt_ref[0, h, :, pl.ds(start, tk)], p)
            out.append((m_new, l_new, acc_new))
        return tuple(out)

    init = tuple((jnp.full((1, tq), -jnp.inf, f32), jnp.zeros((1, tq), f32),
                  jnp.zeros((DH_A, tq), f32)) for _ in range(H_A))
    carry = lax.fori_loop(0, i * per_q, step, init)
    for dblk in range(per_q):
        carry = step(i * per_q + dblk, carry, key_off=dblk * tk)
    for h in range(H_A):
        _, l, acc = carry[h]
        o_ref[0, h * DH_A:(h + 1) * DH_A, :] = (acc / l).astype(bf16)


def _fox_prompt(qat, kat, vt, cum):
    b, _, t, _ = qat.shape
    tq = min(ATTN_Q_TILE, t)
    tk = min(ATTN_K_TILE, tq)
    assert t % tq == 0 and tq % tk == 0
    return pl.pallas_call(
        functools.partial(_fox_kernel, tq=tq, tk=tk),
        grid=(b, t // tq),
        in_specs=[pl.BlockSpec((1, H_A, tq, DH_A), lambda bi, i: (bi, 0, i, 0)),
                  pl.BlockSpec((1, H_A, t, DH_A), lambda bi, i: (bi, 0, 0, 0)),
                  pl.BlockSpec((1, H_A, DH_A, t), lambda bi, i: (bi, 0, 0, 0)),
                  pl.BlockSpec((1, t, H_A), lambda bi, i: (bi, 0, 0))],
        out_specs=pl.BlockSpec((1, W_A, tq), lambda bi, i: (bi, 0, i)),
        out_shape=jax.ShapeDtypeStruct((b, W_A, t), bf16),
        scratch_shapes=[pltpu.VMEM((H_A, t, 2 * DH_A), bf16), pltpu.VMEM((H_A, tq, 2 * DH_A), bf16)],
        compiler_params=_cparams(("parallel", "arbitrary")),
        name="fox_prompt",
    )(qat, kat, vt, cum)


def _fox_decode_kernel(pt_ref, q_ref, knew_ref, vnew_ref, lfnew_ref, kt_hbm, vt_hbm, lft_hbm,
                       o_ref, pages, lfpages, p_scr, m_scr, self_scr, acc_scr, carry_scr,
                       sem_kv, sem_lf, *, pg, nstep, n_pages, db):
    b = pl.program_id(0)
    j = pl.program_id(1)
    q = q_ref[0]
    psz = lfpages.shape[-1]
    lane = lax.broadcasted_iota(i32, (H_A, W_A), 1)
    sub = lax.broadcasted_iota(i32, (H_A, W_A), 0)
    own = (lane // DH_A) == sub
    rnd = lambda z: z.astype(bf16).astype(f32)

    def start_load(seq, t, slot):
        def issue(src_hbm, dst, sem, group):
            for p in range(pg):
                page = pt_ref[seq * n_pages + group * pg + p]
                pltpu.make_async_copy(src_hbm.at[page], dst.at[slot, p], sem.at[slot]).start()

        @pl.when(t < nstep)
        def _():
            issue(kt_hbm, pages, sem_kv, nstep - 1 - t)
            issue(lft_hbm, lfpages, sem_lf, nstep - 1 - t)

        @pl.when(t >= nstep)
        def _():
            issue(vt_hbm, pages, sem_kv, 2 * nstep - 1 - t)

    step = b * (2 * nstep) + j
    slot = step % 2

    @pl.when(step == 0)
    def _():
        start_load(0, 0, 0)

    last = j == 2 * nstep - 1
    nxt_seq = jnp.where(last, b + 1, b)
    nxt_t = jnp.where(last, 0, j + 1)

    @pl.when(nxt_seq < db)
    def _():
        start_load(nxt_seq, nxt_t, 1 - slot)

    pltpu.make_async_copy(kt_hbm.at[pl.ds(0, pg)], pages.at[slot], sem_kv.at[slot]).wait()

    @pl.when(j < nstep)
    def _():
        pltpu.make_async_copy(lft_hbm.at[pl.ds(0, pg)], lfpages.at[slot], sem_lf.at[slot]).wait()

    @pl.when(j == 0)
    def _():
        s_self = jnp.sum(q.astype(f32) * rnd(knew_ref[0]), axis=1, keepdims=True)
        self_scr[...] = s_self
        m_scr[...] = s_self
        carry_scr[...] = lfnew_ref[0]

    @pl.when(j < nstep)
    def _():
        r = lax.broadcasted_iota(i32, (psz, 2 * psz), 0)
        c = lax.broadcasted_iota(i32, (psz, 2 * psz), 1)
        later = jnp.where(c < psz, jnp.where(r > c, 1.0, 0.0), 1.0).astype(bf16)
        group = nstep - 1 - j
        carry = carry_scr[...]
        m = m_scr[...]
        hi, mid, lo = _split3(lfpages[slot].reshape(pg * H_A, psz))
        suf_all = _dot(hi, later) + _dot(mid, later) + _dot(lo, later)
        for p in reversed(range(pg)):
            suf = suf_all[p * H_A:(p + 1) * H_A]
            kt = pages[slot, p].reshape(W_A, psz).astype(bf16)
            s = _dot(q, kt) + (carry + suf[:, :psz])
            carry = carry + suf[:, psz:psz + 1]
            p_scr[group * pg + p] = s
            m = jnp.maximum(m, jnp.max(s, axis=1, keepdims=True))
        carry_scr[...] = carry
        m_scr[...] = m

    @pl.when(j == nstep)
    def _():
        m = m_scr[...]
        e_self = jnp.exp(self_scr[...] - m)
        e = jnp.exp(p_scr[...] - m[None])
        l = e_self + jnp.sum(jnp.sum(e, axis=0), axis=1, keepdims=True)
        p_scr[...] = e / l[None]
        acc_scr[...] = jnp.where(own, rnd(e_self / l) * rnd(vnew_ref[0]), 0.0)

    @pl.when(j >= nstep)
    def _():
        group = 2 * nstep - 1 - j
        acc = acc_scr[...]
        for p in range(pg):
            vt = pages[slot, p].reshape(W_A, psz).astype(bf16)
            acc = acc + _dotg(p_scr[group * pg + p].astype(bf16), vt, _NT)
        acc_scr[...] = acc

    @pl.when(j == 2 * nstep - 1)
    def _():
        o_ref[0] = jnp.sum(jnp.where(own, acc_scr[...], 0.0), axis=0, keepdims=True)


def _fox_decode(qbd, knew, vnew, lfnew, cache_kt, cache_vt, cache_lft, page_table):
    db = qbd.shape[0]
    psz = cache_kt.shape[-1]
    n_pages = page_table.shape[1]
    pg = PAGES_PER_STEP if n_pages % PAGES_PER_STEP == 0 else 1
    nstep = n_pages // pg
    pt = page_table.reshape(-1).astype(i32)
    row3 = lambda a: pl.BlockSpec((1,) + a.shape[1:], lambda bi, j, pt_ref: (bi, 0, 0))
    anyspace = pl.BlockSpec(memory_space=pl.ANY)
    grid_spec = pltpu.PrefetchScalarGridSpec(
        num_scalar_prefetch=1,
        grid=(db, 2 * nstep),
        in_specs=[row3(qbd), row3(knew), row3(vnew), row3(lfnew), anyspace, anyspace, anyspace],
        out_specs=pl.BlockSpec((1, 1, W_A), lambda bi, j, pt_ref: (bi, 0, 0)),
        scratch_shapes=[pltpu.VMEM((2, pg, H_A, DH_A, psz), f32), pltpu.VMEM((2, pg, H_A, psz), f32),
                        pltpu.VMEM((n_pages, H_A, psz), f32), pltpu.VMEM((H_A, 1), f32),
                        pltpu.VMEM((H_A, 1), f32), pltpu.VMEM((H_A, W_A), f32),
                        pltpu.VMEM((H_A, 1), f32),
                        pltpu.SemaphoreType.DMA((2,)), pltpu.SemaphoreType.DMA((2,))])
    return pl.pallas_call(
        functools.partial(_fox_decode_kernel, pg=pg, nstep=nstep, n_pages=n_pages, db=db),
        grid_spec=grid_spec,
        out_shape=jax.ShapeDtypeStruct((db, 1, W_A), f32),
        compiler_params=_cparams(("arbitrary", "arbitrary")),
        name="fox_decode",
    )(pt, qbd, knew, vnew, lfnew, cache_kt, cache_vt, cache_lft)


def _gla_kernel(qb_ref, kb_ref, vb_ref, rb_ref, sm_ref, wal_ref, bal_ref, gg_ref,
                ob_ref, sout_ref, s_scr, *, tt):
    i = pl.program_id(1)

    @pl.when(i == 0)
    def _():
        s_scr[...] = jnp.zeros_like(s_scr)

    q = qb_ref[0]
    k = kb_ref[0]
    z = _dot(sm_ref[0].astype(bf16), wal_ref[...]) + bal_ref[...]
    a = _log_sigmoid(z) * (1.0 / GLA_TAU)

    row = lax.broadcasted_iota(i32, (tt, tt), 0)
    col = lax.broadcasted_iota(i32, (tt, tt), 1)
    tril = jnp.where(col <= row, 1.0, 0.0).astype(bf16)
    hi, mid, lo = _split3(a)
    b = _dot(tril, hi) + _dot(tril, mid) + _dot(tril, lo)

    rowk = lax.broadcasted_iota(i32, (tt, WK_B), 0)
    hs = [slice(h * DK_B, (h + 1) * DK_B) for h in range(H_B)]
    amat = [jnp.zeros((tt, tt), f32) for _ in range(H_B)]
    edge = b
    for lvl in range(tt.bit_length() - 1):
        half = 1 << lvl
        upper = ((rowk >> lvl) & 1) == 1
        kt = jnp.where(upper, 0.0, k * jnp.exp(edge - b)).astype(bf16)
        edge_q = pltpu.roll(edge, half, 0)
        qt = jnp.where(upper, q * jnp.exp(b - edge_q), 0.0).astype(bf16)
        same = (row >> (lvl + 1)) == (col >> (lvl + 1))
        for h in range(H_B):
            amat[h] = amat[h] + jnp.where(same, _dotg(qt[:, hs[h]], kt[:, hs[h]], _NT), 0.0)
        edge = jnp.where(upper, edge, pltpu.roll(edge, tt - half, 0))
    q16 = q.astype(bf16)
    k16 = k.astype(bf16)
    for h in range(H_B):
        amat[h] = amat[h] + jnp.where(row == col, _dotg(q16[:, hs[h]], k16[:, hs[h]], _NT), 0.0)

    qe = (q * jnp.exp(b)).astype(bf16)
    ke = (k * jnp.exp(edge - b)).astype(bf16)
    e_last = jnp.exp(edge[0:1, :])
    v16 = vb_ref[0].astype(bf16)
    r = rb_ref[0]
    er = lax.broadcasted_iota(i32, (DK_B, DK_B), 0)
    ec = lax.broadcasted_iota(i32, (DK_B, DK_B), 1)
    for h in range(H_B):
        vs = slice(h * DV_B, (h + 1) * DV_B)
        vh = v16[:, vs]
        state = s_scr[h]
        o = _dot(amat[h].astype(bf16), vh) + _dot(qe[:, hs[h]], state.astype(bf16))
        decay_col = jnp.sum(
            jnp.where(er == ec, jnp.broadcast_to(e_last[:, hs[h]], (DK_B, DK_B)), 0.0),
            axis=1, keepdims=True)
        s_scr[h] = decay_col * state + _dotg(ke[:, hs[h]], vh, _TN)
        rh = r[:, vs]
        ob_ref[0, :, vs] = (_rms(o) * gg_ref[:, vs] * (rh * _sigmoid(rh))).astype(bf16)

    @pl.when(i == pl.num_programs(1) - 1)
    def _():
        sout_ref[0] = s_scr[...]


def _gla_prompt(qb, kb, vb, rb, sm, wal, bal, gg):
    b, t, _ = qb.shape
    tt = min(GLA_TILE, t)
    assert t % tt == 0 and tt & (tt - 1) == 0
    tok3 = lambda w: pl.BlockSpec((1, tt, w), lambda bi, i: (bi, i, 0))
    const2 = lambda a: pl.BlockSpec(a.shape, lambda bi, i: (0, 0))
    return pl.pallas_call(
        functools.partial(_gla_kernel, tt=tt),
        grid=(b, t // tt),
        in_specs=[tok3(WK_B), tok3(WK_B), tok3(WV_B), tok3(WV_B), tok3(LANES),
                  const2(wal), const2(bal), const2(gg)],
        out_specs=(tok3(WV_B),
                   pl.BlockSpec((1, H_B, DK_B, DV_B), lambda bi, i: (bi, 0, 0, 0))),
        out_shape=(jax.ShapeDtypeStruct((b, t, WV_B), bf16),
                   jax.ShapeDtypeStruct((b, H_B, DK_B, DV_B), f32)),
        scratch_shapes=[pltpu.VMEM((H_B, DK_B, DV_B), f32)],
        compiler_params=_cparams(("parallel", "arbitrary")),
        name="gla_prompt",
    )(qb, kb, vb, rb, sm, wal, bal, gg)


def _loga_kernel(sm_ref, wal_ref, bal_ref, o_ref):
    z = _dot(sm_ref[...].astype(bf16), wal_ref[...]) + bal_ref[...]
    o_ref[...] = _log_sigmoid(z) * (1.0 / GLA_TAU)


def _gla_decode_kernel(q_ref, k_ref, a_ref, v_ref, r_ref, s_ref, gg_ref, o_ref, so_ref):
    for h in range(H_B):
        q, k, a = q_ref[0, h], k_ref[0, h], a_ref[0, h]
        v = v_ref[0, h]
        state = s_ref[0, h]
        ea = jnp.exp(a)
        rnd = lambda z: z.astype(bf16).astype(f32)
        qk = jnp.sum(q * k, axis=0, keepdims=True)
        o = qk * v + jnp.sum(rnd(q * ea) * rnd(state), axis=0, keepdims=True)
        so_ref[0, h] = ea * state + k * v
        r = r_ref[0, h]
        o_ref[0, h] = _rms(o) * gg_ref[h] * (r * _sigmoid(r))


def _gla_decode(qb, kb, vb, rb, sm, wal, bal, gg, state):
    db = qb.shape[0]
    full = lambda a: pl.BlockSpec(a.shape, lambda: (0,) * a.ndim)
    loga = pl.pallas_call(
        _loga_kernel,
        in_specs=[full(sm), full(wal), full(bal)],
        out_specs=pl.BlockSpec((db, WK_B), lambda: (0, 0)),
        out_shape=jax.ShapeDtypeStruct((db, WK_B), f32),
        name="gla_log_decay",
    )(sm, wal, bal)
    col = lambda a: a.reshape(db, H_B, DK_B, 1)
    rowv = lambda a: a.reshape(db, H_B, 1, DV_B)
    gg4 = gg.reshape(H_B, 1, DV_B)
    b4 = lambda shp: pl.BlockSpec((1,) + shp, lambda bi: (bi, 0, 0, 0))
    o, s_new = pl.pallas_call(
        _gla_decode_kernel,
        grid=(db,),
        in_specs=[b4((H_B, DK_B, 1))] * 3 + [b4((H_B, 1, DV_B))] * 2
        + [b4((H_B, DK_B, DV_B)), pl.BlockSpec(gg4.shape, lambda bi: (0, 0, 0))],
        out_specs=(b4((H_B, 1, DV_B)), b4((H_B, DK_B, DV_B))),
        out_shape=(jax.ShapeDtypeStruct((db, H_B, 1, DV_B), f32),
                   jax.ShapeDtypeStruct((db, H_B, DK_B, DV_B), f32)),
        compiler_params=_cparams(("parallel",)),
        name="gla_decode",
    )(col(qb), col(kb), col(loga), rowv(vb), rowv(rb), state, gg4)
    return o.reshape(db, WV_B), s_new


def _mix_kernel(x_ref, oa_ref, ob_ref, ga_ref, gb_ref, gt1_ref, sc2_ref, sh2_ref, gf_ref,
                wba_ref, wbb_ref, wo_ref, wr_ref, br_ref, x1_ref, h2_ref, lg_ref, *, nsub):
    x = x_ref[0]
    tm = x.shape[0]
    ya = _dotg(oa_ref[0].astype(bf16), wba_ref[...], _TN)
    yb = _dot(ob_ref[0].astype(bf16), wbb_ref[...])
    m = _sigmoid(ga_ref[0]) * ya + _sigmoid(gb_ref[0]) * yb
    x1 = x + gt1_ref[0] * _dot(m.astype(bf16), wo_ref[...])
    x1_ref[...] = x1
    h2 = (_rms(x1) * gf_ref[...]) * sc2_ref[0] + sh2_ref[0]
    hb = h2.astype(bf16)
    lg_ref[...] = _dot(hb, wr_ref[...]) + br_ref[...]
    half = nsub * LANES
    hi = lax.bitcast_convert_type(hb[:, :half].astype(f32), jnp.uint32)
    lo = lax.bitcast_convert_type(hb[:, half:].astype(f32), jnp.uint32)
    packed = jnp.bitwise_or(hi, jnp.right_shift(lo, jnp.uint32(16)))
    for s in range(nsub):
        h2_ref[pl.ds(s, tm, stride=nsub), :] = packed[:, s * LANES:(s + 1) * LANES]


def _mix(x, oa, ob, ga, gb, gt1, sc2, sh2, g_ffn, wba, wbb, wo, wr, br):
    b, t, d = x.shape
    nt = b * t
    nsub = d // (2 * LANES)
    tm = min(TOKEN_TILE, t)
    assert t % tm == 0 and d % (2 * LANES) == 0
    rows = gt1.shape[1]
    mod_block = (1, 1, d) if rows == 1 else (1, tm, d)
    mod_map = (lambda bi, i: (bi, 0, 0)) if rows == 1 else (lambda bi, i: (bi, i, 0))
    mod = pl.BlockSpec(mod_block, mod_map)
    tok3 = lambda w: pl.BlockSpec((1, tm, w), lambda bi, i: (bi, i, 0))
    const2 = lambda a: pl.BlockSpec(a.shape, lambda bi, i: (0, 0))
    nti = t // tm
    blk = lambda bi, i: (bi * nti + i, 0)
    g2 = g_ffn.reshape(1, d)
    ins = [x, oa, ob, ga, gb, gt1, sc2, sh2, g2, wba, wbb, wo, wr, br]
    in_specs = [tok3(d), pl.BlockSpec((1, W_A, tm), lambda bi, i: (bi, 0, i)), tok3(WV_B),
                tok3(d), tok3(d), mod, mod, mod,
                const2(g2), const2(wba), const2(wbb), const2(wo), const2(wr), const2(br)]
    return pl.pallas_call(
        functools.partial(_mix_kernel, nsub=nsub),
        grid=(b, nti),
        in_specs=in_specs,
        out_specs=(pl.BlockSpec((tm, d), blk), pl.BlockSpec((tm * nsub, LANES), blk),
                   pl.BlockSpec((tm, LANES), blk)),
        out_shape=(jax.ShapeDtypeStruct((nt, d), f32),
                   jax.ShapeDtypeStruct((nt * nsub, LANES), jnp.uint32),
                   jax.ShapeDtypeStruct((nt, LANES), f32)),
        compiler_params=_cparams(("parallel", "arbitrary")),
        name="branch_mix",
    )(*ins)


def _route_kernel(lg_ref, cin_ref, rec_ref, gate_ref, cnt_ref, carry_scr, *, tr):
    @pl.when(pl.program_id(0) == 0)
    def _():
        carry_scr[...] = cin_ref[...]

    lane = lax.broadcasted_iota(i32, (tr, LANES), 1)
    lanef = lane.astype(f32)
    v = jnp.where(lane < N_EXPERTS, lg_ref[...], -jnp.inf)
    onehot = jnp.zeros((tr, LANES), f32)
    ids, vals = [], []
    for _ in range(TOP_K):
        mx = jnp.max(v, axis=1, keepdims=True)
        idx = jnp.min(jnp.where(v == mx, lanef, float(LANES)), axis=1, keepdims=True)
        sel = lanef == idx
        onehot = jnp.where(sel, 1.0, onehot)
        v = jnp.where(sel, -jnp.inf, v)
        ids.append(idx)
        vals.append(mx)
    es = [jnp.exp(vk - vals[0]) for vk in vals]
    tot = es[0]
    for e in es[1:]:
        tot = tot + e

    r2 = lax.broadcasted_iota(i32, (tr, tr), 0)
    c2 = lax.broadcasted_iota(i32, (tr, tr), 1)
    before = jnp.where(c2 < r2, 1.0, 0.0).astype(bf16)
    pref = _dot(before, onehot.astype(bf16)) + carry_scr[...]
    carry_scr[...] = carry_scr[...] + jnp.sum(onehot, axis=0, keepdims=True)

    rec = jnp.zeros((tr, LANES), f32)
    gate = jnp.zeros((tr, LANES), f32)
    for k in range(TOP_K):
        rank = jnp.sum(jnp.where(lanef == ids[k], pref, 0.0), axis=1, keepdims=True)
        rec = jnp.where(lane == k, rank * float(N_EXPERTS) + ids[k], rec)
        gate = jnp.where(lane == k, es[k] / tot, gate)
    rec_ref[...] = rec.astype(i32)
    gate_ref[...] = gate
    cnt_ref[...] = carry_scr[...]


def _route(logits, counts_in):
    nt = logits.shape[0]
    tr = _divisor_tile(nt, 512, SUBLANES)
    tile = pl.BlockSpec((tr, LANES), lambda i: (i, 0))
    cnt = pl.BlockSpec((1, LANES), lambda i: (0, 0))
    return pl.pallas_call(
        functools.partial(_route_kernel, tr=tr),
        grid=(nt // tr,),
        in_specs=[tile, cnt],
        out_specs=(tile, tile, cnt),
        out_shape=(jax.ShapeDtypeStruct((nt, LANES), i32),
                   jax.ShapeDtypeStruct((nt, LANES), f32),
                   jax.ShapeDtypeStruct((1, LANES), f32)),
        scratch_shapes=[pltpu.VMEM((1, LANES), f32)],
        compiler_params=_cparams(("arbitrary",)),
        name="route_topk",
    )(logits, counts_in)


def _dest_kernel(rec_ref, start_ref, out_ref):
    rec = rec_ref[...]
    lane = lax.broadcasted_iota(i32, rec.shape, 1)
    eid = jnp.bitwise_and(rec, N_EXPERTS - 1).astype(f32)
    rank = jnp.right_shift(rec, N_EXPERTS.bit_length() - 1)
    lanef = lane.astype(f32)
    start = jnp.zeros(rec.shape, f32)
    for k in range(TOP_K):
        mine = lanef == eid[:, k:k + 1]
        st = jnp.sum(jnp.where(mine, start_ref[...], 0.0), axis=1, keepdims=True)
        start = jnp.where(lane == k, st, start)
    out_ref[...] = start.astype(i32) + rank


def _dest_rows(rec, pad_start):
    nt = rec.shape[0]
    tr = _divisor_tile(nt, 512, SUBLANES)
    tile = pl.BlockSpec((tr, LANES), lambda i: (i, 0))
    start = jnp.zeros((1, LANES), f32).at[0, :N_EXPERTS].set(pad_start.astype(f32))
    out = pl.pallas_call(
        _dest_kernel,
        grid=(nt // tr,),
        in_specs=[tile, pl.BlockSpec((1, LANES), lambda i: (0, 0))],
        out_specs=tile,
        out_shape=jax.ShapeDtypeStruct((nt, LANES), i32),
        compiler_params=_cparams(("parallel",)),
        name="moe_dest_rows",
    )(rec, start)
    return out[:, :TOP_K].reshape(-1)


def _dispatch_kernel(dest_ref, h_ref, xin_hbm, xout_hbm, sem, *, nsub, ch):
    del xin_hbm
    base = pl.program_id(0) * ch

    def row_copy(n, k):
        src = pl.multiple_of(n * nsub, nsub)
        dst = pl.multiple_of(dest_ref[(base + n) * TOP_K + k] * nsub, nsub)
        return pltpu.make_async_copy(h_ref.at[pl.ds(src, nsub)], xout_hbm.at[pl.ds(dst, nsub)],
                                     sem)

    def body(n, carry):
        for k in range(TOP_K):
            row_copy(n, k).start()
        return carry

    lax.fori_loop(0, ch, body, 0, unroll=4)
    total = ch * TOP_K * nsub
    pltpu.make_async_copy(xout_hbm.at[pl.ds(0, total)], xout_hbm.at[pl.ds(0, total)], sem).wait()


def _dispatch(dest_flat, h2rows, xrows, nsub):
    n_tok = dest_flat.shape[0] // TOP_K
    ch = min(DISPATCH_CHUNK, n_tok)
    assert n_tok % ch == 0
    grid_spec = pltpu.PrefetchScalarGridSpec(
        num_scalar_prefetch=1,
        grid=(n_tok // ch,),
        in_specs=[pl.BlockSpec((ch * nsub, LANES), lambda i, dr: (i, 0)),
                  pl.BlockSpec(memory_space=pl.ANY)],
        out_specs=pl.BlockSpec(memory_space=pl.ANY),
        scratch_shapes=[pltpu.SemaphoreType.DMA(())])
    return pl.pallas_call(
        functools.partial(_dispatch_kernel, nsub=nsub, ch=ch),
        grid_spec=grid_spec,
        out_shape=jax.ShapeDtypeStruct(xrows.shape, xrows.dtype),
        input_output_aliases={2: 0},
        compiler_params=_cparams(("arbitrary",)),
        name="moe_dispatch",
    )(dest_flat, h2rows, xrows)


def _expert_kernel(be_ref, nused_ref, x_ref, wgu_ref, bgu_ref, wd_ref, bd_ref, y_ref,
                   wgu_b, wd_b, *, tmx, nsub_x, nsub, dff, chunk):
    r = pl.program_id(0)
    live = r < nused_ref[0]

    @pl.when(jnp.logical_not(live))
    def _():
        y_ref[...] = jnp.zeros_like(y_ref)

    new_expert = jnp.logical_or(r == 0, be_ref[r] != be_ref[jnp.maximum(r - 1, 0)])

    @pl.when(jnp.logical_and(live, new_expert))
    def _():
        for c in range(0, wgu_b.shape[0], chunk):
            wgu_b[c:c + chunk, :] = wgu_ref[0, c:c + chunk, :].astype(bf16)
        for c in range(0, wd_b.shape[0], chunk):
            wd_b[c:c + chunk, :] = wd_ref[0, c:c + chunk, :].astype(bf16)

    @pl.when(live)
    def _():
        words = [x_ref[pl.ds(s, tmx, stride=nsub_x), :] for s in range(nsub_x)]
        as_bf16 = lambda w: lax.bitcast_convert_type(w, f32).astype(bf16)
        x = jnp.concatenate(
            [as_bf16(jnp.bitwise_and(w, jnp.uint32(0xFFFF0000))) for w in words]
            + [as_bf16(jnp.left_shift(w, jnp.uint32(16))) for w in words], axis=1)
        gu = _dot(x, wgu_b[...]) + bgu_ref[0]
        gate = jnp.minimum(gu[:, :dff], SWIGLU_LIMIT)
        up = jnp.clip(gu[:, dff:], -SWIGLU_LIMIT, SWIGLU_LIMIT)
        glu = gate * _sigmoid(SWIGLU_ALPHA * gate)
        y = _dot(((up + 1.0) * glu).astype(bf16), wd_b[...]) + bd_ref[0]
        for s in range(nsub):
            y_ref[pl.ds(s, tmx, stride=nsub), :] = y[:, s * LANES:(s + 1) * LANES]


def _experts(block_e, nused, xrows, wgu, bgu, wd, bd):
    n_blocks = block_e.shape[0]
    tmx = EXPERT_TILE
    e, d, dff2 = wgu.shape
    dff = dff2 // 2
    nsub = d // LANES
    nsub_x = d // (2 * LANES)
    live = lambda r, nu: jnp.minimum(r, nu[0] - 1)
    rows = pl.BlockSpec((tmx * nsub_x, LANES), lambda r, be, nu: (live(r, nu), 0))
    per_e = lambda shp: pl.BlockSpec((1,) + shp, lambda r, be, nu: (be[live(r, nu)], 0, 0))
    grid_spec = pltpu.PrefetchScalarGridSpec(
        num_scalar_prefetch=2,
        grid=(n_blocks,),
        in_specs=[rows, per_e((d, dff2)), per_e((1, dff2)), per_e((dff, d)), per_e((1, d))],
        out_specs=pl.BlockSpec((tmx * nsub, LANES), lambda r, be, nu: (r, 0)),
        scratch_shapes=[pltpu.VMEM((d, dff2), bf16), pltpu.VMEM((dff, d), bf16)])
    return pl.pallas_call(
        functools.partial(_expert_kernel, tmx=tmx, nsub_x=nsub_x, nsub=nsub, dff=dff,
                          chunk=min(256, d, dff)),
        grid_spec=grid_spec,
        out_shape=jax.ShapeDtypeStruct((n_blocks * tmx * nsub, LANES), f32),
        compiler_params=_cparams(("arbitrary",)),
        name="moe_experts",
    )(block_e, nused, xrows, wgu, bgu.reshape(e, 1, dff2), wd, bd.reshape(e, 1, d))


def _combine_kernel(dest_ref, y_hbm, gate_ref, x1_ref, gt2_ref, gfin_ref, out_ref,
                    buf, sem, *, tc, nsub):
    base = pl.program_id(0) * tc

    def row_copy(n, k):
        src = pl.multiple_of(dest_ref[(base + n) * TOP_K + k] * nsub, nsub)
        dst = pl.multiple_of((k * tc + n) * nsub, nsub)
        return pltpu.make_async_copy(y_hbm.at[pl.ds(src, nsub)], buf.at[pl.ds(dst, nsub)], sem)

    def body(n, carry):
        for k in range(TOP_K):
            row_copy(n, k).start()
        return carry

    lax.fori_loop(0, tc, body, 0, unroll=4)
    pltpu.make_async_copy(y_hbm.at[pl.ds(0, buf.shape[0])], buf, sem).wait()

    g = gate_ref[...]
    cols = []
    for s in range(nsub):
        acc = None
        for k in range(TOP_K):
            term = g[:, k:k + 1] * buf[pl.ds(k * tc * nsub + s, tc, stride=nsub), :]
            acc = term if acc is None else acc + term
        cols.append(acc)
    y = jnp.concatenate(cols, axis=1)
    out_ref[...] = _rms(x1_ref[...] + gt2_ref[0] * y) * gfin_ref[...]


def _combine(dest_flat, yrows, gates, x1, gt2, g_final, t_per_mod, nsub):
    n_tok, d = x1.shape
    tc = min(TOKEN_TILE, n_tok)
    assert n_tok % tc == 0
    rows = gt2.shape[1]
    if rows == 1:
        per_mod = t_per_mod // tc
        mod = pl.BlockSpec((1, 1, d), lambda i, dr: (i // per_mod, 0, 0))
    else:
        mod = pl.BlockSpec((1, tc, d), lambda i, dr: (0, i, 0))
    blk = lambda i, dr: (i, 0)
    gfin = g_final.reshape(1, d)
    grid_spec = pltpu.PrefetchScalarGridSpec(
        num_scalar_prefetch=1,
        grid=(n_tok // tc,),
        in_specs=[pl.BlockSpec(memory_space=pl.ANY), pl.BlockSpec((tc, LANES), blk),
                  pl.BlockSpec((tc, d), blk), mod,
                  pl.BlockSpec((1, d), lambda i, dr: (0, 0))],
        out_specs=pl.BlockSpec((tc, d), blk),
        scratch_shapes=[pltpu.VMEM((TOP_K * tc * nsub, LANES), f32), pltpu.SemaphoreType.DMA(())])
    return pl.pallas_call(
        functools.partial(_combine_kernel, tc=tc, nsub=nsub),
        grid_spec=grid_spec,
        out_shape=jax.ShapeDtypeStruct((n_tok, d), f32),
        compiler_params=_cparams(("arbitrary",)),
        name="moe_combine",
    )(dest_flat, yrows, gates, x1, gt2, gfin)


def kernel(x_prompt, x_sample, c_prompt, c_sample, cache_k, cache_v, cache_logf, state_gla,
           page_table, g_mix, g_ffn, g_final, w_ada, b_ada, w_in, b_f, w_alpha, b_alpha, g_gla,
           w_branch, w_o, w_router, b_router, w_gu, b_gu, w_d, b_d):
    depth = g_mix.shape[0]
    assert depth == 1, "one decoder layer"
    bp, t, d = x_prompt.shape
    db = x_sample.shape[0]
    assert x_sample.shape[1] == 1
    nsub = d // LANES
    ntp = bp * t
    nt = ntp + db

    nmod = bp + db
    pad = (-nmod) % SUBLANES
    c_all = jnp.concatenate([c_prompt, c_sample, jnp.zeros((pad, d), f32)], axis=0)
    mod = _adaln(c_all, w_ada[0], b_ada[0])
    sh1, sc1, gt1, sh2, sc2, gt2 = [mod[:, i * d:(i + 1) * d] for i in range(6)]
    grp_p = lambda a: a[:bp].reshape(bp, 1, d)
    grp_s = lambda a: a[bp:nmod].reshape(1, db, d)

    w_proj = _inproj_weights(w_in[0], d)
    wal = jnp.zeros((LANES, WK_B), f32).at[H_A:H_A + GLA_RANK].set(w_alpha[0]).astype(bf16)
    bal = b_alpha[0].reshape(1, WK_B)
    gg = g_gla[0].reshape(1, WV_B)
    wba = w_branch[0, :W_A].astype(bf16)
    wbb = w_branch[0, W_A:].astype(bf16)
    wo = w_o[0].astype(bf16)
    wr = jnp.zeros((d, LANES), f32).at[:, :N_EXPERTS].set(w_router[0]).astype(bf16)
    br = jnp.zeros((1, LANES), f32).at[0, :N_EXPERTS].set(b_router[0])

    (qat, kat, vt, ka, va, qb, kb, vb, rb, ga, gb, sm, fat) = _inproj(
        x_prompt, grp_p(1.0 + sc1), grp_p(sh1), g_mix[0], w_proj, True)
    logft, cumt = _logf(fat, b_f[0], True)
    oa = _fox_prompt(qat, kat, vt, jnp.transpose(cumt, (0, 2, 1)))
    ob, gla_p = _gla_prompt(qb, kb, vb, rb, sm, wal, bal, gg)
    x1_p, h2_p, lg_p = _mix(x_prompt, oa, ob, ga, gb, grp_p(gt1), grp_p(1.0 + sc2), grp_p(sh2),
                            g_ffn[0], wba, wbb, wo, wr, br)

    xs = x_sample.reshape(1, db, d)
    (qat_s, ka_s, va_s, qb_s, kb_s, vb_s, rb_s, ga_s, gb_s, sm_s, fat_s) = _inproj(
        xs, grp_s(1.0 + sc1), grp_s(sh1), g_mix[0], w_proj, False)
    logft_s, = _logf(fat_s, b_f[0], False)
    lfnew = jnp.transpose(logft_s[0], (1, 0)).reshape(db, H_A, 1)
    qbd = jnp.einsum('hnd,hc->nchd', qat_s[0], jnp.eye(H_A, dtype=bf16)).reshape(db, H_A, W_A)
    oa_s = _fox_decode(qbd, ka_s.reshape(db, 1, W_A), va_s.reshape(db, 1, W_A), lfnew,
                       jnp.transpose(cache_k[0], (0, 2, 3, 1)),
                       jnp.transpose(cache_v[0], (0, 2, 3, 1)),
                       jnp.transpose(cache_logf[0], (0, 2, 1)), page_table)
    ob_s, gla_s = _gla_decode(qb_s[0], kb_s[0], vb_s[0], rb_s[0], sm_s[0], wal, bal, gg,
                              state_gla[0])
    x1_s, h2_s, lg_s = _mix(xs, oa_s.reshape(db, W_A).T[None], ob_s.reshape(1, db, WV_B), ga_s, gb_s,
                            grp_s(gt1), grp_s(1.0 + sc2), grp_s(sh2), g_ffn[0],
                            wba, wbb, wo, wr, br)

    rec_p, gates_p, cnt_p = _route(lg_p, jnp.zeros((1, LANES), f32))
    rec_s, gates_s, cnt = _route(lg_s, cnt_p)
    counts = cnt[0, :N_EXPERTS].astype(i32)
    padded = (counts + EXPERT_TILE - 1) // EXPERT_TILE * EXPERT_TILE
    pad_end = jnp.cumsum(padded).astype(i32)
    pad_start = pad_end - padded

    dest_p = _dest_rows(rec_p, pad_start)
    dest_s = _dest_rows(rec_s, pad_start)
    n_blocks = -(-(nt * TOP_K + N_EXPERTS * (EXPERT_TILE - 1)) // EXPERT_TILE)
    block_start = jnp.arange(n_blocks, dtype=i32) * EXPERT_TILE
    block_e = jnp.minimum(jnp.sum(pad_end[None, :] <= block_start[:, None], axis=1),
                          N_EXPERTS - 1).astype(i32)
    nused = (pad_end[-1:] // EXPERT_TILE).astype(i32)

    nsub_x = d // (2 * LANES)
    xrows = jnp.zeros((n_blocks * EXPERT_TILE * nsub_x, LANES), jnp.uint32)
    xrows = _dispatch(dest_p, h2_p, xrows, nsub_x)
    xrows = _dispatch(dest_s, h2_s, xrows, nsub_x)
    yrows = _experts(block_e, nused, xrows, w_gu[0], b_gu[0], w_d[0], b_d[0])
    y_p = _combine(dest_p, yrows, gates_p, x1_p, grp_p(gt2), g_final, t, nsub)
    y_s = _combine(dest_s, yrows, gates_s, x1_s, grp_s(gt2), g_final, 1, nsub)

    logf_p = jnp.transpose(logft, (0, 2, 1))
    return (y_p.reshape(bp, t, d), y_s.reshape(db, 1, d),
            jnp.transpose(ka, (0, 3, 1, 2))[None], jnp.transpose(va, (0, 3, 1, 2))[None],
            logf_p.reshape(1, bp, t, H_A), gla_p.reshape(1, bp, H_B, DK_B, DV_B),
            ka_s.reshape(1, db, 1, H_A, DH_A), va_s.reshape(1, db, 1, H_A, DH_A),
            lfnew.reshape(1, db, 1, H_A), gla_s.reshape(1, db, H_B, DK_B, DV_B))
```

```python
import functools

import jax
import jax.numpy as jnp
from jax import lax
from jax.experimental import pallas as pl
from jax.experimental.pallas import tpu as pltpu

f32 = jnp.float32
bf16 = jnp.bfloat16
i32 = jnp.int32

H_A, DH_A = 8, 64
H_B, DK_B, DV_B = 4, 64, 128
GLA_RANK = 16
GLA_TAU = 16.0
N_EXPERTS = 32
TOP_K = 4
SWIGLU_LIMIT = 7.0
SWIGLU_ALPHA = 1.702
RMS_EPS = 1e-6
W_A = H_A * DH_A
WK_B = H_B * DK_B
WV_B = H_B * DV_B

LANES = 128
SUBLANES = 8
VMEM_LIMIT_MB = 56

TOKEN_TILE = 256
ATTN_Q_TILE = 512
ATTN_K_TILE = 256
GLA_TILE = 256
EXPERT_TILE = 512
PAGES_PER_STEP = 32
DISPATCH_CHUNK = 256

_NT = (((1,), (1,)), ((), ()))
_TN = (((0,), (0,)), ((), ()))


def _cparams(semantics, vmem_mb=VMEM_LIMIT_MB):
    return pltpu.CompilerParams(dimension_semantics=semantics,
                                vmem_limit_bytes=vmem_mb << 20)


def _sigmoid(x):
    return 1.0 / (1.0 + jnp.exp(-x))


def _log_sigmoid(x):
    return jnp.minimum(x, 0.0) - jnp.log1p(jnp.exp(-jnp.abs(x)))


def _split3(x):
    hi = x.astype(bf16)
    r = x - hi.astype(f32)
    mid = r.astype(bf16)
    lo = (r - mid.astype(f32)).astype(bf16)
    return hi, mid, lo


def _dot(a, b):
    return jnp.dot(a, b, preferred_element_type=f32)


def _dotg(a, b, dims):
    return lax.dot_general(a, b, dims, preferred_element_type=f32)


def _rms(x):
    return x * lax.rsqrt(jnp.mean(x * x, axis=-1, keepdims=True) + RMS_EPS)


def _divisor_tile(n, cap, mult):
    best = None
    t = mult
    while t <= min(n, cap):
        if n % t == 0:
            best = t
        t += mult
    assert best is not None, (n, cap, mult)
    return best


def _mod_kernel(c_ref, w_ref, b_ref, o_ref):
    c = c_ref[...]
    s = (c * _sigmoid(c)).astype(bf16)
    o_ref[...] = _dot(s, w_ref[...].astype(bf16)) + b_ref[...]


def _adaln(c_all, w_ada, b_ada):
    r, d = c_all.shape
    n = w_ada.shape[1]
    tn = _divisor_tile(n, 1536, LANES)
    return pl.pallas_call(
        _mod_kernel,
        grid=(n // tn,),
        in_specs=[pl.BlockSpec((r, d), lambda j: (0, 0)),
                  pl.BlockSpec((d, tn), lambda j: (0, j)),
                  pl.BlockSpec((1, tn), lambda j: (0, j))],
        out_specs=pl.BlockSpec((r, tn), lambda j: (0, j)),
        out_shape=jax.ShapeDtypeStruct((r, n), f32),
        compiler_params=_cparams(("arbitrary",)),
        name="adaln_mod",
    )(c_all, w_ada, b_ada.reshape(1, n))


def _inproj_kernel(x_ref, sc_ref, sh_ref, g_ref, wbig_ref, wsm_ref, wfa_ref, wkt_ref, wvt_ref,
                   *out_refs, d, time_minor):
    if time_minor:
        (qat_ref, kat_ref, vt_ref, ka_ref, va_ref, qb_ref, kb_ref, vb_ref, rb_ref,
         ga_ref, gb_ref, sm_ref, fat_ref) = out_refs
    else:
        (qat_ref, ka_ref, va_ref, qb_ref, kb_ref, vb_ref, rb_ref,
         ga_ref, gb_ref, sm_ref, fat_ref) = out_refs
    x = x_ref[0]
    h = (_rms(x) * g_ref[...]) * sc_ref[0] + sh_ref[0]
    hb = h.astype(bf16)
    off = [0]

    def mm(width):
        r = _dot(hb, wbig_ref[:, off[0]:off[0] + width])
        off[0] += width
        return r

    qa = mm(W_A)
    ka = mm(W_A)
    va = mm(W_A)
    for hh in range(H_A):
        sl = slice(hh * DH_A, (hh + 1) * DH_A)
        qat_ref[0, hh] = qa[:, sl].astype(bf16)
        if time_minor:
            kat_ref[0, hh] = ka[:, sl].astype(bf16)
    if time_minor:
        tm = x.shape[0]
        kt = _dotg(wkt_ref[...], hb, _NT).reshape(H_A, DH_A, tm)
        vt = _dotg(wvt_ref[...], hb, _NT).reshape(H_A, DH_A, tm)
        ka_ref[0] = kt
        va_ref[0] = vt
        vt_ref[0] = vt.astype(bf16)
    else:
        ka_ref[0] = ka
        va_ref[0] = va
    qb_ref[0] = mm(WK_B)
    kb_ref[0] = mm(WK_B)
    vb_ref[0] = mm(WV_B)
    rb_ref[0] = mm(WV_B)
    ga_ref[0] = mm(d)
    gb_ref[0] = mm(d)
    sm_ref[0] = _dot(hb, wsm_ref[...])
    fat_ref[0] = _dotg(wfa_ref[...].astype(bf16), hb, _NT)


def _inproj_weights(w_in, d):
    sizes = (W_A, W_A, W_A, H_A, WK_B, WK_B, WV_B, WV_B, GLA_RANK, d, d)
    offs = [0]
    for s in sizes:
        offs.append(offs[-1] + s)
    seg = lambda i: w_in[:, offs[i]:offs[i + 1]]
    wbig = jnp.concatenate(
        [seg(0) * (DH_A ** -0.5), seg(1), seg(2), seg(4) * (DK_B ** -0.5), seg(5),
         seg(6), seg(7), seg(9), seg(10)], axis=1).astype(bf16)
    wsm = jnp.concatenate(
        [seg(3), seg(8), jnp.zeros((d, LANES - H_A - GLA_RANK), f32)], axis=1).astype(bf16)
    wfa = seg(3).T
    wkt = seg(1).T.astype(bf16)
    wvt = seg(2).T.astype(bf16)
    return wbig, wsm, wfa, wkt, wvt


def _inproj(x, sc, sh, g_mix, weights, time_minor):
    wbig, wsm, wfa, wkt, wvt = weights
    b, t, d = x.shape
    tm = min(TOKEN_TILE, t)
    assert t % tm == 0
    rows = sc.shape[1]
    mod_block = (1, 1, d) if rows == 1 else (1, tm, d)
    mod_map = (lambda bi, i: (bi, 0, 0)) if rows == 1 else (lambda bi, i: (bi, i, 0))
    tok3 = lambda w: pl.BlockSpec((1, tm, w), lambda bi, i: (bi, i, 0))
    head4 = pl.BlockSpec((1, H_A, tm, DH_A), lambda bi, i: (bi, 0, i, 0))
    const2 = lambda a: pl.BlockSpec(a.shape, lambda bi, i: (0, 0))
    sds = jax.ShapeDtypeStruct
    headt = pl.BlockSpec((1, H_A, DH_A, tm), lambda bi, i: (bi, 0, 0, i))
    if time_minor:
        kv_shape = [sds((b, H_A, t, DH_A), bf16), sds((b, H_A, DH_A, t), bf16),
                    sds((b, H_A, DH_A, t), f32), sds((b, H_A, DH_A, t), f32)]
        kv_specs = [head4, headt, headt, headt]
    else:
        kv_shape = [sds((b, t, W_A), f32), sds((b, t, W_A), f32)]
        kv_specs = [tok3(W_A), tok3(W_A)]
    out_shape = tuple(
        [sds((b, H_A, t, DH_A), bf16)] + kv_shape
        + [sds((b, t, WK_B), f32), sds((b, t, WK_B), f32),
           sds((b, t, WV_B), f32), sds((b, t, WV_B), f32),
           sds((b, t, d), f32), sds((b, t, d), f32),
           sds((b, t, LANES), f32), sds((b, H_A, t), f32)])
    out_specs = tuple(
        [head4] + kv_specs
        + [tok3(WK_B), tok3(WK_B), tok3(WV_B), tok3(WV_B), tok3(d), tok3(d), tok3(LANES),
           pl.BlockSpec((1, H_A, tm), lambda bi, i: (bi, 0, i))])
    g2 = g_mix.reshape(1, d)
    return pl.pallas_call(
        functools.partial(_inproj_kernel, d=d, time_minor=time_minor),
        grid=(b, t // tm),
        in_specs=[tok3(d), pl.BlockSpec(mod_block, mod_map), pl.BlockSpec(mod_block, mod_map),
                  const2(g2), const2(wbig), const2(wsm), const2(wfa), const2(wkt), const2(wvt)],
        out_specs=out_specs,
        out_shape=out_shape,
        compiler_params=_cparams(("parallel", "arbitrary")),
        name="inproj",
    )(x, sc, sh, g2, wbig, wsm, wfa, wkt, wvt)


def _logf_kernel(fa_ref, bf_ref, logf_ref, *cum_refs, t, cb):
    lf = _log_sigmoid(fa_ref[...] + bf_ref[...])
    logf_ref[...] = lf
    if not cum_refs:
        return
    cum_ref, = cum_refs
    r = lax.broadcasted_iota(i32, (cb, cb), 0)
    c = lax.broadcasted_iota(i32, (cb, cb), 1)
    triu = jnp.where(r <= c, 1.0, 0.0).astype(bf16)
    carry = jnp.zeros((lf.shape[0], 1), f32)
    for j in range(t // cb):
        hi, mid, lo = _split3(lf[:, j * cb:(j + 1) * cb])
        blk = _dot(hi, triu) + _dot(mid, triu) + _dot(lo, triu) + carry
        cum_ref[:, j * cb:(j + 1) * cb] = blk
        carry = blk[:, cb - 1:cb]


def _logf(fat, b_f, with_cumsum):
    b, _, t = fat.shape
    rows = b * H_A
    fa2 = fat.reshape(rows, t)
    bf2 = jnp.tile(b_f.reshape(H_A, 1), (b, 1))
    cb = min(256, t)
    full = lambda a: pl.BlockSpec(a.shape, lambda: (0,) * a.ndim)
    n_out = 2 if with_cumsum else 1
    outs = pl.pallas_call(
        functools.partial(_logf_kernel, t=t, cb=cb),
        in_specs=[full(fa2), full(bf2)],
        out_specs=tuple(pl.BlockSpec((rows, t), lambda: (0, 0)) for _ in range(n_out)),
        out_shape=tuple(jax.ShapeDtypeStruct((rows, t), f32) for _ in range(n_out)),
        name="log_forget",
    )(fa2, bf2)
    return tuple(o.reshape(b, H_A, t) for o in outs)


def _bias_lanes(col, ones_first):
    hi, mid, lo = [p.astype(f32) for p in _split3(col)]
    lane = lax.broadcasted_iota(i32, (col.shape[0], DH_A), 1)
    base = 3 if ones_first else 0
    parts = jnp.where(lane == base, hi, jnp.where(lane == base + 1, mid,
                      jnp.where(lane == base + 2, lo, 0.0)))
    ones = (lane < 3) if ones_first else ((lane >= 3) & (lane < 6))
    return jnp.where(ones, 1.0, parts)


def _fox_kernel(q_ref, k_ref, vt_ref, cum_ref, o_ref, kaug, qaug, *, tq, tk):
    i = pl.program_id(1)
    per_q = tq // tk

    @pl.when(i == 0)
    def _():
        cum_all = cum_ref[0]
        for h in range(H_A):
            kaug[h, :, :DH_A] = k_ref[0, h]
            kaug[h, :, DH_A:] = _bias_lanes(-cum_all[:, h:h + 1], False).astype(bf16)

    cum_q = cum_ref[0, pl.ds(pl.multiple_of(i * tq, tq), tq), :]
    for h in range(H_A):
        qaug[h, :, :DH_A] = q_ref[0, h]
        qaug[h, :, DH_A:] = _bias_lanes(cum_q[:, h:h + 1], True).astype(bf16)
    key = lax.broadcasted_iota(i32, (tk, tq), 0)
    qry = lax.broadcasted_iota(i32, (tk, tq), 1)

    def step(j, carry, key_off=None):
        start = pl.multiple_of(j * tk, tk)
        scores = [_dotg(kaug[h, pl.ds(start, tk), :], qaug[h], _NT) for h in range(H_A)]
        stats = []
        for h in range(H_A):
            m_old, l_old, _ = carry[h]
            s = scores[h]
            if key_off is not None:
                s = jnp.where(key + key_off <= qry, s, -jnp.inf)
            m_new = jnp.maximum(m_old, jnp.max(s, axis=0, keepdims=True))
            alpha = jnp.exp(m_old - m_new)
            p = jnp.exp(s - m_new)
            l_new = alpha * l_old + jnp.sum(p, axis=0, keepdims=True)
            stats.append((m_new, l_new, alpha, p.astype(bf16)))
        out = []
        for h in range(H_A):
            m_new, l_new, alpha, p = stats[h]
            acc_new = alpha * carry[h][2] + _dot(vt_ref[0, h, :, pl.ds(start, tk)], p)
            out.append((m_new, l_new, acc_new))
        return tuple(out)

    init = tuple((jnp.full((1, tq), -jnp.inf, f32), jnp.zeros((1, tq), f32),
                  jnp.zeros((DH_A, tq), f32)) for _ in range(H_A))
    carry = lax.fori_loop(0, i * per_q, step, init)
    for dblk in range(per_q):
        carry = step(i * per_q + dblk, carry, key_off=dblk * tk)
    for h in range(H_A):
        _, l, acc = carry[h]
        o_ref[0, h * DH_A:(h + 1) * DH_A, :] = (acc / l).astype(bf16)


def _fox_prompt(qat, kat, vt, cum):
    b, _, t, _ = qat.shape
    tq = min(ATTN_Q_TILE, t)
    tk = min(ATTN_K_TILE, tq)
    assert t % tq == 0 and tq % tk == 0
    return pl.pallas_call(
        functools.partial(_fox_kernel, tq=tq, tk=tk),
        grid=(b, t // tq),
        in_specs=[pl.BlockSpec((1, H_A, tq, DH_A), lambda bi, i: (bi, 0, i, 0)),
                  pl.BlockSpec((1, H_A, t, DH_A), lambda bi, i: (bi, 0, 0, 0)),
                  pl.BlockSpec((1, H_A, DH_A, t), lambda bi, i: (bi, 0, 0, 0)),
                  pl.BlockSpec((1, t, H_A), lambda bi, i: (bi, 0, 0))],
        out_specs=pl.BlockSpec((1, W_A, tq), lambda bi, i: (bi, 0, i)),
        out_shape=jax.ShapeDtypeStruct((b, W_A, t), bf16),
        scratch_shapes=[pltpu.VMEM((H_A, t, 2 * DH_A), bf16), pltpu.VMEM((H_A, tq, 2 * DH_A), bf16)],
        compiler_params=_cparams(("parallel", "arbitrary")),
        name="fox_prompt",
    )(qat, kat, vt, cum)


def _fox_decode_kernel(pt_ref, q_ref, knew_ref, vnew_ref, lfnew_ref, kt_hbm, vt_hbm, lft_hbm,
                       o_ref, pages, lfpages, p_scr, m_scr, self_scr, acc_scr, carry_scr,
                       sem_kv, sem_lf, *, pg, nstep, n_pages, db):
    b = pl.program_id(0)
    j = pl.program_id(1)
    q = q_ref[0]
    psz = lfpages.shape[-1]
    lane = lax.broadcasted_iota(i32, (H_A, W_A), 1)
    sub = lax.broadcasted_iota(i32, (H_A, W_A), 0)
    own = (lane // DH_A) == sub
    rnd = lambda z: z.astype(bf16).astype(f32)

    def start_load(seq, t, slot):
        def issue(src_hbm, dst, sem, group):
            for p in range(pg):
                page = pt_ref[seq * n_pages + group * pg + p]
                pltpu.make_async_copy(src_hbm.at[page], dst.at[slot, p], sem.at[slot]).start()

        @pl.when(t < nstep)
        def _():
            issue(kt_hbm, pages, sem_kv, nstep - 1 - t)
            issue(lft_hbm, lfpages, sem_lf, nstep - 1 - t)

        @pl.when(t >= nstep)
        def _():
            issue(vt_hbm, pages, sem_kv, 2 * nstep - 1 - t)

    step = b * (2 * nstep) + j
    slot = step % 2

    @pl.when(step == 0)
    def _():
        start_load(0, 0, 0)

    last = j == 2 * nstep - 1
    nxt_seq = jnp.where(last, b + 1, b)
    nxt_t = jnp.where(last, 0, j + 1)

    @pl.when(nxt_seq < db)
    def _():
        start_load(nxt_seq, nxt_t, 1 - slot)

    pltpu.make_async_copy(kt_hbm.at[pl.ds(0, pg)], pages.at[slot], sem_kv.at[slot]).wait()

    @pl.when(j < nstep)
    def _():
        pltpu.make_async_copy(lft_hbm.at[pl.ds(0, pg)], lfpages.at[slot], sem_lf.at[slot]).wait()

    @pl.when(j == 0)
    def _():
        s_self = jnp.sum(q.astype(f32) * rnd(knew_ref[0]), axis=1, keepdims=True)
        self_scr[...] = s_self
        m_scr[...] = s_self
        carry_scr[...] = lfnew_ref[0]

    @pl.when(j < nstep)
    def _():
        r = lax.broadcasted_iota(i32, (psz, 2 * psz), 0)
        c = lax.broadcasted_iota(i32, (psz, 2 * psz), 1)
        later = jnp.where(c < psz, jnp.where(r > c, 1.0, 0.0), 1.0).astype(bf16)
        group = nstep - 1 - j
        carry = carry_scr[...]
        m = m_scr[...]
        hi, mid, lo = _split3(lfpages[slot].reshape(pg * H_A, psz))
        suf_all = _dot(hi, later) + _dot(mid, later) + _dot(lo, later)
        for p in reversed(range(pg)):
            suf = suf_all[p * H_A:(p + 1) * H_A]
            kt = pages[slot, p].reshape(W_A, psz).astype(bf16)
            s = _dot(q, kt) + (carry + suf[:, :psz])
            carry = carry + suf[:, psz:psz + 1]
            p_scr[group * pg + p] = s
            m = jnp.maximum(m, jnp.max(s, axis=1, keepdims=True))
        carry_scr[...] = carry
        m_scr[...] = m

    @pl.when(j == nstep)
    def _():
        m = m_scr[...]
        e_self = jnp.exp(self_scr[...] - m)
        e = jnp.exp(p_scr[...] - m[None])
        l = e_self + jnp.sum(jnp.sum(e, axis=0), axis=1, keepdims=True)
        p_scr[...] = e / l[None]
        acc_scr[...] = jnp.where(own, rnd(e_self / l) * rnd(vnew_ref[0]), 0.0)

    @pl.when(j >= nstep)
    def _():
        group = 2 * nstep - 1 - j
        acc = acc_scr[...]
        for p in range(pg):
            vt = pages[slot, p].reshape(W_A, psz).astype(bf16)
            acc = acc + _dotg(p_scr[group * pg + p].astype(bf16), vt, _NT)
        acc_scr[...] = acc

    @pl.when(j == 2 * nstep - 1)
    def _():
        o_ref[0] = jnp.sum(jnp.where(own, acc_scr[...], 0.0), axis=0, keepdims=True)


def _fox_decode(qbd, knew, vnew, lfnew, cache_kt, cache_vt, cache_lft, page_table):
    db = qbd.shape[0]
    psz = cache_kt.shape[-1]
    n_pages = page_table.shape[1]
    pg = PAGES_PER_STEP if n_pages % PAGES_PER_STEP == 0 else 1
    nstep = n_pages // pg
    pt = page_table.reshape(-1).astype(i32)
    row3 = lambda a: pl.BlockSpec((1,) + a.shape[1:], lambda bi, j, pt_ref: (bi, 0, 0))
    anyspace = pl.BlockSpec(memory_space=pl.ANY)
    grid_spec = pltpu.PrefetchScalarGridSpec(
        num_scalar_prefetch=1,
        grid=(db, 2 * nstep),
        in_specs=[row3(qbd), row3(knew), row3(vnew), row3(lfnew), anyspace, anyspace, anyspace],
        out_specs=pl.BlockSpec((1, 1, W_A), lambda bi, j, pt_ref: (bi, 0, 0)),
        scratch_shapes=[pltpu.VMEM((2, pg, H_A, DH_A, psz), f32), pltpu.VMEM((2, pg, H_A, psz), f32),
                        pltpu.VMEM((n_pages, H_A, psz), f32), pltpu.VMEM((H_A, 1), f32),
                        pltpu.VMEM((H_A, 1), f32), pltpu.VMEM((H_A, W_A), f32),
                        pltpu.VMEM((H_A, 1), f32),
                        pltpu.SemaphoreType.DMA((2,)), pltpu.SemaphoreType.DMA((2,))])
    return pl.pallas_call(
        functools.partial(_fox_decode_kernel, pg=pg, nstep=nstep, n_pages=n_pages, db=db),
        grid_spec=grid_spec,
        out_shape=jax.ShapeDtypeStruct((db, 1, W_A), f32),
        compiler_params=_cparams(("arbitrary", "arbitrary")),
        name="fox_decode",
    )(pt, qbd, knew, vnew, lfnew, cache_kt, cache_vt, cache_lft)


def _gla_kernel(qb_ref, kb_ref, vb_ref, rb_ref, sm_ref, wal_ref, bal_ref, gg_ref,
                ob_ref, sout_ref, s_scr, *, tt):
    i = pl.program_id(1)

    @pl.when(i == 0)
    def _():
        s_scr[...] = jnp.zeros_like(s_scr)

    q = qb_ref[0]
    k = kb_ref[0]
    z = _dot(sm_ref[0].astype(bf16), wal_ref[...]) + bal_ref[...]
    a = _log_sigmoid(z) * (1.0 / GLA_TAU)

    row = lax.broadcasted_iota(i32, (tt, tt), 0)
    col = lax.broadcasted_iota(i32, (tt, tt), 1)
    tril = jnp.where(col <= row, 1.0, 0.0).astype(bf16)
    hi, mid, lo = _split3(a)
    b = _dot(tril, hi) + _dot(tril, mid) + _dot(tril, lo)

    rowk = lax.broadcasted_iota(i32, (tt, WK_B), 0)
    hs = [slice(h * DK_B, (h + 1) * DK_B) for h in range(H_B)]
    amat = [jnp.zeros((tt, tt), f32) for _ in range(H_B)]
    edge = b
    for lvl in range(tt.bit_length() - 1):
        half = 1 << lvl
        upper = ((rowk >> lvl) & 1) == 1
        kt = jnp.where(upper, 0.0, k * jnp.exp(edge - b)).astype(bf16)
        edge_q = pltpu.roll(edge, half, 0)
        qt = jnp.where(upper, q * jnp.exp(b - edge_q), 0.0).astype(bf16)
        same = (row >> (lvl + 1)) == (col >> (lvl + 1))
        for h in range(H_B):
            amat[h] = amat[h] + jnp.where(same, _dotg(qt[:, hs[h]], kt[:, hs[h]], _NT), 0.0)
        edge = jnp.where(upper, edge, pltpu.roll(edge, tt - half, 0))
    q16 = q.astype(bf16)
    k16 = k.astype(bf16)
    for h in range(H_B):
        amat[h] = amat[h] + jnp.where(row == col, _dotg(q16[:, hs[h]], k16[:, hs[h]], _NT), 0.0)

    qe = (q * jnp.exp(b)).astype(bf16)
    ke = (k * jnp.exp(edge - b)).astype(bf16)
    e_last = jnp.exp(edge[0:1, :])
    v16 = vb_ref[0].astype(bf16)
    r = rb_ref[0]
    er = lax.broadcasted_iota(i32, (DK_B, DK_B), 0)
    ec = lax.broadcasted_iota(i32, (DK_B, DK_B), 1)
    for h in range(H_B):
        vs = slice(h * DV_B, (h + 1) * DV_B)
        vh = v16[:, vs]
        state = s_scr[h]
        o = _dot(amat[h].astype(bf16), vh) + _dot(qe[:, hs[h]], state.astype(bf16))
        decay_col = jnp.sum(
            jnp.where(er == ec, jnp.broadcast_to(e_last[:, hs[h]], (DK_B, DK_B)), 0.0),
            axis=1, keepdims=True)
        s_scr[h] = decay_col * state + _dotg(ke[:, hs[h]], vh, _TN)
        rh = r[:, vs]
        ob_ref[0, :, vs] = (_rms(o) * gg_ref[:, vs] * (rh * _sigmoid(rh))).astype(bf16)

    @pl.when(i == pl.num_programs(1) - 1)
    def _():
        sout_ref[0] = s_scr[...]


def _gla_prompt(qb, kb, vb, rb, sm, wal, bal, gg):
    b, t, _ = qb.shape
    tt = min(GLA_TILE, t)
    assert t % tt == 0 and tt & (tt - 1) == 0
    tok3 = lambda w: pl.BlockSpec((1, tt, w), lambda bi, i: (bi, i, 0))
    const2 = lambda a: pl.BlockSpec(a.shape, lambda bi, i: (0, 0))
    return pl.pallas_call(
        functools.partial(_gla_kernel, tt=tt),
        grid=(b, t // tt),
        in_specs=[tok3(WK_B), tok3(WK_B), tok3(WV_B), tok3(WV_B), tok3(LANES),
                  const2(wal), const2(bal), const2(gg)],
        out_specs=(tok3(WV_B),
                   pl.BlockSpec((1, H_B, DK_B, DV_B), lambda bi, i: (bi, 0, 0, 0))),
        out_shape=(jax.ShapeDtypeStruct((b, t, WV_B), bf16),
                   jax.ShapeDtypeStruct((b, H_B, DK_B, DV_B), f32)),
        scratch_shapes=[pltpu.VMEM((H_B, DK_B, DV_B), f32)],
        compiler_params=_cparams(("parallel", "arbitrary")),
        name="gla_prompt",
    )(qb, kb, vb, rb, sm, wal, bal, gg)


def _loga_kernel(sm_ref, wal_ref, bal_ref, o_ref):
    z = _dot(sm_ref[...].astype(bf16), wal_ref[...]) + bal_ref[...]
    o_ref[...] = _log_sigmoid(z) * (1.0 / GLA_TAU)


def _gla_decode_kernel(q_ref, k_ref, a_ref, v_ref, r_ref, s_ref, gg_ref, o_ref, so_ref):
    for h in range(H_B):
        q, k, a = q_ref[0, h], k_ref[0, h], a_ref[0, h]
        v = v_ref[0, h]
        state = s_ref[0, h]
        ea = jnp.exp(a)
        rnd = lambda z: z.astype(bf16).astype(f32)
        qk = jnp.sum(q * k, axis=0, keepdims=True)
        o = qk * v + jnp.sum(rnd(q * ea) * rnd(state), axis=0, keepdims=True)
        so_ref[0, h] = ea * state + k * v
        r = r_ref[0, h]
        o_ref[0, h] = _rms(o) * gg_ref[h] * (r * _sigmoid(r))


def _gla_decode(qb, kb, vb, rb, sm, wal, bal, gg, state):
    db = qb.shape[0]
    full = lambda a: pl.BlockSpec(a.shape, lambda: (0,) * a.ndim)
    loga = pl.pallas_call(
        _loga_kernel,
        in_specs=[full(sm), full(wal), full(bal)],
        out_specs=pl.BlockSpec((db, WK_B), lambda: (0, 0)),
        out_shape=jax.ShapeDtypeStruct((db, WK_B), f32),
        name="gla_log_decay",
    )(sm, wal, bal)
    col = lambda a: a.reshape(db, H_B, DK_B, 1)
    rowv = lambda a: a.reshape(db, H_B, 1, DV_B)
    gg4 = gg.reshape(H_B, 1, DV_B)
    b4 = lambda shp: pl.BlockSpec((1,) + shp, lambda bi: (bi, 0, 0, 0))
    o, s_new = pl.pallas_call(
        _gla_decode_kernel,
        grid=(db,),
        in_specs=[b4((H_B, DK_B, 1))] * 3 + [b4((H_B, 1, DV_B))] * 2
        + [b4((H_B, DK_B, DV_B)), pl.BlockSpec(gg4.shape, lambda bi: (0, 0, 0))],
        out_specs=(b4((H_B, 1, DV_B)), b4((H_B, DK_B, DV_B))),
        out_shape=(jax.ShapeDtypeStruct((db, H_B, 1, DV_B), f32),
                   jax.ShapeDtypeStruct((db, H_B, DK_B, DV_B), f32)),
        compiler_params=_cparams(("parallel",)),
        name="gla_decode",
    )(col(qb), col(kb), col(loga), rowv(vb), rowv(rb), state, gg4)
    return o.reshape(db, WV_B), s_new


def _mix_kernel(x_ref, oa_ref, ob_ref, ga_ref, gb_ref, gt1_ref, sc2_ref, sh2_ref, gf_ref,
                wba_ref, wbb_ref, wo_ref, wr_ref, br_ref, x1_ref, h2_ref, lg_ref, *, nsub):
    x = x_ref[0]
    tm = x.shape[0]
    ya = _dotg(oa_ref[0].astype(bf16), wba_ref[...], _TN)
    yb = _dot(ob_ref[0].astype(bf16), wbb_ref[...])
    m = _sigmoid(ga_ref[0]) * ya + _sigmoid(gb_ref[0]) * yb
    x1 = x + gt1_ref[0] * _dot(m.astype(bf16), wo_ref[...])
    x1_ref[...] = x1
    h2 = (_rms(x1) * gf_ref[...]) * sc2_ref[0] + sh2_ref[0]
    hb = h2.astype(bf16)
    lg_ref[...] = _dot(hb, wr_ref[...]) + br_ref[...]
    half = nsub * LANES
    hi = lax.bitcast_convert_type(hb[:, :half].astype(f32), jnp.uint32)
    lo = lax.bitcast_convert_type(hb[:, half:].astype(f32), jnp.uint32)
    packed = jnp.bitwise_or(hi, jnp.right_shift(lo, jnp.uint32(16)))
    for s in range(nsub):
        h2_ref[pl.ds(s, tm, stride=nsub), :] = packed[:, s * LANES:(s + 1) * LANES]


def _mix(x, oa, ob, ga, gb, gt1, sc2, sh2, g_ffn, wba, wbb, wo, wr, br):
    b, t, d = x.shape
    nt = b * t
    nsub = d // (2 * LANES)
    tm = min(TOKEN_TILE, t)
    assert t % tm == 0 and d % (2 * LANES) == 0
    rows = gt1.shape[1]
    mod_block = (1, 1, d) if rows == 1 else (1, tm, d)
    mod_map = (lambda bi, i: (bi, 0, 0)) if rows == 1 else (lambda bi, i: (bi, i, 0))
    mod = pl.BlockSpec(mod_block, mod_map)
    tok3 = lambda w: pl.BlockSpec((1, tm, w), lambda bi, i: (bi, i, 0))
    const2 = lambda a: pl.BlockSpec(a.shape, lambda bi, i: (0, 0))
    nti = t // tm
    blk = lambda bi, i: (bi * nti + i, 0)
    g2 = g_ffn.reshape(1, d)
    ins = [x, oa, ob, ga, gb, gt1, sc2, sh2, g2, wba, wbb, wo, wr, br]
    in_specs = [tok3(d), pl.BlockSpec((1, W_A, tm), lambda bi, i: (bi, 0, i)), tok3(WV_B),
                tok3(d), tok3(d), mod, mod, mod,
                const2(g2), const2(wba), const2(wbb), const2(wo), const2(wr), const2(br)]
    return pl.pallas_call(
        functools.partial(_mix_kernel, nsub=nsub),
        grid=(b, nti),
        in_specs=in_specs,
        out_specs=(pl.BlockSpec((tm, d), blk), pl.BlockSpec((tm * nsub, LANES), blk),
                   pl.BlockSpec((tm, LANES), blk)),
        out_shape=(jax.ShapeDtypeStruct((nt, d), f32),
                   jax.ShapeDtypeStruct((nt * nsub, LANES), jnp.uint32),
                   jax.ShapeDtypeStruct((nt, LANES), f32)),
        compiler_params=_cparams(("parallel", "arbitrary")),
        name="branch_mix",
    )(*ins)


def _route_kernel(lg_ref, cin_ref, rec_ref, gate_ref, cnt_ref, carry_scr, *, tr):
    @pl.when(pl.program_id(0) == 0)
    def _():
        carry_scr[...] = cin_ref[...]

    lane = lax.broadcasted_iota(i32, (tr, LANES), 1)
    lanef = lane.astype(f32)
    v = jnp.where(lane < N_EXPERTS, lg_ref[...], -jnp.inf)
    onehot = jnp.zeros((tr, LANES), f32)
    ids, vals = [], []
    for _ in range(TOP_K):
        mx = jnp.max(v, axis=1, keepdims=True)
        idx = jnp.min(jnp.where(v == mx, lanef, float(LANES)), axis=1, keepdims=True)
        sel = lanef == idx
        onehot = jnp.where(sel, 1.0, onehot)
        v = jnp.where(sel, -jnp.inf, v)
        ids.append(idx)
        vals.append(mx)
    es = [jnp.exp(vk - vals[0]) for vk in vals]
    tot = es[0]
    for e in es[1:]:
        tot = tot + e

    r2 = lax.broadcasted_iota(i32, (tr, tr), 0)
    c2 = lax.broadcasted_iota(i32, (tr, tr), 1)
    before = jnp.where(c2 < r2, 1.0, 0.0).astype(bf16)
    pref = _dot(before, onehot.astype(bf16)) + carry_scr[...]
    carry_scr[...] = carry_scr[...] + jnp.sum(onehot, axis=0, keepdims=True)

    rec = jnp.zeros((tr, LANES), f32)
    gate = jnp.zeros((tr, LANES), f32)
    for k in range(TOP_K):
        rank = jnp.sum(jnp.where(lanef == ids[k], pref, 0.0), axis=1, keepdims=True)
        rec = jnp.where(lane == k, rank * float(N_EXPERTS) + ids[k], rec)
        gate = jnp.where(lane == k, es[k] / tot, gate)
    rec_ref[...] = rec.astype(i32)
    gate_ref[...] = gate
    cnt_ref[...] = carry_scr[...]


def _route(logits, counts_in):
    nt = logits.shape[0]
    tr = _divisor_tile(nt, 512, SUBLANES)
    tile = pl.BlockSpec((tr, LANES), lambda i: (i, 0))
    cnt = pl.BlockSpec((1, LANES), lambda i: (0, 0))
    return pl.pallas_call(
        functools.partial(_route_kernel, tr=tr),
        grid=(nt // tr,),
        in_specs=[tile, cnt],
        out_specs=(tile, tile, cnt),
        out_shape=(jax.ShapeDtypeStruct((nt, LANES), i32),
                   jax.ShapeDtypeStruct((nt, LANES), f32),
                   jax.ShapeDtypeStruct((1, LANES), f32)),
        scratch_shapes=[pltpu.VMEM((1, LANES), f32)],
        compiler_params=_cparams(("arbitrary",)),
        name="route_topk",
    )(logits, counts_in)


def _dest_kernel(rec_ref, start_ref, out_ref):
    rec = rec_ref[...]
    lane = lax.broadcasted_iota(i32, rec.shape, 1)
    eid = jnp.bitwise_and(rec, N_EXPERTS - 1).astype(f32)
    rank = jnp.right_shift(rec, N_EXPERTS.bit_length() - 1)
    lanef = lane.astype(f32)
    start = jnp.zeros(rec.shape, f32)
    for k in range(TOP_K):
        mine = lanef == eid[:, k:k + 1]
        st = jnp.sum(jnp.where(mine, start_ref[...], 0.0), axis=1, keepdims=True)
        start = jnp.where(lane == k, st, start)
    out_ref[...] = start.astype(i32) + rank


def _dest_rows(rec, pad_start):
    nt = rec.shape[0]
    tr = _divisor_tile(nt, 512, SUBLANES)
    tile = pl.BlockSpec((tr, LANES), lambda i: (i, 0))
    start = jnp.zeros((1, LANES), f32).at[0, :N_EXPERTS].set(pad_start.astype(f32))
    out = pl.pallas_call(
        _dest_kernel,
        grid=(nt // tr,),
        in_specs=[tile, pl.BlockSpec((1, LANES), lambda i: (0, 0))],
        out_specs=tile,
        out_shape=jax.ShapeDtypeStruct((nt, LANES), i32),
        compiler_params=_cparams(("parallel",)),
        name="moe_dest_rows",
    )(rec, start)
    return out[:, :TOP_K].reshape(-1)


def _dispatch_kernel(dest_ref, h_ref, xin_hbm, xout_hbm, sem, *, nsub, ch):
    del xin_hbm
    base = pl.program_id(0) * ch

    def row_copy(n, k):
        src = pl.multiple_of(n * nsub, nsub)
        dst = pl.multiple_of(dest_ref[(base + n) * TOP_K + k] * nsub, nsub)
        return pltpu.make_async_copy(h_ref.at[pl.ds(src, nsub)], xout_hbm.at[pl.ds(dst, nsub)],
                                     sem)

    def body(n, carry):
        for k in range(TOP_K):
            row_copy(n, k).start()
        return carry

    lax.fori_loop(0, ch, body, 0, unroll=4)
    total = ch * TOP_K * nsub
    pltpu.make_async_copy(xout_hbm.at[pl.ds(0, total)], xout_hbm.at[pl.ds(0, total)], sem).wait()


def _dispatch(dest_flat, h2rows, xrows, nsub):
    n_tok = dest_flat.shape[0] // TOP_K
    ch = min(DISPATCH_CHUNK, n_tok)
    assert n_tok % ch == 0
    grid_spec = pltpu.PrefetchScalarGridSpec(
        num_scalar_prefetch=1,
        grid=(n_tok // ch,),
        in_specs=[pl.BlockSpec((ch * nsub, LANES), lambda i, dr: (i, 0)),
                  pl.BlockSpec(memory_space=pl.ANY)],
        out_specs=pl.BlockSpec(memory_space=pl.ANY),
        scratch_shapes=[pltpu.SemaphoreType.DMA(())])
    return pl.pallas_call(
        functools.partial(_dispatch_kernel, nsub=nsub, ch=ch),
        grid_spec=grid_spec,
        out_shape=jax.ShapeDtypeStruct(xrows.shape, xrows.dtype),
        input_output_aliases={2: 0},
        compiler_params=_cparams(("arbitrary",)),
        name="moe_dispatch",
    )(dest_flat, h2rows, xrows)


def _expert_kernel(be_ref, nused_ref, x_ref, wgu_ref, bgu_ref, wd_ref, bd_ref, y_ref,
                   wgu_b, wd_b, *, tmx, nsub_x, nsub, dff, chunk):
    r = pl.program_id(0)
    live = r < nused_ref[0]

    @pl.when(jnp.logical_not(live))
    def _():
        y_ref[...] = jnp.zeros_like(y_ref)

    new_expert = jnp.logical_or(r == 0, be_ref[r] != be_ref[jnp.maximum(r - 1, 0)])

    @pl.when(jnp.logical_and(live, new_expert))
    def _():
        for c in range(0, wgu_b.shape[0], chunk):
            wgu_b[c:c + chunk, :] = wgu_ref[0, c:c + chunk, :].astype(bf16)
        for c in range(0, wd_b.shape[0], chunk):
            wd_b[c:c + chunk, :] = wd_ref[0, c:c + chunk, :].astype(bf16)

    @pl.when(live)
    def _():
        words = [x_ref[pl.ds(s, tmx, stride=nsub_x), :] for s in range(nsub_x)]
        as_bf16 = lambda w: lax.bitcast_convert_type(w, f32).astype(bf16)
        x = jnp.concatenate(
            [as_bf16(jnp.bitwise_and(w, jnp.uint32(0xFFFF0000))) for w in words]
            + [as_bf16(jnp.left_shift(w, jnp.uint32(16))) for w in words], axis=1)
        gu = _dot(x, wgu_b[...]) + bgu_ref[0]
        gate = jnp.minimum(gu[:, :dff], SWIGLU_LIMIT)
        up = jnp.clip(gu[:, dff:], -SWIGLU_LIMIT, SWIGLU_LIMIT)
        glu = gate * _sigmoid(SWIGLU_ALPHA * gate)
        y = _dot(((up + 1.0) * glu).astype(bf16), wd_b[...]) + bd_ref[0]
        for s in range(nsub):
            y_ref[pl.ds(s, tmx, stride=nsub), :] = y[:, s * LANES:(s + 1) * LANES]


def _experts(block_e, nused, xrows, wgu, bgu, wd, bd):
    n_blocks = block_e.shape[0]
    tmx = EXPERT_TILE
    e, d, dff2 = wgu.shape
    dff = dff2 // 2
    nsub = d // LANES
    nsub_x = d // (2 * LANES)
    live = lambda r, nu: jnp.minimum(r, nu[0] - 1)
    rows = pl.BlockSpec((tmx * nsub_x, LANES), lambda r, be, nu: (live(r, nu), 0))
    per_e = lambda shp: pl.BlockSpec((1,) + shp, lambda r, be, nu: (be[live(r, nu)], 0, 0))
    grid_spec = pltpu.PrefetchScalarGridSpec(
        num_scalar_prefetch=2,
        grid=(n_blocks,),
        in_specs=[rows, per_e((d, dff2)), per_e((1, dff2)), per_e((dff, d)), per_e((1, d))],
        out_specs=pl.BlockSpec((tmx * nsub, LANES), lambda r, be, nu: (r, 0)),
        scratch_shapes=[pltpu.VMEM((d, dff2), bf16), pltpu.VMEM((dff, d), bf16)])
    return pl.pallas_call(
        functools.partial(_expert_kernel, tmx=tmx, nsub_x=nsub_x, nsub=nsub, dff=dff,
                          chunk=min(256, d, dff)),
        grid_spec=grid_spec,
        out_shape=jax.ShapeDtypeStruct((n_blocks * tmx * nsub, LANES), f32),
        compiler_params=_cparams(("arbitrary",)),
        name="moe_experts",
    )(block_e, nused, xrows, wgu, bgu.reshape(e, 1, dff2), wd, bd.reshape(e, 1, d))


def _combine_kernel(dest_ref, y_hbm, gate_ref, x1_ref, gt2_ref, gfin_ref, out_ref,
                    buf, sems, *, tc, nsub):
    i = pl.program_id(0)
    slot = i % 2

    def start_tile(tile, s):
        def body(n, carry):
            for k in range(TOP_K):
                src = pl.multiple_of(dest_ref[(tile * tc + n) * TOP_K + k] * nsub, nsub)
                dst = pl.multiple_of((k * tc + n) * nsub, nsub)
                pltpu.make_async_copy(y_hbm.at[pl.ds(src, nsub)], buf.at[s, pl.ds(dst, nsub)],
                                      sems.at[s]).start()
            return carry
        lax.fori_loop(0, tc, body, 0, unroll=4)

    @pl.when(i == 0)
    def _():
        start_tile(0, 0)

    @pl.when(i + 1 < pl.num_programs(0))
    def _():
        start_tile(i + 1, 1 - slot)

    pltpu.make_async_copy(y_hbm.at[pl.ds(0, buf.shape[1])], buf.at[slot], sems.at[slot]).wait()

    g = gate_ref[...]
    cols = []
    for s in range(nsub):
        acc = None
        for k in range(TOP_K):
            term = g[:, k:k + 1] * buf[slot, pl.ds(k * tc * nsub + s, tc, stride=nsub), :]
            acc = term if acc is None else acc + term
        cols.append(acc)
    y = jnp.concatenate(cols, axis=1)
    out_ref[...] = _rms(x1_ref[...] + gt2_ref[0] * y) * gfin_ref[...]


def _combine(dest_flat, yrows, gates, x1, gt2, g_final, t_per_mod, nsub):
    n_tok, d = x1.shape
    tc = min(TOKEN_TILE, n_tok)
    assert n_tok % tc == 0
    rows = gt2.shape[1]
    if rows == 1:
        per_mod = t_per_mod // tc
        mod = pl.BlockSpec((1, 1, d), lambda i, dr: (i // per_mod, 0, 0))
    else:
        mod = pl.BlockSpec((1, tc, d), lambda i, dr: (0, i, 0))
    blk = lambda i, dr: (i, 0)
    gfin = g_final.reshape(1, d)
    grid_spec = pltpu.PrefetchScalarGridSpec(
        num_scalar_prefetch=1,
        grid=(n_tok // tc,),
        in_specs=[pl.BlockSpec(memory_space=pl.ANY), pl.BlockSpec((tc, LANES), blk),
                  pl.BlockSpec((tc, d), blk), mod,
                  pl.BlockSpec((1, d), lambda i, dr: (0, 0))],
        out_specs=pl.BlockSpec((tc, d), blk),
        scratch_shapes=[pltpu.VMEM((2, TOP_K * tc * nsub, LANES), f32),
                        pltpu.SemaphoreType.DMA((2,))])
    return pl.pallas_call(
        functools.partial(_combine_kernel, tc=tc, nsub=nsub),
        grid_spec=grid_spec,
        out_shape=jax.ShapeDtypeStruct((n_tok, d), f32),
        compiler_params=_cparams(("arbitrary",)),
        name="moe_combine",
    )(dest_flat, yrows, gates, x1, gt2, gfin)


def kernel(x_prompt, x_sample, c_prompt, c_sample, cache_k, cache_v, cache_logf, state_gla,
           page_table, g_mix, g_ffn, g_final, w_ada, b_ada, w_in, b_f, w_alpha, b_alpha, g_gla,
           w_branch, w_o, w_router, b_router, w_gu, b_gu, w_d, b_d):
    depth = g_mix.shape[0]
    assert depth == 1, "one decoder layer"
    bp, t, d = x_prompt.shape
    db = x_sample.shape[0]
    assert x_sample.shape[1] == 1
    nsub = d // LANES
    ntp = bp * t
    nt = ntp + db

    nmod = bp + db
    pad = (-nmod) % SUBLANES
    c_all = jnp.concatenate([c_prompt, c_sample, jnp.zeros((pad, d), f32)], axis=0)
    mod = _adaln(c_all, w_ada[0], b_ada[0])
    sh1, sc1, gt1, sh2, sc2, gt2 = [mod[:, i * d:(i + 1) * d] for i in range(6)]
    grp_p = lambda a: a[:bp].reshape(bp, 1, d)
    grp_s = lambda a: a[bp:nmod].reshape(1, db, d)

    w_proj = _inproj_weights(w_in[0], d)
    wal = jnp.zeros((LANES, WK_B), f32).at[H_A:H_A + GLA_RANK].set(w_alpha[0]).astype(bf16)
    bal = b_alpha[0].reshape(1, WK_B)
    gg = g_gla[0].reshape(1, WV_B)
    wba = w_branch[0, :W_A].astype(bf16)
    wbb = w_branch[0, W_A:].astype(bf16)
    wo = w_o[0].astype(bf16)
    wr = jnp.zeros((d, LANES), f32).at[:, :N_EXPERTS].set(w_router[0]).astype(bf16)
    br = jnp.zeros((1, LANES), f32).at[0, :N_EXPERTS].set(b_router[0])

    (qat, kat, vt, ka, va, qb, kb, vb, rb, ga, gb, sm, fat) = _inproj(
        x_prompt, grp_p(1.0 + sc1), grp_p(sh1), g_mix[0], w_proj, True)
    logft, cumt = _logf(fat, b_f[0], True)
    oa = _fox_prompt(qat, kat, vt, jnp.transpose(cumt, (0, 2, 1)))
    ob, gla_p = _gla_prompt(qb, kb, vb, rb, sm, wal, bal, gg)
    x1_p, h2_p, lg_p = _mix(x_prompt, oa, ob, ga, gb, grp_p(gt1), grp_p(1.0 + sc2), grp_p(sh2),
                            g_ffn[0], wba, wbb, wo, wr, br)

    xs = x_sample.reshape(1, db, d)
    (qat_s, ka_s, va_s, qb_s, kb_s, vb_s, rb_s, ga_s, gb_s, sm_s, fat_s) = _inproj(
        xs, grp_s(1.0 + sc1), grp_s(sh1), g_mix[0], w_proj, False)
    logft_s, = _logf(fat_s, b_f[0], False)
    lfnew = jnp.transpose(logft_s[0], (1, 0)).reshape(db, H_A, 1)
    qbd = jnp.einsum('hnd,hc->nchd', qat_s[0], jnp.eye(H_A, dtype=bf16)).reshape(db, H_A, W_A)
    oa_s = _fox_decode(qbd, ka_s.reshape(db, 1, W_A), va_s.reshape(db, 1, W_A), lfnew,
                       jnp.transpose(cache_k[0], (0, 2, 3, 1)),
                       jnp.transpose(cache_v[0], (0, 2, 3, 1)),
                       jnp.transpose(cache_logf[0], (0, 2, 1)), page_table)
    ob_s, gla_s = _gla_decode(qb_s[0], kb_s[0], vb_s[0], rb_s[0], sm_s[0], wal, bal, gg,
                              state_gla[0])
    x1_s, h2_s, lg_s = _mix(xs, oa_s.reshape(db, W_A).T[None], ob_s.reshape(1, db, WV_B), ga_s, gb_s,
                            grp_s(gt1), grp_s(1.0 + sc2), grp_s(sh2), g_ffn[0],
                            wba, wbb, wo, wr, br)

    rec_p, gates_p, cnt_p = _route(lg_p, jnp.zeros((1, LANES), f32))
    rec_s, gates_s, cnt = _route(lg_s, cnt_p)
    counts = cnt[0, :N_EXPERTS].astype(i32)
    padded = (counts + EXPERT_TILE - 1) // EXPERT_TILE * EXPERT_TILE
    pad_end = jnp.cumsum(padded).astype(i32)
    pad_start = pad_end - padded

    dest_p = _dest_rows(rec_p, pad_start)
    dest_s = _dest_rows(rec_s, pad_start)
    n_blocks = -(-(nt * TOP_K + N_EXPERTS * (EXPERT_TILE - 1)) // EXPERT_TILE)
    block_start = jnp.arange(n_blocks, dtype=i32) * EXPERT_TILE
    block_e = jnp.minimum(jnp.sum(pad_end[None, :] <= block_start[:, None], axis=1),
                          N_EXPERTS - 1).astype(i32)
    nused = (pad_end[-1:] // EXPERT_TILE).astype(i32)

    nsub_x = d // (2 * LANES)
    xrows = jnp.zeros((n_blocks * EXPERT_TILE * nsub_x, LANES), jnp.uint32)
    xrows = _dispatch(dest_p, h2_p, xrows, nsub_x)
    xrows = _dispatch(dest_s, h2_s, xrows, nsub_x)
    yrows = _experts(block_e, nused, xrows, w_gu[0], b_gu[0], w_d[0], b_d[0])
    y_p = _combine(dest_p, yrows, gates_p, x1_p, grp_p(gt2), g_final, t, nsub)
    y_s = _combine(dest_s, yrows, gates_s, x1_s, grp_s(gt2), g_final, 1, nsub)

    logf_p = jnp.transpose(logft, (0, 2, 1))
    return (y_p.reshape(bp, t, d), y_s.reshape(db, 1, d),
            jnp.transpose(ka, (0, 3, 1, 2))[None], jnp.transpose(va, (0, 3, 1, 2))[None],
            logf_p.reshape(1, bp, t, H_A), gla_p.reshape(1, bp, H_B, DK_B, DV_B),
            ka_s.reshape(1, db, 1, H_A, DH_A), va_s.reshape(1, db, 1, H_A, DH_A),
            lfnew.reshape(1, db, 1, H_A), gla_s.reshape(1, db, H_B, DK_B, DV_B))
```

```python
import functools

import jax
import jax.numpy as jnp
from jax import lax
from jax.experimental import pallas as pl
from jax.experimental.pallas import tpu as pltpu

f32 = jnp.float32
bf16 = jnp.bfloat16
i32 = jnp.int32

H_A, DH_A = 8, 64
H_B, DK_B, DV_B = 4, 64, 128
GLA_RANK = 16
GLA_TAU = 16.0
N_EXPERTS = 32
TOP_K = 4
SWIGLU_LIMIT = 7.0
SWIGLU_ALPHA = 1.702
RMS_EPS = 1e-6
W_A = H_A * DH_A
WK_B = H_B * DK_B
WV_B = H_B * DV_B

LANES = 128
SUBLANES = 8
VMEM_LIMIT_MB = 56

TOKEN_TILE = 256
ATTN_Q_TILE = 512
ATTN_K_TILE = 256
GLA_TILE = 256
EXPERT_TILE = 512
PAGES_PER_STEP = 32
DISPATCH_CHUNK = 256

_NT = (((1,), (1,)), ((), ()))
_TN = (((0,), (0,)), ((), ()))


def _cparams(semantics, vmem_mb=VMEM_LIMIT_MB):
    return pltpu.CompilerParams(dimension_semantics=semantics,
                                vmem_limit_bytes=vmem_mb << 20)


def _sigmoid(x):
    return 1.0 / (1.0 + jnp.exp(-x))


def _log_sigmoid(x):
    return jnp.minimum(x, 0.0) - jnp.log1p(jnp.exp(-jnp.abs(x)))


def _split3(x):
    hi = x.astype(bf16)
    r = x - hi.astype(f32)
    mid = r.astype(bf16)
    lo = (r - mid.astype(f32)).astype(bf16)
    return hi, mid, lo


def _dot(a, b):
    return jnp.dot(a, b, preferred_element_type=f32)


def _dotg(a, b, dims):
    return lax.dot_general(a, b, dims, preferred_element_type=f32)


def _rms(x):
    return x * lax.rsqrt(jnp.mean(x * x, axis=-1, keepdims=True) + RMS_EPS)


def _divisor_tile(n, cap, mult):
    best = None
    t = mult
    while t <= min(n, cap):
        if n % t == 0:
            best = t
        t += mult
    assert best is not None, (n, cap, mult)
    return best


def _mod_kernel(c_ref, w_ref, b_ref, o_ref):
    c = c_ref[...]
    s = (c * _sigmoid(c)).astype(bf16)
    o_ref[...] = _dot(s, w_ref[...].astype(bf16)) + b_ref[...]


def _adaln(c_all, w_ada, b_ada):
    r, d = c_all.shape
    n = w_ada.shape[1]
    tn = _divisor_tile(n, 1536, LANES)
    return pl.pallas_call(
        _mod_kernel,
        grid=(n // tn,),
        in_specs=[pl.BlockSpec((r, d), lambda j: (0, 0)),
                  pl.BlockSpec((d, tn), lambda j: (0, j)),
                  pl.BlockSpec((1, tn), lambda j: (0, j))],
        out_specs=pl.BlockSpec((r, tn), lambda j: (0, j)),
        out_shape=jax.ShapeDtypeStruct((r, n), f32),
        compiler_params=_cparams(("arbitrary",)),
        name="adaln_mod",
    )(c_all, w_ada, b_ada.reshape(1, n))


def _inproj_kernel(x_ref, sc_ref, sh_ref, g_ref, wbig_ref, wsm_ref, wfa_ref, wkt_ref, wvt_ref,
                   *out_refs, d, time_minor):
    if time_minor:
        (qat_ref, kat_ref, vt_ref, ka_ref, va_ref, qb_ref, kb_ref, vb_ref, rb_ref,
         ga_ref, gb_ref, sm_ref, fat_ref) = out_refs
    else:
        (qat_ref, ka_ref, va_ref, qb_ref, kb_ref, vb_ref, rb_ref,
         ga_ref, gb_ref, sm_ref, fat_ref) = out_refs
    x = x_ref[0]
    h = (_rms(x) * g_ref[...]) * sc_ref[0] + sh_ref[0]
    hb = h.astype(bf16)
    off = [0]

    def mm(width):
        r = _dot(hb, wbig_ref[:, off[0]:off[0] + width])
        off[0] += width
        return r

    qa = mm(W_A)
    ka = mm(W_A)
    va = mm(W_A)
    for hh in range(H_A):
        sl = slice(hh * DH_A, (hh + 1) * DH_A)
        qat_ref[0, hh] = qa[:, sl].astype(bf16)
        if time_minor:
            kat_ref[0, hh] = ka[:, sl].astype(bf16)
    if time_minor:
        tm = x.shape[0]
        kt = _dotg(wkt_ref[...], hb, _NT).reshape(H_A, DH_A, tm)
        vt = _dotg(wvt_ref[...], hb, _NT).reshape(H_A, DH_A, tm)
        ka_ref[0] = kt
        va_ref[0] = vt
        vt_ref[0] = vt.astype(bf16)
    else:
        ka_ref[0] = ka
        va_ref[0] = va
    qb_ref[0] = mm(WK_B)
    kb_ref[0] = mm(WK_B)
    vb_ref[0] = mm(WV_B)
    rb_ref[0] = mm(WV_B)
    ga_ref[0] = mm(d)
    gb_ref[0] = mm(d)
    sm_ref[0] = _dot(hb, wsm_ref[...])
    fat_ref[0] = _dotg(wfa_ref[...].astype(bf16), hb, _NT)


def _inproj_weights(w_in, d):
    sizes = (W_A, W_A, W_A, H_A, WK_B, WK_B, WV_B, WV_B, GLA_RANK, d, d)
    offs = [0]
    for s in sizes:
        offs.append(offs[-1] + s)
    seg = lambda i: w_in[:, offs[i]:offs[i + 1]]
    wbig = jnp.concatenate(
        [seg(0) * (DH_A ** -0.5), seg(1), seg(2), seg(4) * (DK_B ** -0.5), seg(5),
         seg(6), seg(7), seg(9), seg(10)], axis=1).astype(bf16)
    wsm = jnp.concatenate(
        [seg(3), seg(8), jnp.zeros((d, LANES - H_A - GLA_RANK), f32)], axis=1).astype(bf16)
    wfa = seg(3).T
    wkt = seg(1).T.astype(bf16)
    wvt = seg(2).T.astype(bf16)
    return wbig, wsm, wfa, wkt, wvt


def _inproj(x, sc, sh, g_mix, weights, time_minor):
    wbig, wsm, wfa, wkt, wvt = weights
    b, t, d = x.shape
    tm = min(TOKEN_TILE, t)
    assert t % tm == 0
    rows = sc.shape[1]
    mod_block = (1, 1, d) if rows == 1 else (1, tm, d)
    mod_map = (lambda bi, i: (bi, 0, 0)) if rows == 1 else (lambda bi, i: (bi, i, 0))
    tok3 = lambda w: pl.BlockSpec((1, tm, w), lambda bi, i: (bi, i, 0))
    head4 = pl.BlockSpec((1, H_A, tm, DH_A), lambda bi, i: (bi, 0, i, 0))
    const2 = lambda a: pl.BlockSpec(a.shape, lambda bi, i: (0, 0))
    sds = jax.ShapeDtypeStruct
    headt = pl.BlockSpec((1, H_A, DH_A, tm), lambda bi, i: (bi, 0, 0, i))
    if time_minor:
        kv_shape = [sds((b, H_A, t, DH_A), bf16), sds((b, H_A, DH_A, t), bf16),
                    sds((b, H_A, DH_A, t), f32), sds((b, H_A, DH_A, t), f32)]
        kv_specs = [head4, headt, headt, headt]
    else:
        kv_shape = [sds((b, t, W_A), f32), sds((b, t, W_A), f32)]
        kv_specs = [tok3(W_A), tok3(W_A)]
    out_shape = tuple(
        [sds((b, H_A, t, DH_A), bf16)] + kv_shape
        + [sds((b, t, WK_B), f32), sds((b, t, WK_B), f32),
           sds((b, t, WV_B), f32), sds((b, t, WV_B), f32),
           sds((b, t, d), f32), sds((b, t, d), f32),
           sds((b, t, LANES), f32), sds((b, H_A, t), f32)])
    out_specs = tuple(
        [head4] + kv_specs
        + [tok3(WK_B), tok3(WK_B), tok3(WV_B), tok3(WV_B), tok3(d), tok3(d), tok3(LANES),
           pl.BlockSpec((1, H_A, tm), lambda bi, i: (bi, 0, i))])
    g2 = g_mix.reshape(1, d)
    return pl.pallas_call(
        functools.partial(_inproj_kernel, d=d, time_minor=time_minor),
        grid=(b, t // tm),
        in_specs=[tok3(d), pl.BlockSpec(mod_block, mod_map), pl.BlockSpec(mod_block, mod_map),
                  const2(g2), const2(wbig), const2(wsm), const2(wfa), const2(wkt), const2(wvt)],
        out_specs=out_specs,
        out_shape=out_shape,
        compiler_params=_cparams(("parallel", "arbitrary")),
        name="inproj",
    )(x, sc, sh, g2, wbig, wsm, wfa, wkt, wvt)


def _logf_kernel(fa_ref, bf_ref, logf_ref, *cum_refs, t, cb):
    lf = _log_sigmoid(fa_ref[...] + bf_ref[...])
    logf_ref[...] = lf
    if not cum_refs:
        return
    cum_ref, = cum_refs
    r = lax.broadcasted_iota(i32, (cb, cb), 0)
    c = lax.broadcasted_iota(i32, (cb, cb), 1)
    triu = jnp.where(r <= c, 1.0, 0.0).astype(bf16)
    carry = jnp.zeros((lf.shape[0], 1), f32)
    for j in range(t // cb):
        hi, mid, lo = _split3(lf[:, j * cb:(j + 1) * cb])
        blk = _dot(hi, triu) + _dot(mid, triu) + _dot(lo, triu) + carry
        cum_ref[:, j * cb:(j + 1) * cb] = blk
        carry = blk[:, cb - 1:cb]


def _logf(fat, b_f, with_cumsum):
    b, _, t = fat.shape
    rows = b * H_A
    fa2 = fat.reshape(rows, t)
    bf2 = jnp.tile(b_f.reshape(H_A, 1), (b, 1))
    cb = min(256, t)
    full = lambda a: pl.BlockSpec(a.shape, lambda: (0,) * a.ndim)
    n_out = 2 if with_cumsum else 1
    outs = pl.pallas_call(
        functools.partial(_logf_kernel, t=t, cb=cb),
        in_specs=[full(fa2), full(bf2)],
        out_specs=tuple(pl.BlockSpec((rows, t), lambda: (0, 0)) for _ in range(n_out)),
        out_shape=tuple(jax.ShapeDtypeStruct((rows, t), f32) for _ in range(n_out)),
        name="log_forget",
    )(fa2, bf2)
    return tuple(o.reshape(b, H_A, t) for o in outs)


def _bias_lanes(col, ones_first):
    hi, mid, lo = [p.astype(f32) for p in _split3(col)]
    lane = lax.broadcasted_iota(i32, (col.shape[0], DH_A), 1)
    base = 3 if ones_first else 0
    parts = jnp.where(lane == base, hi, jnp.where(lane == base + 1, mid,
                      jnp.where(lane == base + 2, lo, 0.0)))
    ones = (lane < 3) if ones_first else ((lane >= 3) & (lane < 6))
    return jnp.where(ones, 1.0, parts)


def _fox_kernel(q_ref, k_ref, vt_ref, cum_ref, o_ref, kaug, qaug, *, tq, tk):
    i = pl.program_id(1)
    per_q = tq // tk

    @pl.when(i == 0)
    def _():
        cum_all = cum_ref[0]
        for h in range(H_A):
            kaug[h, :, :DH_A] = k_ref[0, h]
            kaug[h, :, DH_A:] = _bias_lanes(-cum_all[:, h:h + 1], False).astype(bf16)

    cum_q = cum_ref[0, pl.ds(pl.multiple_of(i * tq, tq), tq), :]
    for h in range(H_A):
        qaug[h, :, :DH_A] = q_ref[0, h]
        qaug[h, :, DH_A:] = _bias_lanes(cum_q[:, h:h + 1], True).astype(bf16)
    key = lax.broadcasted_iota(i32, (tk, tq), 0)
    qry = lax.broadcasted_iota(i32, (tk, tq), 1)

    def step(j, carry, key_off=None):
        start = pl.multiple_of(j * tk, tk)
        scores = [_dotg(kaug[h, pl.ds(start, tk), :], qaug[h], _NT) for h in range(H_A)]
        stats = []
        for h in range(H_A):
            m_old, l_old, _ = carry[h]
            s = scores[h]
            if key_off is not None:
                s = jnp.where(key + key_off <= qry, s, -jnp.inf)
            m_new = jnp.maximum(m_old, jnp.max(s, axis=0, keepdims=True))
            alpha = jnp.exp(m_old - m_new)
            p = jnp.exp(s - m_new)
            l_new = alpha * l_old + jnp.sum(p, axis=0, keepdims=True)
            stats.append((m_new, l_new, alpha, p.astype(bf16)))
        out = []
        for h in range(H_A):
            m_new, l_new, alpha, p = stats[h]
            acc_new = alpha * carry[h][2] + _dot(vt_ref[0, h, :, pl.ds(start, tk)], p)
            out.append((m_new, l_new, acc_new))
        return tuple(out)

    init = tuple((jnp.full((1, tq), -jnp.inf, f32), jnp.zeros((1, tq), f32),
                  jnp.zeros((DH_A, tq), f32)) for _ in range(H_A))
    carry = lax.fori_loop(0, i * per_q, step, init)
    for dblk in range(per_q):
        carry = step(i * per_q + dblk, carry, key_off=dblk * tk)
    for h in range(H_A):
        _, l, acc = carry[h]
        o_ref[0, h * DH_A:(h + 1) * DH_A, :] = (acc / l).astype(bf16)


def _fox_prompt(qat, kat, vt, cum):
    b, _, t, _ = qat.shape
    tq = min(ATTN_Q_TILE, t)
    tk = min(ATTN_K_TILE, tq)
    assert t % tq == 0 and tq % tk == 0
    return pl.pallas_call(
        functools.partial(_fox_kernel, tq=tq, tk=tk),
        grid=(b, t // tq),
        in_specs=[pl.BlockSpec((1, H_A, tq, DH_A), lambda bi, i: (bi, 0, i, 0)),
                  pl.BlockSpec((1, H_A, t, DH_A), lambda bi, i: (bi, 0, 0, 0)),
                  pl.BlockSpec((1, H_A, DH_A, t), lambda bi, i: (bi, 0, 0, 0)),
                  pl.BlockSpec((1, t, H_A), lambda bi, i: (bi, 0, 0))],
        out_specs=pl.BlockSpec((1, W_A, tq), lambda bi, i: (bi, 0, i)),
        out_shape=jax.ShapeDtypeStruct((b, W_A, t), bf16),
        scratch_shapes=[pltpu.VMEM((H_A, t, 2 * DH_A), bf16), pltpu.VMEM((H_A, tq, 2 * DH_A), bf16)],
        compiler_params=_cparams(("parallel", "arbitrary")),
        name="fox_prompt",
    )(qat, kat, vt, cum)


def _fox_decode_kernel(pt_ref, q_ref, knew_ref, vnew_ref, lfnew_ref, kt_hbm, vt_hbm, lft_hbm,
                       o_ref, pages, lfpages, p_scr, m_scr, self_scr, acc_scr, carry_scr,
                       sem_kv, sem_lf, *, pg, nstep, n_pages, db):
    b = pl.program_id(0)
    j = pl.program_id(1)
    q = q_ref[0]
    psz = lfpages.shape[-1]
    lane = lax.broadcasted_iota(i32, (H_A, W_A), 1)
    sub = lax.broadcasted_iota(i32, (H_A, W_A), 0)
    own = (lane // DH_A) == sub
    rnd = lambda z: z.astype(bf16).astype(f32)

    def start_load(seq, t, slot):
        def issue(src_hbm, dst, sem, group):
            for p in range(pg):
                page = pt_ref[seq * n_pages + group * pg + p]
                pltpu.make_async_copy(src_hbm.at[page], dst.at[slot, p], sem.at[slot]).start()

        @pl.when(t < nstep)
        def _():
            issue(kt_hbm, pages, sem_kv, nstep - 1 - t)
            issue(lft_hbm, lfpages, sem_lf, nstep - 1 - t)

        @pl.when(t >= nstep)
        def _():
            issue(vt_hbm, pages, sem_kv, 2 * nstep - 1 - t)

    step = b * (2 * nstep) + j
    slot = step % 2

    @pl.when(step == 0)
    def _():
        start_load(0, 0, 0)

    last = j == 2 * nstep - 1
    nxt_seq = jnp.where(last, b + 1, b)
    nxt_t = jnp.where(last, 0, j + 1)

    @pl.when(nxt_seq < db)
    def _():
        start_load(nxt_seq, nxt_t, 1 - slot)

    pltpu.make_async_copy(kt_hbm.at[pl.ds(0, pg)], pages.at[slot], sem_kv.at[slot]).wait()

    @pl.when(j < nstep)
    def _():
        pltpu.make_async_copy(lft_hbm.at[pl.ds(0, pg)], lfpages.at[slot], sem_lf.at[slot]).wait()

    @pl.when(j == 0)
    def _():
        s_self = jnp.sum(q.astype(f32) * rnd(knew_ref[0]), axis=1, keepdims=True)
        self_scr[...] = s_self
        m_scr[...] = s_self
        carry_scr[...] = lfnew_ref[0]

    @pl.when(j < nstep)
    def _():
        r = lax.broadcasted_iota(i32, (psz, 2 * psz), 0)
        c = lax.broadcasted_iota(i32, (psz, 2 * psz), 1)
        later = jnp.where(c < psz, jnp.where(r > c, 1.0, 0.0), 1.0).astype(bf16)
        group = nstep - 1 - j
        carry = carry_scr[...]
        m = m_scr[...]
        hi, mid, lo = _split3(lfpages[slot].reshape(pg * H_A, psz))
        suf_all = _dot(hi, later) + _dot(mid, later) + _dot(lo, later)
        for p in reversed(range(pg)):
            suf = suf_all[p * H_A:(p + 1) * H_A]
            kt = pages[slot, p].reshape(W_A, psz).astype(bf16)
            s = _dot(q, kt) + (carry + suf[:, :psz])
            carry = carry + suf[:, psz:psz + 1]
            p_scr[group * pg + p] = s
            m = jnp.maximum(m, jnp.max(s, axis=1, keepdims=True))
        carry_scr[...] = carry
        m_scr[...] = m

    @pl.when(j == nstep)
    def _():
        m = m_scr[...]
        e_self = jnp.exp(self_scr[...] - m)
        e = jnp.exp(p_scr[...] - m[None])
        l = e_self + jnp.sum(jnp.sum(e, axis=0), axis=1, keepdims=True)
        p_scr[...] = e / l[None]
        acc_scr[...] = jnp.where(own, rnd(e_self / l) * rnd(vnew_ref[0]), 0.0)

    @pl.when(j >= nstep)
    def _():
        group = 2 * nstep - 1 - j
        acc = acc_scr[...]
        for p in range(pg):
            vt = pages[slot, p].reshape(W_A, psz).astype(bf16)
            acc = acc + _dotg(p_scr[group * pg + p].astype(bf16), vt, _NT)
        acc_scr[...] = acc

    @pl.when(j == 2 * nstep - 1)
    def _():
        o_ref[0] = jnp.sum(jnp.where(own, acc_scr[...], 0.0), axis=0, keepdims=True)


def _fox_decode(qbd, knew, vnew, lfnew, cache_kt, cache_vt, cache_lft, page_table):
    db = qbd.shape[0]
    psz = cache_kt.shape[-1]
    n_pages = page_table.shape[1]
    pg = PAGES_PER_STEP if n_pages % PAGES_PER_STEP == 0 else 1
    nstep = n_pages // pg
    pt = page_table.reshape(-1).astype(i32)
    row3 = lambda a: pl.BlockSpec((1,) + a.shape[1:], lambda bi, j, pt_ref: (bi, 0, 0))
    anyspace = pl.BlockSpec(memory_space=pl.ANY)
    grid_spec = pltpu.PrefetchScalarGridSpec(
        num_scalar_prefetch=1,
        grid=(db, 2 * nstep),
        in_specs=[row3(qbd), row3(knew), row3(vnew), row3(lfnew), anyspace, anyspace, anyspace],
        out_specs=pl.BlockSpec((1, 1, W_A), lambda bi, j, pt_ref: (bi, 0, 0)),
        scratch_shapes=[pltpu.VMEM((2, pg, H_A, DH_A, psz), f32), pltpu.VMEM((2, pg, H_A, psz), f32),
                        pltpu.VMEM((n_pages, H_A, psz), f32), pltpu.VMEM((H_A, 1), f32),
                        pltpu.VMEM((H_A, 1), f32), pltpu.VMEM((H_A, W_A), f32),
                        pltpu.VMEM((H_A, 1), f32),
                        pltpu.SemaphoreType.DMA((2,)), pltpu.SemaphoreType.DMA((2,))])
    return pl.pallas_call(
        functools.partial(_fox_decode_kernel, pg=pg, nstep=nstep, n_pages=n_pages, db=db),
        grid_spec=grid_spec,
        out_shape=jax.ShapeDtypeStruct((db, 1, W_A), f32),
        compiler_params=_cparams(("arbitrary", "arbitrary")),
        name="fox_decode",
    )(pt, qbd, knew, vnew, lfnew, cache_kt, cache_vt, cache_lft)


def _gla_kernel(qb_ref, kb_ref, vb_ref, rb_ref, sm_ref, wal_ref, bal_ref, gg_ref,
                ob_ref, sout_ref, s_scr, *, tt):
    i = pl.program_id(1)

    @pl.when(i == 0)
    def _():
        s_scr[...] = jnp.zeros_like(s_scr)

    q = qb_ref[0]
    k = kb_ref[0]
    z = _dot(sm_ref[0].astype(bf16), wal_ref[...]) + bal_ref[...]
    a = _log_sigmoid(z) * (1.0 / GLA_TAU)

    row = lax.broadcasted_iota(i32, (tt, tt), 0)
    col = lax.broadcasted_iota(i32, (tt, tt), 1)
    tril = jnp.where(col <= row, 1.0, 0.0).astype(bf16)
    hi, mid, lo = _split3(a)
    b = _dot(tril, hi) + _dot(tril, mid) + _dot(tril, lo)

    rowk = lax.broadcasted_iota(i32, (tt, WK_B), 0)
    hs = [slice(h * DK_B, (h + 1) * DK_B) for h in range(H_B)]
    amat = [jnp.zeros((tt, tt), f32) for _ in range(H_B)]
    edge = b
    for lvl in range(tt.bit_length() - 1):
        half = 1 << lvl
        upper = ((rowk >> lvl) & 1) == 1
        kt = jnp.where(upper, 0.0, k * jnp.exp(edge - b)).astype(bf16)
        edge_q = pltpu.roll(edge, half, 0)
        qt = jnp.where(upper, q * jnp.exp(b - edge_q), 0.0).astype(bf16)
        same = (row >> (lvl + 1)) == (col >> (lvl + 1))
        for h in range(H_B):
            amat[h] = amat[h] + jnp.where(same, _dotg(qt[:, hs[h]], kt[:, hs[h]], _NT), 0.0)
        edge = jnp.where(upper, edge, pltpu.roll(edge, tt - half, 0))
    q16 = q.astype(bf16)
    k16 = k.astype(bf16)
    for h in range(H_B):
        amat[h] = amat[h] + jnp.where(row == col, _dotg(q16[:, hs[h]], k16[:, hs[h]], _NT), 0.0)

    qe = (q * jnp.exp(b)).astype(bf16)
    ke = (k * jnp.exp(edge - b)).astype(bf16)
    e_last = jnp.exp(edge[0:1, :])
    v16 = vb_ref[0].astype(bf16)
    r = rb_ref[0]
    er = lax.broadcasted_iota(i32, (DK_B, DK_B), 0)
    ec = lax.broadcasted_iota(i32, (DK_B, DK_B), 1)
    for h in range(H_B):
        vs = slice(h * DV_B, (h + 1) * DV_B)
        vh = v16[:, vs]
        state = s_scr[h]
        o = _dot(amat[h].astype(bf16), vh) + _dot(qe[:, hs[h]], state.astype(bf16))
        decay_col = jnp.sum(
            jnp.where(er == ec, jnp.broadcast_to(e_last[:, hs[h]], (DK_B, DK_B)), 0.0),
            axis=1, keepdims=True)
        s_scr[h] = decay_col * state + _dotg(ke[:, hs[h]], vh, _TN)
        rh = r[:, vs]
        ob_ref[0, :, vs] = (_rms(o) * gg_ref[:, vs] * (rh * _sigmoid(rh))).astype(bf16)

    @pl.when(i == pl.num_programs(1) - 1)
    def _():
        sout_ref[0] = s_scr[...]


def _gla_prompt(qb, kb, vb, rb, sm, wal, bal, gg):
    b, t, _ = qb.shape
    tt = min(GLA_TILE, t)
    assert t % tt == 0 and tt & (tt - 1) == 0
    tok3 = lambda w: pl.BlockSpec((1, tt, w), lambda bi, i: (bi, i, 0))
    const2 = lambda a: pl.BlockSpec(a.shape, lambda bi, i: (0, 0))
    return pl.pallas_call(
        functools.partial(_gla_kernel, tt=tt),
        grid=(b, t // tt),
        in_specs=[tok3(WK_B), tok3(WK_B), tok3(WV_B), tok3(WV_B), tok3(LANES),
                  const2(wal), const2(bal), const2(gg)],
        out_specs=(tok3(WV_B),
                   pl.BlockSpec((1, H_B, DK_B, DV_B), lambda bi, i: (bi, 0, 0, 0))),
        out_shape=(jax.ShapeDtypeStruct((b, t, WV_B), bf16),
                   jax.ShapeDtypeStruct((b, H_B, DK_B, DV_B), f32)),
        scratch_shapes=[pltpu.VMEM((H_B, DK_B, DV_B), f32)],
        compiler_params=_cparams(("parallel", "arbitrary")),
        name="gla_prompt",
    )(qb, kb, vb, rb, sm, wal, bal, gg)


def _loga_kernel(sm_ref, wal_ref, bal_ref, o_ref):
    z = _dot(sm_ref[...].astype(bf16), wal_ref[...]) + bal_ref[...]
    o_ref[...] = _log_sigmoid(z) * (1.0 / GLA_TAU)


def _gla_decode_kernel(q_ref, k_ref, a_ref, v_ref, r_ref, s_ref, gg_ref, o_ref, so_ref):
    for h in range(H_B):
        q, k, a = q_ref[0, h], k_ref[0, h], a_ref[0, h]
        v = v_ref[0, h]
        state = s_ref[0, h]
        ea = jnp.exp(a)
        rnd = lambda z: z.astype(bf16).astype(f32)
        qk = jnp.sum(q * k, axis=0, keepdims=True)
        o = qk * v + jnp.sum(rnd(q * ea) * rnd(state), axis=0, keepdims=True)
        so_ref[0, h] = ea * state + k * v
        r = r_ref[0, h]
        o_ref[0, h] = _rms(o) * gg_ref[h] * (r * _sigmoid(r))


def _gla_decode(qb, kb, vb, rb, sm, wal, bal, gg, state):
    db = qb.shape[0]
    full = lambda a: pl.BlockSpec(a.shape, lambda: (0,) * a.ndim)
    loga = pl.pallas_call(
        _loga_kernel,
        in_specs=[full(sm), full(wal), full(bal)],
        out_specs=pl.BlockSpec((db, WK_B), lambda: (0, 0)),
        out_shape=jax.ShapeDtypeStruct((db, WK_B), f32),
        name="gla_log_decay",
    )(sm, wal, bal)
    col = lambda a: a.reshape(db, H_B, DK_B, 1)
    rowv = lambda a: a.reshape(db, H_B, 1, DV_B)
    gg4 = gg.reshape(H_B, 1, DV_B)
    b4 = lambda shp: pl.BlockSpec((1,) + shp, lambda bi: (bi, 0, 0, 0))
    o, s_new = pl.pallas_call(
        _gla_decode_kernel,
        grid=(db,),
        in_specs=[b4((H_B, DK_B, 1))] * 3 + [b4((H_B, 1, DV_B))] * 2
        + [b4((H_B, DK_B, DV_B)), pl.BlockSpec(gg4.shape, lambda bi: (0, 0, 0))],
        out_specs=(b4((H_B, 1, DV_B)), b4((H_B, DK_B, DV_B))),
        out_shape=(jax.ShapeDtypeStruct((db, H_B, 1, DV_B), f32),
                   jax.ShapeDtypeStruct((db, H_B, DK_B, DV_B), f32)),
        compiler_params=_cparams(("parallel",)),
        name="gla_decode",
    )(col(qb), col(kb), col(loga), rowv(vb), rowv(rb), state, gg4)
    return o.reshape(db, WV_B), s_new


def _mix_kernel(x_ref, oa_ref, ob_ref, ga_ref, gb_ref, gt1_ref, sc2_ref, sh2_ref, gf_ref,
                wba_ref, wbb_ref, wo_ref, wr_ref, br_ref, x1_ref, h2_ref, lg_ref, *, nsub):
    x = x_ref[0]
    tm = x.shape[0]
    ya = _dotg(oa_ref[0].astype(bf16), wba_ref[...], _TN)
    yb = _dot(ob_ref[0].astype(bf16), wbb_ref[...])
    m = _sigmoid(ga_ref[0]) * ya + _sigmoid(gb_ref[0]) * yb
    x1 = x + gt1_ref[0] * _dot(m.astype(bf16), wo_ref[...])
    x1_ref[...] = x1
    h2 = (_rms(x1) * gf_ref[...]) * sc2_ref[0] + sh2_ref[0]
    hb = h2.astype(bf16)
    lg_ref[...] = _dot(hb, wr_ref[...]) + br_ref[...]
    half = nsub * LANES
    hi = lax.bitcast_convert_type(hb[:, :half].astype(f32), jnp.uint32)
    lo = lax.bitcast_convert_type(hb[:, half:].astype(f32), jnp.uint32)
    packed = jnp.bitwise_or(hi, jnp.right_shift(lo, jnp.uint32(16)))
    for s in range(nsub):
        h2_ref[pl.ds(s, tm, stride=nsub), :] = packed[:, s * LANES:(s + 1) * LANES]


def _mix(x, oa, ob, ga, gb, gt1, sc2, sh2, g_ffn, wba, wbb, wo, wr, br):
    b, t, d = x.shape
    nt = b * t
    nsub = d // (2 * LANES)
    tm = min(TOKEN_TILE, t)
    assert t % tm == 0 and d % (2 * LANES) == 0
    rows = gt1.shape[1]
    mod_block = (1, 1, d) if rows == 1 else (1, tm, d)
    mod_map = (lambda bi, i: (bi, 0, 0)) if rows == 1 else (lambda bi, i: (bi, i, 0))
    mod = pl.BlockSpec(mod_block, mod_map)
    tok3 = lambda w: pl.BlockSpec((1, tm, w), lambda bi, i: (bi, i, 0))
    const2 = lambda a: pl.BlockSpec(a.shape, lambda bi, i: (0, 0))
    nti = t // tm
    blk = lambda bi, i: (bi * nti + i, 0)
    g2 = g_ffn.reshape(1, d)
    ins = [x, oa, ob, ga, gb, gt1, sc2, sh2, g2, wba, wbb, wo, wr, br]
    in_specs = [tok3(d), pl.BlockSpec((1, W_A, tm), lambda bi, i: (bi, 0, i)), tok3(WV_B),
                tok3(d), tok3(d), mod, mod, mod,
                const2(g2), const2(wba), const2(wbb), const2(wo), const2(wr), const2(br)]
    return pl.pallas_call(
        functools.partial(_mix_kernel, nsub=nsub),
        grid=(b, nti),
        in_specs=in_specs,
        out_specs=(pl.BlockSpec((tm, d), blk), pl.BlockSpec((tm * nsub, LANES), blk),
                   pl.BlockSpec((tm, LANES), blk)),
        out_shape=(jax.ShapeDtypeStruct((nt, d), f32),
                   jax.ShapeDtypeStruct((nt * nsub, LANES), jnp.uint32),
                   jax.ShapeDtypeStruct((nt, LANES), f32)),
        compiler_params=_cparams(("parallel", "arbitrary")),
        name="branch_mix",
    )(*ins)


def _route_kernel(lg_ref, cin_ref, rec_ref, gate_ref, cnt_ref, carry_scr, *, tr):
    @pl.when(pl.program_id(0) == 0)
    def _():
        carry_scr[...] = cin_ref[...]

    lane = lax.broadcasted_iota(i32, (tr, LANES), 1)
    lanef = lane.astype(f32)
    v = jnp.where(lane < N_EXPERTS, lg_ref[...], -jnp.inf)
    onehot = jnp.zeros((tr, LANES), f32)
    ids, vals = [], []
    for _ in range(TOP_K):
        mx = jnp.max(v, axis=1, keepdims=True)
        idx = jnp.min(jnp.where(v == mx, lanef, float(LANES)), axis=1, keepdims=True)
        sel = lanef == idx
        onehot = jnp.where(sel, 1.0, onehot)
        v = jnp.where(sel, -jnp.inf, v)
        ids.append(idx)
        vals.append(mx)
    es = [jnp.exp(vk - vals[0]) for vk in vals]
    tot = es[0]
    for e in es[1:]:
        tot = tot + e

    r2 = lax.broadcasted_iota(i32, (tr, tr), 0)
    c2 = lax.broadcasted_iota(i32, (tr, tr), 1)
    before = jnp.where(c2 < r2, 1.0, 0.0).astype(bf16)
    pref = _dot(before, onehot.astype(bf16)) + carry_scr[...]
    carry_scr[...] = carry_scr[...] + jnp.sum(onehot, axis=0, keepdims=True)

    rec = jnp.zeros((tr, LANES), f32)
    gate = jnp.zeros((tr, LANES), f32)
    for k in range(TOP_K):
        rank = jnp.sum(jnp.where(lanef == ids[k], pref, 0.0), axis=1, keepdims=True)
        rec = jnp.where(lane == k, rank * float(N_EXPERTS) + ids[k], rec)
        gate = jnp.where(lane == k, es[k] / tot, gate)
    rec_ref[...] = rec.astype(i32)
    gate_ref[...] = gate
    cnt_ref[...] = carry_scr[...]


def _route(logits, counts_in):
    nt = logits.shape[0]
    tr = _divisor_tile(nt, 512, SUBLANES)
    tile = pl.BlockSpec((tr, LANES), lambda i: (i, 0))
    cnt = pl.BlockSpec((1, LANES), lambda i: (0, 0))
    return pl.pallas_call(
        functools.partial(_route_kernel, tr=tr),
        grid=(nt // tr,),
        in_specs=[tile, cnt],
        out_specs=(tile, tile, cnt),
        out_shape=(jax.ShapeDtypeStruct((nt, LANES), i32),
                   jax.ShapeDtypeStruct((nt, LANES), f32),
                   jax.ShapeDtypeStruct((1, LANES), f32)),
        scratch_shapes=[pltpu.VMEM((1, LANES), f32)],
        compiler_params=_cparams(("arbitrary",)),
        name="route_topk",
    )(logits, counts_in)


def _dest_kernel(rec_ref, start_ref, out_ref):
    rec = rec_ref[...]
    lane = lax.broadcasted_iota(i32, rec.shape, 1)
    eid = jnp.bitwise_and(rec, N_EXPERTS - 1).astype(f32)
    rank = jnp.right_shift(rec, N_EXPERTS.bit_length() - 1)
    lanef = lane.astype(f32)
    start = jnp.zeros(rec.shape, f32)
    for k in range(TOP_K):
        mine = lanef == eid[:, k:k + 1]
        st = jnp.sum(jnp.where(mine, start_ref[...], 0.0), axis=1, keepdims=True)
        start = jnp.where(lane == k, st, start)
    out_ref[...] = start.astype(i32) + rank


def _dest_rows(rec, pad_start):
    nt = rec.shape[0]
    tr = _divisor_tile(nt, 512, SUBLANES)
    tile = pl.BlockSpec((tr, LANES), lambda i: (i, 0))
    start = jnp.zeros((1, LANES), f32).at[0, :N_EXPERTS].set(pad_start.astype(f32))
    out = pl.pallas_call(
        _dest_kernel,
        grid=(nt // tr,),
        in_specs=[tile, pl.BlockSpec((1, LANES), lambda i: (0, 0))],
        out_specs=tile,
        out_shape=jax.ShapeDtypeStruct((nt, LANES), i32),
        compiler_params=_cparams(("parallel",)),
        name="moe_dest_rows",
    )(rec, start)
    return out[:, :TOP_K].reshape(-1)


def _dispatch_kernel(dest_ref, h_ref, xin_hbm, xout_hbm, stage, sems, *, nsub, ch):
    del xin_hbm
    i = pl.program_id(0)
    slot = i % 2
    base = i * ch
    total = ch * TOP_K * nsub

    def drain(s):
        pltpu.make_async_copy(xout_hbm.at[pl.ds(0, total)], xout_hbm.at[pl.ds(0, total)],
                              sems.at[s]).wait()

    stage[slot] = h_ref[...]

    def body(n, carry):
        for k in range(TOP_K):
            src = pl.multiple_of(n * nsub, nsub)
            dst = pl.multiple_of(dest_ref[(base + n) * TOP_K + k] * nsub, nsub)
            pltpu.make_async_copy(stage.at[slot, pl.ds(src, nsub)], xout_hbm.at[pl.ds(dst, nsub)],
                                  sems.at[slot]).start()
        return carry

    lax.fori_loop(0, ch, body, 0, unroll=4)

    @pl.when(i > 0)
    def _():
        drain(1 - slot)

    @pl.when(i == pl.num_programs(0) - 1)
    def _():
        drain(slot)


def _dispatch(dest_flat, h2rows, xrows, nsub):
    n_tok = dest_flat.shape[0] // TOP_K
    ch = min(DISPATCH_CHUNK, n_tok)
    assert n_tok % ch == 0
    grid_spec = pltpu.PrefetchScalarGridSpec(
        num_scalar_prefetch=1,
        grid=(n_tok // ch,),
        in_specs=[pl.BlockSpec((ch * nsub, LANES), lambda i, dr: (i, 0)),
                  pl.BlockSpec(memory_space=pl.ANY)],
        out_specs=pl.BlockSpec(memory_space=pl.ANY),
        scratch_shapes=[pltpu.VMEM((2, ch * nsub, LANES), h2rows.dtype),
                        pltpu.SemaphoreType.DMA((2,))])
    return pl.pallas_call(
        functools.partial(_dispatch_kernel, nsub=nsub, ch=ch),
        grid_spec=grid_spec,
        out_shape=jax.ShapeDtypeStruct(xrows.shape, xrows.dtype),
        input_output_aliases={2: 0},
        compiler_params=_cparams(("arbitrary",)),
        name="moe_dispatch",
    )(dest_flat, h2rows, xrows)


def _expert_kernel(be_ref, nused_ref, x_ref, wgu_ref, bgu_ref, wd_ref, bd_ref, y_ref,
                   wgu_b, wd_b, *, tmx, nsub_x, nsub, dff, chunk):
    r = pl.program_id(0)
    live = r < nused_ref[0]

    @pl.when(jnp.logical_not(live))
    def _():
        y_ref[...] = jnp.zeros_like(y_ref)

    new_expert = jnp.logical_or(r == 0, be_ref[r] != be_ref[jnp.maximum(r - 1, 0)])

    @pl.when(jnp.logical_and(live, new_expert))
    def _():
        for c in range(0, wgu_b.shape[0], chunk):
            wgu_b[c:c + chunk, :] = wgu_ref[0, c:c + chunk, :].astype(bf16)
        for c in range(0, wd_b.shape[0], chunk):
            wd_b[c:c + chunk, :] = wd_ref[0, c:c + chunk, :].astype(bf16)

    @pl.when(live)
    def _():
        words = [x_ref[pl.ds(s, tmx, stride=nsub_x), :] for s in range(nsub_x)]
        as_bf16 = lambda w: lax.bitcast_convert_type(w, f32).astype(bf16)
        x = jnp.concatenate(
            [as_bf16(jnp.bitwise_and(w, jnp.uint32(0xFFFF0000))) for w in words]
            + [as_bf16(jnp.left_shift(w, jnp.uint32(16))) for w in words], axis=1)
        gu = _dot(x, wgu_b[...]) + bgu_ref[0]
        gate = jnp.minimum(gu[:, :dff], SWIGLU_LIMIT)
        up = jnp.clip(gu[:, dff:], -SWIGLU_LIMIT, SWIGLU_LIMIT)
        glu = gate * _sigmoid(SWIGLU_ALPHA * gate)
        y = _dot(((up + 1.0) * glu).astype(bf16), wd_b[...]) + bd_ref[0]
        for s in range(nsub):
            y_ref[pl.ds(s, tmx, stride=nsub), :] = y[:, s * LANES:(s + 1) * LANES]


def _experts(block_e, nused, xrows, wgu, bgu, wd, bd):
    n_blocks = block_e.shape[0]
    tmx = EXPERT_TILE
    e, d, dff2 = wgu.shape
    dff = dff2 // 2
    nsub = d // LANES
    nsub_x = d // (2 * LANES)
    live = lambda r, nu: jnp.minimum(r, nu[0] - 1)
    rows = pl.BlockSpec((tmx * nsub_x, LANES), lambda r, be, nu: (live(r, nu), 0))
    per_e = lambda shp: pl.BlockSpec((1,) + shp, lambda r, be, nu: (be[live(r, nu)], 0, 0))
    grid_spec = pltpu.PrefetchScalarGridSpec(
        num_scalar_prefetch=2,
        grid=(n_blocks,),
        in_specs=[rows, per_e((d, dff2)), per_e((1, dff2)), per_e((dff, d)), per_e((1, d))],
        out_specs=pl.BlockSpec((tmx * nsub, LANES), lambda r, be, nu: (r, 0)),
        scratch_shapes=[pltpu.VMEM((d, dff2), bf16), pltpu.VMEM((dff, d), bf16)])
    return pl.pallas_call(
        functools.partial(_expert_kernel, tmx=tmx, nsub_x=nsub_x, nsub=nsub, dff=dff,
                          chunk=min(256, d, dff)),
        grid_spec=grid_spec,
        out_shape=jax.ShapeDtypeStruct((n_blocks * tmx * nsub, LANES), f32),
        compiler_params=_cparams(("arbitrary",)),
        name="moe_experts",
    )(block_e, nused, xrows, wgu, bgu.reshape(e, 1, dff2), wd, bd.reshape(e, 1, d))


def _combine_kernel(dest_ref, y_hbm, gate_ref, x1_ref, gt2_ref, gfin_ref, out_ref,
                    buf, sems, *, tc, nsub):
    i = pl.program_id(0)
    slot = i % 2

    def start_tile(tile, s):
        def body(n, carry):
            for k in range(TOP_K):
                src = pl.multiple_of(dest_ref[(tile * tc + n) * TOP_K + k] * nsub, nsub)
                dst = pl.multiple_of((k * tc + n) * nsub, nsub)
                pltpu.make_async_copy(y_hbm.at[pl.ds(src, nsub)], buf.at[s, pl.ds(dst, nsub)],
                                      sems.at[s]).start()
            return carry
        lax.fori_loop(0, tc, body, 0, unroll=4)

    @pl.when(i == 0)
    def _():
        start_tile(0, 0)

    @pl.when(i + 1 < pl.num_programs(0))
    def _():
        start_tile(i + 1, 1 - slot)

    pltpu.make_async_copy(y_hbm.at[pl.ds(0, buf.shape[1])], buf.at[slot], sems.at[slot]).wait()

    g = gate_ref[...]
    cols = []
    for s in range(nsub):
        acc = None
        for k in range(TOP_K):
            term = g[:, k:k + 1] * buf[slot, pl.ds(k * tc * nsub + s, tc, stride=nsub), :]
            acc = term if acc is None else acc + term
        cols.append(acc)
    y = jnp.concatenate(cols, axis=1)
    out_ref[...] = _rms(x1_ref[...] + gt2_ref[0] * y) * gfin_ref[...]


def _combine(dest_flat, yrows, gates, x1, gt2, g_final, t_per_mod, nsub):
    n_tok, d = x1.shape
    tc = min(TOKEN_TILE, n_tok)
    assert n_tok % tc == 0
    rows = gt2.shape[1]
    if rows == 1:
        per_mod = t_per_mod // tc
        mod = pl.BlockSpec((1, 1, d), lambda i, dr: (i // per_mod, 0, 0))
    else:
        mod = pl.BlockSpec((1, tc, d), lambda i, dr: (0, i, 0))
    blk = lambda i, dr: (i, 0)
    gfin = g_final.reshape(1, d)
    grid_spec = pltpu.PrefetchScalarGridSpec(
        num_scalar_prefetch=1,
        grid=(n_tok // tc,),
        in_specs=[pl.BlockSpec(memory_space=pl.ANY), pl.BlockSpec((tc, LANES), blk),
                  pl.BlockSpec((tc, d), blk), mod,
                  pl.BlockSpec((1, d), lambda i, dr: (0, 0))],
        out_specs=pl.BlockSpec((tc, d), blk),
        scratch_shapes=[pltpu.VMEM((2, TOP_K * tc * nsub, LANES), f32),
                        pltpu.SemaphoreType.DMA((2,))])
    return pl.pallas_call(
        functools.partial(_combine_kernel, tc=tc, nsub=nsub),
        grid_spec=grid_spec,
        out_shape=jax.ShapeDtypeStruct((n_tok, d), f32),
        compiler_params=_cparams(("arbitrary",)),
        name="moe_combine",
    )(dest_flat, yrows, gates, x1, gt2, gfin)


def kernel(x_prompt, x_sample, c_prompt, c_sample, cache_k, cache_v, cache_logf, state_gla,
           page_table, g_mix, g_ffn, g_final, w_ada, b_ada, w_in, b_f, w_alpha, b_alpha, g_gla,
           w_branch, w_o, w_router, b_router, w_gu, b_gu, w_d, b_d):
    depth = g_mix.shape[0]
    assert depth == 1, "one decoder layer"
    bp, t, d = x_prompt.shape
    db = x_sample.shape[0]
    assert x_sample.shape[1] == 1
    nsub = d // LANES
    ntp = bp * t
    nt = ntp + db

    nmod = bp + db
    pad = (-nmod) % SUBLANES
    c_all = jnp.concatenate([c_prompt, c_sample, jnp.zeros((pad, d), f32)], axis=0)
    mod = _adaln(c_all, w_ada[0], b_ada[0])
    sh1, sc1, gt1, sh2, sc2, gt2 = [mod[:, i * d:(i + 1) * d] for i in range(6)]
    grp_p = lambda a: a[:bp].reshape(bp, 1, d)
    grp_s = lambda a: a[bp:nmod].reshape(1, db, d)

    w_proj = _inproj_weights(w_in[0], d)
    wal = jnp.zeros((LANES, WK_B), f32).at[H_A:H_A + GLA_RANK].set(w_alpha[0]).astype(bf16)
    bal = b_alpha[0].reshape(1, WK_B)
    gg = g_gla[0].reshape(1, WV_B)
    wba = w_branch[0, :W_A].astype(bf16)
    wbb = w_branch[0, W_A:].astype(bf16)
    wo = w_o[0].astype(bf16)
    wr = jnp.zeros((d, LANES), f32).at[:, :N_EXPERTS].set(w_router[0]).astype(bf16)
    br = jnp.zeros((1, LANES), f32).at[0, :N_EXPERTS].set(b_router[0])

    (qat, kat, vt, ka, va, qb, kb, vb, rb, ga, gb, sm, fat) = _inproj(
        x_prompt, grp_p(1.0 + sc1), grp_p(sh1), g_mix[0], w_proj, True)
    logft, cumt = _logf(fat, b_f[0], True)
    oa = _fox_prompt(qat, kat, vt, jnp.transpose(cumt, (0, 2, 1)))
    ob, gla_p = _gla_prompt(qb, kb, vb, rb, sm, wal, bal, gg)
    x1_p, h2_p, lg_p = _mix(x_prompt, oa, ob, ga, gb, grp_p(gt1), grp_p(1.0 + sc2), grp_p(sh2),
                            g_ffn[0], wba, wbb, wo, wr, br)

    xs = x_sample.reshape(1, db, d)
    (qat_s, ka_s, va_s, qb_s, kb_s, vb_s, rb_s, ga_s, gb_s, sm_s, fat_s) = _inproj(
        xs, grp_s(1.0 + sc1), grp_s(sh1), g_mix[0], w_proj, False)
    logft_s, = _logf(fat_s, b_f[0], False)
    lfnew = jnp.transpose(logft_s[0], (1, 0)).reshape(db, H_A, 1)
    qbd = jnp.einsum('hnd,hc->nchd', qat_s[0], jnp.eye(H_A, dtype=bf16)).reshape(db, H_A, W_A)
    oa_s = _fox_decode(qbd, ka_s.reshape(db, 1, W_A), va_s.reshape(db, 1, W_A), lfnew,
                       jnp.transpose(cache_k[0], (0, 2, 3, 1)),
                       jnp.transpose(cache_v[0], (0, 2, 3, 1)),
                       jnp.transpose(cache_logf[0], (0, 2, 1)), page_table)
    ob_s, gla_s = _gla_decode(qb_s[0], kb_s[0], vb_s[0], rb_s[0], sm_s[0], wal, bal, gg,
                              state_gla[0])
    x1_s, h2_s, lg_s = _mix(xs, oa_s.reshape(db, W_A).T[None], ob_s.reshape(1, db, WV_B), ga_s, gb_s,
                            grp_s(gt1), grp_s(1.0 + sc2), grp_s(sh2), g_ffn[0],
                            wba, wbb, wo, wr, br)

    rec_p, gates_p, cnt_p = _route(lg_p, jnp.zeros((1, LANES), f32))
    rec_s, gates_s, cnt = _route(lg_s, cnt_p)
    counts = cnt[0, :N_EXPERTS].astype(i32)
    padded = (counts + EXPERT_TILE - 1) // EXPERT_TILE * EXPERT_TILE
    pad_end = jnp.cumsum(padded).astype(i32)
    pad_start = pad_end - padded

    dest_p = _dest_rows(rec_p, pad_start)
    dest_s = _dest_rows(rec_s, pad_start)
    n_blocks = -(-(nt * TOP_K + N_EXPERTS * (EXPERT_TILE - 1)) // EXPERT_TILE)
    block_start = jnp.arange(n_blocks, dtype=i32) * EXPERT_TILE
    block_e = jnp.minimum(jnp.sum(pad_end[None, :] <= block_start[:, None], axis=1),
                          N_EXPERTS - 1).astype(i32)
    nused = (pad_end[-1:] // EXPERT_TILE).astype(i32)

    nsub_x = d // (2 * LANES)
    xrows = jnp.zeros((n_blocks * EXPERT_TILE * nsub_x, LANES), jnp.uint32)
    xrows = _dispatch(dest_p, h2_p, xrows, nsub_x)
    xrows = _dispatch(dest_s, h2_s, xrows, nsub_x)
    yrows = _experts(block_e, nused, xrows, w_gu[0], b_gu[0], w_d[0], b_d[0])
    y_p = _combine(dest_p, yrows, gates_p, x1_p, grp_p(gt2), g_final, t, nsub)
    y_s = _combine(dest_s, yrows, gates_s, x1_s, grp_s(gt2), g_final, 1, nsub)

    logf_p = jnp.transpose(logft, (0, 2, 1))
    return (y_p.reshape(bp, t, d), y_s.reshape(db, 1, d),
            jnp.transpose(ka, (0, 3, 1, 2))[None], jnp.transpose(va, (0, 3, 1, 2))[None],
            logf_p.reshape(1, bp, t, H_A), gla_p.reshape(1, bp, H_B, DK_B, DV_B),
            ka_s.reshape(1, db, 1, H_A, DH_A), va_s.reshape(1, db, 1, H_A, DH_A),
            lfnew.reshape(1, db, 1, H_A), gla_s.reshape(1, db, H_B, DK_B, DV_B))
```

```python
import functools

import jax
import jax.numpy as jnp
from jax import lax
from jax.experimental import pallas as pl
from jax.experimental.pallas import tpu as pltpu

f32 = jnp.float32
bf16 = jnp.bfloat16
i32 = jnp.int32

H_A, DH_A = 8, 64
H_B, DK_B, DV_B = 4, 64, 128
GLA_RANK = 16
GLA_TAU = 16.0
N_EXPERTS = 32
TOP_K = 4
SWIGLU_LIMIT = 7.0
SWIGLU_ALPHA = 1.702
RMS_EPS = 1e-6
W_A = H_A * DH_A
WK_B = H_B * DK_B
WV_B = H_B * DV_B

LANES = 128
SUBLANES = 8
VMEM_LIMIT_MB = 56

TOKEN_TILE = 256
ATTN_Q_TILE = 512
ATTN_K_TILE = 256
GLA_TILE = 256
EXPERT_TILE = 512
PAGES_PER_STEP = 32
DISPATCH_CHUNK = 256

_NT = (((1,), (1,)), ((), ()))
_TN = (((0,), (0,)), ((), ()))


def _cparams(semantics, vmem_mb=VMEM_LIMIT_MB):
    return pltpu.CompilerParams(dimension_semantics=semantics,
                                vmem_limit_bytes=vmem_mb << 20)


def _sigmoid(x):
    return 1.0 / (1.0 + jnp.exp(-x))


def _log_sigmoid(x):
    return jnp.minimum(x, 0.0) - jnp.log1p(jnp.exp(-jnp.abs(x)))


def _split3(x):
    hi = x.astype(bf16)
    r = x - hi.astype(f32)
    mid = r.astype(bf16)
    lo = (r - mid.astype(f32)).astype(bf16)
    return hi, mid, lo


def _dot(a, b):
    return jnp.dot(a, b, preferred_element_type=f32)


def _dotg(a, b, dims):
    return lax.dot_general(a, b, dims, preferred_element_type=f32)


def _rms(x):
    return x * lax.rsqrt(jnp.mean(x * x, axis=-1, keepdims=True) + RMS_EPS)


def _divisor_tile(n, cap, mult):
    best = None
    t = mult
    while t <= min(n, cap):
        if n % t == 0:
            best = t
        t += mult
    assert best is not None, (n, cap, mult)
    return best


def _mod_kernel(c_ref, w_ref, b_ref, o_ref):
    c = c_ref[...]
    s = (c * _sigmoid(c)).astype(bf16)
    o_ref[...] = _dot(s, w_ref[...].astype(bf16)) + b_ref[...]


def _adaln(c_all, w_ada, b_ada):
    r, d = c_all.shape
    n = w_ada.shape[1]
    tn = _divisor_tile(n, 1536, LANES)
    return pl.pallas_call(
        _mod_kernel,
        grid=(n // tn,),
        in_specs=[pl.BlockSpec((r, d), lambda j: (0, 0)),
                  pl.BlockSpec((d, tn), lambda j: (0, j)),
                  pl.BlockSpec((1, tn), lambda j: (0, j))],
        out_specs=pl.BlockSpec((r, tn), lambda j: (0, j)),
        out_shape=jax.ShapeDtypeStruct((r, n), f32),
        compiler_params=_cparams(("arbitrary",)),
        name="adaln_mod",
    )(c_all, w_ada, b_ada.reshape(1, n))


def _inproj_kernel(x_ref, sc_ref, sh_ref, g_ref, wbig_ref, wsm_ref, wfa_ref, wkt_ref, wvt_ref,
                   *out_refs, d, time_minor):
    if time_minor:
        (qat_ref, kat_ref, vt_ref, ka_ref, va_ref, qb_ref, kb_ref, vb_ref, rb_ref,
         ga_ref, gb_ref, sm_ref, fat_ref) = out_refs
    else:
        (qat_ref, ka_ref, va_ref, qb_ref, kb_ref, vb_ref, rb_ref,
         ga_ref, gb_ref, sm_ref, fat_ref) = out_refs
    x = x_ref[0]
    h = (_rms(x) * g_ref[...]) * sc_ref[0] + sh_ref[0]
    hb = h.astype(bf16)
    off = [0]

    def mm(width):
        r = _dot(hb, wbig_ref[:, off[0]:off[0] + width])
        off[0] += width
        return r

    qa = mm(W_A)
    ka = mm(W_A)
    va = mm(W_A)
    for hh in range(H_A):
        sl = slice(hh * DH_A, (hh + 1) * DH_A)
        qat_ref[0, hh] = qa[:, sl].astype(bf16)
        if time_minor:
            kat_ref[0, hh] = ka[:, sl].astype(bf16)
    if time_minor:
        tm = x.shape[0]
        kt = _dotg(wkt_ref[...], hb, _NT).reshape(H_A, DH_A, tm)
        vt = _dotg(wvt_ref[...], hb, _NT).reshape(H_A, DH_A, tm)
        ka_ref[0] = kt
        va_ref[0] = vt
        vt_ref[0] = vt.astype(bf16)
    else:
        ka_ref[0] = ka
        va_ref[0] = va
    qb_ref[0] = mm(WK_B)
    kb_ref[0] = mm(WK_B)
    vb_ref[0] = mm(WV_B)
    rb_ref[0] = mm(WV_B)
    ga_ref[0] = mm(d)
    gb_ref[0] = mm(d)
    sm_ref[0] = _dot(hb, wsm_ref[...])
    fat_ref[0] = _dotg(wfa_ref[...].astype(bf16), hb, _NT)


def _inproj_weights(w_in, d):
    sizes = (W_A, W_A, W_A, H_A, WK_B, WK_B, WV_B, WV_B, GLA_RANK, d, d)
    offs = [0]
    for s in sizes:
        offs.append(offs[-1] + s)
    seg = lambda i: w_in[:, offs[i]:offs[i + 1]]
    wbig = jnp.concatenate(
        [seg(0) * (DH_A ** -0.5), seg(1), seg(2), seg(4) * (DK_B ** -0.5), seg(5),
         seg(6), seg(7), seg(9), seg(10)], axis=1).astype(bf16)
    wsm = jnp.concatenate(
        [seg(3), seg(8), jnp.zeros((d, LANES - H_A - GLA_RANK), f32)], axis=1).astype(bf16)
    wfa = seg(3).T
    wkt = seg(1).T.astype(bf16)
    wvt = seg(2).T.astype(bf16)
    return wbig, wsm, wfa, wkt, wvt


def _inproj(x, sc, sh, g_mix, weights, time_minor):
    wbig, wsm, wfa, wkt, wvt = weights
    b, t, d = x.shape
    tm = min(TOKEN_TILE, t)
    assert t % tm == 0
    rows = sc.shape[1]
    mod_block = (1, 1, d) if rows == 1 else (1, tm, d)
    mod_map = (lambda bi, i: (bi, 0, 0)) if rows == 1 else (lambda bi, i: (bi, i, 0))
    tok3 = lambda w: pl.BlockSpec((1, tm, w), lambda bi, i: (bi, i, 0))
    head4 = pl.BlockSpec((1, H_A, tm, DH_A), lambda bi, i: (bi, 0, i, 0))
    const2 = lambda a: pl.BlockSpec(a.shape, lambda bi, i: (0, 0))
    sds = jax.ShapeDtypeStruct
    headt = pl.BlockSpec((1, H_A, DH_A, tm), lambda bi, i: (bi, 0, 0, i))
    if time_minor:
        kv_shape = [sds((b, H_A, t, DH_A), bf16), sds((b, H_A, DH_A, t), bf16),
                    sds((b, H_A, DH_A, t), f32), sds((b, H_A, DH_A, t), f32)]
        kv_specs = [head4, headt, headt, headt]
    else:
        kv_shape = [sds((b, t, W_A), f32), sds((b, t, W_A), f32)]
        kv_specs = [tok3(W_A), tok3(W_A)]
    out_shape = tuple(
        [sds((b, H_A, t, DH_A), bf16)] + kv_shape
        + [sds((b, t, WK_B), f32), sds((b, t, WK_B), f32),
           sds((b, t, WV_B), f32), sds((b, t, WV_B), f32),
           sds((b, t, d), f32), sds((b, t, d), f32),
           sds((b, t, LANES), f32), sds((b, H_A, t), f32)])
    out_specs = tuple(
        [head4] + kv_specs
        + [tok3(WK_B), tok3(WK_B), tok3(WV_B), tok3(WV_B), tok3(d), tok3(d), tok3(LANES),
           pl.BlockSpec((1, H_A, tm), lambda bi, i: (bi, 0, i))])
    g2 = g_mix.reshape(1, d)
    return pl.pallas_call(
        functools.partial(_inproj_kernel, d=d, time_minor=time_minor),
        grid=(b, t // tm),
        in_specs=[tok3(d), pl.BlockSpec(mod_block, mod_map), pl.BlockSpec(mod_block, mod_map),
                  const2(g2), const2(wbig), const2(wsm), const2(wfa), const2(wkt), const2(wvt)],
        out_specs=out_specs,
        out_shape=out_shape,
        compiler_params=_cparams(("parallel", "arbitrary")),
        name="inproj",
    )(x, sc, sh, g2, wbig, wsm, wfa, wkt, wvt)


def _logf_kernel(fa_ref, bf_ref, logf_ref, *cum_refs, t, cb):
    lf = _log_sigmoid(fa_ref[...] + bf_ref[...])
    logf_ref[...] = lf
    if not cum_refs:
        return
    cum_ref, = cum_refs
    r = lax.broadcasted_iota(i32, (cb, cb), 0)
    c = lax.broadcasted_iota(i32, (cb, cb), 1)
    triu = jnp.where(r <= c, 1.0, 0.0).astype(bf16)
    carry = jnp.zeros((lf.shape[0], 1), f32)
    for j in range(t // cb):
        hi, mid, lo = _split3(lf[:, j * cb:(j + 1) * cb])
        blk = _dot(hi, triu) + _dot(mid, triu) + _dot(lo, triu) + carry
        cum_ref[:, j * cb:(j + 1) * cb] = blk
        carry = blk[:, cb - 1:cb]


def _logf(fat, b_f, with_cumsum):
    b, _, t = fat.shape
    rows = b * H_A
    fa2 = fat.reshape(rows, t)
    bf2 = jnp.tile(b_f.reshape(H_A, 1), (b, 1))
    cb = min(256, t)
    full = lambda a: pl.BlockSpec(a.shape, lambda: (0,) * a.ndim)
    n_out = 2 if with_cumsum else 1
    outs = pl.pallas_call(
        functools.partial(_logf_kernel, t=t, cb=cb),
        in_specs=[full(fa2), full(bf2)],
        out_specs=tuple(pl.BlockSpec((rows, t), lambda: (0, 0)) for _ in range(n_out)),
        out_shape=tuple(jax.ShapeDtypeStruct((rows, t), f32) for _ in range(n_out)),
        name="log_forget",
    )(fa2, bf2)
    return tuple(o.reshape(b, H_A, t) for o in outs)


def _bias_lanes(col, ones_first):
    hi, mid, lo = [p.astype(f32) for p in _split3(col)]
    lane = lax.broadcasted_iota(i32, (col.shape[0], DH_A), 1)
    base = 3 if ones_first else 0
    parts = jnp.where(lane == base, hi, jnp.where(lane == base + 1, mid,
                      jnp.where(lane == base + 2, lo, 0.0)))
    ones = (lane < 3) if ones_first else ((lane >= 3) & (lane < 6))
    return jnp.where(ones, 1.0, parts)


def _fox_kernel(q_ref, k_ref, vt_ref, cum_ref, o_ref, kaug, qaug, *, tq, tk):
    i = pl.program_id(1)
    per_q = tq // tk

    @pl.when(i == 0)
    def _():
        cum_all = cum_ref[0]
        for h in range(H_A):
            kaug[h, :, :DH_A] = k_ref[0, h]
            kaug[h, :, DH_A:] = _bias_lanes(-cum_all[:, h:h + 1], False).astype(bf16)

    cum_q = cum_ref[0, pl.ds(pl.multiple_of(i * tq, tq), tq), :]
    for h in range(H_A):
        qaug[h, :, :DH_A] = q_ref[0, h]
        qaug[h, :, DH_A:] = _bias_lanes(cum_q[:, h:h + 1], True).astype(bf16)
    key = lax.broadcasted_iota(i32, (tk, tq), 0)
    qry = lax.broadcasted_iota(i32, (tk, tq), 1)

    def step(j, carry, key_off=None):
        start = pl.multiple_of(j * tk, tk)
        scores = [_dotg(kaug[h, pl.ds(start, tk), :], qaug[h], _NT) for h in range(H_A)]
        stats = []
        for h in range(H_A):
            m_old, l_old, _ = carry[h]
            s = scores[h]
            if key_off is not None:
                s = jnp.where(key + key_off <= qry, s, -jnp.inf)
            m_new = jnp.maximum(m_old, jnp.max(s, axis=0, keepdims=True))
            alpha = jnp.exp(m_old - m_new)
            p = jnp.exp(s - m_new)
            l_new = alpha * l_old + jnp.sum(p, axis=0, keepdims=True)
            stats.append((m_new, l_new, alpha, p.astype(bf16)))
        out = []
        for h in range(H_A):
            m_new, l_new, alpha, p = stats[h]
            acc_new = alpha * carry[h][2] + _dot(vt_ref[0, h, :, pl.ds(start, tk)], p)
            out.append((m_new, l_new, acc_new))
        return tuple(out)

    init = tuple((jnp.full((1, tq), -jnp.inf, f32), jnp.zeros((1, tq), f32),
                  jnp.zeros((DH_A, tq), f32)) for _ in range(H_A))
    carry = lax.fori_loop(0, i * per_q, step, init)
    for dblk in range(per_q):
        carry = step(i * per_q + dblk, carry, key_off=dblk * tk)
    for h in range(H_A):
        _, l, acc = carry[h]
        o_ref[0, h * DH_A:(h + 1) * DH_A, :] = (acc / l).astype(bf16)


def _fox_prompt(qat, kat, vt, cum):
    b, _, t, _ = qat.shape
    tq = min(ATTN_Q_TILE, t)
    tk = min(ATTN_K_TILE, tq)
    assert t % tq == 0 and tq % tk == 0
    return pl.pallas_call(
        functools.partial(_fox_kernel, tq=tq, tk=tk),
        grid=(b, t // tq),
        in_specs=[pl.BlockSpec((1, H_A, tq, DH_A), lambda bi, i: (bi, 0, i, 0)),
                  pl.BlockSpec((1, H_A, t, DH_A), lambda bi, i: (bi, 0, 0, 0)),
                  pl.BlockSpec((1, H_A, DH_A, t), lambda bi, i: (bi, 0, 0, 0)),
                  pl.BlockSpec((1, t, H_A), lambda bi, i: (bi, 0, 0))],
        out_specs=pl.BlockSpec((1, W_A, tq), lambda bi, i: (bi, 0, i)),
        out_shape=jax.ShapeDtypeStruct((b, W_A, t), bf16),
        scratch_shapes=[pltpu.VMEM((H_A, t, 2 * DH_A), bf16), pltpu.VMEM((H_A, tq, 2 * DH_A), bf16)],
        compiler_params=_cparams(("parallel", "arbitrary")),
        name="fox_prompt",
    )(qat, kat, vt, cum)


def _fox_decode_kernel(pt_ref, q_ref, knew_ref, vnew_ref, lfnew_ref, kt_hbm, vt_hbm, lft_hbm,
                       o_ref, pages, lfpages, p_scr, m_scr, self_scr, acc_scr, carry_scr,
                       sem_kv, sem_lf, *, pg, nstep, n_pages, db):
    b = pl.program_id(0)
    j = pl.program_id(1)
    q = q_ref[0]
    psz = lfpages.shape[-1]
    lane = lax.broadcasted_iota(i32, (H_A, W_A), 1)
    sub = lax.broadcasted_iota(i32, (H_A, W_A), 0)
    own = (lane // DH_A) == sub
    rnd = lambda z: z.astype(bf16).astype(f32)

    def start_load(seq, t, slot):
        def issue(src_hbm, dst, sem, group):
            for p in range(pg):
                page = pt_ref[seq * n_pages + group * pg + p]
                pltpu.make_async_copy(src_hbm.at[page], dst.at[slot, p], sem.at[slot]).start()

        @pl.when(t < nstep)
        def _():
            issue(kt_hbm, pages, sem_kv, nstep - 1 - t)
            issue(lft_hbm, lfpages, sem_lf, nstep - 1 - t)

        @pl.when(t >= nstep)
        def _():
            issue(vt_hbm, pages, sem_kv, 2 * nstep - 1 - t)

    step = b * (2 * nstep) + j
    slot = step % 2

    @pl.when(step == 0)
    def _():
        start_load(0, 0, 0)

    last = j == 2 * nstep - 1
    nxt_seq = jnp.where(last, b + 1, b)
    nxt_t = jnp.where(last, 0, j + 1)

    @pl.when(nxt_seq < db)
    def _():
        start_load(nxt_seq, nxt_t, 1 - slot)

    pltpu.make_async_copy(kt_hbm.at[pl.ds(0, pg)], pages.at[slot], sem_kv.at[slot]).wait()

    @pl.when(j < nstep)
    def _():
        pltpu.make_async_copy(lft_hbm.at[pl.ds(0, pg)], lfpages.at[slot], sem_lf.at[slot]).wait()

    @pl.when(j == 0)
    def _():
        s_self = jnp.sum(q.astype(f32) * rnd(knew_ref[0]), axis=1, keepdims=True)
        self_scr[...] = s_self
        m_scr[...] = s_self
        carry_scr[...] = lfnew_ref[0]

    @pl.when(j < nstep)
    def _():
        r = lax.broadcasted_iota(i32, (psz, 2 * psz), 0)
        c = lax.broadcasted_iota(i32, (psz, 2 * psz), 1)
        later = jnp.where(c < psz, jnp.where(r > c, 1.0, 0.0), 1.0).astype(bf16)
        group = nstep - 1 - j
        carry = carry_scr[...]
        m = m_scr[...]
        hi, mid, lo = _split3(lfpages[slot].reshape(pg * H_A, psz))
        suf_all = _dot(hi, later) + _dot(mid, later) + _dot(lo, later)
        for p in reversed(range(pg)):
            suf = suf_all[p * H_A:(p + 1) * H_A]
            kt = pages[slot, p].reshape(W_A, psz).astype(bf16)
            s = _dot(q, kt) + (carry + suf[:, :psz])
            carry = carry + suf[:, psz:psz + 1]
            p_scr[group * pg + p] = s
            m = jnp.maximum(m, jnp.max(s, axis=1, keepdims=True))
        carry_scr[...] = carry
        m_scr[...] = m

    @pl.when(j == nstep)
    def _():
        m = m_scr[...]
        e_self = jnp.exp(self_scr[...] - m)
        e = jnp.exp(p_scr[...] - m[None])
        l = e_self + jnp.sum(jnp.sum(e, axis=0), axis=1, keepdims=True)
        p_scr[...] = e / l[None]
        acc_scr[...] = jnp.where(own, rnd(e_self / l) * rnd(vnew_ref[0]), 0.0)

    @pl.when(j >= nstep)
    def _():
        group = 2 * nstep - 1 - j
        acc = acc_scr[...]
        for p in range(pg):
            vt = pages[slot, p].reshape(W_A, psz).astype(bf16)
            acc = acc + _dotg(p_scr[group * pg + p].astype(bf16), vt, _NT)
        acc_scr[...] = acc

    @pl.when(j == 2 * nstep - 1)
    def _():
        o_ref[0] = jnp.sum(jnp.where(own, acc_scr[...], 0.0), axis=0, keepdims=True)


def _fox_decode(qbd, knew, vnew, lfnew, cache_kt, cache_vt, cache_lft, page_table):
    db = qbd.shape[0]
    psz = cache_kt.shape[-1]
    n_pages = page_table.shape[1]
    pg = PAGES_PER_STEP if n_pages % PAGES_PER_STEP == 0 else 1
    nstep = n_pages // pg
    pt = page_table.reshape(-1).astype(i32)
    row3 = lambda a: pl.BlockSpec((1,) + a.shape[1:], lambda bi, j, pt_ref: (bi, 0, 0))
    anyspace = pl.BlockSpec(memory_space=pl.ANY)
    grid_spec = pltpu.PrefetchScalarGridSpec(
        num_scalar_prefetch=1,
        grid=(db, 2 * nstep),
        in_specs=[row3(qbd), row3(knew), row3(vnew), row3(lfnew), anyspace, anyspace, anyspace],
        out_specs=pl.BlockSpec((1, 1, W_A), lambda bi, j, pt_ref: (bi, 0, 0)),
        scratch_shapes=[pltpu.VMEM((2, pg, H_A, DH_A, psz), f32), pltpu.VMEM((2, pg, H_A, psz), f32),
                        pltpu.VMEM((n_pages, H_A, psz), f32), pltpu.VMEM((H_A, 1), f32),
                        pltpu.VMEM((H_A, 1), f32), pltpu.VMEM((H_A, W_A), f32),
                        pltpu.VMEM((H_A, 1), f32),
                        pltpu.SemaphoreType.DMA((2,)), pltpu.SemaphoreType.DMA((2,))])
    return pl.pallas_call(
        functools.partial(_fox_decode_kernel, pg=pg, nstep=nstep, n_pages=n_pages, db=db),
        grid_spec=grid_spec,
        out_shape=jax.ShapeDtypeStruct((db, 1, W_A), f32),
        compiler_params=_cparams(("arbitrary", "arbitrary")),
        name="fox_decode",
    )(pt, qbd, knew, vnew, lfnew, cache_kt, cache_vt, cache_lft)


def _gla_kernel(qb_ref, kb_ref, vb_ref, rb_ref, sm_ref, wal_ref, bal_ref, gg_ref,
                ob_ref, sout_ref, s_scr, *, tt):
    i = pl.program_id(1)

    @pl.when(i == 0)
    def _():
        s_scr[...] = jnp.zeros_like(s_scr)

    q = qb_ref[0]
    k = kb_ref[0]
    z = _dot(sm_ref[0].astype(bf16), wal_ref[...]) + bal_ref[...]
    a = _log_sigmoid(z) * (1.0 / GLA_TAU)

    row = lax.broadcasted_iota(i32, (tt, tt), 0)
    col = lax.broadcasted_iota(i32, (tt, tt), 1)
    tril = jnp.where(col <= row, 1.0, 0.0).astype(bf16)
    hi, mid, lo = _split3(a)
    b = _dot(tril, hi) + _dot(tril, mid) + _dot(tril, lo)

    rowk = lax.broadcasted_iota(i32, (tt, WK_B), 0)
    hs = [slice(h * DK_B, (h + 1) * DK_B) for h in range(H_B)]
    amat = [jnp.zeros((tt, tt), f32) for _ in range(H_B)]
    edge = b
    for lvl in range(tt.bit_length() - 1):
        half = 1 << lvl
        upper = ((rowk >> lvl) & 1) == 1
        kt = jnp.where(upper, 0.0, k * jnp.exp(edge - b)).astype(bf16)
        edge_q = pltpu.roll(edge, half, 0)
        qt = jnp.where(upper, q * jnp.exp(b - edge_q), 0.0).astype(bf16)
        same = (row >> (lvl + 1)) == (col >> (lvl + 1))
        for h in range(H_B):
            amat[h] = amat[h] + jnp.where(same, _dotg(qt[:, hs[h]], kt[:, hs[h]], _NT), 0.0)
        edge = jnp.where(upper, edge, pltpu.roll(edge, tt - half, 0))
    q16 = q.astype(bf16)
    k16 = k.astype(bf16)
    for h in range(H_B):
        amat[h] = amat[h] + jnp.where(row == col, _dotg(q16[:, hs[h]], k16[:, hs[h]], _NT), 0.0)

    qe = (q * jnp.exp(b)).astype(bf16)
    ke = (k * jnp.exp(edge - b)).astype(bf16)
    e_last = jnp.exp(edge[0:1, :])
    v16 = vb_ref[0].astype(bf16)
    r = rb_ref[0]
    er = lax.broadcasted_iota(i32, (DK_B, DK_B), 0)
    ec = lax.broadcasted_iota(i32, (DK_B, DK_B), 1)
    for h in range(H_B):
        vs = slice(h * DV_B, (h + 1) * DV_B)
        vh = v16[:, vs]
        state = s_scr[h]
        o = _dot(amat[h].astype(bf16), vh) + _dot(qe[:, hs[h]], state.astype(bf16))
        decay_col = jnp.sum(
            jnp.where(er == ec, jnp.broadcast_to(e_last[:, hs[h]], (DK_B, DK_B)), 0.0),
            axis=1, keepdims=True)
        s_scr[h] = decay_col * state + _dotg(ke[:, hs[h]], vh, _TN)
        rh = r[:, vs]
        ob_ref[0, :, vs] = (_rms(o) * gg_ref[:, vs] * (rh * _sigmoid(rh))).astype(bf16)

    @pl.when(i == pl.num_programs(1) - 1)
    def _():
        sout_ref[0] = s_scr[...]


def _gla_prompt(qb, kb, vb, rb, sm, wal, bal, gg):
    b, t, _ = qb.shape
    tt = min(GLA_TILE, t)
    assert t % tt == 0 and tt & (tt - 1) == 0
    tok3 = lambda w: pl.BlockSpec((1, tt, w), lambda bi, i: (bi, i, 0))
    const2 = lambda a: pl.BlockSpec(a.shape, lambda bi, i: (0, 0))
    return pl.pallas_call(
        functools.partial(_gla_kernel, tt=tt),
        grid=(b, t // tt),
        in_specs=[tok3(WK_B), tok3(WK_B), tok3(WV_B), tok3(WV_B), tok3(LANES),
                  const2(wal), const2(bal), const2(gg)],
        out_specs=(tok3(WV_B),
                   pl.BlockSpec((1, H_B, DK_B, DV_B), lambda bi, i: (bi, 0, 0, 0))),
        out_shape=(jax.ShapeDtypeStruct((b, t, WV_B), bf16),
                   jax.ShapeDtypeStruct((b, H_B, DK_B, DV_B), f32)),
        scratch_shapes=[pltpu.VMEM((H_B, DK_B, DV_B), f32)],
        compiler_params=_cparams(("parallel", "arbitrary")),
        name="gla_prompt",
    )(qb, kb, vb, rb, sm, wal, bal, gg)


def _loga_kernel(sm_ref, wal_ref, bal_ref, o_ref):
    z = _dot(sm_ref[...].astype(bf16), wal_ref[...]) + bal_ref[...]
    o_ref[...] = _log_sigmoid(z) * (1.0 / GLA_TAU)


def _gla_decode_kernel(q_ref, k_ref, a_ref, v_ref, r_ref, s_ref, gg_ref, o_ref, so_ref):
    for h in range(H_B):
        q, k, a = q_ref[0, h], k_ref[0, h], a_ref[0, h]
        v = v_ref[0, h]
        state = s_ref[0, h]
        ea = jnp.exp(a)
        rnd = lambda z: z.astype(bf16).astype(f32)
        qk = jnp.sum(q * k, axis=0, keepdims=True)
        o = qk * v + jnp.sum(rnd(q * ea) * rnd(state), axis=0, keepdims=True)
        so_ref[0, h] = ea * state + k * v
        r = r_ref[0, h]
        o_ref[0, h] = _rms(o) * gg_ref[h] * (r * _sigmoid(r))


def _gla_decode(qb, kb, vb, rb, sm, wal, bal, gg, state):
    db = qb.shape[0]
    full = lambda a: pl.BlockSpec(a.shape, lambda: (0,) * a.ndim)
    loga = pl.pallas_call(
        _loga_kernel,
        in_specs=[full(sm), full(wal), full(bal)],
        out_specs=pl.BlockSpec((db, WK_B), lambda: (0, 0)),
        out_shape=jax.ShapeDtypeStruct((db, WK_B), f32),
        name="gla_log_decay",
    )(sm, wal, bal)
    col = lambda a: a.reshape(db, H_B, DK_B, 1)
    rowv = lambda a: a.reshape(db, H_B, 1, DV_B)
    gg4 = gg.reshape(H_B, 1, DV_B)
    b4 = lambda shp: pl.BlockSpec((1,) + shp, lambda bi: (bi, 0, 0, 0))
    o, s_new = pl.pallas_call(
        _gla_decode_kernel,
        grid=(db,),
        in_specs=[b4((H_B, DK_B, 1))] * 3 + [b4((H_B, 1, DV_B))] * 2
        + [b4((H_B, DK_B, DV_B)), pl.BlockSpec(gg4.shape, lambda bi: (0, 0, 0))],
        out_specs=(b4((H_B, 1, DV_B)), b4((H_B, DK_B, DV_B))),
        out_shape=(jax.ShapeDtypeStruct((db, H_B, 1, DV_B), f32),
                   jax.ShapeDtypeStruct((db, H_B, DK_B, DV_B), f32)),
        compiler_params=_cparams(("parallel",)),
        name="gla_decode",
    )(col(qb), col(kb), col(loga), rowv(vb), rowv(rb), state, gg4)
    return o.reshape(db, WV_B), s_new


def _mix_kernel(x_ref, oa_ref, ob_ref, ga_ref, gb_ref, gt1_ref, sc2_ref, sh2_ref, gf_ref,
                wba_ref, wbb_ref, wo_ref, wr_ref, br_ref, x1_ref, h2_ref, lg_ref, *, nsub):
    x = x_ref[0]
    tm = x.shape[0]
    ya = _dotg(oa_ref[0].astype(bf16), wba_ref[...], _TN)
    yb = _dot(ob_ref[0].astype(bf16), wbb_ref[...])
    m = _sigmoid(ga_ref[0]) * ya + _sigmoid(gb_ref[0]) * yb
    x1 = x + gt1_ref[0] * _dot(m.astype(bf16), wo_ref[...])
    x1_ref[...] = x1
    h2 = (_rms(x1) * gf_ref[...]) * sc2_ref[0] + sh2_ref[0]
    hb = h2.astype(bf16)
    lg_ref[...] = _dot(hb, wr_ref[...]) + br_ref[...]
    half = nsub * LANES
    hi = lax.bitcast_convert_type(hb[:, :half].astype(f32), jnp.uint32)
    lo = lax.bitcast_convert_type(hb[:, half:].astype(f32), jnp.uint32)
    packed = jnp.bitwise_or(hi, jnp.right_shift(lo, jnp.uint32(16)))
    for s in range(nsub):
        h2_ref[pl.ds(s, tm, stride=nsub), :] = packed[:, s * LANES:(s + 1) * LANES]


def _mix(x, oa, ob, ga, gb, gt1, sc2, sh2, g_ffn, wba, wbb, wo, wr, br):
    b, t, d = x.shape
    nt = b * t
    nsub = d // (2 * LANES)
    tm = min(TOKEN_TILE, t)
    assert t % tm == 0 and d % (2 * LANES) == 0
    rows = gt1.shape[1]
    mod_block = (1, 1, d) if rows == 1 else (1, tm, d)
    mod_map = (lambda bi, i: (bi, 0, 0)) if rows == 1 else (lambda bi, i: (bi, i, 0))
    mod = pl.BlockSpec(mod_block, mod_map)
    tok3 = lambda w: pl.BlockSpec((1, tm, w), lambda bi, i: (bi, i, 0))
    const2 = lambda a: pl.BlockSpec(a.shape, lambda bi, i: (0, 0))
    nti = t // tm
    blk = lambda bi, i: (bi * nti + i, 0)
    g2 = g_ffn.reshape(1, d)
    ins = [x, oa, ob, ga, gb, gt1, sc2, sh2, g2, wba, wbb, wo, wr, br]
    in_specs = [tok3(d), pl.BlockSpec((1, W_A, tm), lambda bi, i: (bi, 0, i)), tok3(WV_B),
                tok3(d), tok3(d), mod, mod, mod,
                const2(g2), const2(wba), const2(wbb), const2(wo), const2(wr), const2(br)]
    return pl.pallas_call(
        functools.partial(_mix_kernel, nsub=nsub),
        grid=(b, nti),
        in_specs=in_specs,
        out_specs=(pl.BlockSpec((tm, d), blk), pl.BlockSpec((tm * nsub, LANES), blk),
                   pl.BlockSpec((tm, LANES), blk)),
        out_shape=(jax.ShapeDtypeStruct((nt, d), f32),
                   jax.ShapeDtypeStruct((nt * nsub, LANES), jnp.uint32),
                   jax.ShapeDtypeStruct((nt, LANES), f32)),
        compiler_params=_cparams(("parallel", "arbitrary")),
        name="branch_mix",
    )(*ins)


def _route_kernel(lg_ref, cin_ref, rec_ref, gate_ref, cnt_ref, carry_scr, *, tr):
    @pl.when(pl.program_id(0) == 0)
    def _():
        carry_scr[...] = cin_ref[...]

    lane = lax.broadcasted_iota(i32, (tr, LANES), 1)
    lanef = lane.astype(f32)
    v = jnp.where(lane < N_EXPERTS, lg_ref[...], -jnp.inf)
    onehot = jnp.zeros((tr, LANES), f32)
    ids, vals = [], []
    for _ in range(TOP_K):
        mx = jnp.max(v, axis=1, keepdims=True)
        idx = jnp.min(jnp.where(v == mx, lanef, float(LANES)), axis=1, keepdims=True)
        sel = lanef == idx
        onehot = jnp.where(sel, 1.0, onehot)
        v = jnp.where(sel, -jnp.inf, v)
        ids.append(idx)
        vals.append(mx)
    es = [jnp.exp(vk - vals[0]) for vk in vals]
    tot = es[0]
    for e in es[1:]:
        tot = tot + e

    r2 = lax.broadcasted_iota(i32, (tr, tr), 0)
    c2 = lax.broadcasted_iota(i32, (tr, tr), 1)
    before = jnp.where(c2 < r2, 1.0, 0.0).astype(bf16)
    pref = _dot(before, onehot.astype(bf16)) + carry_scr[...]
    carry_scr[...] = carry_scr[...] + jnp.sum(onehot, axis=0, keepdims=True)

    rec = jnp.zeros((tr, LANES), f32)
    gate = jnp.zeros((tr, LANES), f32)
    for k in range(TOP_K):
        rank = jnp.sum(jnp.where(lanef == ids[k], pref, 0.0), axis=1, keepdims=True)
        rec = jnp.where(lane == k, rank * float(N_EXPERTS) + ids[k], rec)
        gate = jnp.where(lane == k, es[k] / tot, gate)
    rec_ref[...] = rec.astype(i32)
    gate_ref[...] = gate
    cnt_ref[...] = carry_scr[...]


def _route(logits, counts_in):
    nt = logits.shape[0]
    tr = _divisor_tile(nt, 512, SUBLANES)
    tile = pl.BlockSpec((tr, LANES), lambda i: (i, 0))
    cnt = pl.BlockSpec((1, LANES), lambda i: (0, 0))
    return pl.pallas_call(
        functools.partial(_route_kernel, tr=tr),
        grid=(nt // tr,),
        in_specs=[tile, cnt],
        out_specs=(tile, tile, cnt),
        out_shape=(jax.ShapeDtypeStruct((nt, LANES), i32),
                   jax.ShapeDtypeStruct((nt, LANES), f32),
                   jax.ShapeDtypeStruct((1, LANES), f32)),
        scratch_shapes=[pltpu.VMEM((1, LANES), f32)],
        compiler_params=_cparams(("arbitrary",)),
        name="route_topk",
    )(logits, counts_in)


def _dest_row(rec_ref, start_ref, i):
    packed = rec_ref[i]
    return (start_ref[jnp.bitwise_and(packed, N_EXPERTS - 1)]
            + jnp.right_shift(packed, N_EXPERTS.bit_length() - 1))


def _dispatch_kernel(rec_ref, start_ref, h_ref, xin_hbm, xout_hbm, stage, sems, *, nsub, ch):
    del xin_hbm
    i = pl.program_id(0)
    slot = i % 2
    base = i * ch
    total = ch * TOP_K * nsub

    def drain(s):
        pltpu.make_async_copy(xout_hbm.at[pl.ds(0, total)], xout_hbm.at[pl.ds(0, total)],
                              sems.at[s]).wait()

    stage[slot] = h_ref[...]

    def body(n, carry):
        for k in range(TOP_K):
            src = pl.multiple_of(n * nsub, nsub)
            dst = pl.multiple_of(
                _dest_row(rec_ref, start_ref, (base + n) * TOP_K + k) * nsub, nsub)
            pltpu.make_async_copy(stage.at[slot, pl.ds(src, nsub)], xout_hbm.at[pl.ds(dst, nsub)],
                                  sems.at[slot]).start()
        return carry

    lax.fori_loop(0, ch, body, 0, unroll=4)

    @pl.when(i > 0)
    def _():
        drain(1 - slot)

    @pl.when(i == pl.num_programs(0) - 1)
    def _():
        drain(slot)


def _dispatch(rec_flat, pad_start, h2rows, xrows, nsub):
    n_tok = rec_flat.shape[0] // TOP_K
    ch = min(DISPATCH_CHUNK, n_tok)
    assert n_tok % ch == 0
    grid_spec = pltpu.PrefetchScalarGridSpec(
        num_scalar_prefetch=2,
        grid=(n_tok // ch,),
        in_specs=[pl.BlockSpec((ch * nsub, LANES), lambda i, rr, sr: (i, 0)),
                  pl.BlockSpec(memory_space=pl.ANY)],
        out_specs=pl.BlockSpec(memory_space=pl.ANY),
        scratch_shapes=[pltpu.VMEM((2, ch * nsub, LANES), h2rows.dtype),
                        pltpu.SemaphoreType.DMA((2,))])
    return pl.pallas_call(
        functools.partial(_dispatch_kernel, nsub=nsub, ch=ch),
        grid_spec=grid_spec,
        out_shape=jax.ShapeDtypeStruct(xrows.shape, xrows.dtype),
        input_output_aliases={3: 0},
        compiler_params=_cparams(("arbitrary",)),
        name="moe_dispatch",
    )(rec_flat, pad_start, h2rows, xrows)


def _expert_kernel(be_ref, nused_ref, x_ref, wgu_ref, bgu_ref, wd_ref, bd_ref, y_ref,
                   wgu_b, wd_b, *, tmx, nsub_x, nsub, dff, chunk):
    r = pl.program_id(0)
    live = r < nused_ref[0]

    @pl.when(jnp.logical_not(live))
    def _():
        y_ref[...] = jnp.zeros_like(y_ref)

    new_expert = jnp.logical_or(r == 0, be_ref[r] != be_ref[jnp.maximum(r - 1, 0)])

    @pl.when(jnp.logical_and(live, new_expert))
    def _():
        for c in range(0, wgu_b.shape[0], chunk):
            wgu_b[c:c + chunk, :] = wgu_ref[0, c:c + chunk, :].astype(bf16)
        for c in range(0, wd_b.shape[0], chunk):
            wd_b[c:c + chunk, :] = wd_ref[0, c:c + chunk, :].astype(bf16)

    @pl.when(live)
    def _():
        words = [x_ref[pl.ds(s, tmx, stride=nsub_x), :] for s in range(nsub_x)]
        as_bf16 = lambda w: lax.bitcast_convert_type(w, f32).astype(bf16)
        x = jnp.concatenate(
            [as_bf16(jnp.bitwise_and(w, jnp.uint32(0xFFFF0000))) for w in words]
            + [as_bf16(jnp.left_shift(w, jnp.uint32(16))) for w in words], axis=1)
        gu = _dot(x, wgu_b[...]) + bgu_ref[0]
        gate = jnp.minimum(gu[:, :dff], SWIGLU_LIMIT)
        up = jnp.clip(gu[:, dff:], -SWIGLU_LIMIT, SWIGLU_LIMIT)
        glu = gate * _sigmoid(SWIGLU_ALPHA * gate)
        y = _dot(((up + 1.0) * glu).astype(bf16), wd_b[...]) + bd_ref[0]
        for s in range(nsub):
            y_ref[pl.ds(s, tmx, stride=nsub), :] = y[:, s * LANES:(s + 1) * LANES]


def _experts(block_e, nused, xrows, wgu, bgu, wd, bd):
    n_blocks = block_e.shape[0]
    tmx = EXPERT_TILE
    e, d, dff2 = wgu.shape
    dff = dff2 // 2
    nsub = d // LANES
    nsub_x = d // (2 * LANES)
    live = lambda r, nu: jnp.minimum(r, nu[0] - 1)
    rows = pl.BlockSpec((tmx * nsub_x, LANES), lambda r, be, nu: (live(r, nu), 0))
    per_e = lambda shp: pl.BlockSpec((1,) + shp, lambda r, be, nu: (be[live(r, nu)], 0, 0))
    grid_spec = pltpu.PrefetchScalarGridSpec(
        num_scalar_prefetch=2,
        grid=(n_blocks,),
        in_specs=[rows, per_e((d, dff2)), per_e((1, dff2)), per_e((dff, d)), per_e((1, d))],
        out_specs=pl.BlockSpec((tmx * nsub, LANES), lambda r, be, nu: (r, 0)),
        scratch_shapes=[pltpu.VMEM((d, dff2), bf16), pltpu.VMEM((dff, d), bf16)])
    return pl.pallas_call(
        functools.partial(_expert_kernel, tmx=tmx, nsub_x=nsub_x, nsub=nsub, dff=dff,
                          chunk=min(256, d, dff)),
        grid_spec=grid_spec,
        out_shape=jax.ShapeDtypeStruct((n_blocks * tmx * nsub, LANES), f32),
        compiler_params=_cparams(("arbitrary",)),
        name="moe_experts",
    )(block_e, nused, xrows, wgu, bgu.reshape(e, 1, dff2), wd, bd.reshape(e, 1, d))


def _combine_kernel(rec_ref, start_ref, y_hbm, gate_ref, x1_ref, gt2_ref, gfin_ref, out_ref,
                    buf, sems, *, tc, nsub):
    i = pl.program_id(0)
    slot = i % 2

    def start_tile(tile, s):
        def body(n, carry):
            for k in range(TOP_K):
                src = pl.multiple_of(
                    _dest_row(rec_ref, start_ref, (tile * tc + n) * TOP_K + k) * nsub, nsub)
                dst = pl.multiple_of((k * tc + n) * nsub, nsub)
                pltpu.make_async_copy(y_hbm.at[pl.ds(src, nsub)], buf.at[s, pl.ds(dst, nsub)],
                                      sems.at[s]).start()
            return carry
        lax.fori_loop(0, tc, body, 0, unroll=4)

    @pl.when(i == 0)
    def _():
        start_tile(0, 0)

    @pl.when(i + 1 < pl.num_programs(0))
    def _():
        start_tile(i + 1, 1 - slot)

    pltpu.make_async_copy(y_hbm.at[pl.ds(0, buf.shape[1])], buf.at[slot], sems.at[slot]).wait()

    g = gate_ref[...]
    cols = []
    for s in range(nsub):
        acc = None
        for k in range(TOP_K):
            term = g[:, k:k + 1] * buf[slot, pl.ds(k * tc * nsub + s, tc, stride=nsub), :]
            acc = term if acc is None else acc + term
        cols.append(acc)
    y = jnp.concatenate(cols, axis=1)
    out_ref[...] = _rms(x1_ref[...] + gt2_ref[0] * y) * gfin_ref[...]


def _combine(rec_flat, pad_start, yrows, gates, x1, gt2, g_final, t_per_mod, nsub):
    n_tok, d = x1.shape
    tc = min(TOKEN_TILE, n_tok)
    assert n_tok % tc == 0
    rows = gt2.shape[1]
    if rows == 1:
        per_mod = t_per_mod // tc
        mod = pl.BlockSpec((1, 1, d), lambda i, rr, sr: (i // per_mod, 0, 0))
    else:
        mod = pl.BlockSpec((1, tc, d), lambda i, rr, sr: (0, i, 0))
    blk = lambda i, rr, sr: (i, 0)
    gfin = g_final.reshape(1, d)
    grid_spec = pltpu.PrefetchScalarGridSpec(
        num_scalar_prefetch=2,
        grid=(n_tok // tc,),
        in_specs=[pl.BlockSpec(memory_space=pl.ANY), pl.BlockSpec((tc, LANES), blk),
                  pl.BlockSpec((tc, d), blk), mod,
                  pl.BlockSpec((1, d), lambda i, rr, sr: (0, 0))],
        out_specs=pl.BlockSpec((tc, d), blk),
        scratch_shapes=[pltpu.VMEM((2, TOP_K * tc * nsub, LANES), f32),
                        pltpu.SemaphoreType.DMA((2,))])
    return pl.pallas_call(
        functools.partial(_combine_kernel, tc=tc, nsub=nsub),
        grid_spec=grid_spec,
        out_shape=jax.ShapeDtypeStruct((n_tok, d), f32),
        compiler_params=_cparams(("arbitrary",)),
        name="moe_combine",
    )(rec_flat, pad_start, yrows, gates, x1, gt2, gfin)


def kernel(x_prompt, x_sample, c_prompt, c_sample, cache_k, cache_v, cache_logf, state_gla,
           page_table, g_mix, g_ffn, g_final, w_ada, b_ada, w_in, b_f, w_alpha, b_alpha, g_gla,
           w_branch, w_o, w_router, b_router, w_gu, b_gu, w_d, b_d):
    depth = g_mix.shape[0]
    assert depth == 1, "one decoder layer"
    bp, t, d = x_prompt.shape
    db = x_sample.shape[0]
    assert x_sample.shape[1] == 1
    nsub = d // LANES
    ntp = bp * t
    nt = ntp + db

    nmod = bp + db
    pad = (-nmod) % SUBLANES
    c_all = jnp.concatenate([c_prompt, c_sample, jnp.zeros((pad, d), f32)], axis=0)
    mod = _adaln(c_all, w_ada[0], b_ada[0])
    sh1, sc1, gt1, sh2, sc2, gt2 = [mod[:, i * d:(i + 1) * d] for i in range(6)]
    grp_p = lambda a: a[:bp].reshape(bp, 1, d)
    grp_s = lambda a: a[bp:nmod].reshape(1, db, d)

    w_proj = _inproj_weights(w_in[0], d)
    wal = jnp.zeros((LANES, WK_B), f32).at[H_A:H_A + GLA_RANK].set(w_alpha[0]).astype(bf16)
    bal = b_alpha[0].reshape(1, WK_B)
    gg = g_gla[0].reshape(1, WV_B)
    wba = w_branch[0, :W_A].astype(bf16)
    wbb = w_branch[0, W_A:].astype(bf16)
    wo = w_o[0].astype(bf16)
    wr = jnp.zeros((d, LANES), f32).at[:, :N_EXPERTS].set(w_router[0]).astype(bf16)
    br = jnp.zeros((1, LANES), f32).at[0, :N_EXPERTS].set(b_router[0])

    (qat, kat, vt, ka, va, qb, kb, vb, rb, ga, gb, sm, fat) = _inproj(
        x_prompt, grp_p(1.0 + sc1), grp_p(sh1), g_mix[0], w_proj, True)
    logft, cumt = _logf(fat, b_f[0], True)
    oa = _fox_prompt(qat, kat, vt, jnp.transpose(cumt, (0, 2, 1)))
    ob, gla_p = _gla_prompt(qb, kb, vb, rb, sm, wal, bal, gg)
    x1_p, h2_p, lg_p = _mix(x_prompt, oa, ob, ga, gb, grp_p(gt1), grp_p(1.0 + sc2), grp_p(sh2),
                            g_ffn[0], wba, wbb, wo, wr, br)

    xs = x_sample.reshape(1, db, d)
    (qat_s, ka_s, va_s, qb_s, kb_s, vb_s, rb_s, ga_s, gb_s, sm_s, fat_s) = _inproj(
        xs, grp_s(1.0 + sc1), grp_s(sh1), g_mix[0], w_proj, False)
    logft_s, = _logf(fat_s, b_f[0], False)
    lfnew = jnp.transpose(logft_s[0], (1, 0)).reshape(db, H_A, 1)
    qbd = jnp.einsum('hnd,hc->nchd', qat_s[0], jnp.eye(H_A, dtype=bf16)).reshape(db, H_A, W_A)
    oa_s = _fox_decode(qbd, ka_s.reshape(db, 1, W_A), va_s.reshape(db, 1, W_A), lfnew,
                       jnp.transpose(cache_k[0], (0, 2, 3, 1)),
                       jnp.transpose(cache_v[0], (0, 2, 3, 1)),
                       jnp.transpose(cache_logf[0], (0, 2, 1)), page_table)
    ob_s, gla_s = _gla_decode(qb_s[0], kb_s[0], vb_s[0], rb_s[0], sm_s[0], wal, bal, gg,
                              state_gla[0])
    x1_s, h2_s, lg_s = _mix(xs, oa_s.reshape(db, W_A).T[None], ob_s.reshape(1, db, WV_B), ga_s, gb_s,
                            grp_s(gt1), grp_s(1.0 + sc2), grp_s(sh2), g_ffn[0],
                            wba, wbb, wo, wr, br)

    rec_p, gates_p, cnt_p = _route(lg_p, jnp.zeros((1, LANES), f32))
    rec_s, gates_s, cnt = _route(lg_s, cnt_p)
    counts = cnt[0, :N_EXPERTS].astype(i32)
    padded = (counts + EXPERT_TILE - 1) // EXPERT_TILE * EXPERT_TILE
    pad_end = jnp.cumsum(padded).astype(i32)
    pad_start = pad_end - padded

    rflat_p = rec_p[:, :TOP_K].reshape(-1)
    rflat_s = rec_s[:, :TOP_K].reshape(-1)
    n_blocks = -(-(nt * TOP_K + N_EXPERTS * (EXPERT_TILE - 1)) // EXPERT_TILE)
    block_start = jnp.arange(n_blocks, dtype=i32) * EXPERT_TILE
    block_e = jnp.minimum(jnp.sum(pad_end[None, :] <= block_start[:, None], axis=1),
                          N_EXPERTS - 1).astype(i32)
    nused = (pad_end[-1:] // EXPERT_TILE).astype(i32)

    nsub_x = d // (2 * LANES)
    xrows = jnp.zeros((n_blocks * EXPERT_TILE * nsub_x, LANES), jnp.uint32)
    xrows = _dispatch(rflat_p, pad_start, h2_p, xrows, nsub_x)
    xrows = _dispatch(rflat_s, pad_start, h2_s, xrows, nsub_x)
    yrows = _experts(block_e, nused, xrows, w_gu[0], b_gu[0], w_d[0], b_d[0])
    y_p = _combine(rflat_p, pad_start, yrows, gates_p, x1_p, grp_p(gt2), g_final, t, nsub)
    y_s = _combine(rflat_s, pad_start, yrows, gates_s, x1_s, grp_s(gt2), g_final, 1, nsub)

    logf_p = jnp.transpose(logft, (0, 2, 1))
    return (y_p.reshape(bp, t, d), y_s.reshape(db, 1, d),
            jnp.transpose(ka, (0, 3, 1, 2))[None], jnp.transpose(va, (0, 3, 1, 2))[None],
            logf_p.reshape(1, bp, t, H_A), gla_p.reshape(1, bp, H_B, DK_B, DV_B),
            ka_s.reshape(1, db, 1, H_A, DH_A), va_s.reshape(1, db, 1, H_A, DH_A),
            lfnew.reshape(1, db, 1, H_A), gla_s.reshape(1, db, H_B, DK_B, DV_B))
```

```python
import functools

import jax
import jax.numpy as jnp
from jax import lax
from jax.experimental import pallas as pl
from jax.experimental.pallas import tpu as pltpu

f32 = jnp.float32
bf16 = jnp.bfloat16
i32 = jnp.int32

H_A, DH_A = 8, 64
H_B, DK_B, DV_B = 4, 64, 128
GLA_RANK = 16
GLA_TAU = 16.0
N_EXPERTS = 32
TOP_K = 4
SWIGLU_LIMIT = 7.0
SWIGLU_ALPHA = 1.702
RMS_EPS = 1e-6
W_A = H_A * DH_A
WK_B = H_B * DK_B
WV_B = H_B * DV_B

LANES = 128
SUBLANES = 8
VMEM_LIMIT_MB = 56

TOKEN_TILE = 256
ATTN_Q_TILE = 512
ATTN_K_TILE = 256
GLA_TILE = 256
EXPERT_TILE = 512
PAGES_PER_STEP = 32
DISPATCH_CHUNK = 256

_NT = (((1,), (1,)), ((), ()))
_TN = (((0,), (0,)), ((), ()))


def _cparams(semantics, vmem_mb=VMEM_LIMIT_MB):
    return pltpu.CompilerParams(dimension_semantics=semantics,
                                vmem_limit_bytes=vmem_mb << 20)


def _sigmoid(x):
    return 1.0 / (1.0 + jnp.exp(-x))


def _log_sigmoid(x):
    return jnp.minimum(x, 0.0) - jnp.log1p(jnp.exp(-jnp.abs(x)))


def _split3(x):
    hi = x.astype(bf16)
    r = x - hi.astype(f32)
    mid = r.astype(bf16)
    lo = (r - mid.astype(f32)).astype(bf16)
    return hi, mid, lo


def _dot(a, b):
    return jnp.dot(a, b, preferred_element_type=f32)


def _dotg(a, b, dims):
    return lax.dot_general(a, b, dims, preferred_element_type=f32)


def _rms(x):
    return x * lax.rsqrt(jnp.mean(x * x, axis=-1, keepdims=True) + RMS_EPS)


def _divisor_tile(n, cap, mult):
    best = None
    t = mult
    while t <= min(n, cap):
        if n % t == 0:
            best = t
        t += mult
    assert best is not None, (n, cap, mult)
    return best


def _mod_kernel(c_ref, w_ref, b_ref, o_ref):
    c = c_ref[...]
    s = (c * _sigmoid(c)).astype(bf16)
    o_ref[...] = _dot(s, w_ref[...].astype(bf16)) + b_ref[...]


def _adaln(c_all, w_ada, b_ada):
    r, d = c_all.shape
    n = w_ada.shape[1]
    tn = _divisor_tile(n, 1536, LANES)
    return pl.pallas_call(
        _mod_kernel,
        grid=(n // tn,),
        in_specs=[pl.BlockSpec((r, d), lambda j: (0, 0)),
                  pl.BlockSpec((d, tn), lambda j: (0, j)),
                  pl.BlockSpec((1, tn), lambda j: (0, j))],
        out_specs=pl.BlockSpec((r, tn), lambda j: (0, j)),
        out_shape=jax.ShapeDtypeStruct((r, n), f32),
        compiler_params=_cparams(("arbitrary",)),
        name="adaln_mod",
    )(c_all, w_ada, b_ada.reshape(1, n))


def _inproj_kernel(x_ref, sc_ref, sh_ref, g_ref, wbig_ref, wsm_ref, wfa_ref, wkt_ref, wvt_ref,
                   *out_refs, d, time_minor):
    if time_minor:
        (qat_ref, kat_ref, vt_ref, ka_ref, va_ref, qb_ref, kb_ref, vb_ref, rb_ref,
         ga_ref, gb_ref, sm_ref, fat_ref) = out_refs
    else:
        (qat_ref, ka_ref, va_ref, qb_ref, kb_ref, vb_ref, rb_ref,
         ga_ref, gb_ref, sm_ref, fat_ref) = out_refs
    x = x_ref[0]
    h = (_rms(x) * g_ref[...]) * sc_ref[0] + sh_ref[0]
    hb = h.astype(bf16)
    off = [0]

    def mm(width):
        r = _dot(hb, wbig_ref[:, off[0]:off[0] + width])
        off[0] += width
        return r

    qa = mm(W_A)
    ka = mm(W_A)
    va = mm(W_A)
    for hh in range(H_A):
        sl = slice(hh * DH_A, (hh + 1) * DH_A)
        qat_ref[0, hh] = qa[:, sl].astype(bf16)
        if time_minor:
            kat_ref[0, hh] = ka[:, sl].astype(bf16)
    if time_minor:
        tm = x.shape[0]
        kt = _dotg(wkt_ref[...], hb, _NT).reshape(H_A, DH_A, tm)
        vt = _dotg(wvt_ref[...], hb, _NT).reshape(H_A, DH_A, tm)
        ka_ref[0] = kt
        va_ref[0] = vt
        vt_ref[0] = vt.astype(bf16)
    else:
        ka_ref[0] = ka
        va_ref[0] = va
    qb_ref[0] = mm(WK_B)
    kb_ref[0] = mm(WK_B)
    vb_ref[0] = mm(WV_B)
    rb_ref[0] = mm(WV_B)
    ga_ref[0] = mm(d)
    gb_ref[0] = mm(d)
    sm_ref[0] = _dot(hb, wsm_ref[...])
    fat_ref[0] = _dotg(wfa_ref[...].astype(bf16), hb, _NT)


def _inproj_weights(w_in, d):
    sizes = (W_A, W_A, W_A, H_A, WK_B, WK_B, WV_B, WV_B, GLA_RANK, d, d)
    offs = [0]
    for s in sizes:
        offs.append(offs[-1] + s)
    seg = lambda i: w_in[:, offs[i]:offs[i + 1]]
    wbig = jnp.concatenate(
        [seg(0) * (DH_A ** -0.5), seg(1), seg(2), seg(4) * (DK_B ** -0.5), seg(5),
         seg(6), seg(7), seg(9), seg(10)], axis=1).astype(bf16)
    wsm = jnp.concatenate(
        [seg(3), seg(8), jnp.zeros((d, LANES - H_A - GLA_RANK), f32)], axis=1).astype(bf16)
    wfa = seg(3).T
    wkt = seg(1).T.astype(bf16)
    wvt = seg(2).T.astype(bf16)
    return wbig, wsm, wfa, wkt, wvt


def _inproj(x, sc, sh, g_mix, weights, time_minor):
    wbig, wsm, wfa, wkt, wvt = weights
    b, t, d = x.shape
    tm = min(TOKEN_TILE, t)
    assert t % tm == 0
    rows = sc.shape[1]
    mod_block = (1, 1, d) if rows == 1 else (1, tm, d)
    mod_map = (lambda bi, i: (bi, 0, 0)) if rows == 1 else (lambda bi, i: (bi, i, 0))
    tok3 = lambda w: pl.BlockSpec((1, tm, w), lambda bi, i: (bi, i, 0))
    head4 = pl.BlockSpec((1, H_A, tm, DH_A), lambda bi, i: (bi, 0, i, 0))
    const2 = lambda a: pl.BlockSpec(a.shape, lambda bi, i: (0, 0))
    sds = jax.ShapeDtypeStruct
    headt = pl.BlockSpec((1, H_A, DH_A, tm), lambda bi, i: (bi, 0, 0, i))
    if time_minor:
        kv_shape = [sds((b, H_A, t, DH_A), bf16), sds((b, H_A, DH_A, t), bf16),
                    sds((b, H_A, DH_A, t), f32), sds((b, H_A, DH_A, t), f32)]
        kv_specs = [head4, headt, headt, headt]
    else:
        kv_shape = [sds((b, t, W_A), f32), sds((b, t, W_A), f32)]
        kv_specs = [tok3(W_A), tok3(W_A)]
    out_shape = tuple(
        [sds((b, H_A, t, DH_A), bf16)] + kv_shape
        + [sds((b, t, WK_B), f32), sds((b, t, WK_B), f32),
           sds((b, t, WV_B), f32), sds((b, t, WV_B), f32),
           sds((b, t, d), f32), sds((b, t, d), f32),
           sds((b, t, LANES), f32), sds((b, H_A, t), f32)])
    out_specs = tuple(
        [head4] + kv_specs
        + [tok3(WK_B), tok3(WK_B), tok3(WV_B), tok3(WV_B), tok3(d), tok3(d), tok3(LANES),
           pl.BlockSpec((1, H_A, tm), lambda bi, i: (bi, 0, i))])
    g2 = g_mix.reshape(1, d)
    return pl.pallas_call(
        functools.partial(_inproj_kernel, d=d, time_minor=time_minor),
        grid=(b, t // tm),
        in_specs=[tok3(d), pl.BlockSpec(mod_block, mod_map), pl.BlockSpec(mod_block, mod_map),
                  const2(g2), const2(wbig), const2(wsm), const2(wfa), const2(wkt), const2(wvt)],
        out_specs=out_specs,
        out_shape=out_shape,
        compiler_params=_cparams(("parallel", "arbitrary")),
        name="inproj",
    )(x, sc, sh, g2, wbig, wsm, wfa, wkt, wvt)


def _logf_kernel(fa_ref, bf_ref, logf_ref, *cum_refs, t, cb):
    lf = _log_sigmoid(fa_ref[...] + bf_ref[...])
    logf_ref[...] = lf
    if not cum_refs:
        return
    cum_ref, = cum_refs
    r = lax.broadcasted_iota(i32, (cb, cb), 0)
    c = lax.broadcasted_iota(i32, (cb, cb), 1)
    triu = jnp.where(r <= c, 1.0, 0.0).astype(bf16)
    carry = jnp.zeros((lf.shape[0], 1), f32)
    for j in range(t // cb):
        hi, mid, lo = _split3(lf[:, j * cb:(j + 1) * cb])
        blk = _dot(hi, triu) + _dot(mid, triu) + _dot(lo, triu) + carry
        cum_ref[:, j * cb:(j + 1) * cb] = blk
        carry = blk[:, cb - 1:cb]


def _logf(fat, b_f, with_cumsum):
    b, _, t = fat.shape
    rows = b * H_A
    fa2 = fat.reshape(rows, t)
    bf2 = jnp.tile(b_f.reshape(H_A, 1), (b, 1))
    cb = min(256, t)
    full = lambda a: pl.BlockSpec(a.shape, lambda: (0,) * a.ndim)
    n_out = 2 if with_cumsum else 1
    outs = pl.pallas_call(
        functools.partial(_logf_kernel, t=t, cb=cb),
        in_specs=[full(fa2), full(bf2)],
        out_specs=tuple(pl.BlockSpec((rows, t), lambda: (0, 0)) for _ in range(n_out)),
        out_shape=tuple(jax.ShapeDtypeStruct((rows, t), f32) for _ in range(n_out)),
        name="log_forget",
    )(fa2, bf2)
    return tuple(o.reshape(b, H_A, t) for o in outs)


def _bias_lanes(col, ones_first):
    hi, mid, lo = [p.astype(f32) for p in _split3(col)]
    lane = lax.broadcasted_iota(i32, (col.shape[0], DH_A), 1)
    base = 3 if ones_first else 0
    parts = jnp.where(lane == base, hi, jnp.where(lane == base + 1, mid,
                      jnp.where(lane == base + 2, lo, 0.0)))
    ones = (lane < 3) if ones_first else ((lane >= 3) & (lane < 6))
    return jnp.where(ones, 1.0, parts)


def _fox_kernel(q_ref, k_ref, vt_ref, cum_ref, o_ref, kaug, qaug, *, tq, tk):
    i = pl.program_id(1)
    per_q = tq // tk

    @pl.when(i == 0)
    def _():
        cum_all = cum_ref[0]
        for h in range(H_A):
            kaug[h, :, :DH_A] = k_ref[0, h]
            kaug[h, :, DH_A:] = _bias_lanes(-cum_all[:, h:h + 1], False).astype(bf16)

    cum_q = cum_ref[0, pl.ds(pl.multiple_of(i * tq, tq), tq), :]
    for h in range(H_A):
        qaug[h, :, :DH_A] = q_ref[0, h]
        qaug[h, :, DH_A:] = _bias_lanes(cum_q[:, h:h + 1], True).astype(bf16)
    key = lax.broadcasted_iota(i32, (tk, tq), 0)
    qry = lax.broadcasted_iota(i32, (tk, tq), 1)

    def step(j, carry, key_off=None):
        start = pl.multiple_of(j * tk, tk)
        scores = [_dotg(kaug[h, pl.ds(start, tk), :], qaug[h], _NT) for h in range(H_A)]
        stats = []
        for h in range(H_A):
            m_old, l_old, _ = carry[h]
            s = scores[h]
            if key_off is not None:
                s = jnp.where(key + key_off <= qry, s, -jnp.inf)
            m_new = jnp.maximum(m_old, jnp.max(s, axis=0, keepdims=True))
            alpha = jnp.exp(m_old - m_new)
            p = jnp.exp(s - m_new)
            l_new = alpha * l_old + jnp.sum(p, axis=0, keepdims=True)
            stats.append((m_new, l_new, alpha, p.astype(bf16)))
        out = []
        for h in range(H_A):
            m_new, l_new, alpha, p = stats[h]
            acc_new = alpha * carry[h][2] + _dot(vt_ref[0, h, :, pl.ds(start, tk)], p)
            out.append((m_new, l_new, acc_new))
        return tuple(out)

    init = tuple((jnp.full((1, tq), -jnp.inf, f32), jnp.zeros((1, tq), f32),
                  jnp.zeros((DH_A, tq), f32)) for _ in range(H_A))
    carry = lax.fori_loop(0, i * per_q, step, init)
    for dblk in range(per_q):
        carry = step(i * per_q + dblk, carry, key_off=dblk * tk)
    for h in range(H_A):
        _, l, acc = carry[h]
        o_ref[0, h * DH_A:(h + 1) * DH_A, :] = (acc / l).astype(bf16)


def _fox_prompt(qat, kat, vt, cum):
    b, _, t, _ = qat.shape
    tq = min(ATTN_Q_TILE, t)
    tk = min(ATTN_K_TILE, tq)
    assert t % tq == 0 and tq % tk == 0
    return pl.pallas_call(
        functools.partial(_fox_kernel, tq=tq, tk=tk),
        grid=(b, t // tq),
        in_specs=[pl.BlockSpec((1, H_A, tq, DH_A), lambda bi, i: (bi, 0, i, 0)),
                  pl.BlockSpec((1, H_A, t, DH_A), lambda bi, i: (bi, 0, 0, 0)),
                  pl.BlockSpec((1, H_A, DH_A, t), lambda bi, i: (bi, 0, 0, 0)),
                  pl.BlockSpec((1, t, H_A), lambda bi, i: (bi, 0, 0))],
        out_specs=pl.BlockSpec((1, W_A, tq), lambda bi, i: (bi, 0, i)),
        out_shape=jax.ShapeDtypeStruct((b, W_A, t), bf16),
        scratch_shapes=[pltpu.VMEM((H_A, t, 2 * DH_A), bf16), pltpu.VMEM((H_A, tq, 2 * DH_A), bf16)],
        compiler_params=_cparams(("parallel", "arbitrary")),
        name="fox_prompt",
    )(qat, kat, vt, cum)


def _fox_decode_kernel(pt_ref, q_ref, knew_ref, vnew_ref, lfnew_ref, kt_hbm, vt_hbm, lft_hbm,
                       o_ref, pages, lfpages, p_scr, m_scr, self_scr, acc_scr, carry_scr,
                       sem_kv, sem_lf, *, pg, nstep, n_pages, db):
    b = pl.program_id(0)
    j = pl.program_id(1)
    q = q_ref[0]
    psz = lfpages.shape[-1]
    lane = lax.broadcasted_iota(i32, (H_A, W_A), 1)
    sub = lax.broadcasted_iota(i32, (H_A, W_A), 0)
    own = (lane // DH_A) == sub
    rnd = lambda z: z.astype(bf16).astype(f32)

    def start_load(seq, t, slot):
        def issue(src_hbm, dst, sem, group):
            for p in range(pg):
                page = pt_ref[seq * n_pages + group * pg + p]
                pltpu.make_async_copy(src_hbm.at[page], dst.at[slot, p], sem.at[slot]).start()

        @pl.when(t < nstep)
        def _():
            issue(kt_hbm, pages, sem_kv, nstep - 1 - t)
            issue(lft_hbm, lfpages, sem_lf, nstep - 1 - t)

        @pl.when(t >= nstep)
        def _():
            issue(vt_hbm, pages, sem_kv, 2 * nstep - 1 - t)

    step = b * (2 * nstep) + j
    slot = step % 2

    @pl.when(step == 0)
    def _():
        start_load(0, 0, 0)

    last = j == 2 * nstep - 1
    nxt_seq = jnp.where(last, b + 1, b)
    nxt_t = jnp.where(last, 0, j + 1)

    @pl.when(nxt_seq < db)
    def _():
        start_load(nxt_seq, nxt_t, 1 - slot)

    pltpu.make_async_copy(kt_hbm.at[pl.ds(0, pg)], pages.at[slot], sem_kv.at[slot]).wait()

    @pl.when(j < nstep)
    def _():
        pltpu.make_async_copy(lft_hbm.at[pl.ds(0, pg)], lfpages.at[slot], sem_lf.at[slot]).wait()

    @pl.when(j == 0)
    def _():
        s_self = jnp.sum(q.astype(f32) * rnd(knew_ref[0]), axis=1, keepdims=True)
        self_scr[...] = s_self
        m_scr[...] = s_self
        carry_scr[...] = lfnew_ref[0]

    @pl.when(j < nstep)
    def _():
        r = lax.broadcasted_iota(i32, (psz, 2 * psz), 0)
        c = lax.broadcasted_iota(i32, (psz, 2 * psz), 1)
        later = jnp.where(c < psz, jnp.where(r > c, 1.0, 0.0), 1.0).astype(bf16)
        group = nstep - 1 - j
        carry = carry_scr[...]
        m = m_scr[...]
        hi, mid, lo = _split3(lfpages[slot].reshape(pg * H_A, psz))
        suf_all = _dot(hi, later) + _dot(mid, later) + _dot(lo, later)
        for p in reversed(range(pg)):
            suf = suf_all[p * H_A:(p + 1) * H_A]
            kt = pages[slot, p].reshape(W_A, psz).astype(bf16)
            s = _dot(q, kt) + (carry + suf[:, :psz])
            carry = carry + suf[:, psz:psz + 1]
            p_scr[group * pg + p] = s
            m = jnp.maximum(m, jnp.max(s, axis=1, keepdims=True))
        carry_scr[...] = carry
        m_scr[...] = m

    @pl.when(j == nstep)
    def _():
        m = m_scr[...]
        e_self = jnp.exp(self_scr[...] - m)
        e = jnp.exp(p_scr[...] - m[None])
        l = e_self + jnp.sum(jnp.sum(e, axis=0), axis=1, keepdims=True)
        p_scr[...] = e / l[None]
        acc_scr[...] = jnp.where(own, rnd(e_self / l) * rnd(vnew_ref[0]), 0.0)

    @pl.when(j >= nstep)
    def _():
        group = 2 * nstep - 1 - j
        acc = acc_scr[...]
        for p in range(pg):
            vt = pages[slot, p].reshape(W_A, psz).astype(bf16)
            acc = acc + _dotg(p_scr[group * pg + p].astype(bf16), vt, _NT)
        acc_scr[...] = acc

    @pl.when(j == 2 * nstep - 1)
    def _():
        o_ref[0] = jnp.sum(jnp.where(own, acc_scr[...], 0.0), axis=0, keepdims=True)


def _fox_decode(qbd, knew, vnew, lfnew, cache_kt, cache_vt, cache_lft, page_table):
    db = qbd.shape[0]
    psz = cache_kt.shape[-1]
    n_pages = page_table.shape[1]
    pg = PAGES_PER_STEP if n_pages % PAGES_PER_STEP == 0 else 1
    nstep = n_pages // pg
    pt = page_table.reshape(-1).astype(i32)
    row3 = lambda a: pl.BlockSpec((1,) + a.shape[1:], lambda bi, j, pt_ref: (bi, 0, 0))
    anyspace = pl.BlockSpec(memory_space=pl.ANY)
    grid_spec = pltpu.PrefetchScalarGridSpec(
        num_scalar_prefetch=1,
        grid=(db, 2 * nstep),
        in_specs=[row3(qbd), row3(knew), row3(vnew), row3(lfnew), anyspace, anyspace, anyspace],
        out_specs=pl.BlockSpec((1, 1, W_A), lambda bi, j, pt_ref: (bi, 0, 0)),
        scratch_shapes=[pltpu.VMEM((2, pg, H_A, DH_A, psz), f32), pltpu.VMEM((2, pg, H_A, psz), f32),
                        pltpu.VMEM((n_pages, H_A, psz), f32), pltpu.VMEM((H_A, 1), f32),
                        pltpu.VMEM((H_A, 1), f32), pltpu.VMEM((H_A, W_A), f32),
                        pltpu.VMEM((H_A, 1), f32),
                        pltpu.SemaphoreType.DMA((2,)), pltpu.SemaphoreType.DMA((2,))])
    return pl.pallas_call(
        functools.partial(_fox_decode_kernel, pg=pg, nstep=nstep, n_pages=n_pages, db=db),
        grid_spec=grid_spec,
        out_shape=jax.ShapeDtypeStruct((db, 1, W_A), f32),
        compiler_params=_cparams(("arbitrary", "arbitrary")),
        name="fox_decode",
    )(pt, qbd, knew, vnew, lfnew, cache_kt, cache_vt, cache_lft)


def _gla_kernel(qb_ref, kb_ref, vb_ref, rb_ref, sm_ref, wal_ref, bal_ref, gg_ref,
                ob_ref, sout_ref, s_scr, *, tt):
    i = pl.program_id(1)

    @pl.when(i == 0)
    def _():
        s_scr[...] = jnp.zeros_like(s_scr)

    q = qb_ref[0]
    k = kb_ref[0]
    z = _dot(sm_ref[0].astype(bf16), wal_ref[...]) + bal_ref[...]
    a = _log_sigmoid(z) * (1.0 / GLA_TAU)

    row = lax.broadcasted_iota(i32, (tt, tt), 0)
    col = lax.broadcasted_iota(i32, (tt, tt), 1)
    tril = jnp.where(col <= row, 1.0, 0.0).astype(bf16)
    hi, mid, lo = _split3(a)
    b = _dot(tril, hi) + _dot(tril, mid) + _dot(tril, lo)

    rowk = lax.broadcasted_iota(i32, (tt, WK_B), 0)
    hs = [slice(h * DK_B, (h + 1) * DK_B) for h in range(H_B)]
    amat = [jnp.zeros((tt, tt), f32) for _ in range(H_B)]
    edge = b
    for lvl in range(tt.bit_length() - 1):
        half = 1 << lvl
        upper = ((rowk >> lvl) & 1) == 1
        kt = jnp.where(upper, 0.0, k * jnp.exp(edge - b)).astype(bf16)
        edge_q = pltpu.roll(edge, half, 0)
        qt = jnp.where(upper, q * jnp.exp(b - edge_q), 0.0).astype(bf16)
        same = (row >> (lvl + 1)) == (col >> (lvl + 1))
        for h in range(H_B):
            amat[h] = amat[h] + jnp.where(same, _dotg(qt[:, hs[h]], kt[:, hs[h]], _NT), 0.0)
        edge = jnp.where(upper, edge, pltpu.roll(edge, tt - half, 0))
    q16 = q.astype(bf16)
    k16 = k.astype(bf16)
    for h in range(H_B):
        amat[h] = amat[h] + jnp.where(row == col, _dotg(q16[:, hs[h]], k16[:, hs[h]], _NT), 0.0)

    qe = (q * jnp.exp(b)).astype(bf16)
    ke = (k * jnp.exp(edge - b)).astype(bf16)
    e_last = jnp.exp(edge[0:1, :])
    v16 = vb_ref[0].astype(bf16)
    r = rb_ref[0]
    er = lax.broadcasted_iota(i32, (DK_B, DK_B), 0)
    ec = lax.broadcasted_iota(i32, (DK_B, DK_B), 1)
    for h in range(H_B):
        vs = slice(h * DV_B, (h + 1) * DV_B)
        vh = v16[:, vs]
        state = s_scr[h]
        o = _dot(amat[h].astype(bf16), vh) + _dot(qe[:, hs[h]], state.astype(bf16))
        decay_col = jnp.sum(
            jnp.where(er == ec, jnp.broadcast_to(e_last[:, hs[h]], (DK_B, DK_B)), 0.0),
            axis=1, keepdims=True)
        s_scr[h] = decay_col * state + _dotg(ke[:, hs[h]], vh, _TN)
        rh = r[:, vs]
        ob_ref[0, :, vs] = (_rms(o) * gg_ref[:, vs] * (rh * _sigmoid(rh))).astype(bf16)

    @pl.when(i == pl.num_programs(1) - 1)
    def _():
        sout_ref[0] = s_scr[...]


def _gla_prompt(qb, kb, vb, rb, sm, wal, bal, gg):
    b, t, _ = qb.shape
    tt = min(GLA_TILE, t)
    assert t % tt == 0 and tt & (tt - 1) == 0
    tok3 = lambda w: pl.BlockSpec((1, tt, w), lambda bi, i: (bi, i, 0))
    const2 = lambda a: pl.BlockSpec(a.shape, lambda bi, i: (0, 0))
    return pl.pallas_call(
        functools.partial(_gla_kernel, tt=tt),
        grid=(b, t // tt),
        in_specs=[tok3(WK_B), tok3(WK_B), tok3(WV_B), tok3(WV_B), tok3(LANES),
                  const2(wal), const2(bal), const2(gg)],
        out_specs=(tok3(WV_B),
                   pl.BlockSpec((1, H_B, DK_B, DV_B), lambda bi, i: (bi, 0, 0, 0))),
        out_shape=(jax.ShapeDtypeStruct((b, t, WV_B), bf16),
                   jax.ShapeDtypeStruct((b, H_B, DK_B, DV_B), f32)),
        scratch_shapes=[pltpu.VMEM((H_B, DK_B, DV_B), f32)],
        compiler_params=_cparams(("parallel", "arbitrary")),
        name="gla_prompt",
    )(qb, kb, vb, rb, sm, wal, bal, gg)


def _loga_kernel(sm_ref, wal_ref, bal_ref, o_ref):
    z = _dot(sm_ref[...].astype(bf16), wal_ref[...]) + bal_ref[...]
    o_ref[...] = _log_sigmoid(z) * (1.0 / GLA_TAU)


def _gla_decode_kernel(q_ref, k_ref, a_ref, v_ref, r_ref, s_ref, gg_ref, o_ref, so_ref):
    for h in range(H_B):
        q, k, a = q_ref[0, h], k_ref[0, h], a_ref[0, h]
        v = v_ref[0, h]
        state = s_ref[0, h]
        ea = jnp.exp(a)
        rnd = lambda z: z.astype(bf16).astype(f32)
        qk = jnp.sum(q * k, axis=0, keepdims=True)
        o = qk * v + jnp.sum(rnd(q * ea) * rnd(state), axis=0, keepdims=True)
        so_ref[0, h] = ea * state + k * v
        r = r_ref[0, h]
        o_ref[0, h] = _rms(o) * gg_ref[h] * (r * _sigmoid(r))


def _gla_decode(qb, kb, vb, rb, sm, wal, bal, gg, state):
    db = qb.shape[0]
    full = lambda a: pl.BlockSpec(a.shape, lambda: (0,) * a.ndim)
    loga = pl.pallas_call(
        _loga_kernel,
        in_specs=[full(sm), full(wal), full(bal)],
        out_specs=pl.BlockSpec((db, WK_B), lambda: (0, 0)),
        out_shape=jax.ShapeDtypeStruct((db, WK_B), f32),
        name="gla_log_decay",
    )(sm, wal, bal)
    col = lambda a: a.reshape(db, H_B, DK_B, 1)
    rowv = lambda a: a.reshape(db, H_B, 1, DV_B)
    gg4 = gg.reshape(H_B, 1, DV_B)
    b4 = lambda shp: pl.BlockSpec((1,) + shp, lambda bi: (bi, 0, 0, 0))
    o, s_new = pl.pallas_call(
        _gla_decode_kernel,
        grid=(db,),
        in_specs=[b4((H_B, DK_B, 1))] * 3 + [b4((H_B, 1, DV_B))] * 2
        + [b4((H_B, DK_B, DV_B)), pl.BlockSpec(gg4.shape, lambda bi: (0, 0, 0))],
        out_specs=(b4((H_B, 1, DV_B)), b4((H_B, DK_B, DV_B))),
        out_shape=(jax.ShapeDtypeStruct((db, H_B, 1, DV_B), f32),
                   jax.ShapeDtypeStruct((db, H_B, DK_B, DV_B), f32)),
        compiler_params=_cparams(("parallel",)),
        name="gla_decode",
    )(col(qb), col(kb), col(loga), rowv(vb), rowv(rb), state, gg4)
    return o.reshape(db, WV_B), s_new


def _mix_kernel(x_ref, oa_ref, ob_ref, ga_ref, gb_ref, gt1_ref, sc2_ref, sh2_ref, gf_ref,
                wba_ref, wbb_ref, wo_ref, wr_ref, br_ref, x1_ref, h2_ref, lg_ref, *, nsub):
    x = x_ref[0]
    tm = x.shape[0]
    ya = _dotg(oa_ref[0].astype(bf16), wba_ref[...], _TN)
    yb = _dot(ob_ref[0].astype(bf16), wbb_ref[...])
    m = _sigmoid(ga_ref[0]) * ya + _sigmoid(gb_ref[0]) * yb
    x1 = x + gt1_ref[0] * _dot(m.astype(bf16), wo_ref[...])
    x1_ref[...] = x1
    h2 = (_rms(x1) * gf_ref[...]) * sc2_ref[0] + sh2_ref[0]
    hb = h2.astype(bf16)
    lg_ref[...] = _dot(hb, wr_ref[...]) + br_ref[...]
    half = nsub * LANES
    hi = lax.bitcast_convert_type(hb[:, :half].astype(f32), jnp.uint32)
    lo = lax.bitcast_convert_type(hb[:, half:].astype(f32), jnp.uint32)
    packed = jnp.bitwise_or(hi, jnp.right_shift(lo, jnp.uint32(16)))
    for s in range(nsub):
        h2_ref[pl.ds(s, tm, stride=nsub), :] = packed[:, s * LANES:(s + 1) * LANES]


def _mix(x, oa, ob, ga, gb, gt1, sc2, sh2, g_ffn, wba, wbb, wo, wr, br):
    b, t, d = x.shape
    nt = b * t
    nsub = d // (2 * LANES)
    tm = min(TOKEN_TILE, t)
    assert t % tm == 0 and d % (2 * LANES) == 0
    rows = gt1.shape[1]
    mod_block = (1, 1, d) if rows == 1 else (1, tm, d)
    mod_map = (lambda bi, i: (bi, 0, 0)) if rows == 1 else (lambda bi, i: (bi, i, 0))
    mod = pl.BlockSpec(mod_block, mod_map)
    tok3 = lambda w: pl.BlockSpec((1, tm, w), lambda bi, i: (bi, i, 0))
    const2 = lambda a: pl.BlockSpec(a.shape, lambda bi, i: (0, 0))
    nti = t // tm
    blk = lambda bi, i: (bi * nti + i, 0)
    g2 = g_ffn.reshape(1, d)
    ins = [x, oa, ob, ga, gb, gt1, sc2, sh2, g2, wba, wbb, wo, wr, br]
    in_specs = [tok3(d), pl.BlockSpec((1, W_A, tm), lambda bi, i: (bi, 0, i)), tok3(WV_B),
                tok3(d), tok3(d), mod, mod, mod,
                const2(g2), const2(wba), const2(wbb), const2(wo), const2(wr), const2(br)]
    return pl.pallas_call(
        functools.partial(_mix_kernel, nsub=nsub),
        grid=(b, nti),
        in_specs=in_specs,
        out_specs=(pl.BlockSpec((tm, d), blk), pl.BlockSpec((tm * nsub, LANES), blk),
                   pl.BlockSpec((tm, LANES), blk)),
        out_shape=(jax.ShapeDtypeStruct((nt, d), f32),
                   jax.ShapeDtypeStruct((nt * nsub, LANES), jnp.uint32),
                   jax.ShapeDtypeStruct((nt, LANES), f32)),
        compiler_params=_cparams(("parallel", "arbitrary")),
        name="branch_mix",
    )(*ins)


def _route_kernel(lg_ref, cin_ref, rec_ref, gate_ref, cnt_ref, carry_scr, *, tr):
    @pl.when(pl.program_id(0) == 0)
    def _():
        carry_scr[...] = cin_ref[...]

    lane = lax.broadcasted_iota(i32, (tr, LANES), 1)
    lanef = lane.astype(f32)
    v = jnp.where(lane < N_EXPERTS, lg_ref[...], -jnp.inf)
    onehot = jnp.zeros((tr, LANES), f32)
    ids, vals = [], []
    for _ in range(TOP_K):
        mx = jnp.max(v, axis=1, keepdims=True)
        idx = jnp.min(jnp.where(v == mx, lanef, float(LANES)), axis=1, keepdims=True)
        sel = lanef == idx
        onehot = jnp.where(sel, 1.0, onehot)
        v = jnp.where(sel, -jnp.inf, v)
        ids.append(idx)
        vals.append(mx)
    es = [jnp.exp(vk - vals[0]) for vk in vals]
    tot = es[0]
    for e in es[1:]:
        tot = tot + e

    r2 = lax.broadcasted_iota(i32, (tr, tr), 0)
    c2 = lax.broadcasted_iota(i32, (tr, tr), 1)
    before = jnp.where(c2 < r2, 1.0, 0.0).astype(bf16)
    pref = _dot(before, onehot.astype(bf16)) + carry_scr[...]
    carry_scr[...] = carry_scr[...] + jnp.sum(onehot, axis=0, keepdims=True)

    rec = jnp.zeros((tr, LANES), f32)
    gate = jnp.zeros((tr, LANES), f32)
    for k in range(TOP_K):
        rank = jnp.sum(jnp.where(lanef == ids[k], pref, 0.0), axis=1, keepdims=True)
        rec = jnp.where(lane == k, rank * float(N_EXPERTS) + ids[k], rec)
        gate = jnp.where(lane == k, es[k] / tot, gate)
    rec_ref[...] = rec.astype(i32)
    gate_ref[...] = gate
    cnt_ref[...] = carry_scr[...]


def _route(logits, counts_in):
    nt = logits.shape[0]
    tr = _divisor_tile(nt, 512, SUBLANES)
    tile = pl.BlockSpec((tr, LANES), lambda i: (i, 0))
    cnt = pl.BlockSpec((1, LANES), lambda i: (0, 0))
    return pl.pallas_call(
        functools.partial(_route_kernel, tr=tr),
        grid=(nt // tr,),
        in_specs=[tile, cnt],
        out_specs=(tile, tile, cnt),
        out_shape=(jax.ShapeDtypeStruct((nt, LANES), i32),
                   jax.ShapeDtypeStruct((nt, LANES), f32),
                   jax.ShapeDtypeStruct((1, LANES), f32)),
        scratch_shapes=[pltpu.VMEM((1, LANES), f32)],
        compiler_params=_cparams(("arbitrary",)),
        name="route_topk",
    )(logits, counts_in)


def _dest_kernel(rec_ref, start_ref, out_ref):
    rec = rec_ref[...]
    lane = lax.broadcasted_iota(i32, rec.shape, 1)
    eid = jnp.bitwise_and(rec, N_EXPERTS - 1).astype(f32)
    rank = jnp.right_shift(rec, N_EXPERTS.bit_length() - 1)
    lanef = lane.astype(f32)
    start = jnp.zeros(rec.shape, f32)
    for k in range(TOP_K):
        mine = lanef == eid[:, k:k + 1]
        st = jnp.sum(jnp.where(mine, start_ref[...], 0.0), axis=1, keepdims=True)
        start = jnp.where(lane == k, st, start)
    out_ref[...] = start.astype(i32) + rank


def _dest_rows(rec, pad_start):
    nt = rec.shape[0]
    tr = _divisor_tile(nt, 512, SUBLANES)
    tile = pl.BlockSpec((tr, LANES), lambda i: (i, 0))
    start = jnp.zeros((1, LANES), f32).at[0, :N_EXPERTS].set(pad_start.astype(f32))
    out = pl.pallas_call(
        _dest_kernel,
        grid=(nt // tr,),
        in_specs=[tile, pl.BlockSpec((1, LANES), lambda i: (0, 0))],
        out_specs=tile,
        out_shape=jax.ShapeDtypeStruct((nt, LANES), i32),
        compiler_params=_cparams(("parallel",)),
        name="moe_dest_rows",
    )(rec, start)
    return out[:, :TOP_K].reshape(-1)


def _dispatch_kernel(dest_ref, h_ref, xin_hbm, xout_hbm, stage, sems, *, nsub, ch):
    del xin_hbm
    i = pl.program_id(0)
    slot = i % 2
    base = i * ch
    total = ch * TOP_K * nsub

    def drain(s):
        pltpu.make_async_copy(xout_hbm.at[pl.ds(0, total)], xout_hbm.at[pl.ds(0, total)],
                              sems.at[s]).wait()

    stage[slot] = h_ref[...]

    def body(n, carry):
        for k in range(TOP_K):
            src = pl.multiple_of(n * nsub, nsub)
            dst = pl.multiple_of(dest_ref[(base + n) * TOP_K + k] * nsub, nsub)
            pltpu.make_async_copy(stage.at[slot, pl.ds(src, nsub)], xout_hbm.at[pl.ds(dst, nsub)],
                                  sems.at[slot]).start(priority=k % 2)
        return carry

    lax.fori_loop(0, ch, body, 0, unroll=4)

    @pl.when(i > 0)
    def _():
        drain(1 - slot)

    @pl.when(i == pl.num_programs(0) - 1)
    def _():
        drain(slot)


def _dispatch(dest_flat, h2rows, xrows, nsub):
    n_tok = dest_flat.shape[0] // TOP_K
    ch = min(DISPATCH_CHUNK, n_tok)
    assert n_tok % ch == 0
    grid_spec = pltpu.PrefetchScalarGridSpec(
        num_scalar_prefetch=1,
        grid=(n_tok // ch,),
        in_specs=[pl.BlockSpec((ch * nsub, LANES), lambda i, dr: (i, 0)),
                  pl.BlockSpec(memory_space=pl.ANY)],
        out_specs=pl.BlockSpec(memory_space=pl.ANY),
        scratch_shapes=[pltpu.VMEM((2, ch * nsub, LANES), h2rows.dtype),
                        pltpu.SemaphoreType.DMA((2,))])
    return pl.pallas_call(
        functools.partial(_dispatch_kernel, nsub=nsub, ch=ch),
        grid_spec=grid_spec,
        out_shape=jax.ShapeDtypeStruct(xrows.shape, xrows.dtype),
        input_output_aliases={2: 0},
        compiler_params=_cparams(("arbitrary",)),
        name="moe_dispatch",
    )(dest_flat, h2rows, xrows)


def _expert_kernel(be_ref, nused_ref, x_ref, wgu_ref, bgu_ref, wd_ref, bd_ref, y_ref,
                   wgu_b, wd_b, *, tmx, nsub_x, nsub, dff, chunk):
    r = pl.program_id(0)
    live = r < nused_ref[0]

    @pl.when(jnp.logical_not(live))
    def _():
        y_ref[...] = jnp.zeros_like(y_ref)

    new_expert = jnp.logical_or(r == 0, be_ref[r] != be_ref[jnp.maximum(r - 1, 0)])

    @pl.when(jnp.logical_and(live, new_expert))
    def _():
        for c in range(0, wgu_b.shape[0], chunk):
            wgu_b[c:c + chunk, :] = wgu_ref[0, c:c + chunk, :].astype(bf16)
        for c in range(0, wd_b.shape[0], chunk):
            wd_b[c:c + chunk, :] = wd_ref[0, c:c + chunk, :].astype(bf16)

    @pl.when(live)
    def _():
        words = [x_ref[pl.ds(s, tmx, stride=nsub_x), :] for s in range(nsub_x)]
        as_bf16 = lambda w: lax.bitcast_convert_type(w, f32).astype(bf16)
        x = jnp.concatenate(
            [as_bf16(jnp.bitwise_and(w, jnp.uint32(0xFFFF0000))) for w in words]
            + [as_bf16(jnp.left_shift(w, jnp.uint32(16))) for w in words], axis=1)
        gu = _dot(x, wgu_b[...]) + bgu_ref[0]
        gate = jnp.minimum(gu[:, :dff], SWIGLU_LIMIT)
        up = jnp.clip(gu[:, dff:], -SWIGLU_LIMIT, SWIGLU_LIMIT)
        glu = gate * _sigmoid(SWIGLU_ALPHA * gate)
        y = _dot(((up + 1.0) * glu).astype(bf16), wd_b[...]) + bd_ref[0]
        for s in range(nsub):
            y_ref[pl.ds(s, tmx, stride=nsub), :] = y[:, s * LANES:(s + 1) * LANES]


def _experts(block_e, nused, xrows, wgu, bgu, wd, bd):
    n_blocks = block_e.shape[0]
    tmx = EXPERT_TILE
    e, d, dff2 = wgu.shape
    dff = dff2 // 2
    nsub = d // LANES
    nsub_x = d // (2 * LANES)
    live = lambda r, nu: jnp.minimum(r, nu[0] - 1)
    rows = pl.BlockSpec((tmx * nsub_x, LANES), lambda r, be, nu: (live(r, nu), 0))
    per_e = lambda shp: pl.BlockSpec((1,) + shp, lambda r, be, nu: (be[live(r, nu)], 0, 0))
    grid_spec = pltpu.PrefetchScalarGridSpec(
        num_scalar_prefetch=2,
        grid=(n_blocks,),
        in_specs=[rows, per_e((d, dff2)), per_e((1, dff2)), per_e((dff, d)), per_e((1, d))],
        out_specs=pl.BlockSpec((tmx * nsub, LANES), lambda r, be, nu: (r, 0)),
        scratch_shapes=[pltpu.VMEM((d, dff2), bf16), pltpu.VMEM((dff, d), bf16)])
    return pl.pallas_call(
        functools.partial(_expert_kernel, tmx=tmx, nsub_x=nsub_x, nsub=nsub, dff=dff,
                          chunk=min(256, d, dff)),
        grid_spec=grid_spec,
        out_shape=jax.ShapeDtypeStruct((n_blocks * tmx * nsub, LANES), f32),
        compiler_params=_cparams(("arbitrary",)),
        name="moe_experts",
    )(block_e, nused, xrows, wgu, bgu.reshape(e, 1, dff2), wd, bd.reshape(e, 1, d))


def _combine_kernel(dest_ref, y_hbm, gate_ref, x1_ref, gt2_ref, gfin_ref, out_ref,
                    buf, sems, *, tc, nsub):
    i = pl.program_id(0)
    slot = i % 2

    def start_tile(tile, s):
        def body(n, carry):
            for k in range(TOP_K):
                src = pl.multiple_of(dest_ref[(tile * tc + n) * TOP_K + k] * nsub, nsub)
                dst = pl.multiple_of((k * tc + n) * nsub, nsub)
                pltpu.make_async_copy(y_hbm.at[pl.ds(src, nsub)], buf.at[s, pl.ds(dst, nsub)],
                                      sems.at[s]).start(priority=k % 2)
            return carry
        lax.fori_loop(0, tc, body, 0, unroll=4)

    @pl.when(i == 0)
    def _():
        start_tile(0, 0)

    @pl.when(i + 1 < pl.num_programs(0))
    def _():
        start_tile(i + 1, 1 - slot)

    pltpu.make_async_copy(y_hbm.at[pl.ds(0, buf.shape[1])], buf.at[slot], sems.at[slot]).wait()

    g = gate_ref[...]
    cols = []
    for s in range(nsub):
        acc = None
        for k in range(TOP_K):
            term = g[:, k:k + 1] * buf[slot, pl.ds(k * tc * nsub + s, tc, stride=nsub), :]
            acc = term if acc is None else acc + term
        cols.append(acc)
    y = jnp.concatenate(cols, axis=1)
    out_ref[...] = _rms(x1_ref[...] + gt2_ref[0] * y) * gfin_ref[...]


def _combine(dest_flat, yrows, gates, x1, gt2, g_final, t_per_mod, nsub):
    n_tok, d = x1.shape
    tc = min(TOKEN_TILE, n_tok)
    assert n_tok % tc == 0
    rows = gt2.shape[1]
    if rows == 1:
        per_mod = t_per_mod // tc
        mod = pl.BlockSpec((1, 1, d), lambda i, dr: (i // per_mod, 0, 0))
    else:
        mod = pl.BlockSpec((1, tc, d), lambda i, dr: (0, i, 0))
    blk = lambda i, dr: (i, 0)
    gfin = g_final.reshape(1, d)
    grid_spec = pltpu.PrefetchScalarGridSpec(
        num_scalar_prefetch=1,
        grid=(n_tok // tc,),
        in_specs=[pl.BlockSpec(memory_space=pl.ANY), pl.BlockSpec((tc, LANES), blk),
                  pl.BlockSpec((tc, d), blk), mod,
                  pl.BlockSpec((1, d), lambda i, dr: (0, 0))],
        out_specs=pl.BlockSpec((tc, d), blk),
        scratch_shapes=[pltpu.VMEM((2, TOP_K * tc * nsub, LANES), f32),
                        pltpu.SemaphoreType.DMA((2,))])
    return pl.pallas_call(
        functools.partial(_combine_kernel, tc=tc, nsub=nsub),
        grid_spec=grid_spec,
        out_shape=jax.ShapeDtypeStruct((n_tok, d), f32),
        compiler_params=_cparams(("arbitrary",)),
        name="moe_combine",
    )(dest_flat, yrows, gates, x1, gt2, gfin)


def kernel(x_prompt, x_sample, c_prompt, c_sample, cache_k, cache_v, cache_logf, state_gla,
           page_table, g_mix, g_ffn, g_final, w_ada, b_ada, w_in, b_f, w_alpha, b_alpha, g_gla,
           w_branch, w_o, w_router, b_router, w_gu, b_gu, w_d, b_d):
    depth = g_mix.shape[0]
    assert depth == 1, "one decoder layer"
    bp, t, d = x_prompt.shape
    db = x_sample.shape[0]
    assert x_sample.shape[1] == 1
    nsub = d // LANES
    ntp = bp * t
    nt = ntp + db

    nmod = bp + db
    pad = (-nmod) % SUBLANES
    c_all = jnp.concatenate([c_prompt, c_sample, jnp.zeros((pad, d), f32)], axis=0)
    mod = _adaln(c_all, w_ada[0], b_ada[0])
    sh1, sc1, gt1, sh2, sc2, gt2 = [mod[:, i * d:(i + 1) * d] for i in range(6)]
    grp_p = lambda a: a[:bp].reshape(bp, 1, d)
    grp_s = lambda a: a[bp:nmod].reshape(1, db, d)

    w_proj = _inproj_weights(w_in[0], d)
    wal = jnp.zeros((LANES, WK_B), f32).at[H_A:H_A + GLA_RANK].set(w_alpha[0]).astype(bf16)
    bal = b_alpha[0].reshape(1, WK_B)
    gg = g_gla[0].reshape(1, WV_B)
    wba = w_branch[0, :W_A].astype(bf16)
    wbb = w_branch[0, W_A:].astype(bf16)
    wo = w_o[0].astype(bf16)
    wr = jnp.zeros((d, LANES), f32).at[:, :N_EXPERTS].set(w_router[0]).astype(bf16)
    br = jnp.zeros((1, LANES), f32).at[0, :N_EXPERTS].set(b_router[0])

    (qat, kat, vt, ka, va, qb, kb, vb, rb, ga, gb, sm, fat) = _inproj(
        x_prompt, grp_p(1.0 + sc1), grp_p(sh1), g_mix[0], w_proj, True)
    logft, cumt = _logf(fat, b_f[0], True)
    oa = _fox_prompt(qat, kat, vt, jnp.transpose(cumt, (0, 2, 1)))
    ob, gla_p = _gla_prompt(qb, kb, vb, rb, sm, wal, bal, gg)
    x1_p, h2_p, lg_p = _mix(x_prompt, oa, ob, ga, gb, grp_p(gt1), grp_p(1.0 + sc2), grp_p(sh2),
                            g_ffn[0], wba, wbb, wo, wr, br)

    xs = x_sample.reshape(1, db, d)
    (qat_s, ka_s, va_s, qb_s, kb_s, vb_s, rb_s, ga_s, gb_s, sm_s, fat_s) = _inproj(
        xs, grp_s(1.0 + sc1), grp_s(sh1), g_mix[0], w_proj, False)
    logft_s, = _logf(fat_s, b_f[0], False)
    lfnew = jnp.transpose(logft_s[0], (1, 0)).reshape(db, H_A, 1)
    qbd = jnp.einsum('hnd,hc->nchd', qat_s[0], jnp.eye(H_A, dtype=bf16)).reshape(db, H_A, W_A)
    oa_s = _fox_decode(qbd, ka_s.reshape(db, 1, W_A), va_s.reshape(db, 1, W_A), lfnew,
                       jnp.transpose(cache_k[0], (0, 2, 3, 1)),
                       jnp.transpose(cache_v[0], (0, 2, 3, 1)),
                       jnp.transpose(cache_logf[0], (0, 2, 1)), page_table)
    ob_s, gla_s = _gla_decode(qb_s[0], kb_s[0], vb_s[0], rb_s[0], sm_s[0], wal, bal, gg,
                              state_gla[0])
    x1_s, h2_s, lg_s = _mix(xs, oa_s.reshape(db, W_A).T[None], ob_s.reshape(1, db, WV_B), ga_s, gb_s,
                            grp_s(gt1), grp_s(1.0 + sc2), grp_s(sh2), g_ffn[0],
                            wba, wbb, wo, wr, br)

    rec_p, gates_p, cnt_p = _route(lg_p, jnp.zeros((1, LANES), f32))
    rec_s, gates_s, cnt = _route(lg_s, cnt_p)
    counts = cnt[0, :N_EXPERTS].astype(i32)
    padded = (counts + EXPERT_TILE - 1) // EXPERT_TILE * EXPERT_TILE
    pad_end = jnp.cumsum(padded).astype(i32)
    pad_start = pad_end - padded

    dest_p = _dest_rows(rec_p, pad_start)
    dest_s = _dest_rows(rec_s, pad_start)
    n_blocks = -(-(nt * TOP_K + N_EXPERTS * (EXPERT_TILE - 1)) // EXPERT_TILE)
    block_start = jnp.arange(n_blocks, dtype=i32) * EXPERT_TILE
    block_e = jnp.minimum(jnp.sum(pad_end[None, :] <= block_start[:, None], axis=1),
                          N_EXPERTS - 1).astype(i32)
    nused = (pad_end[-1:] // EXPERT_TILE).astype(i32)

    nsub_x = d // (2 * LANES)
    xrows = jnp.zeros((n_blocks * EXPERT_TILE * nsub_x, LANES), jnp.uint32)
    xrows = _dispatch(dest_p, h2_p, xrows, nsub_x)
    xrows = _dispatch(dest_s, h2_s, xrows, nsub_x)
    yrows = _experts(block_e, nused, xrows, w_gu[0], b_gu[0], w_d[0], b_d[0])
    y_p = _combine(dest_p, yrows, gates_p, x1_p, grp_p(gt2), g_final, t, nsub)
    y_s = _combine(dest_s, yrows, gates_s, x1_s, grp_s(gt2), g_final, 1, nsub)

    logf_p = jnp.transpose(logft, (0, 2, 1))
    return (y_p.reshape(bp, t, d), y_s.reshape(db, 1, d),
            jnp.transpose(ka, (0, 3, 1, 2))[None], jnp.transpose(va, (0, 3, 1, 2))[None],
            logf_p.reshape(1, bp, t, H_A), gla_p.reshape(1, bp, H_B, DK_B, DV_B),
            ka_s.reshape(1, db, 1, H_A, DH_A), va_s.reshape(1, db, 1, H_A, DH_A),
            lfnew.reshape(1, db, 1, H_A), gla_s.reshape(1, db, H_B, DK_B, DV_B))
```
